```python
import math
import jax, jax.numpy as jnp
from jax import lax
import numpy as np

D_MODEL = 1024
BATCH = 2
SEQ = 8192
DEPTH = 1
DEC_BATCH = 128
DEC_SEQ = 1
PAST_LEN = 16384
PAGE_SIZE = 128

HEAD_DIM = 64
N_HEADS = D_MODEL // HEAD_DIM
N_KV_HEADS = N_HEADS // 4
GROUP = N_HEADS // N_KV_HEADS
ROT_DIMS = HEAD_DIM // 4
ROPE_THETA = 500000.0
WINDOW = 128
BLOCK = 128
POOL_WIDTH = D_MODEL // 2
POOL_WINDOWS = (2, 4, 8, 16)
POOL_GROUPS = len(POOL_WINDOWS)
POOL_GC = POOL_WIDTH // POOL_GROUPS
POOL_STATE = max(POOL_WINDOWS) - 1
FFN_HIDDEN = -(-8 * D_MODEL // (3 * 256)) * 256
PLE_DIM = 256
EPS = 1e-6
NEG_INF = -1e30

Q_W = N_HEADS * HEAD_DIM
KV_W = N_KV_HEADS * HEAD_DIM
IN_COLS = POOL_WIDTH + Q_W + 2 * KV_W + 2 * D_MODEL
SPLITS = (POOL_WIDTH, POOL_WIDTH + Q_W, POOL_WIDTH + Q_W + KV_W, POOL_WIDTH + Q_W + 2 * KV_W, POOL_WIDTH + Q_W + 2 * KV_W + D_MODEL)

kernel_name = "hybrid_pool_swa_sink_decoder_step"


def _rms_norm(x, g):
    xf = x.astype(jnp.float32)
    y = xf * lax.rsqrt(jnp.mean(xf * xf, axis=-1, keepdims=True) + EPS)
    return (y * g.astype(jnp.float32)).astype(x.dtype)


def _partial_rope(x, pos):
    half = ROT_DIMS // 2
    inv = ROPE_THETA ** (-(jnp.arange(0, ROT_DIMS, 2, dtype=jnp.float32) / ROT_DIMS))
    ang = pos.astype(jnp.float32)[:, None] * inv[None, :]
    cos = jnp.cos(ang)[None, :, None, :]
    sin = jnp.sin(ang)[None, :, None, :]
    xr = x[..., :ROT_DIMS].astype(jnp.float32)
    x1, x2 = xr[..., :half], xr[..., half:]
    rot = jnp.concatenate([x1 * cos - x2 * sin, x2 * cos + x1 * sin], axis=-1)
    return jnp.concatenate([rot.astype(x.dtype), x[..., ROT_DIMS:]], axis=-1)


def _layer_inputs(h, ln1, w_in, pos):
    B, T, _ = h.shape
    z = _rms_norm(h, ln1) @ w_in
    u, q, k, v, gp, ga = jnp.split(z, SPLITS, axis=-1)
    q = _partial_rope(q.reshape(B, T, N_HEADS, HEAD_DIM), pos)
    k = _partial_rope(k.reshape(B, T, N_KV_HEADS, HEAD_DIM), pos)
    v = v.reshape(B, T, N_KV_HEADS, HEAD_DIM)
    return u, q, k, v, gp, ga


def _pool_branch(u_prev, u, pos, group_w, scale):
    T = u.shape[1]
    P = POOL_STATE
    ext = jnp.concatenate([u_prev, u], axis=1)
    extf = ext.astype(jnp.float32)
    cs = jnp.concatenate([jnp.zeros_like(extf[:, :1]), jnp.cumsum(extf, axis=1)], axis=1)
    means = []
    for g, w in enumerate(POOL_WINDOWS):
        c0, c1 = g * POOL_GC, (g + 1) * POOL_GC
        s = cs[:, P + 1:P + 1 + T, c0:c1] - cs[:, P + 1 - w:P + 1 - w + T, c0:c1]
        cnt = jnp.minimum(w, pos + 1).astype(jnp.float32)[None, :, None]
        means.append(s / cnt)
    m = (jnp.concatenate(means, axis=-1) - u.astype(jnp.float32)).astype(u.dtype)
    B = u.shape[0]
    mixed = jnp.einsum('btgc,gcd->btgd', m.reshape(B, T, POOL_GROUPS, POOL_GC), group_w)
    mixed = mixed.reshape(B, T, POOL_WIDTH) * scale
    return mixed, ext[:, -P:]


def _sink_attn(q, k, v, mask, sinks):
    s = jnp.einsum('...qkgd,...skd->...kgqs', q.astype(jnp.float32), k.astype(jnp.float32)) * (HEAD_DIM ** -0.5)
    s = jnp.where(mask, s, jnp.float32(NEG_INF))
    sink = jnp.broadcast_to(sinks.astype(jnp.float32)[:, :, None, None], s.shape[:-1] + (1,))
    pr = jax.nn.softmax(jnp.concatenate([s, sink], axis=-1), axis=-1)[..., :-1]
    o = jnp.einsum('...kgqs,...skd->...qkgd', pr, v.astype(jnp.float32))
    return o.astype(q.dtype)


def _attn_prompt(q, k, v, sinks):
    B, S = q.shape[0], q.shape[1]
    nb = S // BLOCK
    qb = q.reshape(B, nb, BLOCK, N_KV_HEADS, GROUP, HEAD_DIM)
    kb = k.reshape(B, nb, BLOCK, N_KV_HEADS, HEAD_DIM)
    vb = v.reshape(B, nb, BLOCK, N_KV_HEADS, HEAD_DIM)
    kk = jnp.concatenate([jnp.concatenate([jnp.zeros_like(kb[:, :1]), kb[:, :-1]], axis=1), kb], axis=2)
    vv = jnp.concatenate([jnp.concatenate([jnp.zeros_like(vb[:, :1]), vb[:, :-1]], axis=1), vb], axis=2)
    qi = jnp.arange(BLOCK)[:, None]
    si = jnp.arange(2 * BLOCK)[None, :]
    rel = qi + BLOCK - si
    kpos = (jnp.arange(nb)[:, None, None] - 1) * BLOCK + si[None]
    mask = (rel >= 0)[None] & (rel < WINDOW)[None] & (kpos >= 0)
    o = _sink_attn(qb, kk, vv, mask[:, None, None], sinks.reshape(N_KV_HEADS, GROUP))
    return o.reshape(B, S, Q_W)


def _attn_sample(q, k, v, cache_k, cache_v, sinks):
    Bd, T = q.shape[0], q.shape[1]
    w_cache = cache_k.shape[1]
    kk = jnp.concatenate([cache_k, k], axis=1)
    vv = jnp.concatenate([cache_v, v], axis=1)
    qpos = PAST_LEN + jnp.arange(T)
    kpos = jnp.concatenate([PAST_LEN - w_cache + jnp.arange(w_cache), qpos])
    rel = qpos[:, None] - kpos[None, :]
    mask = (rel >= 0) & (rel < WINDOW)
    o = _sink_attn(q.reshape(Bd, T, N_KV_HEADS, GROUP, HEAD_DIM), kk, vv, mask, sinks.reshape(N_KV_HEADS, GROUP))
    return o.reshape(Bd, T, Q_W), kk[:, -w_cache:], vv[:, -w_cache:]


def _layer_outputs(h, p, pooled, attn, gp, ga, w_pool_branch, w_attn_branch, w_out, ln2, w_ffn_in, w_ffn_out, w_ple_proj, ple_norm, w_ple_gate):
    merged = jax.nn.sigmoid(gp) * (pooled @ w_pool_branch) + jax.nn.sigmoid(ga) * (attn @ w_attn_branch)
    h = h + merged @ w_out
    gate, up = jnp.split(_rms_norm(h, ln2) @ w_ffn_in, 2, axis=-1)
    h = h + (jax.nn.silu(gate) * up) @ w_ffn_out
    e = _rms_norm(p @ w_ple_proj, ple_norm)
    return h + jax.nn.sigmoid(h @ w_ple_gate) * e


def setup_inputs(seed: int = 0) -> dict:
    key = jax.random.key(seed)
    ks = jax.random.split(key, 32)
    f32 = jnp.float32
    w_cache = min(WINDOW, PAST_LEN)
    nrm = lambda k, shape, s=1.0: jax.random.normal(k, shape, f32) * s
    return {
        "x_prompt": nrm(ks[0], (BATCH, SEQ, D_MODEL)),
        "x_sample": nrm(ks[1], (DEC_BATCH, DEC_SEQ, D_MODEL)),
        "p_prompt": nrm(ks[2], (DEPTH, BATCH, SEQ, PLE_DIM)),
        "p_sample": nrm(ks[3], (DEPTH, DEC_BATCH, DEC_SEQ, PLE_DIM)),
        "cache_k": nrm(ks[4], (DEPTH, DEC_BATCH, w_cache, N_KV_HEADS, HEAD_DIM)),
        "cache_v": nrm(ks[5], (DEPTH, DEC_BATCH, w_cache, N_KV_HEADS, HEAD_DIM)),
        "state_pool": nrm(ks[6], (DEPTH, DEC_BATCH, POOL_STATE, POOL_WIDTH)),
        "ln1": 1.0 + nrm(ks[7], (DEPTH, D_MODEL), 0.02),
        "w_in": nrm(ks[8], (DEPTH, D_MODEL, IN_COLS), D_MODEL ** -0.5),
        "pool_group_w": nrm(ks[9], (DEPTH, POOL_GROUPS, POOL_GC, POOL_GC), POOL_GC ** -0.5),
        "pool_scale": 1.0 + nrm(ks[10], (DEPTH, POOL_WIDTH), 0.02),
        "attn_sinks": nrm(ks[11], (DEPTH, N_HEADS), 0.5),
        "w_pool_branch": nrm(ks[12], (DEPTH, POOL_WIDTH, D_MODEL), POOL_WIDTH ** -0.5),
        "w_attn_branch": nrm(ks[13], (DEPTH, Q_W, D_MODEL), Q_W ** -0.5),
        "w_out": nrm(ks[14], (DEPTH, D_MODEL, D_MODEL), D_MODEL ** -0.5),
        "ln2": 1.0 + nrm(ks[15], (DEPTH, D_MODEL), 0.02),
        "w_ffn_in": nrm(ks[16], (DEPTH, D_MODEL, 2 * FFN_HIDDEN), D_MODEL ** -0.5),
        "w_ffn_out": nrm(ks[17], (DEPTH, FFN_HIDDEN, D_MODEL), FFN_HIDDEN ** -0.5),
        "w_ple_proj": nrm(ks[18], (DEPTH, PLE_DIM, D_MODEL), PLE_DIM ** -0.5),
        "ple_norm": 1.0 + nrm(ks[19], (DEPTH, D_MODEL), 0.02),
        "w_ple_gate": nrm(ks[20], (DEPTH, D_MODEL, D_MODEL), D_MODEL ** -0.5),
        "final_norm": 1.0 + nrm(ks[21], (D_MODEL,), 0.02),
    }


def reference(x_prompt, x_sample, p_prompt, p_sample, cache_k, cache_v, state_pool, ln1, w_in, pool_group_w, pool_scale, attn_sinks, w_pool_branch, w_attn_branch, w_out, ln2, w_ffn_in, w_ffn_out, w_ple_proj, ple_norm, w_ple_gate, final_norm):
    B, S, _ = x_prompt.shape
    T = x_sample.shape[1]
    w_cache = cache_k.shape[2]
    pos_p = jnp.arange(S, dtype=jnp.int32)
    pos_s = PAST_LEN + jnp.arange(T, dtype=jnp.int32)
    hp, hs = x_prompt, x_sample
    nkp, nvp, npp, nks, nvs, nps = [], [], [], [], [], []
    for i in range(DEPTH):
        u, q, k, v, gp, ga = _layer_inputs(hp, ln1[i], w_in[i], pos_p)
        pooled, st = _pool_branch(jnp.zeros((B, POOL_STATE, POOL_WIDTH), u.dtype), u, pos_p, pool_group_w[i], pool_scale[i])
        att = _attn_prompt(q, k, v, attn_sinks[i])
        hp = _layer_outputs(hp, p_prompt[i], pooled, att, gp, ga, w_pool_branch[i], w_attn_branch[i], w_out[i], ln2[i], w_ffn_in[i], w_ffn_out[i], w_ple_proj[i], ple_norm[i], w_ple_gate[i])
        nkp.append(k[:, -w_cache:])
        nvp.append(v[:, -w_cache:])
        npp.append(st)
        u, q, k, v, gp, ga = _layer_inputs(hs, ln1[i], w_in[i], pos_s)
        pooled, st = _pool_branch(state_pool[i], u, pos_s, pool_group_w[i], pool_scale[i])
        att, kn, vn = _attn_sample(q, k, v, cache_k[i], cache_v[i], attn_sinks[i])
        hs = _layer_outputs(hs, p_sample[i], pooled, att, gp, ga, w_pool_branch[i], w_attn_branch[i], w_out[i], ln2[i], w_ffn_in[i], w_ffn_out[i], w_ple_proj[i], ple_norm[i], w_ple_gate[i])
        nks.append(kn)
        nvs.append(vn)
        nps.append(st)
    y_prompt = _rms_norm(hp, final_norm)
    y_sample = _rms_norm(hs, final_norm)
    return (y_prompt, y_sample, jnp.stack(nkp), jnp.stack(nvp), jnp.stack(npp), jnp.stack(nks), jnp.stack(nvs), jnp.stack(nps))
```

```python
import functools

import jax
import jax.numpy as jnp
from jax import lax
from jax.experimental import pallas as pl
from jax.experimental.pallas import tpu as pltpu

D_MODEL = 1024
HEAD_DIM = 64
N_HEADS = D_MODEL // HEAD_DIM
N_KV_HEADS = N_HEADS // 4
GROUP = N_HEADS // N_KV_HEADS
ROT_DIMS = HEAD_DIM // 4
ROPE_THETA = 500000.0
WINDOW = 128
BLOCK = 128
POOL_WIDTH = D_MODEL // 2
POOL_WINDOWS = (2, 4, 8, 16)
POOL_GC = POOL_WIDTH // len(POOL_WINDOWS)
POOL_STATE = max(POOL_WINDOWS) - 1
FFN_HIDDEN = -(-8 * D_MODEL // (3 * 256)) * 256
PLE_DIM = 256
EPS = 1e-6
NEG_INF = -1e30
PAST_LEN = 16384

Q_W = N_HEADS * HEAD_DIM
KV_W = N_KV_HEADS * HEAD_DIM
C_U, C_Q, C_K, C_V, C_GP, C_GA, C_END = 0, POOL_WIDTH, POOL_WIDTH + Q_W, POOL_WIDTH + Q_W + KV_W, \
    POOL_WIDTH + Q_W + 2 * KV_W, POOL_WIDTH + Q_W + 2 * KV_W + D_MODEL, POOL_WIDTH + Q_W + 2 * KV_W + 2 * D_MODEL

LANES = 128
U_HALO = 16
VMEM_LIMIT = 56 * 1024 * 1024

BF16 = jnp.bfloat16
F32 = jnp.float32


def _dot(a, b):
    return jnp.dot(a, b, preferred_element_type=F32)


def _dot_nt(a, b):
    return lax.dot_general(a, b, (((1,), (1,)), ((), ())), preferred_element_type=F32)


def _rms(x, g):
    y = x * lax.rsqrt(jnp.mean(x * x, axis=-1, keepdims=True) + EPS)
    return y * g


def _rope(x, cos, sg, lo8):
    outs = []
    for c in range(x.shape[1] // LANES):
        xc = x[:, c * LANES:(c + 1) * LANES]
        partner = jnp.where(lo8, pltpu.roll(xc, LANES - ROT_DIMS // 2, 1), pltpu.roll(xc, ROT_DIMS // 2, 1))
        outs.append(xc * cos + partner * sg)
    return jnp.concatenate(outs, axis=1)


def _lane_masks():
    lane = lax.broadcasted_iota(jnp.int32, (1, LANES), 1)
    lo8 = (lane % HEAD_DIM) < (ROT_DIMS // 2)
    lo64 = lane < HEAD_DIM
    return lo8, lo64


def _pool_mix(u, win_sum_fn, cnt_fn, gw_ref, pscale_ref):
    mixed = []
    for g, w in enumerate(POOL_WINDOWS):
        cols = slice(g * POOL_GC, (g + 1) * POOL_GC)
        ug = u[:, cols]
        m = win_sum_fn(g, w, ug) / cnt_fn(w) - ug
        mixed.append(_dot(m.astype(BF16), gw_ref[g]) * pscale_ref[:, cols])
    return jnp.concatenate(mixed, axis=1)


def _mixer_kernel(sinks_ref, x_ref, cos_ref, sg_ref, ln1_ref, w_in_ref, gw_ref, pscale_ref, w_pb_ref, w_ab_ref,
                  w_out_ref, h_ref, ko_ref, vo_ref, po_ref, kl_scr, kh_scr, vl_scr, vh_scr, u_scr, attn_scr, *, tm):
    t = pl.program_id(1)
    lo8, lo64 = _lane_masks()

    @pl.when(t == 0)
    def _():
        for scr in (kl_scr, kh_scr, vl_scr, vh_scr):
            scr[:, 0:BLOCK, :] = jnp.zeros((N_KV_HEADS, BLOCK, LANES), BF16)
        u_scr[0:U_HALO, :] = jnp.zeros((U_HALO, POOL_WIDTH), F32)

    x = x_ref[0]
    xn = _rms(x, ln1_ref[...]).astype(BF16)
    cos = cos_ref[...]
    sg = sg_ref[...]

    u = _dot(xn, w_in_ref[:, C_U:C_Q])
    q = (_rope(_dot(xn, w_in_ref[:, C_Q:C_K]), cos, sg, lo8) * (HEAD_DIM ** -0.5)).astype(BF16)
    k = _rope(_dot(xn, w_in_ref[:, C_K:C_V]), cos, sg, lo8)
    v = _dot(xn, w_in_ref[:, C_V:C_GP])
    ko_ref[0] = k[tm - WINDOW:, :]
    vo_ref[0] = v[tm - WINDOW:, :]

    zero = jnp.zeros((tm, LANES), F32)
    for src, lo_scr, hi_scr in ((k, kl_scr, kh_scr), (v, vl_scr, vh_scr)):
        for p in range(KV_W // LANES):
            xp = src[:, p * LANES:(p + 1) * LANES]
            xs = pltpu.roll(xp, HEAD_DIM, 1)
            lo_scr[2 * p, BLOCK:, :] = jnp.where(lo64, xp, zero).astype(BF16)
            hi_scr[2 * p, BLOCK:, :] = jnp.where(lo64, zero, xs).astype(BF16)
            lo_scr[2 * p + 1, BLOCK:, :] = jnp.where(lo64, xs, zero).astype(BF16)
            hi_scr[2 * p + 1, BLOCK:, :] = jnp.where(lo64, zero, xp).astype(BF16)

    qi = lax.broadcasted_iota(jnp.int32, (BLOCK, 2 * BLOCK), 0)
    si = lax.broadcasted_iota(jnp.int32, (BLOCK, 2 * BLOCK), 1)
    band = (si > qi) & (si <= qi + WINDOW)
    bias_band = jnp.where(band, 0.0, NEG_INF).astype(F32)
    bias_first = jnp.where(band & (si >= BLOCK), 0.0, NEG_INF).astype(F32)

    for j in range(tm // BLOCK):
        bias = jnp.where(t == 0, bias_first, bias_band) if j == 0 else bias_band
        rows = slice(j * BLOCK, (j + 1) * BLOCK)
        win = slice(j * BLOCK, (j + 2) * BLOCK)
        for c in range(Q_W // LANES):
            kh = c // 2
            qc = q[rows, c * LANES:(c + 1) * LANES]
            kcat = jnp.concatenate([kl_scr[kh, win, :], kh_scr[kh, win, :]], axis=0)
            vcat = jnp.concatenate([vl_scr[kh, win, :], vh_scr[kh, win, :]], axis=0)
            s = _dot_nt(qc, kcat)
            es, ls = [], []
            for half in range(2):
                sh = s[:, half * 2 * BLOCK:(half + 1) * 2 * BLOCK] + bias
                sink = sinks_ref[2 * c + half]
                m = jnp.maximum(jnp.max(sh, axis=1, keepdims=True), sink)
                e = jnp.exp(sh - m)
                es.append(e)
                ls.append(jnp.sum(e, axis=1, keepdims=True) + jnp.exp(sink - m))
            p = jnp.concatenate(es, axis=1).astype(BF16)
            o = _dot(p, vcat)
            o = o * jnp.where(lo64, 1.0 / ls[0], 1.0 / ls[1])
            attn_scr[rows, c * LANES:(c + 1) * LANES] = o.astype(BF16)

    u_scr[U_HALO:, :] = u
    po_ref[0] = u_scr[U_HALO + tm - POOL_STATE:U_HALO + tm, :]
    pos = t * tm + lax.broadcasted_iota(jnp.int32, (tm, 1), 0)

    def win_sum(g, w, ug):
        acc = ug
        for i in range(1, w):
            acc = acc + u_scr[U_HALO - i:U_HALO - i + tm, g * POOL_GC:(g + 1) * POOL_GC]
        return acc

    pooled = _pool_mix(u, win_sum, lambda w: jnp.minimum(w, pos + 1).astype(F32), gw_ref, pscale_ref)

    for scr in (kl_scr, kh_scr, vl_scr, vh_scr):
        scr[:, 0:BLOCK, :] = scr[:, tm:tm + BLOCK, :]
    u_scr[0:U_HALO, :] = u_scr[tm:tm + U_HALO, :]

    merged = jax.nn.sigmoid(_dot(xn, w_in_ref[:, C_GP:C_GA])) * _dot(pooled.astype(BF16), w_pb_ref[...])
    merged = merged + jax.nn.sigmoid(_dot(xn, w_in_ref[:, C_GA:C_END])) * _dot(attn_scr[...], w_ab_ref[...])
    h_ref[0] = x + _dot(merged.astype(BF16), w_out_ref[...])


def _const_spec(shape):
    nd = len(shape)
    return pl.BlockSpec(shape, lambda *_: (0,) * nd, pipeline_mode=pl.Buffered(1))


def _prompt_mixer(x, cos, sg, sinks, ln1, w_in, gw, pscale, w_pb, w_ab, w_out, tm):
    b, s, d = x.shape
    nt = s // tm
    grid_spec = pltpu.PrefetchScalarGridSpec(
        num_scalar_prefetch=1,
        grid=(b, nt),
        in_specs=[
            pl.BlockSpec((1, tm, d), lambda bi, ti, *_: (bi, ti, 0)),
            pl.BlockSpec((tm, LANES), lambda bi, ti, *_: (ti, 0)),
            pl.BlockSpec((tm, LANES), lambda bi, ti, *_: (ti, 0)),
            _const_spec((1, d)),
            _const_spec(w_in.shape),
            _const_spec(gw.shape),
            _const_spec((1, POOL_WIDTH)),
            _const_spec(w_pb.shape),
            _const_spec(w_ab.shape),
            _const_spec(w_out.shape),
        ],
        out_specs=[
            pl.BlockSpec((1, tm, d), lambda bi, ti, *_: (bi, ti, 0)),
            pl.BlockSpec((1, WINDOW, KV_W), lambda bi, ti, *_: (bi, 0, 0)),
            pl.BlockSpec((1, WINDOW, KV_W), lambda bi, ti, *_: (bi, 0, 0)),
            pl.BlockSpec((1, POOL_STATE, POOL_WIDTH), lambda bi, ti, *_: (bi, 0, 0)),
        ],
        scratch_shapes=[
            pltpu.VMEM((N_KV_HEADS, BLOCK + tm, LANES), BF16),
            pltpu.VMEM((N_KV_HEADS, BLOCK + tm, LANES), BF16),
            pltpu.VMEM((N_KV_HEADS, BLOCK + tm, LANES), BF16),
            pltpu.VMEM((N_KV_HEADS, BLOCK + tm, LANES), BF16),
            pltpu.VMEM((U_HALO + tm, POOL_WIDTH), F32),
            pltpu.VMEM((tm, Q_W), BF16),
        ],
    )
    return pl.pallas_call(
        functools.partial(_mixer_kernel, tm=tm),
        grid_spec=grid_spec,
        out_shape=[
            jax.ShapeDtypeStruct((b, s, d), F32),
            jax.ShapeDtypeStruct((b, WINDOW, KV_W), F32),
            jax.ShapeDtypeStruct((b, WINDOW, KV_W), F32),
            jax.ShapeDtypeStruct((b, POOL_STATE, POOL_WIDTH), F32),
        ],
        compiler_params=pltpu.CompilerParams(
            dimension_semantics=("arbitrary", "arbitrary"), vmem_limit_bytes=VMEM_LIMIT),
        name="prompt_mixer",
    )(sinks, x, cos, sg, ln1, w_in, gw, pscale, w_pb, w_ab, w_out)


FFN_CHUNKS = ((0, 1024), (1024, 2048), (2048, FFN_HIDDEN))


def _ffn_kernel(h_ref, p_ref, ln2_ref, w1_ref, w2_ref, w_pp_ref, pn_ref, w_pg_ref, fn_ref, y_ref):
    h = h_ref[...]
    hn = _rms(h, ln2_ref[...]).astype(BF16)
    acc = h
    for lo, hi in FFN_CHUNKS:
        gate = _dot(hn, w1_ref[:, lo:hi])
        up = _dot(hn, w1_ref[:, FFN_HIDDEN + lo:FFN_HIDDEN + hi])
        act = (gate * jax.nn.sigmoid(gate) * up).astype(BF16)
        acc = acc + _dot(act, w2_ref[lo:hi, :])
    e = _rms(_dot(p_ref[...].astype(BF16), w_pp_ref[...]), pn_ref[...])
    h3 = acc + jax.nn.sigmoid(_dot(acc.astype(BF16), w_pg_ref[...])) * e
    y_ref[...] = _rms(h3, fn_ref[...])


def _ffn(h, p, ln2, w1, w2, w_pp, pn, w_pg, fn, tm):
    n, d = h.shape
    return pl.pallas_call(
        _ffn_kernel,
        grid=(n // tm,),
        in_specs=[
            pl.BlockSpec((tm, d), lambda i: (i, 0)),
            pl.BlockSpec((tm, PLE_DIM), lambda i: (i, 0)),
            _const_spec((1, d)),
            _const_spec(w1.shape),
            _const_spec(w2.shape),
            _const_spec(w_pp.shape),
            _const_spec((1, d)),
            _const_spec(w_pg.shape),
            _const_spec((1, d)),
        ],
        out_specs=pl.BlockSpec((tm, d), lambda i: (i, 0)),
        out_shape=jax.ShapeDtypeStruct((n, d), F32),
        compiler_params=pltpu.CompilerParams(dimension_semantics=("arbitrary",), vmem_limit_bytes=VMEM_LIMIT),
        name="ffn_ple_norm",
    )(h, p, ln2, w1, w2, w_pp, pn, w_pg, fn)


def _sample_pre_kernel(x_ref, cos_ref, sg_ref, ln1_ref, w_in_ref, qe_ref, kn_ref, vn_ref, u_ref):
    lo8, _ = _lane_masks()
    xn = _rms(x_ref[...], ln1_ref[...]).astype(BF16)
    cos = cos_ref[...]
    sg = sg_ref[...]
    u_ref[...] = _dot(xn, w_in_ref[:, C_U:C_Q])
    q = (_rope(_dot(xn, w_in_ref[:, C_Q:C_K]), cos, sg, lo8) * (HEAD_DIM ** -0.5)).astype(BF16)
    kn_ref[...] = _rope(_dot(xn, w_in_ref[:, C_K:C_V]), cos, sg, lo8)
    vn_ref[...] = _dot(xn, w_in_ref[:, C_V:C_GP])
    ii = lax.broadcasted_iota(jnp.int32, (Q_W, KV_W), 0)
    jj = lax.broadcasted_iota(jnp.int32, (Q_W, KV_W), 1)
    for r in range(N_HEADS):
        kh = r // GROUP
        sel = ((ii - r * HEAD_DIM) == (jj - kh * HEAD_DIM)) & (jj >= kh * HEAD_DIM) & (jj < (kh + 1) * HEAD_DIM)
        qe_ref[:, r * KV_W:(r + 1) * KV_W] = _dot(q, jnp.where(sel, 1.0, 0.0).astype(BF16))


def _sample_pre(x, cos, sg, ln1, w_in):
    n, d = x.shape
    return pl.pallas_call(
        _sample_pre_kernel,
        grid=(1,),
        in_specs=[
            _const_spec((n, d)),
            _const_spec((1, LANES)),
            _const_spec((1, LANES)),
            _const_spec((1, d)),
            pl.BlockSpec((d, C_GP), lambda i: (0, 0), pipeline_mode=pl.Buffered(1)),
        ],
        out_specs=[
            pl.BlockSpec((n, N_HEADS * KV_W), lambda i: (0, 0)),
            pl.BlockSpec((n, KV_W), lambda i: (0, 0)),
            pl.BlockSpec((n, KV_W), lambda i: (0, 0)),
            pl.BlockSpec((n, POOL_WIDTH), lambda i: (0, 0)),
        ],
        out_shape=[
            jax.ShapeDtypeStruct((n, N_HEADS * KV_W), F32),
            jax.ShapeDtypeStruct((n, KV_W), F32),
            jax.ShapeDtypeStruct((n, KV_W), F32),
            jax.ShapeDtypeStruct((n, POOL_WIDTH), F32),
        ],
        compiler_params=pltpu.CompilerParams(dimension_semantics=("arbitrary",), vmem_limit_bytes=VMEM_LIMIT),
        name="sample_pre",
    )(x, cos, sg, ln1, w_in)


def _sample_attn_kernel(qe_ref, kn_ref, vn_ref, un_ref, ck_ref, cv_ref, st_ref, sink_ref,
                        o_ref, nk_ref, nv_ref, np_ref, ps_ref):
    w_cache = ck_ref.shape[1]
    nk_ref[:, 0:w_cache - 1, :] = ck_ref[:, 1:w_cache, :]
    nk_ref[:, w_cache - 1:w_cache, :] = kn_ref[...]
    nv_ref[:, 0:w_cache - 1, :] = cv_ref[:, 1:w_cache, :]
    nv_ref[:, w_cache - 1:w_cache, :] = vn_ref[...]
    np_ref[:, 0:POOL_STATE - 1, :] = st_ref[:, 1:POOL_STATE, :]
    np_ref[:, POOL_STATE - 1:POOL_STATE, :] = un_ref[...]
    for g, w in enumerate(POOL_WINDOWS):
        cols = slice(g * POOL_GC, (g + 1) * POOL_GC)
        ps_ref[:, :, cols] = jnp.sum(st_ref[:, POOL_STATE - (w - 1):POOL_STATE, cols], axis=1, keepdims=True)

    kk = nk_ref[...].astype(BF16)
    vv = nv_ref[...].astype(BF16)
    s = jnp.einsum('brd,bsd->brs', qe_ref[...].astype(BF16), kk, preferred_element_type=F32)
    sink = sink_ref[...][None]
    m = jnp.maximum(jnp.max(s, axis=-1, keepdims=True), sink)
    e = jnp.exp(s - m)
    l = jnp.sum(e, axis=-1, keepdims=True) + jnp.exp(sink - m)
    o = jnp.einsum('brs,bsd->brd', e.astype(BF16), vv, preferred_element_type=F32)
    o_ref[...] = o / l


def _sample_attn(qe, kn, vn, un, ck, cv, st, sink, bb):
    n, w_cache, _ = ck.shape
    blk = lambda *shape: pl.BlockSpec((bb,) + shape, lambda i: (i,) + (0,) * len(shape))
    return pl.pallas_call(
        _sample_attn_kernel,
        grid=(n // bb,),
        in_specs=[
            blk(N_HEADS, KV_W), blk(1, KV_W), blk(1, KV_W), blk(1, POOL_WIDTH),
            blk(w_cache, KV_W), blk(w_cache, KV_W), blk(POOL_STATE, POOL_WIDTH),
            _const_spec((N_HEADS, 1)),
        ],
        out_specs=[
            blk(N_HEADS, KV_W), blk(w_cache, KV_W), blk(w_cache, KV_W), blk(POOL_STATE, POOL_WIDTH),
            blk(1, POOL_WIDTH),
        ],
        out_shape=[
            jax.ShapeDtypeStruct((n, N_HEADS, KV_W), F32),
            jax.ShapeDtypeStruct((n, w_cache, KV_W), F32),
            jax.ShapeDtypeStruct((n, w_cache, KV_W), F32),
            jax.ShapeDtypeStruct((n, POOL_STATE, POOL_WIDTH), F32),
            jax.ShapeDtypeStruct((n, 1, POOL_WIDTH), F32),
        ],
        compiler_params=pltpu.CompilerParams(dimension_semantics=("arbitrary",), vmem_limit_bytes=VMEM_LIMIT),
        name="sample_attn",
    )(qe, kn, vn, un, ck, cv, st, sink)


def _sample_post_kernel(x_ref, u_ref, ps_ref, o_ref, ln1_ref, w_g_ref, gw_ref, pscale_ref, w_pb_ref, w_ab_ref,
                        w_out_ref, h_ref):
    x = x_ref[...]
    n = x.shape[0]
    xn = _rms(x, ln1_ref[...]).astype(BF16)
    u = u_ref[...]
    ps = ps_ref[...]
    pooled = _pool_mix(
        u, lambda g, w, ug: ug + ps[:, g * POOL_GC:(g + 1) * POOL_GC],
        lambda w: jnp.float32(min(w, PAST_LEN + 1)), gw_ref, pscale_ref)
    merged = jax.nn.sigmoid(_dot(xn, w_g_ref[:, 0:D_MODEL])) * _dot(pooled.astype(BF16), w_pb_ref[...])
    ab = jnp.zeros((n, D_MODEL), F32)
    for r in range(N_HEADS):
        kh = r // GROUP
        parts = []
        if kh > 0:
            parts.append(jnp.zeros((kh * HEAD_DIM, D_MODEL), BF16))
        parts.append(w_ab_ref[r * HEAD_DIM:(r + 1) * HEAD_DIM, :])
        if kh < N_KV_HEADS - 1:
            parts.append(jnp.zeros(((N_KV_HEADS - 1 - kh) * HEAD_DIM, D_MODEL), BF16))
        ab = ab + _dot(o_ref[:, r * KV_W:(r + 1) * KV_W].astype(BF16), jnp.concatenate(parts, axis=0))
    merged = merged + jax.nn.sigmoid(_dot(xn, w_g_ref[:, D_MODEL:2 * D_MODEL])) * ab
    h_ref[...] = x + _dot(merged.astype(BF16), w_out_ref[...])


def _sample_post(x, u, ps, o, ln1, w_in, gw, pscale, w_pb, w_ab, w_out):
    n, d = x.shape
    return pl.pallas_call(
        _sample_post_kernel,
        grid=(1,),
        in_specs=[
            _const_spec((n, d)),
            _const_spec((n, POOL_WIDTH)),
            _const_spec((n, POOL_WIDTH)),
            _const_spec((n, N_HEADS * KV_W)),
            _const_spec((1, d)),
            pl.BlockSpec((d, 2 * D_MODEL), lambda i: (0, 1), pipeline_mode=pl.Buffered(1)),
            _const_spec(gw.shape),
            _const_spec((1, POOL_WIDTH)),
            _const_spec(w_pb.shape),
            _const_spec(w_ab.shape),
            _const_spec(w_out.shape),
        ],
        out_specs=pl.BlockSpec((n, d), lambda i: (0, 0)),
        out_shape=jax.ShapeDtypeStruct((n, d), F32),
        compiler_params=pltpu.CompilerParams(dimension_semantics=("arbitrary",), vmem_limit_bytes=VMEM_LIMIT),
        name="sample_post",
    )(x, u, ps, o, ln1, w_in, gw, pscale, w_pb, w_ab, w_out)


def _rope_tables(pos):
    half = ROT_DIMS // 2
    inv = ROPE_THETA ** (-(jnp.arange(0, ROT_DIMS, 2, dtype=F32) / ROT_DIMS))
    ang = pos.astype(F32)[:, None] * inv[None, :]
    cos, sin = jnp.cos(ang), jnp.sin(ang)
    t = pos.shape[0]
    rest = HEAD_DIM - 2 * half
    c64 = jnp.concatenate([cos, cos, jnp.ones((t, rest), F32)], axis=1)
    s64 = jnp.concatenate([-sin, sin, jnp.zeros((t, rest), F32)], axis=1)
    reps = LANES // HEAD_DIM
    return jnp.tile(c64, (1, reps)), jnp.tile(s64, (1, reps))


def kernel(x_prompt, x_sample, p_prompt, p_sample, cache_k, cache_v, state_pool, ln1, w_in, pool_group_w, pool_scale,
           attn_sinks, w_pool_branch, w_attn_branch, w_out, ln2, w_ffn_in, w_ffn_out, w_ple_proj, ple_norm,
           w_ple_gate, final_norm):
    depth = ln1.shape[0]
    b, s, d = x_prompt.shape
    bd, t_dec, _ = x_sample.shape
    w_cache = cache_k.shape[2]
    assert depth == 1 and t_dec == 1 and w_cache == WINDOW and s % BLOCK == 0 and d == D_MODEL
    tm = 512 if s % 512 == 0 else BLOCK

    cos_p, sg_p = _rope_tables(jnp.arange(s, dtype=jnp.int32))
    cos_s, sg_s = _rope_tables(PAST_LEN + jnp.arange(t_dec, dtype=jnp.int32))

    hp = x_prompt
    hs = x_sample.reshape(bd, d)
    row = lambda a: a.reshape(1, -1)
    nkp, nvp, npp, nks, nvs, nps = [], [], [], [], [], []
    for i in range(depth):
        wi = w_in[i].astype(BF16)
        gw = pool_group_w[i].astype(BF16)
        wpb = w_pool_branch[i].astype(BF16)
        wab = w_attn_branch[i].astype(BF16)
        wo = w_out[i].astype(BF16)
        w1 = w_ffn_in[i].astype(BF16)
        w2 = w_ffn_out[i].astype(BF16)
        wpp = w_ple_proj[i].astype(BF16)
        wpg = w_ple_gate[i].astype(BF16)
        ffn_args = (row(ln2[i]), w1, w2, wpp, row(ple_norm[i]), wpg)
        h1, kp, vp, pp = _prompt_mixer(hp, cos_p, sg_p, attn_sinks[i], row(ln1[i]), wi, gw, row(pool_scale[i]),
                                       wpb, wab, wo, tm)
        hp = _ffn(h1.reshape(b * s, d), p_prompt[i].reshape(b * s, PLE_DIM), *ffn_args, row(final_norm),
                  tm).reshape(b, s, d)
        nkp.append(kp.reshape(b, w_cache, N_KV_HEADS, HEAD_DIM))
        nvp.append(vp.reshape(b, w_cache, N_KV_HEADS, HEAD_DIM))
        npp.append(pp)

        qe, kn, vn, un = _sample_pre(hs, cos_s, sg_s, row(ln1[i]), wi)
        o, nk, nv, npool, psum = _sample_attn(
            qe.reshape(bd, N_HEADS, KV_W), kn.reshape(bd, 1, KV_W), vn.reshape(bd, 1, KV_W),
            un.reshape(bd, 1, POOL_WIDTH), cache_k[i].reshape(bd, w_cache, KV_W), cache_v[i].reshape(bd, w_cache, KV_W),
            state_pool[i], attn_sinks[i].reshape(N_HEADS, 1), 8)
        h1s = _sample_post(hs, un, psum.reshape(bd, POOL_WIDTH), o.reshape(bd, N_HEADS * KV_W), row(ln1[i]), wi, gw,
                           row(pool_scale[i]), wpb, wab, wo)
        hs = _ffn(h1s, p_sample[i].reshape(bd * t_dec, PLE_DIM), *ffn_args, row(final_norm), bd)
        nks.append(nk.reshape(bd, w_cache, N_KV_HEADS, HEAD_DIM))
        nvs.append(nv.reshape(bd, w_cache, N_KV_HEADS, HEAD_DIM))
        nps.append(npool)

    return (hp, hs.reshape(bd, t_dec, d), jnp.stack(nkp), jnp.stack(nvp), jnp.stack(npp),
            jnp.stack(nks), jnp.stack(nvs), jnp.stack(nps))
```

```python
import functools

import jax
import jax.numpy as jnp
from jax import lax
from jax.experimental import pallas as pl
from jax.experimental.pallas import tpu as pltpu

D_MODEL = 1024
HEAD_DIM = 64
N_HEADS = D_MODEL // HEAD_DIM
N_KV_HEADS = N_HEADS // 4
GROUP = N_HEADS // N_KV_HEADS
ROT_DIMS = HEAD_DIM // 4
ROPE_THETA = 500000.0
WINDOW = 128
BLOCK = 128
POOL_WIDTH = D_MODEL // 2
POOL_WINDOWS = (2, 4, 8, 16)
POOL_GC = POOL_WIDTH // len(POOL_WINDOWS)
POOL_STATE = max(POOL_WINDOWS) - 1
FFN_HIDDEN = -(-8 * D_MODEL // (3 * 256)) * 256
PLE_DIM = 256
EPS = 1e-6
NEG_INF = -1e30
PAST_LEN = 16384

Q_W = N_HEADS * HEAD_DIM
KV_W = N_KV_HEADS * HEAD_DIM
C_U, C_Q, C_K, C_V, C_GP, C_GA, C_END = 0, POOL_WIDTH, POOL_WIDTH + Q_W, POOL_WIDTH + Q_W + KV_W, \
    POOL_WIDTH + Q_W + 2 * KV_W, POOL_WIDTH + Q_W + 2 * KV_W + D_MODEL, POOL_WIDTH + Q_W + 2 * KV_W + 2 * D_MODEL

LANES = 128
S_AHEAD = 16
S_SLOTS = 16
U_HALO = 16
VMEM_LIMIT = 56 * 1024 * 1024

BF16 = jnp.bfloat16
F32 = jnp.float32


def _dot(a, b):
    return jnp.dot(a, b, preferred_element_type=F32)


def _dot_nt(a, b):
    return lax.dot_general(a, b, (((1,), (1,)), ((), ())), preferred_element_type=F32)


def _rms(x, g):
    y = x * lax.rsqrt(jnp.mean(x * x, axis=-1, keepdims=True) + EPS)
    return y * g


def _rope(x, cos, sg, lo8):
    outs = []
    for c in range(x.shape[1] // LANES):
        xc = x[:, c * LANES:(c + 1) * LANES]
        partner = jnp.where(lo8, pltpu.roll(xc, LANES - ROT_DIMS // 2, 1), pltpu.roll(xc, ROT_DIMS // 2, 1))
        outs.append(xc * cos + partner * sg)
    return jnp.concatenate(outs, axis=1)


def _lane_masks():
    lane = lax.broadcasted_iota(jnp.int32, (1, LANES), 1)
    lo8 = (lane % HEAD_DIM) < (ROT_DIMS // 2)
    lo64 = lane < HEAD_DIM
    return lo8, lo64


def _pool_mix(u, win_sum_fn, cnt_fn, gw_ref, pscale_ref):
    mixed = []
    for g, w in enumerate(POOL_WINDOWS):
        cols = slice(g * POOL_GC, (g + 1) * POOL_GC)
        ug = u[:, cols]
        m = win_sum_fn(g, w, ug) / cnt_fn(w) - ug
        mixed.append(_dot(m.astype(BF16), gw_ref[g]) * pscale_ref[:, cols])
    return jnp.concatenate(mixed, axis=1)


def _mixer_kernel(sinks_ref, x_ref, cos_ref, sg_ref, ln1_ref, w_in_ref, gw_ref, pscale_ref, w_pb_ref, w_ab_ref,
                  w_out_ref, h_ref, ko_ref, vo_ref, po_ref, kl_scr, kh_scr, vl_scr, vh_scr, u_scr, attn_scr, q_scr, s_scr,
                  *, tm):
    t = pl.program_id(1)
    lo8, lo64 = _lane_masks()

    @pl.when(t == 0)
    def _():
        for scr in (kl_scr, kh_scr, vl_scr, vh_scr):
            scr[:, 0:BLOCK, :] = jnp.zeros((N_KV_HEADS, BLOCK, LANES), BF16)
        u_scr[0:U_HALO, :] = jnp.zeros((U_HALO, POOL_WIDTH), F32)

    x = x_ref[0]
    xn = _rms(x, ln1_ref[...]).astype(BF16)
    cos = cos_ref[...]
    sg = sg_ref[...]

    u = _dot(xn, w_in_ref[:, C_U:C_Q])
    q = (_rope(_dot(xn, w_in_ref[:, C_Q:C_K]), cos, sg, lo8) * (HEAD_DIM ** -0.5)).astype(BF16)
    k = _rope(_dot(xn, w_in_ref[:, C_K:C_V]), cos, sg, lo8)
    v = _dot(xn, w_in_ref[:, C_V:C_GP])
    ko_ref[0] = k[tm - WINDOW:, :]
    vo_ref[0] = v[tm - WINDOW:, :]

    zero = jnp.zeros((tm, LANES), F32)
    for src, lo_scr, hi_scr in ((k, kl_scr, kh_scr), (v, vl_scr, vh_scr)):
        for p in range(KV_W // LANES):
            xp = src[:, p * LANES:(p + 1) * LANES]
            xs = pltpu.roll(xp, HEAD_DIM, 1)
            lo_scr[2 * p, BLOCK:, :] = jnp.where(lo64, xp, zero).astype(BF16)
            hi_scr[2 * p, BLOCK:, :] = jnp.where(lo64, zero, xs).astype(BF16)
            lo_scr[2 * p + 1, BLOCK:, :] = jnp.where(lo64, xs, zero).astype(BF16)
            hi_scr[2 * p + 1, BLOCK:, :] = jnp.where(lo64, zero, xp).astype(BF16)

    qi = lax.broadcasted_iota(jnp.int32, (BLOCK, 2 * BLOCK), 0)
    si = lax.broadcasted_iota(jnp.int32, (BLOCK, 2 * BLOCK), 1)
    band = (si > qi) & (si <= qi + WINDOW)
    bias_band = jnp.where(band, 0.0, NEG_INF).astype(F32)
    bias_first = jnp.where(band & (si >= BLOCK), 0.0, NEG_INF).astype(F32)

    q_scr[...] = q
    bias0 = jnp.where(t == 0, bias_first, bias_band)
    ones_lo = jnp.broadcast_to(jnp.where(lo64, 1.0, 0.0).astype(BF16), (2 * BLOCK, LANES))
    ones_hi = jnp.broadcast_to(jnp.where(lo64, 0.0, 1.0).astype(BF16), (2 * BLOCK, LANES))

    units = [(j, kh) for j in range(tm // BLOCK) for kh in range(N_KV_HEADS)]

    def scores(i):
        j, kh = units[i]
        rows = slice(j * BLOCK, (j + 1) * BLOCK)
        win = slice(j * BLOCK, (j + 2) * BLOCK)
        qq = jnp.concatenate([q_scr[rows, (2 * kh + a) * LANES:(2 * kh + a + 1) * LANES] for a in range(2)], axis=0)
        kcat = jnp.concatenate([kl_scr[kh, win, :], kh_scr[kh, win, :]], axis=0)
        s = _dot_nt(qq, kcat)
        bias = bias0 if j == 0 else bias_band
        for a in range(2):
            for half in range(2):
                blk = (slice(a * BLOCK, (a + 1) * BLOCK), slice(half * 2 * BLOCK, (half + 1) * 2 * BLOCK))
                s_scr[(i % S_SLOTS,) + blk] = s[blk] + bias

    for i in range(S_AHEAD):
        scores(i)
    for i, (j, kh) in enumerate(units):
        if i + S_AHEAD < len(units):
            scores(i + S_AHEAD)
        rows = slice(j * BLOCK, (j + 1) * BLOCK)
        win = slice(j * BLOCK, (j + 2) * BLOCK)
        vcat = jnp.concatenate([
            jnp.concatenate([vl_scr[kh, win, :], ones_lo], axis=1),
            jnp.concatenate([vh_scr[kh, win, :], ones_hi], axis=1)], axis=0)
        ps, sink_terms = [], []
        for a in range(2):
            es, st = [], []
            for half in range(2):
                sh = s_scr[i % S_SLOTS, a * BLOCK:(a + 1) * BLOCK, half * 2 * BLOCK:(half + 1) * 2 * BLOCK]
                sink = sinks_ref[4 * kh + 2 * a + half]
                m = jnp.maximum(jnp.max(sh, axis=1, keepdims=True), sink)
                es.append(jnp.exp(sh - m).astype(BF16))
                st.append(jnp.exp(sink - m))
            ps.append(jnp.concatenate(es, axis=1))
            sink_terms.append(jnp.where(lo64, st[0], st[1]))
        o = _dot(jnp.concatenate(ps, axis=0), vcat)
        for a in range(2):
            oa = o[a * BLOCK:(a + 1) * BLOCK]
            attn_scr[rows, (2 * kh + a) * LANES:(2 * kh + a + 1) * LANES] = (
                oa[:, :LANES] / (oa[:, LANES:] + sink_terms[a])).astype(BF16)

    u_scr[U_HALO:, :] = u
    po_ref[0] = u_scr[U_HALO + tm - POOL_STATE:U_HALO + tm, :]
    pos = t * tm + lax.broadcasted_iota(jnp.int32, (tm, 1), 0)

    def win_sum(g, w, ug):
        acc = ug
        for i in range(1, w):
            acc = acc + u_scr[U_HALO - i:U_HALO - i + tm, g * POOL_GC:(g + 1) * POOL_GC]
        return acc

    pooled = _pool_mix(u, win_sum, lambda w: jnp.minimum(w, pos + 1).astype(F32), gw_ref, pscale_ref)

    for scr in (kl_scr, kh_scr, vl_scr, vh_scr):
        scr[:, 0:BLOCK, :] = scr[:, tm:tm + BLOCK, :]
    u_scr[0:U_HALO, :] = u_scr[tm:tm + U_HALO, :]

    merged = jax.nn.sigmoid(_dot(xn, w_in_ref[:, C_GP:C_GA])) * _dot(pooled.astype(BF16), w_pb_ref[...])
    merged = merged + jax.nn.sigmoid(_dot(xn, w_in_ref[:, C_GA:C_END])) * _dot(attn_scr[...], w_ab_ref[...])
    h_ref[0] = x + _dot(merged.astype(BF16), w_out_ref[...])


def _const_spec(shape):
    nd = len(shape)
    return pl.BlockSpec(shape, lambda *_: (0,) * nd, pipeline_mode=pl.Buffered(1))


def _prompt_mixer(x, cos, sg, sinks, ln1, w_in, gw, pscale, w_pb, w_ab, w_out, tm):
    b, s, d = x.shape
    nt = s // tm
    grid_spec = pltpu.PrefetchScalarGridSpec(
        num_scalar_prefetch=1,
        grid=(b, nt),
        in_specs=[
            pl.BlockSpec((1, tm, d), lambda bi, ti, *_: (bi, ti, 0)),
            pl.BlockSpec((tm, LANES), lambda bi, ti, *_: (ti, 0)),
            pl.BlockSpec((tm, LANES), lambda bi, ti, *_: (ti, 0)),
            _const_spec((1, d)),
            _const_spec(w_in.shape),
            _const_spec(gw.shape),
            _const_spec((1, POOL_WIDTH)),
            _const_spec(w_pb.shape),
            _const_spec(w_ab.shape),
            _const_spec(w_out.shape),
        ],
        out_specs=[
            pl.BlockSpec((1, tm, d), lambda bi, ti, *_: (bi, ti, 0)),
            pl.BlockSpec((1, WINDOW, KV_W), lambda bi, ti, *_: (bi, 0, 0)),
            pl.BlockSpec((1, WINDOW, KV_W), lambda bi, ti, *_: (bi, 0, 0)),
            pl.BlockSpec((1, POOL_STATE, POOL_WIDTH), lambda bi, ti, *_: (bi, 0, 0)),
        ],
        scratch_shapes=[
            pltpu.VMEM((N_KV_HEADS, BLOCK + tm, LANES), BF16),
            pltpu.VMEM((N_KV_HEADS, BLOCK + tm, LANES), BF16),
            pltpu.VMEM((N_KV_HEADS, BLOCK + tm, LANES), BF16),
            pltpu.VMEM((N_KV_HEADS, BLOCK + tm, LANES), BF16),
            pltpu.VMEM((U_HALO + tm, POOL_WIDTH), F32),
            pltpu.VMEM((tm, Q_W), BF16),
            pltpu.VMEM((tm, Q_W), BF16),
            pltpu.VMEM((S_SLOTS, 2 * BLOCK, 4 * BLOCK), F32),
        ],
    )
    return pl.pallas_call(
        functools.partial(_mixer_kernel, tm=tm),
        grid_spec=grid_spec,
        out_shape=[
            jax.ShapeDtypeStruct((b, s, d), F32),
            jax.ShapeDtypeStruct((b, WINDOW, KV_W), F32),
            jax.ShapeDtypeStruct((b, WINDOW, KV_W), F32),
            jax.ShapeDtypeStruct((b, POOL_STATE, POOL_WIDTH), F32),
        ],
        compiler_params=pltpu.CompilerParams(
            dimension_semantics=("arbitrary", "arbitrary"), vmem_limit_bytes=VMEM_LIMIT),
        name="prompt_mixer",
    )(sinks, x, cos, sg, ln1, w_in, gw, pscale, w_pb, w_ab, w_out)


FFN_CHUNKS = ((0, 1024), (1024, 2048), (2048, FFN_HIDDEN))


def _ffn_kernel(h_ref, p_ref, ln2_ref, w1_ref, w2_ref, w_pp_ref, pn_ref, w_pg_ref, fn_ref, y_ref):
    h = h_ref[...]
    hn = _rms(h, ln2_ref[...]).astype(BF16)
    acc = h
    for lo, hi in FFN_CHUNKS:
        gate = _dot(hn, w1_ref[:, lo:hi])
        up = _dot(hn, w1_ref[:, FFN_HIDDEN + lo:FFN_HIDDEN + hi])
        act = (gate * jax.nn.sigmoid(gate) * up).astype(BF16)
        acc = acc + _dot(act, w2_ref[lo:hi, :])
    e = _rms(_dot(p_ref[...].astype(BF16), w_pp_ref[...]), pn_ref[...])
    h3 = acc + jax.nn.sigmoid(_dot(acc.astype(BF16), w_pg_ref[...])) * e
    y_ref[...] = _rms(h3, fn_ref[...])


def _ffn(h, p, ln2, w1, w2, w_pp, pn, w_pg, fn, tm):
    n, d = h.shape
    return pl.pallas_call(
        _ffn_kernel,
        grid=(n // tm,),
        in_specs=[
            pl.BlockSpec((tm, d), lambda i: (i, 0)),
            pl.BlockSpec((tm, PLE_DIM), lambda i: (i, 0)),
            _const_spec((1, d)),
            _const_spec(w1.shape),
            _const_spec(w2.shape),
            _const_spec(w_pp.shape),
            _const_spec((1, d)),
            _const_spec(w_pg.shape),
            _const_spec((1, d)),
        ],
        out_specs=pl.BlockSpec((tm, d), lambda i: (i, 0)),
        out_shape=jax.ShapeDtypeStruct((n, d), F32),
        compiler_params=pltpu.CompilerParams(dimension_semantics=("arbitrary",), vmem_limit_bytes=VMEM_LIMIT),
        name="ffn_ple_norm",
    )(h, p, ln2, w1, w2, w_pp, pn, w_pg, fn)


def _sample_pre_kernel(x_ref, cos_ref, sg_ref, ln1_ref, w_in_ref, qe_ref, kn_ref, vn_ref, u_ref):
    lo8, _ = _lane_masks()
    xn = _rms(x_ref[...], ln1_ref[...]).astype(BF16)
    cos = cos_ref[...]
    sg = sg_ref[...]
    u_ref[...] = _dot(xn, w_in_ref[:, C_U:C_Q])
    q = (_rope(_dot(xn, w_in_ref[:, C_Q:C_K]), cos, sg, lo8) * (HEAD_DIM ** -0.5)).astype(BF16)
    kn_ref[...] = _rope(_dot(xn, w_in_ref[:, C_K:C_V]), cos, sg, lo8)
    vn_ref[...] = _dot(xn, w_in_ref[:, C_V:C_GP])
    ii = lax.broadcasted_iota(jnp.int32, (Q_W, KV_W), 0)
    jj = lax.broadcasted_iota(jnp.int32, (Q_W, KV_W), 1)
    for r in range(N_HEADS):
        kh = r // GROUP
        sel = ((ii - r * HEAD_DIM) == (jj - kh * HEAD_DIM)) & (jj >= kh * HEAD_DIM) & (jj < (kh + 1) * HEAD_DIM)
        qe_ref[:, r * KV_W:(r + 1) * KV_W] = _dot(q, jnp.where(sel, 1.0, 0.0).astype(BF16))


def _sample_pre(x, cos, sg, ln1, w_in):
    n, d = x.shape
    return pl.pallas_call(
        _sample_pre_kernel,
        grid=(1,),
        in_specs=[
            _const_spec((n, d)),
            _const_spec((1, LANES)),
            _const_spec((1, LANES)),
            _const_spec((1, d)),
            pl.BlockSpec((d, C_GP), lambda i: (0, 0), pipeline_mode=pl.Buffered(1)),
        ],
        out_specs=[
            pl.BlockSpec((n, N_HEADS * KV_W), lambda i: (0, 0)),
            pl.BlockSpec((n, KV_W), lambda i: (0, 0)),
            pl.BlockSpec((n, KV_W), lambda i: (0, 0)),
            pl.BlockSpec((n, POOL_WIDTH), lambda i: (0, 0)),
        ],
        out_shape=[
            jax.ShapeDtypeStruct((n, N_HEADS * KV_W), F32),
            jax.ShapeDtypeStruct((n, KV_W), F32),
            jax.ShapeDtypeStruct((n, KV_W), F32),
            jax.ShapeDtypeStruct((n, POOL_WIDTH), F32),
        ],
        compiler_params=pltpu.CompilerParams(dimension_semantics=("arbitrary",), vmem_limit_bytes=VMEM_LIMIT),
        name="sample_pre",
    )(x, cos, sg, ln1, w_in)


def _sample_attn_kernel(qe_ref, kn_ref, vn_ref, un_ref, ck_ref, cv_ref, st_ref, sink_ref,
                        o_ref, nk_ref, nv_ref, np_ref, ps_ref):
    w_cache = ck_ref.shape[1]
    nk_ref[:, 0:w_cache - 1, :] = ck_ref[:, 1:w_cache, :]
    nk_ref[:, w_cache - 1:w_cache, :] = kn_ref[...]
    nv_ref[:, 0:w_cache - 1, :] = cv_ref[:, 1:w_cache, :]
    nv_ref[:, w_cache - 1:w_cache, :] = vn_ref[...]
    np_ref[:, 0:POOL_STATE - 1, :] = st_ref[:, 1:POOL_STATE, :]
    np_ref[:, POOL_STATE - 1:POOL_STATE, :] = un_ref[...]
    for g, w in enumerate(POOL_WINDOWS):
        cols = slice(g * POOL_GC, (g + 1) * POOL_GC)
        ps_ref[:, :, cols] = jnp.sum(st_ref[:, POOL_STATE - (w - 1):POOL_STATE, cols], axis=1, keepdims=True)

    kk = nk_ref[...].astype(BF16)
    vv = nv_ref[...].astype(BF16)
    s = jnp.einsum('brd,bsd->brs', qe_ref[...].astype(BF16), kk, preferred_element_type=F32)
    sink = sink_ref[...][None]
    m = jnp.maximum(jnp.max(s, axis=-1, keepdims=True), sink)
    e = jnp.exp(s - m)
    l = jnp.sum(e, axis=-1, keepdims=True) + jnp.exp(sink - m)
    o = jnp.einsum('brs,bsd->brd', e.astype(BF16), vv, preferred_element_type=F32)
    o_ref[...] = o / l


def _sample_attn(qe, kn, vn, un, ck, cv, st, sink, bb):
    n, w_cache, _ = ck.shape
    blk = lambda *shape: pl.BlockSpec((bb,) + shape, lambda i: (i,) + (0,) * len(shape))
    return pl.pallas_call(
        _sample_attn_kernel,
        grid=(n // bb,),
        in_specs=[
            blk(N_HEADS, KV_W), blk(1, KV_W), blk(1, KV_W), blk(1, POOL_WIDTH),
            blk(w_cache, KV_W), blk(w_cache, KV_W), blk(POOL_STATE, POOL_WIDTH),
            _const_spec((N_HEADS, 1)),
        ],
        out_specs=[
            blk(N_HEADS, KV_W), blk(w_cache, KV_W), blk(w_cache, KV_W), blk(POOL_STATE, POOL_WIDTH),
            blk(1, POOL_WIDTH),
        ],
        out_shape=[
            jax.ShapeDtypeStruct((n, N_HEADS, KV_W), F32),
            jax.ShapeDtypeStruct((n, w_cache, KV_W), F32),
            jax.ShapeDtypeStruct((n, w_cache, KV_W), F32),
            jax.ShapeDtypeStruct((n, POOL_STATE, POOL_WIDTH), F32),
            jax.ShapeDtypeStruct((n, 1, POOL_WIDTH), F32),
        ],
        compiler_params=pltpu.CompilerParams(dimension_semantics=("arbitrary",), vmem_limit_bytes=VMEM_LIMIT),
        name="sample_attn",
    )(qe, kn, vn, un, ck, cv, st, sink)


def _sample_post_kernel(x_ref, u_ref, ps_ref, o_ref, ln1_ref, w_g_ref, gw_ref, pscale_ref, w_pb_ref, w_ab_ref,
                        w_out_ref, h_ref):
    x = x_ref[...]
    n = x.shape[0]
    xn = _rms(x, ln1_ref[...]).astype(BF16)
    u = u_ref[...]
    ps = ps_ref[...]
    pooled = _pool_mix(
        u, lambda g, w, ug: ug + ps[:, g * POOL_GC:(g + 1) * POOL_GC],
        lambda w: jnp.float32(min(w, PAST_LEN + 1)), gw_ref, pscale_ref)
    merged = jax.nn.sigmoid(_dot(xn, w_g_ref[:, 0:D_MODEL])) * _dot(pooled.astype(BF16), w_pb_ref[...])
    ab = jnp.zeros((n, D_MODEL), F32)
    for r in range(N_HEADS):
        kh = r // GROUP
        parts = []
        if kh > 0:
            parts.append(jnp.zeros((kh * HEAD_DIM, D_MODEL), BF16))
        parts.append(w_ab_ref[r * HEAD_DIM:(r + 1) * HEAD_DIM, :])
        if kh < N_KV_HEADS - 1:
            parts.append(jnp.zeros(((N_KV_HEADS - 1 - kh) * HEAD_DIM, D_MODEL), BF16))
        ab = ab + _dot(o_ref[:, r * KV_W:(r + 1) * KV_W].astype(BF16), jnp.concatenate(parts, axis=0))
    merged = merged + jax.nn.sigmoid(_dot(xn, w_g_ref[:, D_MODEL:2 * D_MODEL])) * ab
    h_ref[...] = x + _dot(merged.astype(BF16), w_out_ref[...])


def _sample_post(x, u, ps, o, ln1, w_in, gw, pscale, w_pb, w_ab, w_out):
    n, d = x.shape
    return pl.pallas_call(
        _sample_post_kernel,
        grid=(1,),
        in_specs=[
            _const_spec((n, d)),
            _const_spec((n, POOL_WIDTH)),
            _const_spec((n, POOL_WIDTH)),
            _const_spec((n, N_HEADS * KV_W)),
            _const_spec((1, d)),
            pl.BlockSpec((d, 2 * D_MODEL), lambda i: (0, 1), pipeline_mode=pl.Buffered(1)),
            _const_spec(gw.shape),
            _const_spec((1, POOL_WIDTH)),
            _const_spec(w_pb.shape),
            _const_spec(w_ab.shape),
            _const_spec(w_out.shape),
        ],
        out_specs=pl.BlockSpec((n, d), lambda i: (0, 0)),
        out_shape=jax.ShapeDtypeStruct((n, d), F32),
        compiler_params=pltpu.CompilerParams(dimension_semantics=("arbitrary",), vmem_limit_bytes=VMEM_LIMIT),
        name="sample_post",
    )(x, u, ps, o, ln1, w_in, gw, pscale, w_pb, w_ab, w_out)


def _rope_tables(pos):
    half = ROT_DIMS // 2
    inv = ROPE_THETA ** (-(jnp.arange(0, ROT_DIMS, 2, dtype=F32) / ROT_DIMS))
    ang = pos.astype(F32)[:, None] * inv[None, :]
    cos, sin = jnp.cos(ang), jnp.sin(ang)
    t = pos.shape[0]
    rest = HEAD_DIM - 2 * half
    c64 = jnp.concatenate([cos, cos, jnp.ones((t, rest), F32)], axis=1)
    s64 = jnp.concatenate([-sin, sin, jnp.zeros((t, rest), F32)], axis=1)
    reps = LANES // HEAD_DIM
    return jnp.tile(c64, (1, reps)), jnp.tile(s64, (1, reps))


def kernel(x_prompt, x_sample, p_prompt, p_sample, cache_k, cache_v, state_pool, ln1, w_in, pool_group_w, pool_scale,
           attn_sinks, w_pool_branch, w_attn_branch, w_out, ln2, w_ffn_in, w_ffn_out, w_ple_proj, ple_norm,
           w_ple_gate, final_norm):
    depth = ln1.shape[0]
    b, s, d = x_prompt.shape
    bd, t_dec, _ = x_sample.shape
    w_cache = cache_k.shape[2]
    assert depth == 1 and t_dec == 1 and w_cache == WINDOW and s % BLOCK == 0 and d == D_MODEL
    tm = 512 if s % 512 == 0 else BLOCK

    cos_p, sg_p = _rope_tables(jnp.arange(s, dtype=jnp.int32))
    cos_s, sg_s = _rope_tables(PAST_LEN + jnp.arange(t_dec, dtype=jnp.int32))

    hp = x_prompt
    hs = x_sample.reshape(bd, d)
    row = lambda a: a.reshape(1, -1)
    nkp, nvp, npp, nks, nvs, nps = [], [], [], [], [], []
    for i in range(depth):
        wi = w_in[i].astype(BF16)
        gw = pool_group_w[i].astype(BF16)
        wpb = w_pool_branch[i].astype(BF16)
        wab = w_attn_branch[i].astype(BF16)
        wo = w_out[i].astype(BF16)
        w1 = w_ffn_in[i].astype(BF16)
        w2 = w_ffn_out[i].astype(BF16)
        wpp = w_ple_proj[i].astype(BF16)
        wpg = w_ple_gate[i].astype(BF16)
        ffn_args = (row(ln2[i]), w1, w2, wpp, row(ple_norm[i]), wpg)
        h1, kp, vp, pp = _prompt_mixer(hp, cos_p, sg_p, attn_sinks[i], row(ln1[i]), wi, gw, row(pool_scale[i]),
                                       wpb, wab, wo, tm)
        hp = _ffn(h1.reshape(b * s, d), p_prompt[i].reshape(b * s, PLE_DIM), *ffn_args, row(final_norm),
                  tm).reshape(b, s, d)
        nkp.append(kp.reshape(b, w_cache, N_KV_HEADS, HEAD_DIM))
        nvp.append(vp.reshape(b, w_cache, N_KV_HEADS, HEAD_DIM))
        npp.append(pp)

        qe, kn, vn, un = _sample_pre(hs, cos_s, sg_s, row(ln1[i]), wi)
        o, nk, nv, npool, psum = _sample_attn(
            qe.reshape(bd, N_HEADS, KV_W), kn.reshape(bd, 1, KV_W), vn.reshape(bd, 1, KV_W),
            un.reshape(bd, 1, POOL_WIDTH), cache_k[i].reshape(bd, w_cache, KV_W), cache_v[i].reshape(bd, w_cache, KV_W),
            state_pool[i], attn_sinks[i].reshape(N_HEADS, 1), 8)
        h1s = _sample_post(hs, un, psum.reshape(bd, POOL_WIDTH), o.reshape(bd, N_HEADS * KV_W), row(ln1[i]), wi, gw,
                           row(pool_scale[i]), wpb, wab, wo)
        hs = _ffn(h1s, p_sample[i].reshape(bd * t_dec, PLE_DIM), *ffn_args, row(final_norm), bd)
        nks.append(nk.reshape(bd, w_cache, N_KV_HEADS, HEAD_DIM))
        nvs.append(nv.reshape(bd, w_cache, N_KV_HEADS, HEAD_DIM))
        nps.append(npool)

    return (hp, hs.reshape(bd, t_dec, d), jnp.stack(nkp), jnp.stack(nvp), jnp.stack(npp),
            jnp.stack(nks), jnp.stack(nvs), jnp.stack(nps))
```

```python
import functools

import jax
import jax.numpy as jnp
import numpy as np
from jax import lax
from jax.experimental import pallas as pl
from jax.experimental.pallas import tpu as pltpu

D_MODEL = 1024
HEAD_DIM = 64
N_HEADS = D_MODEL // HEAD_DIM
N_KV_HEADS = N_HEADS // 4
GROUP = N_HEADS // N_KV_HEADS
ROT_DIMS = HEAD_DIM // 4
ROPE_THETA = 500000.0
WINDOW = 128
BLOCK = 128
POOL_WIDTH = D_MODEL // 2
POOL_WINDOWS = (2, 4, 8, 16)
POOL_GC = POOL_WIDTH // len(POOL_WINDOWS)
POOL_STATE = max(POOL_WINDOWS) - 1
FFN_HIDDEN = -(-8 * D_MODEL // (3 * 256)) * 256
PLE_DIM = 256
EPS = 1e-6
NEG_INF = -1e30
PAST_LEN = 16384

Q_W = N_HEADS * HEAD_DIM
KV_W = N_KV_HEADS * HEAD_DIM
C_U, C_Q, C_K, C_V, C_GP, C_GA, C_END = 0, POOL_WIDTH, POOL_WIDTH + Q_W, POOL_WIDTH + Q_W + KV_W, \
    POOL_WIDTH + Q_W + 2 * KV_W, POOL_WIDTH + Q_W + 2 * KV_W + D_MODEL, POOL_WIDTH + Q_W + 2 * KV_W + 2 * D_MODEL

LANES = 128
S_AHEAD = 16
S_SLOTS = 16
GATE_COLS = 256
U_HALO = 16
VMEM_LIMIT = 56 * 1024 * 1024

BF16 = jnp.bfloat16
F32 = jnp.float32


def _dot(a, b):
    return jnp.dot(a, b, preferred_element_type=F32)


def _dot_nt(a, b):
    return lax.dot_general(a, b, (((1,), (1,)), ((), ())), preferred_element_type=F32)


def _rms(x, g):
    y = x * lax.rsqrt(jnp.mean(x * x, axis=-1, keepdims=True) + EPS)
    return y * g


def _rope(x, cos, sg, lo8):
    outs = []
    for c in range(x.shape[1] // LANES):
        xc = x[:, c * LANES:(c + 1) * LANES]
        partner = jnp.where(lo8, pltpu.roll(xc, LANES - ROT_DIMS // 2, 1), pltpu.roll(xc, ROT_DIMS // 2, 1))
        outs.append(xc * cos + partner * sg)
    return jnp.concatenate(outs, axis=1)


def _lane_masks():
    lane = lax.broadcasted_iota(jnp.int32, (1, LANES), 1)
    lo8 = (lane % HEAD_DIM) < (ROT_DIMS // 2)
    lo64 = lane < HEAD_DIM
    return lo8, lo64


def _pool_mix(u, win_sum_fn, cnt_fn, gw_ref, pscale_ref):
    mixed = []
    for g, w in enumerate(POOL_WINDOWS):
        cols = slice(g * POOL_GC, (g + 1) * POOL_GC)
        ug = u[:, cols]
        m = win_sum_fn(g, w, ug) / cnt_fn(w) - ug
        mixed.append(_dot(m.astype(BF16), gw_ref[g]) * pscale_ref[:, cols])
    return jnp.concatenate(mixed, axis=1)


def _mixer_kernel(sinks_ref, x_ref, cos_ref, sg_ref, ln1_ref, w_in_ref, gw_ref, pscale_ref, w_pb_ref, w_ab_ref,
                  w_out_ref, h_ref, ko_ref, vo_ref, po_ref, kl_scr, kh_scr, vl_scr, vh_scr, u_scr, attn_scr, q_scr, s_scr, g_scr,
                  *, tm):
    t = pl.program_id(1)
    lo8, lo64 = _lane_masks()

    @pl.when(t == 0)
    def _():
        for scr in (kl_scr, kh_scr, vl_scr, vh_scr):
            scr[:, 0:BLOCK, :] = jnp.zeros((N_KV_HEADS, BLOCK, LANES), BF16)
        u_scr[0:U_HALO, :] = jnp.zeros((U_HALO, POOL_WIDTH), F32)

    x = x_ref[0]
    xn = _rms(x, ln1_ref[...]).astype(BF16)
    cos = cos_ref[...]
    sg = sg_ref[...]

    u = _dot(xn, w_in_ref[:, C_U:C_Q])
    q = (_rope(_dot(xn, w_in_ref[:, C_Q:C_K]), cos, sg, lo8) * (HEAD_DIM ** -0.5)).astype(BF16)
    k = _rope(_dot(xn, w_in_ref[:, C_K:C_V]), cos, sg, lo8)
    v = _dot(xn, w_in_ref[:, C_V:C_GP])
    ko_ref[0] = k[tm - WINDOW:, :]
    vo_ref[0] = v[tm - WINDOW:, :]

    zero = jnp.zeros((tm, LANES), F32)
    for src, lo_scr, hi_scr in ((k, kl_scr, kh_scr), (v, vl_scr, vh_scr)):
        for p in range(KV_W // LANES):
            xp = src[:, p * LANES:(p + 1) * LANES]
            xs = pltpu.roll(xp, HEAD_DIM, 1)
            lo_scr[2 * p, BLOCK:, :] = jnp.where(lo64, xp, zero).astype(BF16)
            hi_scr[2 * p, BLOCK:, :] = jnp.where(lo64, zero, xs).astype(BF16)
            lo_scr[2 * p + 1, BLOCK:, :] = jnp.where(lo64, xs, zero).astype(BF16)
            hi_scr[2 * p + 1, BLOCK:, :] = jnp.where(lo64, zero, xp).astype(BF16)

    qi = lax.broadcasted_iota(jnp.int32, (BLOCK, 2 * BLOCK), 0)
    si = lax.broadcasted_iota(jnp.int32, (BLOCK, 2 * BLOCK), 1)
    band = (si > qi) & (si <= qi + WINDOW)
    bias_band = jnp.where(band, 0.0, NEG_INF).astype(F32)
    bias_first = jnp.where(band & (si >= BLOCK), 0.0, NEG_INF).astype(F32)

    q_scr[...] = q
    bias0 = jnp.where(t == 0, bias_first, bias_band)
    ones_lo = jnp.broadcast_to(jnp.where(lo64, 1.0, 0.0).astype(BF16), (2 * BLOCK, LANES))
    ones_hi = jnp.broadcast_to(jnp.where(lo64, 0.0, 1.0).astype(BF16), (2 * BLOCK, LANES))

    units = [(j, kh) for j in range(tm // BLOCK) for kh in range(N_KV_HEADS)]

    def scores(i):
        j, kh = units[i]
        rows = slice(j * BLOCK, (j + 1) * BLOCK)
        win = slice(j * BLOCK, (j + 2) * BLOCK)
        qq = jnp.concatenate([q_scr[rows, (2 * kh + a) * LANES:(2 * kh + a + 1) * LANES] for a in range(2)], axis=0)
        kcat = jnp.concatenate([kl_scr[kh, win, :], kh_scr[kh, win, :]], axis=0)
        s = _dot_nt(qq, kcat)
        bias = bias0 if j == 0 else bias_band
        for a in range(2):
            for half in range(2):
                blk = (slice(a * BLOCK, (a + 1) * BLOCK), slice(half * 2 * BLOCK, (half + 1) * 2 * BLOCK))
                s_scr[(i % S_SLOTS,) + blk] = s[blk] + bias

    units_per_gate = len(units) * GATE_COLS // (2 * D_MODEL)
    for i in range(S_AHEAD):
        scores(i)
    for i, (j, kh) in enumerate(units):
        if i + S_AHEAD < len(units):
            scores(i + S_AHEAD)
        rows = slice(j * BLOCK, (j + 1) * BLOCK)
        win = slice(j * BLOCK, (j + 2) * BLOCK)
        vcat = jnp.concatenate([
            jnp.concatenate([vl_scr[kh, win, :], ones_lo], axis=1),
            jnp.concatenate([vh_scr[kh, win, :], ones_hi], axis=1)], axis=0)
        ps, sink_terms = [], []
        for a in range(2):
            es, st = [], []
            for half in range(2):
                sh = s_scr[i % S_SLOTS, a * BLOCK:(a + 1) * BLOCK, half * 2 * BLOCK:(half + 1) * 2 * BLOCK]
                sink = sinks_ref[4 * kh + 2 * a + half]
                m = jnp.maximum(jnp.max(sh, axis=1, keepdims=True), sink)
                es.append(jnp.exp(sh - m).astype(BF16))
                st.append(jnp.exp(sink - m))
            ps.append(jnp.concatenate(es, axis=1))
            sink_terms.append(jnp.where(lo64, st[0], st[1]))
        o = _dot(jnp.concatenate(ps, axis=0), vcat)
        for a in range(2):
            oa = o[a * BLOCK:(a + 1) * BLOCK]
            attn_scr[rows, (2 * kh + a) * LANES:(2 * kh + a + 1) * LANES] = (
                oa[:, :LANES] / (oa[:, LANES:] + sink_terms[a])).astype(BF16)
        if i % units_per_gate == units_per_gate - 1:
            gcols = slice((i // units_per_gate) * GATE_COLS, (i // units_per_gate + 1) * GATE_COLS)
            g_scr[:, gcols] = jax.nn.sigmoid(_dot(xn, w_in_ref[:, C_GP + gcols.start:C_GP + gcols.stop]))

    u_scr[U_HALO:, :] = u
    po_ref[0] = u_scr[U_HALO + tm - POOL_STATE:U_HALO + tm, :]
    pos = t * tm + lax.broadcasted_iota(jnp.int32, (tm, 1), 0)

    def win_sum(g, w, ug):
        acc = ug
        for i in range(1, w):
            acc = acc + u_scr[U_HALO - i:U_HALO - i + tm, g * POOL_GC:(g + 1) * POOL_GC]
        return acc

    pooled = _pool_mix(u, win_sum, lambda w: jnp.minimum(w, pos + 1).astype(F32), gw_ref, pscale_ref)

    for scr in (kl_scr, kh_scr, vl_scr, vh_scr):
        scr[:, 0:BLOCK, :] = scr[:, tm:tm + BLOCK, :]
    u_scr[0:U_HALO, :] = u_scr[tm:tm + U_HALO, :]

    merged = g_scr[:, 0:D_MODEL] * _dot(pooled.astype(BF16), w_pb_ref[...])
    merged = merged + g_scr[:, D_MODEL:2 * D_MODEL] * _dot(attn_scr[...], w_ab_ref[...])
    h_ref[0] = x + _dot(merged.astype(BF16), w_out_ref[...])


def _const_spec(shape):
    nd = len(shape)
    return pl.BlockSpec(shape, lambda *_: (0,) * nd, pipeline_mode=pl.Buffered(1))


def _prompt_mixer(x, cos, sg, sinks, ln1, w_in, gw, pscale, w_pb, w_ab, w_out, tm):
    b, s, d = x.shape
    nt = s // tm
    grid_spec = pltpu.PrefetchScalarGridSpec(
        num_scalar_prefetch=1,
        grid=(b, nt),
        in_specs=[
            pl.BlockSpec((1, tm, d), lambda bi, ti, *_: (bi, ti, 0)),
            pl.BlockSpec((tm, LANES), lambda bi, ti, *_: (ti, 0)),
            pl.BlockSpec((tm, LANES), lambda bi, ti, *_: (ti, 0)),
            _const_spec((1, d)),
            _const_spec(w_in.shape),
            _const_spec(gw.shape),
            _const_spec((1, POOL_WIDTH)),
            _const_spec(w_pb.shape),
            _const_spec(w_ab.shape),
            _const_spec(w_out.shape),
        ],
        out_specs=[
            pl.BlockSpec((1, tm, d), lambda bi, ti, *_: (bi, ti, 0)),
            pl.BlockSpec((1, WINDOW, KV_W), lambda bi, ti, *_: (bi, 0, 0)),
            pl.BlockSpec((1, WINDOW, KV_W), lambda bi, ti, *_: (bi, 0, 0)),
            pl.BlockSpec((1, POOL_STATE, POOL_WIDTH), lambda bi, ti, *_: (bi, 0, 0)),
        ],
        scratch_shapes=[
            pltpu.VMEM((N_KV_HEADS, BLOCK + tm, LANES), BF16),
            pltpu.VMEM((N_KV_HEADS, BLOCK + tm, LANES), BF16),
            pltpu.VMEM((N_KV_HEADS, BLOCK + tm, LANES), BF16),
            pltpu.VMEM((N_KV_HEADS, BLOCK + tm, LANES), BF16),
            pltpu.VMEM((U_HALO + tm, POOL_WIDTH), F32),
            pltpu.VMEM((tm, Q_W), BF16),
            pltpu.VMEM((tm, Q_W), BF16),
            pltpu.VMEM((S_SLOTS, 2 * BLOCK, 4 * BLOCK), F32),
            pltpu.VMEM((tm, 2 * D_MODEL), F32),
        ],
    )
    return pl.pallas_call(
        functools.partial(_mixer_kernel, tm=tm),
        grid_spec=grid_spec,
        out_shape=[
            jax.ShapeDtypeStruct((b, s, d), F32),
            jax.ShapeDtypeStruct((b, WINDOW, KV_W), F32),
            jax.ShapeDtypeStruct((b, WINDOW, KV_W), F32),
            jax.ShapeDtypeStruct((b, POOL_STATE, POOL_WIDTH), F32),
        ],
        compiler_params=pltpu.CompilerParams(
            dimension_semantics=("arbitrary", "arbitrary"), vmem_limit_bytes=VMEM_LIMIT),
        name="prompt_mixer",
    )(sinks, x, cos, sg, ln1, w_in, gw, pscale, w_pb, w_ab, w_out)


FFN_CHUNKS = ((0, 1024), (1024, 2048), (2048, FFN_HIDDEN))


def _ffn_kernel(h_ref, p_ref, ln2_ref, w1_ref, w2_ref, w_pp_ref, pn_ref, w_pg_ref, fn_ref, y_ref):
    h = h_ref[...]
    hn = _rms(h, ln2_ref[...]).astype(BF16)
    acc = h
    for lo, hi in FFN_CHUNKS:
        gate = _dot(hn, w1_ref[:, lo:hi])
        up = _dot(hn, w1_ref[:, FFN_HIDDEN + lo:FFN_HIDDEN + hi])
        act = (gate * jax.nn.sigmoid(gate) * up).astype(BF16)
        acc = acc + _dot(act, w2_ref[lo:hi, :])
    e = _rms(_dot(p_ref[...].astype(BF16), w_pp_ref[...]), pn_ref[...])
    h3 = acc + jax.nn.sigmoid(_dot(acc.astype(BF16), w_pg_ref[...])) * e
    y_ref[...] = _rms(h3, fn_ref[...])


def _ffn(h, p, ln2, w1, w2, w_pp, pn, w_pg, fn, tm):
    n, d = h.shape
    return pl.pallas_call(
        _ffn_kernel,
        grid=(n // tm,),
        in_specs=[
            pl.BlockSpec((tm, d), lambda i: (i, 0)),
            pl.BlockSpec((tm, PLE_DIM), lambda i: (i, 0)),
            _const_spec((1, d)),
            _const_spec(w1.shape),
            _const_spec(w2.shape),
            _const_spec(w_pp.shape),
            _const_spec((1, d)),
            _const_spec(w_pg.shape),
            _const_spec((1, d)),
        ],
        out_specs=pl.BlockSpec((tm, d), lambda i: (i, 0)),
        out_shape=jax.ShapeDtypeStruct((n, d), F32),
        compiler_params=pltpu.CompilerParams(dimension_semantics=("arbitrary",), vmem_limit_bytes=VMEM_LIMIT),
        name="ffn_ple_norm",
    )(h, p, ln2, w1, w2, w_pp, pn, w_pg, fn)


def _sample_pre_kernel(x_ref, cos_ref, sg_ref, ln1_ref, w_in_ref, qe_ref, kn_ref, vn_ref, u_ref):
    lo8, _ = _lane_masks()
    xn = _rms(x_ref[...], ln1_ref[...]).astype(BF16)
    cos = cos_ref[...]
    sg = sg_ref[...]
    u_ref[...] = _dot(xn, w_in_ref[:, C_U:C_Q])
    q = (_rope(_dot(xn, w_in_ref[:, C_Q:C_K]), cos, sg, lo8) * (HEAD_DIM ** -0.5)).astype(BF16)
    kn_ref[...] = _rope(_dot(xn, w_in_ref[:, C_K:C_V]), cos, sg, lo8)
    vn_ref[...] = _dot(xn, w_in_ref[:, C_V:C_GP])
    ii = lax.broadcasted_iota(jnp.int32, (Q_W, KV_W), 0)
    jj = lax.broadcasted_iota(jnp.int32, (Q_W, KV_W), 1)
    for r in range(N_HEADS):
        kh = r // GROUP
        sel = ((ii - r * HEAD_DIM) == (jj - kh * HEAD_DIM)) & (jj >= kh * HEAD_DIM) & (jj < (kh + 1) * HEAD_DIM)
        qe_ref[:, r * KV_W:(r + 1) * KV_W] = _dot(q, jnp.where(sel, 1.0, 0.0).astype(BF16))


def _sample_pre(x, cos, sg, ln1, w_in):
    n, d = x.shape
    return pl.pallas_call(
        _sample_pre_kernel,
        grid=(1,),
        in_specs=[
            _const_spec((n, d)),
            _const_spec((1, LANES)),
            _const_spec((1, LANES)),
            _const_spec((1, d)),
            pl.BlockSpec((d, C_GP), lambda i: (0, 0), pipeline_mode=pl.Buffered(1)),
        ],
        out_specs=[
            pl.BlockSpec((n, N_HEADS * KV_W), lambda i: (0, 0)),
            pl.BlockSpec((n, KV_W), lambda i: (0, 0)),
            pl.BlockSpec((n, KV_W), lambda i: (0, 0)),
            pl.BlockSpec((n, POOL_WIDTH), lambda i: (0, 0)),
        ],
        out_shape=[
            jax.ShapeDtypeStruct((n, N_HEADS * KV_W), F32),
            jax.ShapeDtypeStruct((n, KV_W), F32),
            jax.ShapeDtypeStruct((n, KV_W), F32),
            jax.ShapeDtypeStruct((n, POOL_WIDTH), F32),
        ],
        compiler_params=pltpu.CompilerParams(dimension_semantics=("arbitrary",), vmem_limit_bytes=VMEM_LIMIT),
        name="sample_pre",
    )(x, cos, sg, ln1, w_in)


def _sample_attn_kernel(qe_ref, kn_ref, vn_ref, un_ref, ck_ref, cv_ref, st_ref, sink_ref,
                        o_ref, nk_ref, nv_ref, np_ref, ps_ref):
    w_cache = ck_ref.shape[1]
    nk_ref[:, 0:w_cache - 1, :] = ck_ref[:, 1:w_cache, :]
    nk_ref[:, w_cache - 1:w_cache, :] = kn_ref[...]
    nv_ref[:, 0:w_cache - 1, :] = cv_ref[:, 1:w_cache, :]
    nv_ref[:, w_cache - 1:w_cache, :] = vn_ref[...]
    np_ref[:, 0:POOL_STATE - 1, :] = st_ref[:, 1:POOL_STATE, :]
    np_ref[:, POOL_STATE - 1:POOL_STATE, :] = un_ref[...]
    for g, w in enumerate(POOL_WINDOWS):
        cols = slice(g * POOL_GC, (g + 1) * POOL_GC)
        ps_ref[:, :, cols] = jnp.sum(st_ref[:, POOL_STATE - (w - 1):POOL_STATE, cols], axis=1, keepdims=True)

    kk = nk_ref[...].astype(BF16)
    vv = nv_ref[...].astype(BF16)
    s = jnp.einsum('brd,bsd->brs', qe_ref[...].astype(BF16), kk, preferred_element_type=F32)
    sink = sink_ref[...][None]
    m = jnp.maximum(jnp.max(s, axis=-1, keepdims=True), sink)
    e = jnp.exp(s - m)
    l = jnp.sum(e, axis=-1, keepdims=True) + jnp.exp(sink - m)
    o = jnp.einsum('brs,bsd->brd', e.astype(BF16), vv, preferred_element_type=F32)
    o_ref[...] = o / l


def _sample_attn(qe, kn, vn, un, ck, cv, st, sink, bb):
    n, w_cache, _ = ck.shape
    blk = lambda *shape: pl.BlockSpec((bb,) + shape, lambda i: (i,) + (0,) * len(shape))
    return pl.pallas_call(
        _sample_attn_kernel,
        grid=(n // bb,),
        in_specs=[
            blk(N_HEADS, KV_W), blk(1, KV_W), blk(1, KV_W), blk(1, POOL_WIDTH),
            blk(w_cache, KV_W), blk(w_cache, KV_W), blk(POOL_STATE, POOL_WIDTH),
            _const_spec((N_HEADS, 1)),
        ],
        out_specs=[
            blk(N_HEADS, KV_W), blk(w_cache, KV_W), blk(w_cache, KV_W), blk(POOL_STATE, POOL_WIDTH),
            blk(1, POOL_WIDTH),
        ],
        out_shape=[
            jax.ShapeDtypeStruct((n, N_HEADS, KV_W), F32),
            jax.ShapeDtypeStruct((n, w_cache, KV_W), F32),
            jax.ShapeDtypeStruct((n, w_cache, KV_W), F32),
            jax.ShapeDtypeStruct((n, POOL_STATE, POOL_WIDTH), F32),
            jax.ShapeDtypeStruct((n, 1, POOL_WIDTH), F32),
        ],
        compiler_params=pltpu.CompilerParams(dimension_semantics=("arbitrary",), vmem_limit_bytes=VMEM_LIMIT),
        name="sample_attn",
    )(qe, kn, vn, un, ck, cv, st, sink)


def _sample_post_kernel(x_ref, u_ref, ps_ref, o_ref, ln1_ref, w_g_ref, gw_ref, pscale_ref, w_pb_ref, w_ab_ref,
                        w_out_ref, h_ref):
    x = x_ref[...]
    n = x.shape[0]
    xn = _rms(x, ln1_ref[...]).astype(BF16)
    u = u_ref[...]
    ps = ps_ref[...]
    pooled = _pool_mix(
        u, lambda g, w, ug: ug + ps[:, g * POOL_GC:(g + 1) * POOL_GC],
        lambda w: jnp.float32(min(w, PAST_LEN + 1)), gw_ref, pscale_ref)
    merged = jax.nn.sigmoid(_dot(xn, w_g_ref[:, 0:D_MODEL])) * _dot(pooled.astype(BF16), w_pb_ref[...])
    ab = jnp.zeros((n, D_MODEL), F32)
    for r in range(N_HEADS):
        kh = r // GROUP
        parts = []
        if kh > 0:
            parts.append(jnp.zeros((kh * HEAD_DIM, D_MODEL), BF16))
        parts.append(w_ab_ref[r * HEAD_DIM:(r + 1) * HEAD_DIM, :])
        if kh < N_KV_HEADS - 1:
            parts.append(jnp.zeros(((N_KV_HEADS - 1 - kh) * HEAD_DIM, D_MODEL), BF16))
        ab = ab + _dot(o_ref[:, r * KV_W:(r + 1) * KV_W].astype(BF16), jnp.concatenate(parts, axis=0))
    merged = merged + jax.nn.sigmoid(_dot(xn, w_g_ref[:, D_MODEL:2 * D_MODEL])) * ab
    h_ref[...] = x + _dot(merged.astype(BF16), w_out_ref[...])


def _sample_post(x, u, ps, o, ln1, w_in, gw, pscale, w_pb, w_ab, w_out):
    n, d = x.shape
    return pl.pallas_call(
        _sample_post_kernel,
        grid=(1,),
        in_specs=[
            _const_spec((n, d)),
            _const_spec((n, POOL_WIDTH)),
            _const_spec((n, POOL_WIDTH)),
            _const_spec((n, N_HEADS * KV_W)),
            _const_spec((1, d)),
            pl.BlockSpec((d, 2 * D_MODEL), lambda i: (0, 1), pipeline_mode=pl.Buffered(1)),
            _const_spec(gw.shape),
            _const_spec((1, POOL_WIDTH)),
            _const_spec(w_pb.shape),
            _const_spec(w_ab.shape),
            _const_spec(w_out.shape),
        ],
        out_specs=pl.BlockSpec((n, d), lambda i: (0, 0)),
        out_shape=jax.ShapeDtypeStruct((n, d), F32),
        compiler_params=pltpu.CompilerParams(dimension_semantics=("arbitrary",), vmem_limit_bytes=VMEM_LIMIT),
        name="sample_post",
    )(x, u, ps, o, ln1, w_in, gw, pscale, w_pb, w_ab, w_out)


def _rope_tables(first_pos, n):
    half = ROT_DIMS // 2
    inv = ROPE_THETA ** (-(np.arange(0, ROT_DIMS, 2, dtype=np.float64) / ROT_DIMS))
    ang = np.arange(first_pos, first_pos + n, dtype=np.float64)[:, None] * inv[None, :]
    cos, sin = np.cos(ang), np.sin(ang)
    rest = HEAD_DIM - 2 * half
    c64 = np.concatenate([cos, cos, np.ones((n, rest))], axis=1)
    s64 = np.concatenate([-sin, sin, np.zeros((n, rest))], axis=1)
    reps = LANES // HEAD_DIM
    return jnp.asarray(np.tile(c64, (1, reps)), F32), jnp.asarray(np.tile(s64, (1, reps)), F32)


def kernel(x_prompt, x_sample, p_prompt, p_sample, cache_k, cache_v, state_pool, ln1, w_in, pool_group_w, pool_scale,
           attn_sinks, w_pool_branch, w_attn_branch, w_out, ln2, w_ffn_in, w_ffn_out, w_ple_proj, ple_norm,
           w_ple_gate, final_norm):
    depth = ln1.shape[0]
    b, s, d = x_prompt.shape
    bd, t_dec, _ = x_sample.shape
    w_cache = cache_k.shape[2]
    assert depth == 1 and t_dec == 1 and w_cache == WINDOW and s % BLOCK == 0 and d == D_MODEL
    tm = 512
    assert s % tm == 0

    cos_p, sg_p = _rope_tables(0, s)
    cos_s, sg_s = _rope_tables(PAST_LEN, t_dec)

    hp = x_prompt
    hs = x_sample.reshape(bd, d)
    row = lambda a: a.reshape(1, -1)
    nkp, nvp, npp, nks, nvs, nps = [], [], [], [], [], []
    for i in range(depth):
        wi = w_in[i].astype(BF16)
        gw = pool_group_w[i].astype(BF16)
        wpb = w_pool_branch[i].astype(BF16)
        wab = w_attn_branch[i].astype(BF16)
        wo = w_out[i].astype(BF16)
        w1 = w_ffn_in[i].astype(BF16)
        w2 = w_ffn_out[i].astype(BF16)
        wpp = w_ple_proj[i].astype(BF16)
        wpg = w_ple_gate[i].astype(BF16)
        ffn_args = (row(ln2[i]), w1, w2, wpp, row(ple_norm[i]), wpg)
        h1, kp, vp, pp = _prompt_mixer(hp, cos_p, sg_p, attn_sinks[i], row(ln1[i]), wi, gw, row(pool_scale[i]),
                                       wpb, wab, wo, tm)
        hp = _ffn(h1.reshape(b * s, d), p_prompt[i].reshape(b * s, PLE_DIM), *ffn_args, row(final_norm),
                  tm).reshape(b, s, d)
        nkp.append(kp.reshape(b, w_cache, N_KV_HEADS, HEAD_DIM))
        nvp.append(vp.reshape(b, w_cache, N_KV_HEADS, HEAD_DIM))
        npp.append(pp)

        qe, kn, vn, un = _sample_pre(hs, cos_s, sg_s, row(ln1[i]), wi)
        o, nk, nv, npool, psum = _sample_attn(
            qe.reshape(bd, N_HEADS, KV_W), kn.reshape(bd, 1, KV_W), vn.reshape(bd, 1, KV_W),
            un.reshape(bd, 1, POOL_WIDTH), cache_k[i].reshape(bd, w_cache, KV_W), cache_v[i].reshape(bd, w_cache, KV_W),
            state_pool[i], attn_sinks[i].reshape(N_HEADS, 1), 8)
        h1s = _sample_post(hs, un, psum.reshape(bd, POOL_WIDTH), o.reshape(bd, N_HEADS * KV_W), row(ln1[i]), wi, gw,
                           row(pool_scale[i]), wpb, wab, wo)
        hs = _ffn(h1s, p_sample[i].reshape(bd * t_dec, PLE_DIM), *ffn_args, row(final_norm), bd)
        nks.append(nk.reshape(bd, w_cache, N_KV_HEADS, HEAD_DIM))
        nvs.append(nv.reshape(bd, w_cache, N_KV_HEADS, HEAD_DIM))
        nps.append(npool)

    return (hp, hs.reshape(bd, t_dec, d), jnp.stack(nkp), jnp.stack(nvp), jnp.stack(npp),
            jnp.stack(nks), jnp.stack(nvs), jnp.stack(nps))
```

```python
import functools

import jax
import jax.numpy as jnp
import numpy as np
from jax import lax
from jax.experimental import pallas as pl
from jax.experimental.pallas import tpu as pltpu

D_MODEL = 1024
HEAD_DIM = 64
N_HEADS = D_MODEL // HEAD_DIM
N_KV_HEADS = N_HEADS // 4
GROUP = N_HEADS // N_KV_HEADS
ROT_DIMS = HEAD_DIM // 4
ROPE_THETA = 500000.0
WINDOW = 128
BLOCK = 128
POOL_WIDTH = D_MODEL // 2
POOL_WINDOWS = (2, 4, 8, 16)
POOL_GC = POOL_WIDTH // len(POOL_WINDOWS)
POOL_STATE = max(POOL_WINDOWS) - 1
FFN_HIDDEN = -(-8 * D_MODEL // (3 * 256)) * 256
PLE_DIM = 256
EPS = 1e-6
NEG_INF = -1e30
PAST_LEN = 16384

Q_W = N_HEADS * HEAD_DIM
KV_W = N_KV_HEADS * HEAD_DIM
C_U, C_Q, C_K, C_V, C_GP, C_GA, C_END = 0, POOL_WIDTH, POOL_WIDTH + Q_W, POOL_WIDTH + Q_W + KV_W, \
    POOL_WIDTH + Q_W + 2 * KV_W, POOL_WIDTH + Q_W + 2 * KV_W + D_MODEL, POOL_WIDTH + Q_W + 2 * KV_W + 2 * D_MODEL

LANES = 128
S_AHEAD = 16
S_SLOTS = 16
GATE_COLS = 256
U_HALO = 16
VMEM_LIMIT = 56 * 1024 * 1024

BF16 = jnp.bfloat16
F32 = jnp.float32


def _dot(a, b):
    return jnp.dot(a, b, preferred_element_type=F32)


def _dot_nt(a, b):
    return lax.dot_general(a, b, (((1,), (1,)), ((), ())), preferred_element_type=F32)


def _rms(x, g):
    y = x * lax.rsqrt(jnp.mean(x * x, axis=-1, keepdims=True) + EPS)
    return y * g


def _rope(x, cos, sg, lo8):
    outs = []
    for c in range(x.shape[1] // LANES):
        xc = x[:, c * LANES:(c + 1) * LANES]
        partner = jnp.where(lo8, pltpu.roll(xc, LANES - ROT_DIMS // 2, 1), pltpu.roll(xc, ROT_DIMS // 2, 1))
        outs.append(xc * cos + partner * sg)
    return jnp.concatenate(outs, axis=1)


def _lane_masks():
    lane = lax.broadcasted_iota(jnp.int32, (1, LANES), 1)
    lo8 = (lane % HEAD_DIM) < (ROT_DIMS // 2)
    lo64 = lane < HEAD_DIM
    return lo8, lo64


def _pool_mix(u, win_sum_fn, cnt_fn, gw_ref, pscale_ref):
    mixed = []
    for g, w in enumerate(POOL_WINDOWS):
        cols = slice(g * POOL_GC, (g + 1) * POOL_GC)
        ug = u[:, cols]
        m = win_sum_fn(g, w, ug) / cnt_fn(w) - ug
        mixed.append(_dot(m.astype(BF16), gw_ref[g]) * pscale_ref[:, cols])
    return jnp.concatenate(mixed, axis=1)


def _mixer_kernel(sinks_ref, x_ref, cos_ref, sg_ref, ln1_ref, w_in_ref, gw_ref, pscale_ref, w_pb_ref, w_ab_ref,
                  w_out_ref, h_ref, ko_ref, vo_ref, po_ref, kl_scr, kh_scr, vl_scr, vh_scr, u_scr, attn_scr, q_scr, s_scr, g_scr,
                  *, tm):
    t = pl.program_id(1)
    lo8, lo64 = _lane_masks()

    @pl.when(t == 0)
    def _():
        for scr in (kl_scr, kh_scr, vl_scr, vh_scr):
            scr[:, 0:BLOCK, :] = jnp.zeros((N_KV_HEADS, BLOCK, LANES), BF16)
        u_scr[0:U_HALO, :] = jnp.zeros((U_HALO, POOL_WIDTH), F32)

    x = x_ref[0]
    xn = _rms(x, ln1_ref[...]).astype(BF16)
    cos = cos_ref[...]
    sg = sg_ref[...]

    u = _dot(xn, w_in_ref[:, C_U:C_Q])
    q = (_rope(_dot(xn, w_in_ref[:, C_Q:C_K]), cos, sg, lo8) * (HEAD_DIM ** -0.5)).astype(BF16)
    k = _rope(_dot(xn, w_in_ref[:, C_K:C_V]), cos, sg, lo8)
    v = _dot(xn, w_in_ref[:, C_V:C_GP])
    ko_ref[0] = k[tm - WINDOW:, :].T
    vo_ref[0] = v[tm - WINDOW:, :].T

    zero = jnp.zeros((tm, LANES), F32)
    for src, lo_scr, hi_scr in ((k, kl_scr, kh_scr), (v, vl_scr, vh_scr)):
        for p in range(KV_W // LANES):
            xp = src[:, p * LANES:(p + 1) * LANES]
            xs = pltpu.roll(xp, HEAD_DIM, 1)
            lo_scr[2 * p, BLOCK:, :] = jnp.where(lo64, xp, zero).astype(BF16)
            hi_scr[2 * p, BLOCK:, :] = jnp.where(lo64, zero, xs).astype(BF16)
            lo_scr[2 * p + 1, BLOCK:, :] = jnp.where(lo64, xs, zero).astype(BF16)
            hi_scr[2 * p + 1, BLOCK:, :] = jnp.where(lo64, zero, xp).astype(BF16)

    qi = lax.broadcasted_iota(jnp.int32, (BLOCK, 2 * BLOCK), 0)
    si = lax.broadcasted_iota(jnp.int32, (BLOCK, 2 * BLOCK), 1)
    band = (si > qi) & (si <= qi + WINDOW)
    bias_band = jnp.where(band, 0.0, NEG_INF).astype(F32)
    bias_first = jnp.where(band & (si >= BLOCK), 0.0, NEG_INF).astype(F32)

    q_scr[...] = q
    bias0 = jnp.where(t == 0, bias_first, bias_band)
    ones_lo = jnp.broadcast_to(jnp.where(lo64, 1.0, 0.0).astype(BF16), (2 * BLOCK, LANES))
    ones_hi = jnp.broadcast_to(jnp.where(lo64, 0.0, 1.0).astype(BF16), (2 * BLOCK, LANES))

    units = [(j, kh) for j in range(tm // BLOCK) for kh in range(N_KV_HEADS)]

    def scores(i):
        j, kh = units[i]
        rows = slice(j * BLOCK, (j + 1) * BLOCK)
        win = slice(j * BLOCK, (j + 2) * BLOCK)
        qq = jnp.concatenate([q_scr[rows, (2 * kh + a) * LANES:(2 * kh + a + 1) * LANES] for a in range(2)], axis=0)
        kcat = jnp.concatenate([kl_scr[kh, win, :], kh_scr[kh, win, :]], axis=0)
        s = _dot_nt(qq, kcat)
        bias = bias0 if j == 0 else bias_band
        for a in range(2):
            for half in range(2):
                blk = (slice(a * BLOCK, (a + 1) * BLOCK), slice(half * 2 * BLOCK, (half + 1) * 2 * BLOCK))
                s_scr[(i % S_SLOTS,) + blk] = s[blk] + bias

    units_per_gate = len(units) * GATE_COLS // (2 * D_MODEL)
    for i in range(S_AHEAD):
        scores(i)
    for i, (j, kh) in enumerate(units):
        if i + S_AHEAD < len(units):
            scores(i + S_AHEAD)
        rows = slice(j * BLOCK, (j + 1) * BLOCK)
        win = slice(j * BLOCK, (j + 2) * BLOCK)
        vcat = jnp.concatenate([
            jnp.concatenate([vl_scr[kh, win, :], ones_lo], axis=1),
            jnp.concatenate([vh_scr[kh, win, :], ones_hi], axis=1)], axis=0)
        ps, sink_terms = [], []
        for a in range(2):
            es, st = [], []
            for half in range(2):
                sh = s_scr[i % S_SLOTS, a * BLOCK:(a + 1) * BLOCK, half * 2 * BLOCK:(half + 1) * 2 * BLOCK]
                sink = sinks_ref[4 * kh + 2 * a + half]
                m = jnp.maximum(jnp.max(sh, axis=1, keepdims=True), sink)
                es.append(jnp.exp(sh - m).astype(BF16))
                st.append(jnp.exp(sink - m))
            ps.append(jnp.concatenate(es, axis=1))
            sink_terms.append(jnp.where(lo64, st[0], st[1]))
        o = _dot(jnp.concatenate(ps, axis=0), vcat)
        for a in range(2):
            oa = o[a * BLOCK:(a + 1) * BLOCK]
            attn_scr[rows, (2 * kh + a) * LANES:(2 * kh + a + 1) * LANES] = (
                oa[:, :LANES] / (oa[:, LANES:] + sink_terms[a])).astype(BF16)
        if i % units_per_gate == units_per_gate - 1:
            gcols = slice((i // units_per_gate) * GATE_COLS, (i // units_per_gate + 1) * GATE_COLS)
            g_scr[:, gcols] = jax.nn.sigmoid(_dot(xn, w_in_ref[:, C_GP + gcols.start:C_GP + gcols.stop]))

    u_scr[U_HALO:, :] = u
    po_ref[0] = u_scr[U_HALO + tm - POOL_STATE:U_HALO + tm, :]
    pos = t * tm + lax.broadcasted_iota(jnp.int32, (tm, 1), 0)

    def win_sum(g, w, ug):
        acc = ug
        for i in range(1, w):
            acc = acc + u_scr[U_HALO - i:U_HALO - i + tm, g * POOL_GC:(g + 1) * POOL_GC]
        return acc

    pooled = _pool_mix(u, win_sum, lambda w: jnp.minimum(w, pos + 1).astype(F32), gw_ref, pscale_ref)

    for scr in (kl_scr, kh_scr, vl_scr, vh_scr):
        scr[:, 0:BLOCK, :] = scr[:, tm:tm + BLOCK, :]
    u_scr[0:U_HALO, :] = u_scr[tm:tm + U_HALO, :]

    merged = g_scr[:, 0:D_MODEL] * _dot(pooled.astype(BF16), w_pb_ref[...])
    merged = merged + g_scr[:, D_MODEL:2 * D_MODEL] * _dot(attn_scr[...], w_ab_ref[...])
    h_ref[0] = x + _dot(merged.astype(BF16), w_out_ref[...])


def _const_spec(shape):
    nd = len(shape)
    return pl.BlockSpec(shape, lambda *_: (0,) * nd, pipeline_mode=pl.Buffered(1))


def _prompt_mixer(x, cos, sg, sinks, ln1, w_in, gw, pscale, w_pb, w_ab, w_out, tm):
    b, s, d = x.shape
    nt = s // tm
    grid_spec = pltpu.PrefetchScalarGridSpec(
        num_scalar_prefetch=1,
        grid=(b, nt),
        in_specs=[
            pl.BlockSpec((1, tm, d), lambda bi, ti, *_: (bi, ti, 0)),
            pl.BlockSpec((tm, LANES), lambda bi, ti, *_: (ti, 0)),
            pl.BlockSpec((tm, LANES), lambda bi, ti, *_: (ti, 0)),
            _const_spec((1, d)),
            _const_spec(w_in.shape),
            _const_spec(gw.shape),
            _const_spec((1, POOL_WIDTH)),
            _const_spec(w_pb.shape),
            _const_spec(w_ab.shape),
            _const_spec(w_out.shape),
        ],
        out_specs=[
            pl.BlockSpec((1, tm, d), lambda bi, ti, *_: (bi, ti, 0)),
            pl.BlockSpec((1, KV_W, WINDOW), lambda bi, ti, *_: (bi, 0, 0)),
            pl.BlockSpec((1, KV_W, WINDOW), lambda bi, ti, *_: (bi, 0, 0)),
            pl.BlockSpec((1, POOL_STATE, POOL_WIDTH), lambda bi, ti, *_: (bi, 0, 0)),
        ],
        scratch_shapes=[
            pltpu.VMEM((N_KV_HEADS, BLOCK + tm, LANES), BF16),
            pltpu.VMEM((N_KV_HEADS, BLOCK + tm, LANES), BF16),
            pltpu.VMEM((N_KV_HEADS, BLOCK + tm, LANES), BF16),
            pltpu.VMEM((N_KV_HEADS, BLOCK + tm, LANES), BF16),
            pltpu.VMEM((U_HALO + tm, POOL_WIDTH), F32),
            pltpu.VMEM((tm, Q_W), BF16),
            pltpu.VMEM((tm, Q_W), BF16),
            pltpu.VMEM((S_SLOTS, 2 * BLOCK, 4 * BLOCK), F32),
            pltpu.VMEM((tm, 2 * D_MODEL), F32),
        ],
    )
    return pl.pallas_call(
        functools.partial(_mixer_kernel, tm=tm),
        grid_spec=grid_spec,
        out_shape=[
            jax.ShapeDtypeStruct((b, s, d), F32),
            jax.ShapeDtypeStruct((b, KV_W, WINDOW), F32),
            jax.ShapeDtypeStruct((b, KV_W, WINDOW), F32),
            jax.ShapeDtypeStruct((b, POOL_STATE, POOL_WIDTH), F32),
        ],
        compiler_params=pltpu.CompilerParams(
            dimension_semantics=("arbitrary", "arbitrary"), vmem_limit_bytes=VMEM_LIMIT),
        name="prompt_mixer",
    )(sinks, x, cos, sg, ln1, w_in, gw, pscale, w_pb, w_ab, w_out)


FFN_CHUNKS = ((0, 1024), (1024, 2048), (2048, FFN_HIDDEN))


def _ffn_kernel(h_ref, p_ref, ln2_ref, w1_ref, w2_ref, w_pp_ref, pn_ref, w_pg_ref, fn_ref, y_ref):
    h = h_ref[...]
    hn = _rms(h, ln2_ref[...]).astype(BF16)
    acc = h
    for lo, hi in FFN_CHUNKS:
        gate = _dot(hn, w1_ref[:, lo:hi])
        up = _dot(hn, w1_ref[:, FFN_HIDDEN + lo:FFN_HIDDEN + hi])
        act = (gate * jax.nn.sigmoid(gate) * up).astype(BF16)
        acc = acc + _dot(act, w2_ref[lo:hi, :])
    e = _rms(_dot(p_ref[...].astype(BF16), w_pp_ref[...]), pn_ref[...])
    h3 = acc + jax.nn.sigmoid(_dot(acc.astype(BF16), w_pg_ref[...])) * e
    y_ref[...] = _rms(h3, fn_ref[...])


def _ffn(h, p, ln2, w1, w2, w_pp, pn, w_pg, fn, tm):
    n, d = h.shape
    return pl.pallas_call(
        _ffn_kernel,
        grid=(n // tm,),
        in_specs=[
            pl.BlockSpec((tm, d), lambda i: (i, 0)),
            pl.BlockSpec((tm, PLE_DIM), lambda i: (i, 0)),
            _const_spec((1, d)),
            _const_spec(w1.shape),
            _const_spec(w2.shape),
            _const_spec(w_pp.shape),
            _const_spec((1, d)),
            _const_spec(w_pg.shape),
            _const_spec((1, d)),
        ],
        out_specs=pl.BlockSpec((tm, d), lambda i: (i, 0)),
        out_shape=jax.ShapeDtypeStruct((n, d), F32),
        compiler_params=pltpu.CompilerParams(dimension_semantics=("arbitrary",), vmem_limit_bytes=VMEM_LIMIT),
        name="ffn_ple_norm",
    )(h, p, ln2, w1, w2, w_pp, pn, w_pg, fn)


def _sample_pre_kernel(x_ref, cos_ref, sg_ref, ln1_ref, w_in_ref, qe_ref, knt_ref, vnt_ref, u_ref):
    n = x_ref.shape[0]
    lo8, _ = _lane_masks()
    xn = _rms(x_ref[...], ln1_ref[...]).astype(BF16)
    cos = cos_ref[...]
    sg = sg_ref[...]
    u_ref[...] = _dot(xn, w_in_ref[:, C_U:C_Q])
    q = (_rope(_dot(xn, w_in_ref[:, C_Q:C_K]), cos, sg, lo8) * (HEAD_DIM ** -0.5)).astype(BF16)
    knt_ref[...] = _rope(_dot(xn, w_in_ref[:, C_K:C_V]), cos, sg, lo8).T
    vnt_ref[...] = _dot(xn, w_in_ref[:, C_V:C_GP]).T
    ii = lax.broadcasted_iota(jnp.int32, (Q_W, KV_W), 0)
    jj = lax.broadcasted_iota(jnp.int32, (Q_W, KV_W), 1)
    for r in range(N_HEADS):
        kh = r // GROUP
        sel = ((ii - r * HEAD_DIM) == (jj - kh * HEAD_DIM)) & (jj >= kh * HEAD_DIM) & (jj < (kh + 1) * HEAD_DIM)
        qr = _dot(q, jnp.where(sel, 1.0, 0.0).astype(BF16))
        for c in range(KV_W // LANES):
            qe_ref[c, pl.ds(r, n, stride=N_HEADS), :] = qr[:, c * LANES:(c + 1) * LANES]


def _sample_pre(x, cos, sg, ln1, w_in):
    n, d = x.shape
    return pl.pallas_call(
        _sample_pre_kernel,
        grid=(1,),
        in_specs=[
            _const_spec((n, d)),
            _const_spec((1, LANES)),
            _const_spec((1, LANES)),
            _const_spec((1, d)),
            pl.BlockSpec((d, C_GP), lambda i: (0, 0), pipeline_mode=pl.Buffered(1)),
        ],
        out_specs=[
            pl.BlockSpec((KV_W // LANES, n * N_HEADS, LANES), lambda i: (0, 0, 0)),
            pl.BlockSpec((KV_W, n), lambda i: (0, 0)),
            pl.BlockSpec((KV_W, n), lambda i: (0, 0)),
            pl.BlockSpec((n, POOL_WIDTH), lambda i: (0, 0)),
        ],
        out_shape=[
            jax.ShapeDtypeStruct((KV_W // LANES, n * N_HEADS, LANES), F32),
            jax.ShapeDtypeStruct((KV_W, n), F32),
            jax.ShapeDtypeStruct((KV_W, n), F32),
            jax.ShapeDtypeStruct((n, POOL_WIDTH), F32),
        ],
        compiler_params=pltpu.CompilerParams(dimension_semantics=("arbitrary",), vmem_limit_bytes=VMEM_LIMIT),
        name="sample_pre",
    )(x, cos, sg, ln1, w_in)


def _sample_attn_kernel(qe_ref, knt_ref, vnt_ref, ckt_ref, cvt_ref, sink_ref, o_ref, nkt_ref, nvt_ref, *, bb):
    n = knt_ref.shape[1]
    w_cache = ckt_ref.shape[2]
    i = pl.program_id(0)
    newest = lax.broadcasted_iota(jnp.int32, (1, w_cache), 1) == w_cache - 1
    shift = lax.rem(n - i * bb, n)
    kcols = pltpu.roll(knt_ref[...], shift, 1)
    vcols = pltpu.roll(vnt_ref[...], shift, 1)
    sink = sink_ref[...]
    scores = []
    for bl in range(bb):
        newk = jnp.where(newest, kcols[:, bl:bl + 1], pltpu.roll(ckt_ref[bl], w_cache - 1, 1))
        nkt_ref[bl] = newk
        nvt_ref[bl] = jnp.where(newest, vcols[:, bl:bl + 1], pltpu.roll(cvt_ref[bl], w_cache - 1, 1))
        rows = slice(bl * N_HEADS, (bl + 1) * N_HEADS)
        qb = jnp.concatenate([qe_ref[c, rows, :] for c in range(KV_W // LANES)], axis=1)
        scores.append(_dot(qb.astype(BF16), newk.astype(BF16)))
    probs, denoms = [], []
    for s in scores:
        m = jnp.maximum(jnp.max(s, axis=-1, keepdims=True), sink)
        e = jnp.exp(s - m)
        probs.append(e.astype(BF16))
        denoms.append(jnp.sum(e, axis=-1, keepdims=True) + jnp.exp(sink - m))
    for bl in range(bb):
        rows = slice(bl * N_HEADS, (bl + 1) * N_HEADS)
        o = _dot_nt(probs[bl], nvt_ref[bl].astype(BF16)) / denoms[bl]
        for c in range(KV_W // LANES):
            o_ref[c, rows, :] = o[:, c * LANES:(c + 1) * LANES]


def _sample_attn(qe, knt, vnt, ckt, cvt, sink, bb):
    n, _, w_cache = ckt.shape
    cache_spec = pl.BlockSpec((bb, KV_W, w_cache), lambda i: (i, 0, 0))
    head_spec = pl.BlockSpec((KV_W // LANES, bb * N_HEADS, LANES), lambda i: (0, i, 0))
    return pl.pallas_call(
        functools.partial(_sample_attn_kernel, bb=bb),
        grid=(n // bb,),
        in_specs=[head_spec, _const_spec((KV_W, n)), _const_spec((KV_W, n)), cache_spec, cache_spec,
                  _const_spec((N_HEADS, 1))],
        out_specs=[head_spec, cache_spec, cache_spec],
        out_shape=[
            jax.ShapeDtypeStruct((KV_W // LANES, n * N_HEADS, LANES), F32),
            jax.ShapeDtypeStruct((n, KV_W, w_cache), F32),
            jax.ShapeDtypeStruct((n, KV_W, w_cache), F32),
        ],
        compiler_params=pltpu.CompilerParams(dimension_semantics=("arbitrary",), vmem_limit_bytes=VMEM_LIMIT),
        name="sample_attn",
    )(qe, knt, vnt, ckt, cvt, sink)


def _sample_post_kernel(x_ref, u_ref, st_ref, o_ref, ln1_ref, w_g_ref, gw_ref, pscale_ref, w_pb_ref, w_ab_ref,
                        w_out_ref, h_ref, nst_ref):
    x = x_ref[...]
    n = x.shape[0]
    xn = _rms(x, ln1_ref[...]).astype(BF16)
    u = u_ref[...]
    nst_ref[0:POOL_STATE - 1] = st_ref[1:POOL_STATE]
    nst_ref[POOL_STATE - 1] = u

    def win_sum(g, w, ug):
        acc = ug
        for i in range(1, w):
            acc = acc + st_ref[POOL_STATE - i, :, g * POOL_GC:(g + 1) * POOL_GC]
        return acc

    pooled = _pool_mix(u, win_sum, lambda w: jnp.float32(min(w, PAST_LEN + 1)), gw_ref, pscale_ref)
    merged = jax.nn.sigmoid(_dot(xn, w_g_ref[:, 0:D_MODEL])) * _dot(pooled.astype(BF16), w_pb_ref[...])
    ab = jnp.zeros((n, D_MODEL), F32)
    for r in range(N_HEADS):
        kh = r // GROUP
        parts = []
        if kh > 0:
            parts.append(jnp.zeros((kh * HEAD_DIM, D_MODEL), BF16))
        parts.append(w_ab_ref[r * HEAD_DIM:(r + 1) * HEAD_DIM, :])
        if kh < N_KV_HEADS - 1:
            parts.append(jnp.zeros(((N_KV_HEADS - 1 - kh) * HEAD_DIM, D_MODEL), BF16))
        o_r = jnp.concatenate([o_ref[c, pl.ds(r, n, stride=N_HEADS), :] for c in range(KV_W // LANES)], axis=1)
        ab = ab + _dot(o_r.astype(BF16), jnp.concatenate(parts, axis=0))
    merged = merged + jax.nn.sigmoid(_dot(xn, w_g_ref[:, D_MODEL:2 * D_MODEL])) * ab
    h_ref[...] = x + _dot(merged.astype(BF16), w_out_ref[...])


def _sample_post(x, u, st, o, ln1, w_in, gw, pscale, w_pb, w_ab, w_out):
    n, d = x.shape
    return pl.pallas_call(
        _sample_post_kernel,
        grid=(1,),
        in_specs=[
            _const_spec((n, d)),
            _const_spec((n, POOL_WIDTH)),
            _const_spec(st.shape),
            _const_spec((KV_W // LANES, n * N_HEADS, LANES)),
            _const_spec((1, d)),
            pl.BlockSpec((d, 2 * D_MODEL), lambda i: (0, 1), pipeline_mode=pl.Buffered(1)),
            _const_spec(gw.shape),
            _const_spec((1, POOL_WIDTH)),
            _const_spec(w_pb.shape),
            _const_spec(w_ab.shape),
            _const_spec(w_out.shape),
        ],
        out_specs=[pl.BlockSpec((n, d), lambda i: (0, 0)), pl.BlockSpec(st.shape, lambda i: (0, 0, 0))],
        out_shape=[jax.ShapeDtypeStruct((n, d), F32), jax.ShapeDtypeStruct(st.shape, F32)],
        compiler_params=pltpu.CompilerParams(dimension_semantics=("arbitrary",), vmem_limit_bytes=VMEM_LIMIT),
        name="sample_post",
    )(x, u, st, o, ln1, w_in, gw, pscale, w_pb, w_ab, w_out)


def _rope_tables(first_pos, n):
    half = ROT_DIMS // 2
    inv = ROPE_THETA ** (-(np.arange(0, ROT_DIMS, 2, dtype=np.float64) / ROT_DIMS))
    ang = np.arange(first_pos, first_pos + n, dtype=np.float64)[:, None] * inv[None, :]
    cos, sin = np.cos(ang), np.sin(ang)
    rest = HEAD_DIM - 2 * half
    c64 = np.concatenate([cos, cos, np.ones((n, rest))], axis=1)
    s64 = np.concatenate([-sin, sin, np.zeros((n, rest))], axis=1)
    reps = LANES // HEAD_DIM
    return jnp.asarray(np.tile(c64, (1, reps)), F32), jnp.asarray(np.tile(s64, (1, reps)), F32)


def kernel(x_prompt, x_sample, p_prompt, p_sample, cache_k, cache_v, state_pool, ln1, w_in, pool_group_w, pool_scale,
           attn_sinks, w_pool_branch, w_attn_branch, w_out, ln2, w_ffn_in, w_ffn_out, w_ple_proj, ple_norm,
           w_ple_gate, final_norm):
    depth = ln1.shape[0]
    b, s, d = x_prompt.shape
    bd, t_dec, _ = x_sample.shape
    w_cache = cache_k.shape[2]
    assert depth == 1 and t_dec == 1 and w_cache == WINDOW and s % BLOCK == 0 and d == D_MODEL
    tm = 512
    assert s % tm == 0

    cos_p, sg_p = _rope_tables(0, s)
    cos_s, sg_s = _rope_tables(PAST_LEN, t_dec)

    hp = x_prompt
    hs = x_sample.reshape(bd, d)
    row = lambda a: a.reshape(1, -1)
    nkp, nvp, npp, nks, nvs, nps = [], [], [], [], [], []
    for i in range(depth):
        wi = w_in[i].astype(BF16)
        gw = pool_group_w[i].astype(BF16)
        wpb = w_pool_branch[i].astype(BF16)
        wab = w_attn_branch[i].astype(BF16)
        wo = w_out[i].astype(BF16)
        w1 = w_ffn_in[i].astype(BF16)
        w2 = w_ffn_out[i].astype(BF16)
        wpp = w_ple_proj[i].astype(BF16)
        wpg = w_ple_gate[i].astype(BF16)
        ffn_args = (row(ln2[i]), w1, w2, wpp, row(ple_norm[i]), wpg)
        h1, kp, vp, pp = _prompt_mixer(hp, cos_p, sg_p, attn_sinks[i], row(ln1[i]), wi, gw, row(pool_scale[i]),
                                       wpb, wab, wo, tm)
        hp = _ffn(h1.reshape(b * s, d), p_prompt[i].reshape(b * s, PLE_DIM), *ffn_args, row(final_norm),
                  tm).reshape(b, s, d)
        from_fm_p = lambda c: jnp.transpose(c.reshape(b, N_KV_HEADS, HEAD_DIM, w_cache), (0, 3, 1, 2))
        nkp.append(from_fm_p(kp))
        nvp.append(from_fm_p(vp))
        npp.append(pp)

        to_fm = lambda c: jnp.transpose(c, (0, 2, 3, 1)).reshape(bd, KV_W, w_cache)
        from_fm = lambda c: jnp.transpose(c.reshape(bd, N_KV_HEADS, HEAD_DIM, w_cache), (0, 3, 1, 2))
        qe, knt, vnt, un = _sample_pre(hs, cos_s, sg_s, row(ln1[i]), wi)
        o, nkt, nvt = _sample_attn(qe, knt, vnt, to_fm(cache_k[i]), to_fm(cache_v[i]),
                                   attn_sinks[i].reshape(N_HEADS, 1), 8)
        h1s, nst = _sample_post(hs, un, jnp.transpose(state_pool[i], (1, 0, 2)), o, row(ln1[i]), wi, gw,
                                row(pool_scale[i]), wpb, wab, wo)
        hs = _ffn(h1s, p_sample[i].reshape(bd * t_dec, PLE_DIM), *ffn_args, row(final_norm), bd)
        nks.append(from_fm(nkt))
        nvs.append(from_fm(nvt))
        nps.append(jnp.transpose(nst, (1, 0, 2)))

    return (hp, hs.reshape(bd, t_dec, d), jnp.stack(nkp), jnp.stack(nvp), jnp.stack(npp),
            jnp.stack(nks), jnp.stack(nvs), jnp.stack(nps))
```

```python
import functools

import jax
import jax.numpy as jnp
import numpy as np
from jax import lax
from jax.experimental import pallas as pl
from jax.experimental.pallas import tpu as pltpu

D_MODEL = 1024
HEAD_DIM = 64
N_HEADS = D_MODEL // HEAD_DIM
N_KV_HEADS = N_HEADS // 4
GROUP = N_HEADS // N_KV_HEADS
ROT_DIMS = HEAD_DIM // 4
ROPE_THETA = 500000.0
WINDOW = 128
BLOCK = 128
POOL_WIDTH = D_MODEL // 2
POOL_WINDOWS = (2, 4, 8, 16)
POOL_GC = POOL_WIDTH // len(POOL_WINDOWS)
POOL_STATE = max(POOL_WINDOWS) - 1
FFN_HIDDEN = -(-8 * D_MODEL // (3 * 256)) * 256
PLE_DIM = 256
EPS = 1e-6
NEG_INF = -1e30
PAST_LEN = 16384

Q_W = N_HEADS * HEAD_DIM
KV_W = N_KV_HEADS * HEAD_DIM
C_U, C_Q, C_K, C_V, C_GP, C_GA, C_END = 0, POOL_WIDTH, POOL_WIDTH + Q_W, POOL_WIDTH + Q_W + KV_W, \
    POOL_WIDTH + Q_W + 2 * KV_W, POOL_WIDTH + Q_W + 2 * KV_W + D_MODEL, POOL_WIDTH + Q_W + 2 * KV_W + 2 * D_MODEL

LANES = 128
S_AHEAD = 16
S_SLOTS = 16
GATE_COLS = 256
U_HALO = 24
SUBLANES = 8
VMEM_LIMIT = 56 * 1024 * 1024

BF16 = jnp.bfloat16
F32 = jnp.float32


def _dot(a, b):
    return jnp.dot(a, b, preferred_element_type=F32)


def _dot_nt(a, b):
    return lax.dot_general(a, b, (((1,), (1,)), ((), ())), preferred_element_type=F32)


def _rms(x, g):
    y = x * lax.rsqrt(jnp.mean(x * x, axis=-1, keepdims=True) + EPS)
    return y * g


def _rope(x, cos, sg, lo8):
    outs = []
    for c in range(x.shape[1] // LANES):
        xc = x[:, c * LANES:(c + 1) * LANES]
        partner = jnp.where(lo8, pltpu.roll(xc, LANES - ROT_DIMS // 2, 1), pltpu.roll(xc, ROT_DIMS // 2, 1))
        outs.append(xc * cos + partner * sg)
    return jnp.concatenate(outs, axis=1)


def _lane_masks():
    lane = lax.broadcasted_iota(jnp.int32, (1, LANES), 1)
    lo8 = (lane % HEAD_DIM) < (ROT_DIMS // 2)
    lo64 = lane < HEAD_DIM
    return lo8, lo64


def _pool_mix(u, win_sum_fn, cnt_fn, gw_ref, pscale_ref):
    mixed = []
    for g, w in enumerate(POOL_WINDOWS):
        cols = slice(g * POOL_GC, (g + 1) * POOL_GC)
        ug = u[:, cols]
        m = win_sum_fn(g, w, ug) / cnt_fn(w) - ug
        mixed.append(_dot(m.astype(BF16), gw_ref[g]) * pscale_ref[:, cols])
    return jnp.concatenate(mixed, axis=1)


def _mixer_kernel(sinks_ref, x_ref, cos_ref, sg_ref, ln1_ref, w_in_ref, gw_ref, pscale_ref, w_pb_ref, w_ab_ref,
                  w_out_ref, h_ref, ko_ref, vo_ref, po_ref, kl_scr, kh_scr, vl_scr, vh_scr, u_scr, lvl_scr, attn_scr, q_scr, s_scr, g_scr,
                  *, tm):
    t = pl.program_id(1)
    lo8, lo64 = _lane_masks()

    @pl.when(t == 0)
    def _():
        for scr in (kl_scr, kh_scr, vl_scr, vh_scr):
            scr[:, 0:BLOCK, :] = jnp.zeros((N_KV_HEADS, BLOCK, LANES), BF16)
        u_scr[0:U_HALO, :] = jnp.zeros((U_HALO, POOL_WIDTH), F32)
        lvl_scr[:, 0:SUBLANES, :] = jnp.zeros((len(POOL_WINDOWS), SUBLANES, POOL_GC), F32)

    x = x_ref[0]
    xn = _rms(x, ln1_ref[...]).astype(BF16)
    cos = cos_ref[...]
    sg = sg_ref[...]

    k = _rope(_dot(xn, w_in_ref[:, C_K:C_V]), cos, sg, lo8)
    v = _dot(xn, w_in_ref[:, C_V:C_GP])
    q = (_rope(_dot(xn, w_in_ref[:, C_Q:C_K]), cos, sg, lo8) * (HEAD_DIM ** -0.5)).astype(BF16)
    q_scr[...] = q
    u = _dot(xn, w_in_ref[:, C_U:C_Q])
    ko_ref[0] = k[tm - WINDOW:, :].T
    vo_ref[0] = v[tm - WINDOW:, :].T

    zero = jnp.zeros((tm, LANES), F32)
    for src, lo_scr, hi_scr in ((k, kl_scr, kh_scr), (v, vl_scr, vh_scr)):
        for p in range(KV_W // LANES):
            xp = src[:, p * LANES:(p + 1) * LANES]
            xs = pltpu.roll(xp, HEAD_DIM, 1)
            lo_scr[2 * p, BLOCK:, :] = jnp.where(lo64, xp, zero).astype(BF16)
            hi_scr[2 * p, BLOCK:, :] = jnp.where(lo64, zero, xs).astype(BF16)
            lo_scr[2 * p + 1, BLOCK:, :] = jnp.where(lo64, xs, zero).astype(BF16)
            hi_scr[2 * p + 1, BLOCK:, :] = jnp.where(lo64, zero, xp).astype(BF16)

    qi = lax.broadcasted_iota(jnp.int32, (BLOCK, BLOCK), 0)
    ci = lax.broadcasted_iota(jnp.int32, (BLOCK, BLOCK), 1)
    from_prev = ci > qi
    bias0 = jnp.where(jnp.logical_and(t == 0, from_prev), NEG_INF, 0.0).astype(F32)

    ones_lo = jnp.broadcast_to(jnp.where(lo64, 1.0, 0.0).astype(BF16), (2 * BLOCK, LANES))
    ones_hi = jnp.broadcast_to(jnp.where(lo64, 0.0, 1.0).astype(BF16), (2 * BLOCK, LANES))

    units = [(j, kh) for j in range(tm // BLOCK) for kh in range(N_KV_HEADS)]

    def scores(i):
        j, kh = units[i]
        rows = slice(j * BLOCK, (j + 1) * BLOCK)
        win = slice(j * BLOCK, (j + 2) * BLOCK)
        qq = jnp.concatenate([q_scr[rows, (2 * kh + a) * LANES:(2 * kh + a + 1) * LANES] for a in range(2)], axis=0)
        kcat = jnp.concatenate([kl_scr[kh, win, :], kh_scr[kh, win, :]], axis=0)
        s = _dot_nt(qq, kcat)
        for a in range(2):
            for half in range(2):
                sa = s[a * BLOCK:(a + 1) * BLOCK, half * 2 * BLOCK:(half + 1) * 2 * BLOCK]
                folded = jnp.where(from_prev, sa[:, :BLOCK], sa[:, BLOCK:])
                if j == 0:
                    folded = folded + bias0
                s_scr[i % S_SLOTS, a * BLOCK:(a + 1) * BLOCK, half * BLOCK:(half + 1) * BLOCK] = folded

    units_per_gate = len(units) * GATE_COLS // (2 * D_MODEL)
    for i in range(S_AHEAD):
        scores(i)
    for i, (j, kh) in enumerate(units):
        if i + S_AHEAD < len(units):
            scores(i + S_AHEAD)
        rows = slice(j * BLOCK, (j + 1) * BLOCK)
        win = slice(j * BLOCK, (j + 2) * BLOCK)
        vcat = jnp.concatenate([
            jnp.concatenate([vl_scr[kh, win, :], ones_lo], axis=1),
            jnp.concatenate([vh_scr[kh, win, :], ones_hi], axis=1)], axis=0)
        ps, sink_terms = [], []
        for a in range(2):
            es, st = [], []
            for half in range(2):
                sh = s_scr[i % S_SLOTS, a * BLOCK:(a + 1) * BLOCK, half * BLOCK:(half + 1) * BLOCK]
                sink = sinks_ref[4 * kh + 2 * a + half]
                m = jnp.maximum(jnp.max(sh, axis=1, keepdims=True), sink)
                e = jnp.exp(sh - m)
                es.append(jnp.where(from_prev, e, 0.0).astype(BF16))
                es.append(jnp.where(from_prev, 0.0, e).astype(BF16))
                st.append(jnp.exp(sink - m))
            ps.append(jnp.concatenate(es, axis=1))
            sink_terms.append(jnp.where(lo64, st[0], st[1]))
        o = _dot(jnp.concatenate(ps, axis=0), vcat)
        for a in range(2):
            oa = o[a * BLOCK:(a + 1) * BLOCK]
            attn_scr[rows, (2 * kh + a) * LANES:(2 * kh + a + 1) * LANES] = (
                oa[:, :LANES] / (oa[:, LANES:] + sink_terms[a])).astype(BF16)
        if i % units_per_gate == units_per_gate - 1:
            gcols = slice((i // units_per_gate) * GATE_COLS, (i // units_per_gate + 1) * GATE_COLS)
            g_scr[:, gcols] = jax.nn.sigmoid(_dot(xn, w_in_ref[:, C_GP + gcols.start:C_GP + gcols.stop]))

    u_scr[U_HALO:, :] = u
    po_ref[0] = u_scr[U_HALO + tm - POOL_STATE:U_HALO + tm, :]
    pos = t * tm + lax.broadcasted_iota(jnp.int32, (tm, 1), 0)

    def win_sum(g, w, ug):
        cols = slice(g * POOL_GC, (g + 1) * POOL_GC)
        n = U_HALO - SUBLANES + tm
        src, span = u_scr, 1
        while span < w:
            lvl = src[SUBLANES:SUBLANES + n, cols] + src[SUBLANES - span:SUBLANES - span + n, cols]
            span *= 2
            if span < w:
                lvl_scr[g, SUBLANES:SUBLANES + n, :] = lvl
                src, cols = lvl_scr.at[g], slice(None)
        return lvl[U_HALO - SUBLANES:, :]

    pooled = _pool_mix(u, win_sum, lambda w: jnp.minimum(w, pos + 1).astype(F32), gw_ref, pscale_ref)

    for scr in (kl_scr, kh_scr, vl_scr, vh_scr):
        scr[:, 0:BLOCK, :] = scr[:, tm:tm + BLOCK, :]
    u_scr[0:U_HALO, :] = u_scr[tm:tm + U_HALO, :]

    merged = g_scr[:, 0:D_MODEL] * _dot(pooled.astype(BF16), w_pb_ref[...])
    merged = merged + g_scr[:, D_MODEL:2 * D_MODEL] * _dot(attn_scr[...], w_ab_ref[...])
    h_ref[0] = x + _dot(merged.astype(BF16), w_out_ref[...])


def _const_spec(shape):
    nd = len(shape)
    return pl.BlockSpec(shape, lambda *_: (0,) * nd, pipeline_mode=pl.Buffered(1))


def _prompt_mixer(x, cos, sg, sinks, ln1, w_in, gw, pscale, w_pb, w_ab, w_out, tm):
    b, s, d = x.shape
    nt = s // tm
    grid_spec = pltpu.PrefetchScalarGridSpec(
        num_scalar_prefetch=1,
        grid=(b, nt),
        in_specs=[
            pl.BlockSpec((1, tm, d), lambda bi, ti, *_: (bi, ti, 0)),
            pl.BlockSpec((tm, LANES), lambda bi, ti, *_: (ti, 0)),
            pl.BlockSpec((tm, LANES), lambda bi, ti, *_: (ti, 0)),
            _const_spec((1, d)),
            _const_spec(w_in.shape),
            _const_spec(gw.shape),
            _const_spec((1, POOL_WIDTH)),
            _const_spec(w_pb.shape),
            _const_spec(w_ab.shape),
            _const_spec(w_out.shape),
        ],
        out_specs=[
            pl.BlockSpec((1, tm, d), lambda bi, ti, *_: (bi, ti, 0)),
            pl.BlockSpec((1, KV_W, WINDOW), lambda bi, ti, *_: (bi, 0, 0)),
            pl.BlockSpec((1, KV_W, WINDOW), lambda bi, ti, *_: (bi, 0, 0)),
            pl.BlockSpec((1, POOL_STATE, POOL_WIDTH), lambda bi, ti, *_: (bi, 0, 0)),
        ],
        scratch_shapes=[
            pltpu.VMEM((N_KV_HEADS, BLOCK + tm, LANES), BF16),
            pltpu.VMEM((N_KV_HEADS, BLOCK + tm, LANES), BF16),
            pltpu.VMEM((N_KV_HEADS, BLOCK + tm, LANES), BF16),
            pltpu.VMEM((N_KV_HEADS, BLOCK + tm, LANES), BF16),
            pltpu.VMEM((U_HALO + tm, POOL_WIDTH), F32),
            pltpu.VMEM((len(POOL_WINDOWS), U_HALO + tm, POOL_GC), F32),
            pltpu.VMEM((tm, Q_W), BF16),
            pltpu.VMEM((tm, Q_W), BF16),
            pltpu.VMEM((S_SLOTS, 2 * BLOCK, 2 * BLOCK), F32),
            pltpu.VMEM((tm, 2 * D_MODEL), F32),
        ],
    )
    return pl.pallas_call(
        functools.partial(_mixer_kernel, tm=tm),
        grid_spec=grid_spec,
        out_shape=[
            jax.ShapeDtypeStruct((b, s, d), F32),
            jax.ShapeDtypeStruct((b, KV_W, WINDOW), F32),
            jax.ShapeDtypeStruct((b, KV_W, WINDOW), F32),
            jax.ShapeDtypeStruct((b, POOL_STATE, POOL_WIDTH), F32),
        ],
        compiler_params=pltpu.CompilerParams(
            dimension_semantics=("arbitrary", "arbitrary"), vmem_limit_bytes=VMEM_LIMIT),
        name="prompt_mixer",
    )(sinks, x, cos, sg, ln1, w_in, gw, pscale, w_pb, w_ab, w_out)


FFN_CHUNKS = ((0, 1024), (1024, 2048), (2048, FFN_HIDDEN))


def _ffn_kernel(h_ref, p_ref, ln2_ref, w1_ref, w2_ref, w_pp_ref, pn_ref, w_pg_ref, fn_ref, y_ref):
    h = h_ref[...]
    hn = _rms(h, ln2_ref[...]).astype(BF16)
    acc = h
    for lo, hi in FFN_CHUNKS:
        gate = _dot(hn, w1_ref[:, lo:hi])
        up = _dot(hn, w1_ref[:, FFN_HIDDEN + lo:FFN_HIDDEN + hi])
        act = (gate * jax.nn.sigmoid(gate) * up).astype(BF16)
        acc = acc + _dot(act, w2_ref[lo:hi, :])
    e = _rms(_dot(p_ref[...].astype(BF16), w_pp_ref[...]), pn_ref[...])
    h3 = acc + jax.nn.sigmoid(_dot(acc.astype(BF16), w_pg_ref[...])) * e
    y_ref[...] = _rms(h3, fn_ref[...])


def _ffn(h, p, ln2, w1, w2, w_pp, pn, w_pg, fn, tm):
    n, d = h.shape
    return pl.pallas_call(
        _ffn_kernel,
        grid=(n // tm,),
        in_specs=[
            pl.BlockSpec((tm, d), lambda i: (i, 0)),
            pl.BlockSpec((tm, PLE_DIM), lambda i: (i, 0)),
            _const_spec((1, d)),
            _const_spec(w1.shape),
            _const_spec(w2.shape),
            _const_spec(w_pp.shape),
            _const_spec((1, d)),
            _const_spec(w_pg.shape),
            _const_spec((1, d)),
        ],
        out_specs=pl.BlockSpec((tm, d), lambda i: (i, 0)),
        out_shape=jax.ShapeDtypeStruct((n, d), F32),
        compiler_params=pltpu.CompilerParams(dimension_semantics=("arbitrary",), vmem_limit_bytes=VMEM_LIMIT),
        name="ffn_ple_norm",
    )(h, p, ln2, w1, w2, w_pp, pn, w_pg, fn)


def _sample_pre_kernel(x_ref, cos_ref, sg_ref, ln1_ref, w_in_ref, qe_ref, knt_ref, vnt_ref, u_ref):
    n = x_ref.shape[0]
    lo8, _ = _lane_masks()
    xn = _rms(x_ref[...], ln1_ref[...]).astype(BF16)
    cos = cos_ref[...]
    sg = sg_ref[...]
    u_ref[...] = _dot(xn, w_in_ref[:, C_U:C_Q])
    q = (_rope(_dot(xn, w_in_ref[:, C_Q:C_K]), cos, sg, lo8) * (HEAD_DIM ** -0.5)).astype(BF16)
    knt_ref[...] = _rope(_dot(xn, w_in_ref[:, C_K:C_V]), cos, sg, lo8).T
    vnt_ref[...] = _dot(xn, w_in_ref[:, C_V:C_GP]).T
    ii = lax.broadcasted_iota(jnp.int32, (Q_W, KV_W), 0)
    jj = lax.broadcasted_iota(jnp.int32, (Q_W, KV_W), 1)
    for r in range(N_HEADS):
        kh = r // GROUP
        sel = ((ii - r * HEAD_DIM) == (jj - kh * HEAD_DIM)) & (jj >= kh * HEAD_DIM) & (jj < (kh + 1) * HEAD_DIM)
        qr = _dot(q, jnp.where(sel, 1.0, 0.0).astype(BF16))
        for c in range(KV_W // LANES):
            qe_ref[c, pl.ds(r, n, stride=N_HEADS), :] = qr[:, c * LANES:(c + 1) * LANES]


def _sample_pre(x, cos, sg, ln1, w_in):
    n, d = x.shape
    return pl.pallas_call(
        _sample_pre_kernel,
        grid=(1,),
        in_specs=[
            _const_spec((n, d)),
            _const_spec((1, LANES)),
            _const_spec((1, LANES)),
            _const_spec((1, d)),
            pl.BlockSpec((d, C_GP), lambda i: (0, 0), pipeline_mode=pl.Buffered(1)),
        ],
        out_specs=[
            pl.BlockSpec((KV_W // LANES, n * N_HEADS, LANES), lambda i: (0, 0, 0)),
            pl.BlockSpec((KV_W, n), lambda i: (0, 0)),
            pl.BlockSpec((KV_W, n), lambda i: (0, 0)),
            pl.BlockSpec((n, POOL_WIDTH), lambda i: (0, 0)),
        ],
        out_shape=[
            jax.ShapeDtypeStruct((KV_W // LANES, n * N_HEADS, LANES), F32),
            jax.ShapeDtypeStruct((KV_W, n), F32),
            jax.ShapeDtypeStruct((KV_W, n), F32),
            jax.ShapeDtypeStruct((n, POOL_WIDTH), F32),
        ],
        compiler_params=pltpu.CompilerParams(dimension_semantics=("arbitrary",), vmem_limit_bytes=VMEM_LIMIT),
        name="sample_pre",
    )(x, cos, sg, ln1, w_in)


def _sample_attn_kernel(qe_ref, knt_ref, vnt_ref, ckt_ref, cvt_ref, sink_ref, o_ref, nkt_ref, nvt_ref, *, bb):
    n = knt_ref.shape[1]
    w_cache = ckt_ref.shape[2]
    i = pl.program_id(0)
    newest = lax.broadcasted_iota(jnp.int32, (1, w_cache), 1) == w_cache - 1
    shift = lax.rem(n - i * bb, n)
    kcols = pltpu.roll(knt_ref[...], shift, 1)
    vcols = pltpu.roll(vnt_ref[...], shift, 1)
    sink = sink_ref[...]
    scores = []
    for bl in range(bb):
        newk = jnp.where(newest, kcols[:, bl:bl + 1], pltpu.roll(ckt_ref[bl], w_cache - 1, 1))
        nkt_ref[bl] = newk
        nvt_ref[bl] = jnp.where(newest, vcols[:, bl:bl + 1], pltpu.roll(cvt_ref[bl], w_cache - 1, 1))
        rows = slice(bl * N_HEADS, (bl + 1) * N_HEADS)
        qb = jnp.concatenate([qe_ref[c, rows, :] for c in range(KV_W // LANES)], axis=1)
        scores.append(_dot(qb.astype(BF16), newk.astype(BF16)))
    probs, denoms = [], []
    for s in scores:
        m = jnp.maximum(jnp.max(s, axis=-1, keepdims=True), sink)
        e = jnp.exp(s - m)
        probs.append(e.astype(BF16))
        denoms.append(jnp.sum(e, axis=-1, keepdims=True) + jnp.exp(sink - m))
    for bl in range(bb):
        rows = slice(bl * N_HEADS, (bl + 1) * N_HEADS)
        o = _dot_nt(probs[bl], nvt_ref[bl].astype(BF16)) / denoms[bl]
        for c in range(KV_W // LANES):
            o_ref[c, rows, :] = o[:, c * LANES:(c + 1) * LANES]


def _sample_attn(qe, knt, vnt, ckt, cvt, sink, bb):
    n, _, w_cache = ckt.shape
    cache_spec = pl.BlockSpec((bb, KV_W, w_cache), lambda i: (i, 0, 0))
    head_spec = pl.BlockSpec((KV_W // LANES, bb * N_HEADS, LANES), lambda i: (0, i, 0))
    return pl.pallas_call(
        functools.partial(_sample_attn_kernel, bb=bb),
        grid=(n // bb,),
        in_specs=[head_spec, _const_spec((KV_W, n)), _const_spec((KV_W, n)), cache_spec, cache_spec,
                  _const_spec((N_HEADS, 1))],
        out_specs=[head_spec, cache_spec, cache_spec],
        out_shape=[
            jax.ShapeDtypeStruct((KV_W // LANES, n * N_HEADS, LANES), F32),
            jax.ShapeDtypeStruct((n, KV_W, w_cache), F32),
            jax.ShapeDtypeStruct((n, KV_W, w_cache), F32),
        ],
        compiler_params=pltpu.CompilerParams(dimension_semantics=("arbitrary",), vmem_limit_bytes=VMEM_LIMIT),
        name="sample_attn",
    )(qe, knt, vnt, ckt, cvt, sink)


def _sample_post_kernel(x_ref, u_ref, st_ref, o_ref, ln1_ref, w_g_ref, gw_ref, pscale_ref, w_pb_ref, w_ab_ref,
                        w_out_ref, h_ref, nst_ref):
    x = x_ref[...]
    n = x.shape[0]
    xn = _rms(x, ln1_ref[...]).astype(BF16)
    u = u_ref[...]
    nst_ref[0:POOL_STATE - 1] = st_ref[1:POOL_STATE]
    nst_ref[POOL_STATE - 1] = u

    def win_sum(g, w, ug):
        acc = ug
        for i in range(1, w):
            acc = acc + st_ref[POOL_STATE - i, :, g * POOL_GC:(g + 1) * POOL_GC]
        return acc

    pooled = _pool_mix(u, win_sum, lambda w: jnp.float32(min(w, PAST_LEN + 1)), gw_ref, pscale_ref)
    merged = jax.nn.sigmoid(_dot(xn, w_g_ref[:, 0:D_MODEL])) * _dot(pooled.astype(BF16), w_pb_ref[...])
    ab = jnp.zeros((n, D_MODEL), F32)
    for r in range(N_HEADS):
        kh = r // GROUP
        parts = []
        if kh > 0:
            parts.append(jnp.zeros((kh * HEAD_DIM, D_MODEL), BF16))
        parts.append(w_ab_ref[r * HEAD_DIM:(r + 1) * HEAD_DIM, :])
        if kh < N_KV_HEADS - 1:
            parts.append(jnp.zeros(((N_KV_HEADS - 1 - kh) * HEAD_DIM, D_MODEL), BF16))
        o_r = jnp.concatenate([o_ref[c, pl.ds(r, n, stride=N_HEADS), :] for c in range(KV_W // LANES)], axis=1)
        ab = ab + _dot(o_r.astype(BF16), jnp.concatenate(parts, axis=0))
    merged = merged + jax.nn.sigmoid(_dot(xn, w_g_ref[:, D_MODEL:2 * D_MODEL])) * ab
    h_ref[...] = x + _dot(merged.astype(BF16), w_out_ref[...])


def _sample_post(x, u, st, o, ln1, w_in, gw, pscale, w_pb, w_ab, w_out):
    n, d = x.shape
    return pl.pallas_call(
        _sample_post_kernel,
        grid=(1,),
        in_specs=[
            _const_spec((n, d)),
            _const_spec((n, POOL_WIDTH)),
            _const_spec(st.shape),
            _const_spec((KV_W // LANES, n * N_HEADS, LANES)),
            _const_spec((1, d)),
            pl.BlockSpec((d, 2 * D_MODEL), lambda i: (0, 1), pipeline_mode=pl.Buffered(1)),
            _const_spec(gw.shape),
            _const_spec((1, POOL_WIDTH)),
            _const_spec(w_pb.shape),
            _const_spec(w_ab.shape),
            _const_spec(w_out.shape),
        ],
        out_specs=[pl.BlockSpec((n, d), lambda i: (0, 0)), pl.BlockSpec(st.shape, lambda i: (0, 0, 0))],
        out_shape=[jax.ShapeDtypeStruct((n, d), F32), jax.ShapeDtypeStruct(st.shape, F32)],
        compiler_params=pltpu.CompilerParams(dimension_semantics=("arbitrary",), vmem_limit_bytes=VMEM_LIMIT),
        name="sample_post",
    )(x, u, st, o, ln1, w_in, gw, pscale, w_pb, w_ab, w_out)


def _rope_tables(first_pos, n):
    half = ROT_DIMS // 2
    inv = ROPE_THETA ** (-(np.arange(0, ROT_DIMS, 2, dtype=np.float64) / ROT_DIMS))
    ang = np.arange(first_pos, first_pos + n, dtype=np.float64)[:, None] * inv[None, :]
    cos, sin = np.cos(ang), np.sin(ang)
    rest = HEAD_DIM - 2 * half
    c64 = np.concatenate([cos, cos, np.ones((n, rest))], axis=1)
    s64 = np.concatenate([-sin, sin, np.zeros((n, rest))], axis=1)
    reps = LANES // HEAD_DIM
    return jnp.asarray(np.tile(c64, (1, reps)), F32), jnp.asarray(np.tile(s64, (1, reps)), F32)


def kernel(x_prompt, x_sample, p_prompt, p_sample, cache_k, cache_v, state_pool, ln1, w_in, pool_group_w, pool_scale,
           attn_sinks, w_pool_branch, w_attn_branch, w_out, ln2, w_ffn_in, w_ffn_out, w_ple_proj, ple_norm,
           w_ple_gate, final_norm):
    depth = ln1.shape[0]
    b, s, d = x_prompt.shape
    bd, t_dec, _ = x_sample.shape
    w_cache = cache_k.shape[2]
    assert depth == 1 and t_dec == 1 and w_cache == WINDOW and s % BLOCK == 0 and d == D_MODEL
    tm = 512
    assert s % tm == 0

    cos_p, sg_p = _rope_tables(0, s)
    cos_s, sg_s = _rope_tables(PAST_LEN, t_dec)

    hp = x_prompt
    hs = x_sample.reshape(bd, d)
    row = lambda a: a.reshape(1, -1)
    nkp, nvp, npp, nks, nvs, nps = [], [], [], [], [], []
    for i in range(depth):
        wi = w_in[i].astype(BF16)
        gw = pool_group_w[i].astype(BF16)
        wpb = w_pool_branch[i].astype(BF16)
        wab = w_attn_branch[i].astype(BF16)
        wo = w_out[i].astype(BF16)
        w1 = w_ffn_in[i].astype(BF16)
        w2 = w_ffn_out[i].astype(BF16)
        wpp = w_ple_proj[i].astype(BF16)
        wpg = w_ple_gate[i].astype(BF16)
        ffn_args = (row(ln2[i]), w1, w2, wpp, row(ple_norm[i]), wpg)
        h1, kp, vp, pp = _prompt_mixer(hp, cos_p, sg_p, attn_sinks[i], row(ln1[i]), wi, gw, row(pool_scale[i]),
                                       wpb, wab, wo, tm)
        hp = _ffn(h1.reshape(b * s, d), p_prompt[i].reshape(b * s, PLE_DIM), *ffn_args, row(final_norm),
                  tm).reshape(b, s, d)
        from_fm_p = lambda c: jnp.transpose(c.reshape(b, N_KV_HEADS, HEAD_DIM, w_cache), (0, 3, 1, 2))
        nkp.append(from_fm_p(kp))
        nvp.append(from_fm_p(vp))
        npp.append(pp)

        to_fm = lambda c: jnp.transpose(c, (0, 2, 3, 1)).reshape(bd, KV_W, w_cache)
        from_fm = lambda c: jnp.transpose(c.reshape(bd, N_KV_HEADS, HEAD_DIM, w_cache), (0, 3, 1, 2))
        qe, knt, vnt, un = _sample_pre(hs, cos_s, sg_s, row(ln1[i]), wi)
        o, nkt, nvt = _sample_attn(qe, knt, vnt, to_fm(cache_k[i]), to_fm(cache_v[i]),
                                   attn_sinks[i].reshape(N_HEADS, 1), 8)
        h1s, nst = _sample_post(hs, un, jnp.transpose(state_pool[i], (1, 0, 2)), o, row(ln1[i]), wi, gw,
                                row(pool_scale[i]), wpb, wab, wo)
        hs = _ffn(h1s, p_sample[i].reshape(bd * t_dec, PLE_DIM), *ffn_args, row(final_norm), bd)
        nks.append(from_fm(nkt))
        nvs.append(from_fm(nvt))
        nps.append(jnp.transpose(nst, (1, 0, 2)))

    return (hp, hs.reshape(bd, t_dec, d), jnp.stack(nkp), jnp.stack(nvp), jnp.stack(npp),
            jnp.stack(nks), jnp.stack(nvs), jnp.stack(nps))
```

```python
import functools

import jax
import jax.numpy as jnp
import numpy as np
from jax import lax
from jax.experimental import pallas as pl
from jax.experimental.pallas import tpu as pltpu

D_MODEL = 1024
HEAD_DIM = 64
N_HEADS = D_MODEL // HEAD_DIM
N_KV_HEADS = N_HEADS // 4
GROUP = N_HEADS // N_KV_HEADS
ROT_DIMS = HEAD_DIM // 4
ROPE_THETA = 500000.0
WINDOW = 128
BLOCK = 128
POOL_WIDTH = D_MODEL // 2
POOL_WINDOWS = (2, 4, 8, 16)
POOL_GC = POOL_WIDTH // len(POOL_WINDOWS)
POOL_STATE = max(POOL_WINDOWS) - 1
FFN_HIDDEN = -(-8 * D_MODEL // (3 * 256)) * 256
PLE_DIM = 256
EPS = 1e-6
NEG_INF = -1e30
PAST_LEN = 16384

Q_W = N_HEADS * HEAD_DIM
KV_W = N_KV_HEADS * HEAD_DIM
C_U, C_Q, C_K, C_V, C_GP, C_GA, C_END = 0, POOL_WIDTH, POOL_WIDTH + Q_W, POOL_WIDTH + Q_W + KV_W, \
    POOL_WIDTH + Q_W + 2 * KV_W, POOL_WIDTH + Q_W + 2 * KV_W + D_MODEL, POOL_WIDTH + Q_W + 2 * KV_W + 2 * D_MODEL

LANES = 128
S_AHEAD = 16
S_SLOTS = 16
GATE_COLS = 256
U_HALO = 24
SUBLANES = 8
VMEM_LIMIT = 56 * 1024 * 1024

BF16 = jnp.bfloat16
F32 = jnp.float32


def _dot(a, b):
    return jnp.dot(a, b, preferred_element_type=F32)


def _dot_nt(a, b):
    return lax.dot_general(a, b, (((1,), (1,)), ((), ())), preferred_element_type=F32)


def _rms(x, g):
    y = x * lax.rsqrt(jnp.mean(x * x, axis=-1, keepdims=True) + EPS)
    return y * g


def _rope(x, cos, sg, lo8):
    outs = []
    for c in range(x.shape[1] // LANES):
        xc = x[:, c * LANES:(c + 1) * LANES]
        partner = jnp.where(lo8, pltpu.roll(xc, LANES - ROT_DIMS // 2, 1), pltpu.roll(xc, ROT_DIMS // 2, 1))
        outs.append(xc * cos + partner * sg)
    return jnp.concatenate(outs, axis=1)


def _lane_masks():
    lane = lax.broadcasted_iota(jnp.int32, (1, LANES), 1)
    lo8 = (lane % HEAD_DIM) < (ROT_DIMS // 2)
    lo64 = lane < HEAD_DIM
    return lo8, lo64


def _pool_mix(u, win_sum_fn, cnt_fn, gw_ref, pscale_ref):
    mixed = []
    for g, w in enumerate(POOL_WINDOWS):
        cols = slice(g * POOL_GC, (g + 1) * POOL_GC)
        ug = u[:, cols]
        m = win_sum_fn(g, w, ug) / cnt_fn(w) - ug
        mixed.append(_dot(m.astype(BF16), gw_ref[g]) * pscale_ref[:, cols])
    return jnp.concatenate(mixed, axis=1)


def _mixer_kernel(sinks_ref, x_ref, cos_ref, sg_ref, ln1_ref, w_in_ref, gw_ref, pscale_ref, w_pb_ref, w_ab_ref,
                  w_out_ref, *rest, tm, n_cast):
    cast_src = rest[:n_cast]
    h_ref, ko_ref, vo_ref, po_ref = rest[n_cast:n_cast + 4]
    cast_dst = rest[n_cast + 4:2 * n_cast + 4]
    kl_scr, kh_scr, vl_scr, vh_scr, u_scr, lvl_scr, attn_scr, q_scr, s_scr, g_scr = rest[2 * n_cast + 4:]
    t = pl.program_id(1)
    for src, dst in zip(cast_src, cast_dst):
        dst[...] = src[...].astype(BF16)
    lo8, lo64 = _lane_masks()

    @pl.when(t == 0)
    def _():
        for scr in (kl_scr, kh_scr, vl_scr, vh_scr):
            scr[:, 0:BLOCK, :] = jnp.zeros((N_KV_HEADS, BLOCK, LANES), BF16)
        u_scr[0:U_HALO, :] = jnp.zeros((U_HALO, POOL_WIDTH), F32)
        lvl_scr[:, 0:SUBLANES, :] = jnp.zeros((len(POOL_WINDOWS), SUBLANES, POOL_GC), F32)

    x = x_ref[0]
    xn = _rms(x, ln1_ref[...]).astype(BF16)
    cos = cos_ref[...]
    sg = sg_ref[...]

    k = _rope(_dot(xn, w_in_ref[:, C_K:C_V]), cos, sg, lo8)
    v = _dot(xn, w_in_ref[:, C_V:C_GP])
    q = (_rope(_dot(xn, w_in_ref[:, C_Q:C_K]), cos, sg, lo8) * (HEAD_DIM ** -0.5)).astype(BF16)
    q_scr[...] = q
    u = _dot(xn, w_in_ref[:, C_U:C_Q])
    ko_ref[0] = k[tm - WINDOW:, :].T
    vo_ref[0] = v[tm - WINDOW:, :].T

    zero = jnp.zeros((tm, LANES), F32)
    for src, lo_scr, hi_scr in ((k, kl_scr, kh_scr), (v, vl_scr, vh_scr)):
        for p in range(KV_W // LANES):
            xp = src[:, p * LANES:(p + 1) * LANES]
            xs = pltpu.roll(xp, HEAD_DIM, 1)
            lo_scr[2 * p, BLOCK:, :] = jnp.where(lo64, xp, zero).astype(BF16)
            hi_scr[2 * p, BLOCK:, :] = jnp.where(lo64, zero, xs).astype(BF16)
            lo_scr[2 * p + 1, BLOCK:, :] = jnp.where(lo64, xs, zero).astype(BF16)
            hi_scr[2 * p + 1, BLOCK:, :] = jnp.where(lo64, zero, xp).astype(BF16)

    qi = lax.broadcasted_iota(jnp.int32, (BLOCK, BLOCK), 0)
    ci = lax.broadcasted_iota(jnp.int32, (BLOCK, BLOCK), 1)
    from_prev = ci > qi
    bias0 = jnp.where(jnp.logical_and(t == 0, from_prev), NEG_INF, 0.0).astype(F32)

    ones_lo = jnp.broadcast_to(jnp.where(lo64, 1.0, 0.0).astype(BF16), (2 * BLOCK, LANES))
    ones_hi = jnp.broadcast_to(jnp.where(lo64, 0.0, 1.0).astype(BF16), (2 * BLOCK, LANES))

    units = [(j, kh) for j in range(tm // BLOCK) for kh in range(N_KV_HEADS)]

    def scores(i):
        j, kh = units[i]
        rows = slice(j * BLOCK, (j + 1) * BLOCK)
        win = slice(j * BLOCK, (j + 2) * BLOCK)
        qq = jnp.concatenate([q_scr[rows, (2 * kh + a) * LANES:(2 * kh + a + 1) * LANES] for a in range(2)], axis=0)
        kcat = jnp.concatenate([kl_scr[kh, win, :], kh_scr[kh, win, :]], axis=0)
        s = _dot_nt(qq, kcat)
        for a in range(2):
            for half in range(2):
                sa = s[a * BLOCK:(a + 1) * BLOCK, half * 2 * BLOCK:(half + 1) * 2 * BLOCK]
                folded = jnp.where(from_prev, sa[:, :BLOCK], sa[:, BLOCK:])
                if j == 0:
                    folded = folded + bias0
                s_scr[i % S_SLOTS, a * BLOCK:(a + 1) * BLOCK, half * BLOCK:(half + 1) * BLOCK] = folded

    units_per_gate = len(units) * GATE_COLS // (2 * D_MODEL)
    for i in range(S_AHEAD):
        scores(i)
    for i, (j, kh) in enumerate(units):
        if i + S_AHEAD < len(units):
            scores(i + S_AHEAD)
        rows = slice(j * BLOCK, (j + 1) * BLOCK)
        win = slice(j * BLOCK, (j + 2) * BLOCK)
        vcat = jnp.concatenate([
            jnp.concatenate([vl_scr[kh, win, :], ones_lo], axis=1),
            jnp.concatenate([vh_scr[kh, win, :], ones_hi], axis=1)], axis=0)
        ps, sink_terms = [], []
        for a in range(2):
            es, st = [], []
            for half in range(2):
                sh = s_scr[i % S_SLOTS, a * BLOCK:(a + 1) * BLOCK, half * BLOCK:(half + 1) * BLOCK]
                sink = sinks_ref[4 * kh + 2 * a + half]
                m = jnp.maximum(jnp.max(sh, axis=1, keepdims=True), sink)
                e = jnp.exp(sh - m)
                es.append(jnp.where(from_prev, e, 0.0).astype(BF16))
                es.append(jnp.where(from_prev, 0.0, e).astype(BF16))
                st.append(jnp.exp(sink - m))
            ps.append(jnp.concatenate(es, axis=1))
            sink_terms.append(jnp.where(lo64, st[0], st[1]))
        o = _dot(jnp.concatenate(ps, axis=0), vcat)
        for a in range(2):
            oa = o[a * BLOCK:(a + 1) * BLOCK]
            attn_scr[rows, (2 * kh + a) * LANES:(2 * kh + a + 1) * LANES] = (
                oa[:, :LANES] / (oa[:, LANES:] + sink_terms[a])).astype(BF16)
        if i % units_per_gate == units_per_gate - 1:
            gcols = slice((i // units_per_gate) * GATE_COLS, (i // units_per_gate + 1) * GATE_COLS)
            g_scr[:, gcols] = jax.nn.sigmoid(_dot(xn, w_in_ref[:, C_GP + gcols.start:C_GP + gcols.stop]))

    u_scr[U_HALO:, :] = u
    po_ref[0] = u_scr[U_HALO + tm - POOL_STATE:U_HALO + tm, :]
    pos = t * tm + lax.broadcasted_iota(jnp.int32, (tm, 1), 0)

    def win_sum(g, w, ug):
        cols = slice(g * POOL_GC, (g + 1) * POOL_GC)
        n = U_HALO - SUBLANES + tm
        src, span = u_scr, 1
        while span < w:
            lvl = src[SUBLANES:SUBLANES + n, cols] + src[SUBLANES - span:SUBLANES - span + n, cols]
            span *= 2
            if span < w:
                lvl_scr[g, SUBLANES:SUBLANES + n, :] = lvl
                src, cols = lvl_scr.at[g], slice(None)
        return lvl[U_HALO - SUBLANES:, :]

    pooled = _pool_mix(u, win_sum, lambda w: jnp.minimum(w, pos + 1).astype(F32), gw_ref, pscale_ref)

    for scr in (kl_scr, kh_scr, vl_scr, vh_scr):
        scr[:, 0:BLOCK, :] = scr[:, tm:tm + BLOCK, :]
    u_scr[0:U_HALO, :] = u_scr[tm:tm + U_HALO, :]

    merged = g_scr[:, 0:D_MODEL] * _dot(pooled.astype(BF16), w_pb_ref[...])
    merged = merged + g_scr[:, D_MODEL:2 * D_MODEL] * _dot(attn_scr[...], w_ab_ref[...])
    h_ref[0] = x + _dot(merged.astype(BF16), w_out_ref[...])


def _const_spec(shape):
    nd = len(shape)
    return pl.BlockSpec(shape, lambda *_: (0,) * nd, pipeline_mode=pl.Buffered(1))


def _cast_block_rows(rows, steps):
    br = 2 * SUBLANES
    while rows % br or rows // br > steps:
        br *= 2
    return br


def _prompt_mixer(x, cos, sg, sinks, ln1, w_in, gw, pscale, w_pb, w_ab, w_out, to_cast, tm):
    b, s, d = x.shape
    nt = s // tm
    cast_specs = []
    for w in to_cast:
        br = _cast_block_rows(w.shape[0], b * nt)
        last = w.shape[0] // br - 1
        cast_specs.append(pl.BlockSpec(
            (br, w.shape[1]), lambda bi, ti, *_, last=last: (jnp.minimum(bi * nt + ti, last), 0)))
    grid_spec = pltpu.PrefetchScalarGridSpec(
        num_scalar_prefetch=1,
        grid=(b, nt),
        in_specs=[
            pl.BlockSpec((1, tm, d), lambda bi, ti, *_: (bi, ti, 0)),
            pl.BlockSpec((tm, LANES), lambda bi, ti, *_: (ti, 0)),
            pl.BlockSpec((tm, LANES), lambda bi, ti, *_: (ti, 0)),
            _const_spec((1, d)),
            _const_spec(w_in.shape),
            _const_spec(gw.shape),
            _const_spec((1, POOL_WIDTH)),
            _const_spec(w_pb.shape),
            _const_spec(w_ab.shape),
            _const_spec(w_out.shape),
        ] + cast_specs,
        out_specs=[
            pl.BlockSpec((1, tm, d), lambda bi, ti, *_: (bi, ti, 0)),
            pl.BlockSpec((1, KV_W, WINDOW), lambda bi, ti, *_: (bi, 0, 0)),
            pl.BlockSpec((1, KV_W, WINDOW), lambda bi, ti, *_: (bi, 0, 0)),
            pl.BlockSpec((1, POOL_STATE, POOL_WIDTH), lambda bi, ti, *_: (bi, 0, 0)),
        ] + cast_specs,
        scratch_shapes=[
            pltpu.VMEM((N_KV_HEADS, BLOCK + tm, LANES), BF16),
            pltpu.VMEM((N_KV_HEADS, BLOCK + tm, LANES), BF16),
            pltpu.VMEM((N_KV_HEADS, BLOCK + tm, LANES), BF16),
            pltpu.VMEM((N_KV_HEADS, BLOCK + tm, LANES), BF16),
            pltpu.VMEM((U_HALO + tm, POOL_WIDTH), F32),
            pltpu.VMEM((len(POOL_WINDOWS), U_HALO + tm, POOL_GC), F32),
            pltpu.VMEM((tm, Q_W), BF16),
            pltpu.VMEM((tm, Q_W), BF16),
            pltpu.VMEM((S_SLOTS, 2 * BLOCK, 2 * BLOCK), F32),
            pltpu.VMEM((tm, 2 * D_MODEL), F32),
        ],
    )
    return pl.pallas_call(
        functools.partial(_mixer_kernel, tm=tm, n_cast=len(to_cast)),
        grid_spec=grid_spec,
        out_shape=[
            jax.ShapeDtypeStruct((b, s, d), F32),
            jax.ShapeDtypeStruct((b, KV_W, WINDOW), F32),
            jax.ShapeDtypeStruct((b, KV_W, WINDOW), F32),
            jax.ShapeDtypeStruct((b, POOL_STATE, POOL_WIDTH), F32),
        ] + [jax.ShapeDtypeStruct(w.shape, BF16) for w in to_cast],
        compiler_params=pltpu.CompilerParams(
            dimension_semantics=("arbitrary", "arbitrary"), vmem_limit_bytes=VMEM_LIMIT),
        name="prompt_mixer",
    )(sinks, x, cos, sg, ln1, w_in, gw, pscale, w_pb, w_ab, w_out, *to_cast)


FFN_CHUNKS = ((0, 1024), (1024, 2048), (2048, FFN_HIDDEN))


def _ffn_kernel(h_ref, p_ref, ln2_ref, w1_ref, w2_ref, w_pp_ref, pn_ref, w_pg_ref, fn_ref, y_ref):
    h = h_ref[...]
    hn = _rms(h, ln2_ref[...]).astype(BF16)
    acc = h
    for lo, hi in FFN_CHUNKS:
        gate = _dot(hn, w1_ref[:, lo:hi])
        up = _dot(hn, w1_ref[:, FFN_HIDDEN + lo:FFN_HIDDEN + hi])
        act = (gate * jax.nn.sigmoid(gate) * up).astype(BF16)
        acc = acc + _dot(act, w2_ref[lo:hi, :])
    e = _rms(_dot(p_ref[...].astype(BF16), w_pp_ref[...]), pn_ref[...])
    h3 = acc + jax.nn.sigmoid(_dot(acc.astype(BF16), w_pg_ref[...])) * e
    y_ref[...] = _rms(h3, fn_ref[...])


def _ffn(h, p, ln2, w1, w2, w_pp, pn, w_pg, fn, tm):
    n, d = h.shape
    return pl.pallas_call(
        _ffn_kernel,
        grid=(n // tm,),
        in_specs=[
            pl.BlockSpec((tm, d), lambda i: (i, 0)),
            pl.BlockSpec((tm, PLE_DIM), lambda i: (i, 0)),
            _const_spec((1, d)),
            _const_spec(w1.shape),
            _const_spec(w2.shape),
            _const_spec(w_pp.shape),
            _const_spec((1, d)),
            _const_spec(w_pg.shape),
            _const_spec((1, d)),
        ],
        out_specs=pl.BlockSpec((tm, d), lambda i: (i, 0)),
        out_shape=jax.ShapeDtypeStruct((n, d), F32),
        compiler_params=pltpu.CompilerParams(dimension_semantics=("arbitrary",), vmem_limit_bytes=VMEM_LIMIT),
        name="ffn_ple_norm",
    )(h, p, ln2, w1, w2, w_pp, pn, w_pg, fn)


def _sample_pre_kernel(x_ref, cos_ref, sg_ref, ln1_ref, w_in_ref, qe_ref, knt_ref, vnt_ref, u_ref):
    n = x_ref.shape[0]
    lo8, _ = _lane_masks()
    xn = _rms(x_ref[...], ln1_ref[...]).astype(BF16)
    cos = cos_ref[...]
    sg = sg_ref[...]
    u_ref[...] = _dot(xn, w_in_ref[:, C_U:C_Q])
    q = (_rope(_dot(xn, w_in_ref[:, C_Q:C_K]), cos, sg, lo8) * (HEAD_DIM ** -0.5)).astype(BF16)
    knt_ref[...] = _rope(_dot(xn, w_in_ref[:, C_K:C_V]), cos, sg, lo8).T
    vnt_ref[...] = _dot(xn, w_in_ref[:, C_V:C_GP]).T
    ii = lax.broadcasted_iota(jnp.int32, (Q_W, KV_W), 0)
    jj = lax.broadcasted_iota(jnp.int32, (Q_W, KV_W), 1)
    for r in range(N_HEADS):
        kh = r // GROUP
        sel = ((ii - r * HEAD_DIM) == (jj - kh * HEAD_DIM)) & (jj >= kh * HEAD_DIM) & (jj < (kh + 1) * HEAD_DIM)
        qr = _dot(q, jnp.where(sel, 1.0, 0.0).astype(BF16))
        for c in range(KV_W // LANES):
            qe_ref[c, pl.ds(r, n, stride=N_HEADS), :] = qr[:, c * LANES:(c + 1) * LANES]


def _sample_pre(x, cos, sg, ln1, w_in):
    n, d = x.shape
    return pl.pallas_call(
        _sample_pre_kernel,
        grid=(1,),
        in_specs=[
            _const_spec((n, d)),
            _const_spec((1, LANES)),
            _const_spec((1, LANES)),
            _const_spec((1, d)),
            pl.BlockSpec((d, C_GP), lambda i: (0, 0), pipeline_mode=pl.Buffered(1)),
        ],
        out_specs=[
            pl.BlockSpec((KV_W // LANES, n * N_HEADS, LANES), lambda i: (0, 0, 0)),
            pl.BlockSpec((KV_W, n), lambda i: (0, 0)),
            pl.BlockSpec((KV_W, n), lambda i: (0, 0)),
            pl.BlockSpec((n, POOL_WIDTH), lambda i: (0, 0)),
        ],
        out_shape=[
            jax.ShapeDtypeStruct((KV_W // LANES, n * N_HEADS, LANES), F32),
            jax.ShapeDtypeStruct((KV_W, n), F32),
            jax.ShapeDtypeStruct((KV_W, n), F32),
            jax.ShapeDtypeStruct((n, POOL_WIDTH), F32),
        ],
        compiler_params=pltpu.CompilerParams(dimension_semantics=("arbitrary",), vmem_limit_bytes=VMEM_LIMIT),
        name="sample_pre",
    )(x, cos, sg, ln1, w_in)


def _sample_attn_kernel(qe_ref, knt_ref, vnt_ref, ckt_ref, cvt_ref, sink_ref, o_ref, nkt_ref, nvt_ref, *, bb):
    n = knt_ref.shape[1]
    w_cache = ckt_ref.shape[2]
    i = pl.program_id(0)
    newest = lax.broadcasted_iota(jnp.int32, (1, w_cache), 1) == w_cache - 1
    shift = lax.rem(n - i * bb, n)
    kcols = pltpu.roll(knt_ref[...], shift, 1)
    vcols = pltpu.roll(vnt_ref[...], shift, 1)
    sink = sink_ref[...]
    scores = []
    for bl in range(bb):
        newk = jnp.where(newest, kcols[:, bl:bl + 1], pltpu.roll(ckt_ref[bl], w_cache - 1, 1))
        nkt_ref[bl] = newk
        nvt_ref[bl] = jnp.where(newest, vcols[:, bl:bl + 1], pltpu.roll(cvt_ref[bl], w_cache - 1, 1))
        rows = slice(bl * N_HEADS, (bl + 1) * N_HEADS)
        qb = jnp.concatenate([qe_ref[c, rows, :] for c in range(KV_W // LANES)], axis=1)
        scores.append(_dot(qb.astype(BF16), newk.astype(BF16)))
    probs, denoms = [], []
    for s in scores:
        m = jnp.maximum(jnp.max(s, axis=-1, keepdims=True), sink)
        e = jnp.exp(s - m)
        probs.append(e.astype(BF16))
        denoms.append(jnp.sum(e, axis=-1, keepdims=True) + jnp.exp(sink - m))
    for bl in range(bb):
        rows = slice(bl * N_HEADS, (bl + 1) * N_HEADS)
        o = _dot_nt(probs[bl], nvt_ref[bl].astype(BF16)) / denoms[bl]
        for c in range(KV_W // LANES):
            o_ref[c, rows, :] = o[:, c * LANES:(c + 1) * LANES]


def _sample_attn(qe, knt, vnt, ckt, cvt, sink, bb):
    n, _, w_cache = ckt.shape
    cache_spec = pl.BlockSpec((bb, KV_W, w_cache), lambda i: (i, 0, 0))
    head_spec = pl.BlockSpec((KV_W // LANES, bb * N_HEADS, LANES), lambda i: (0, i, 0))
    return pl.pallas_call(
        functools.partial(_sample_attn_kernel, bb=bb),
        grid=(n // bb,),
        in_specs=[head_spec, _const_spec((KV_W, n)), _const_spec((KV_W, n)), cache_spec, cache_spec,
                  _const_spec((N_HEADS, 1))],
        out_specs=[head_spec, cache_spec, cache_spec],
        out_shape=[
            jax.ShapeDtypeStruct((KV_W // LANES, n * N_HEADS, LANES), F32),
            jax.ShapeDtypeStruct((n, KV_W, w_cache), F32),
            jax.ShapeDtypeStruct((n, KV_W, w_cache), F32),
        ],
        compiler_params=pltpu.CompilerParams(dimension_semantics=("arbitrary",), vmem_limit_bytes=VMEM_LIMIT),
        name="sample_attn",
    )(qe, knt, vnt, ckt, cvt, sink)


def _sample_post_kernel(x_ref, u_ref, st_ref, o_ref, ln1_ref, w_g_ref, gw_ref, pscale_ref, w_pb_ref, w_ab_ref,
                        w_out_ref, h_ref, nst_ref):
    x = x_ref[...]
    n = x.shape[0]
    xn = _rms(x, ln1_ref[...]).astype(BF16)
    u = u_ref[...]
    nst_ref[0:POOL_STATE - 1] = st_ref[1:POOL_STATE]
    nst_ref[POOL_STATE - 1] = u

    def win_sum(g, w, ug):
        acc = ug
        for i in range(1, w):
            acc = acc + st_ref[POOL_STATE - i, :, g * POOL_GC:(g + 1) * POOL_GC]
        return acc

    pooled = _pool_mix(u, win_sum, lambda w: jnp.float32(min(w, PAST_LEN + 1)), gw_ref, pscale_ref)
    merged = jax.nn.sigmoid(_dot(xn, w_g_ref[:, 0:D_MODEL])) * _dot(pooled.astype(BF16), w_pb_ref[...])
    ab = jnp.zeros((n, D_MODEL), F32)
    for r in range(N_HEADS):
        kh = r // GROUP
        parts = []
        if kh > 0:
            parts.append(jnp.zeros((kh * HEAD_DIM, D_MODEL), BF16))
        parts.append(w_ab_ref[r * HEAD_DIM:(r + 1) * HEAD_DIM, :])
        if kh < N_KV_HEADS - 1:
            parts.append(jnp.zeros(((N_KV_HEADS - 1 - kh) * HEAD_DIM, D_MODEL), BF16))
        o_r = jnp.concatenate([o_ref[c, pl.ds(r, n, stride=N_HEADS), :] for c in range(KV_W // LANES)], axis=1)
        ab = ab + _dot(o_r.astype(BF16), jnp.concatenate(parts, axis=0))
    merged = merged + jax.nn.sigmoid(_dot(xn, w_g_ref[:, D_MODEL:2 * D_MODEL])) * ab
    h_ref[...] = x + _dot(merged.astype(BF16), w_out_ref[...])


def _sample_post(x, u, st, o, ln1, w_in, gw, pscale, w_pb, w_ab, w_out):
    n, d = x.shape
    return pl.pallas_call(
        _sample_post_kernel,
        grid=(1,),
        in_specs=[
            _const_spec((n, d)),
            _const_spec((n, POOL_WIDTH)),
            _const_spec(st.shape),
            _const_spec((KV_W // LANES, n * N_HEADS, LANES)),
            _const_spec((1, d)),
            pl.BlockSpec((d, 2 * D_MODEL), lambda i: (0, 1), pipeline_mode=pl.Buffered(1)),
            _const_spec(gw.shape),
            _const_spec((1, POOL_WIDTH)),
            _const_spec(w_pb.shape),
            _const_spec(w_ab.shape),
            _const_spec(w_out.shape),
        ],
        out_specs=[pl.BlockSpec((n, d), lambda i: (0, 0)), pl.BlockSpec(st.shape, lambda i: (0, 0, 0))],
        out_shape=[jax.ShapeDtypeStruct((n, d), F32), jax.ShapeDtypeStruct(st.shape, F32)],
        compiler_params=pltpu.CompilerParams(dimension_semantics=("arbitrary",), vmem_limit_bytes=VMEM_LIMIT),
        name="sample_post",
    )(x, u, st, o, ln1, w_in, gw, pscale, w_pb, w_ab, w_out)


def _rope_tables(first_pos, n):
    half = ROT_DIMS // 2
    inv = ROPE_THETA ** (-(np.arange(0, ROT_DIMS, 2, dtype=np.float64) / ROT_DIMS))
    ang = np.arange(first_pos, first_pos + n, dtype=np.float64)[:, None] * inv[None, :]
    cos, sin = np.cos(ang), np.sin(ang)
    rest = HEAD_DIM - 2 * half
    c64 = np.concatenate([cos, cos, np.ones((n, rest))], axis=1)
    s64 = np.concatenate([-sin, sin, np.zeros((n, rest))], axis=1)
    reps = LANES // HEAD_DIM
    return jnp.asarray(np.tile(c64, (1, reps)), F32), jnp.asarray(np.tile(s64, (1, reps)), F32)


def kernel(x_prompt, x_sample, p_prompt, p_sample, cache_k, cache_v, state_pool, ln1, w_in, pool_group_w, pool_scale,
           attn_sinks, w_pool_branch, w_attn_branch, w_out, ln2, w_ffn_in, w_ffn_out, w_ple_proj, ple_norm,
           w_ple_gate, final_norm):
    depth = ln1.shape[0]
    b, s, d = x_prompt.shape
    bd, t_dec, _ = x_sample.shape
    w_cache = cache_k.shape[2]
    assert depth == 1 and t_dec == 1 and w_cache == WINDOW and s % BLOCK == 0 and d == D_MODEL
    tm = 512
    assert s % tm == 0

    cos_p, sg_p = _rope_tables(0, s)
    cos_s, sg_s = _rope_tables(PAST_LEN, t_dec)

    hp = x_prompt
    hs = x_sample.reshape(bd, d)
    row = lambda a: a.reshape(1, -1)
    nkp, nvp, npp, nks, nvs, nps = [], [], [], [], [], []
    for i in range(depth):
        wi = w_in[i].astype(BF16)
        gw = pool_group_w[i].astype(BF16)
        wpb = w_pool_branch[i].astype(BF16)
        wab = w_attn_branch[i].astype(BF16)
        wo = w_out[i].astype(BF16)
        h1, kp, vp, pp, w1, w2, wpp, wpg = _prompt_mixer(
            hp, cos_p, sg_p, attn_sinks[i], row(ln1[i]), wi, gw, row(pool_scale[i]), wpb, wab, wo,
            (w_ffn_in[i], w_ffn_out[i], w_ple_proj[i], w_ple_gate[i]), tm)
        ffn_args = (row(ln2[i]), w1, w2, wpp, row(ple_norm[i]), wpg)
        hp = _ffn(h1.reshape(b * s, d), p_prompt[i].reshape(b * s, PLE_DIM), *ffn_args, row(final_norm),
                  tm).reshape(b, s, d)
        from_fm_p = lambda c: jnp.transpose(c.reshape(b, N_KV_HEADS, HEAD_DIM, w_cache), (0, 3, 1, 2))
        nkp.append(from_fm_p(kp))
        nvp.append(from_fm_p(vp))
        npp.append(pp)

        to_fm = lambda c: jnp.transpose(c, (0, 2, 3, 1)).reshape(bd, KV_W, w_cache)
        from_fm = lambda c: jnp.transpose(c.reshape(bd, N_KV_HEADS, HEAD_DIM, w_cache), (0, 3, 1, 2))
        qe, knt, vnt, un = _sample_pre(hs, cos_s, sg_s, row(ln1[i]), wi)
        o, nkt, nvt = _sample_attn(qe, knt, vnt, to_fm(cache_k[i]), to_fm(cache_v[i]),
                                   attn_sinks[i].reshape(N_HEADS, 1), 8)
        h1s, nst = _sample_post(hs, un, jnp.transpose(state_pool[i], (1, 0, 2)), o, row(ln1[i]), wi, gw,
                                row(pool_scale[i]), wpb, wab, wo)
        hs = _ffn(h1s, p_sample[i].reshape(bd * t_dec, PLE_DIM), *ffn_args, row(final_norm), bd)
        nks.append(from_fm(nkt))
        nvs.append(from_fm(nvt))
        nps.append(jnp.transpose(nst, (1, 0, 2)))

    return (hp, hs.reshape(bd, t_dec, d), jnp.stack(nkp), jnp.stack(nvp), jnp.stack(npp),
            jnp.stack(nks), jnp.stack(nvs), jnp.stack(nps))
```

```python
import functools

import jax
import jax.numpy as jnp
import numpy as np
from jax import lax
from jax.experimental import pallas as pl
from jax.experimental.pallas import tpu as pltpu

D_MODEL = 1024
HEAD_DIM = 64
N_HEADS = D_MODEL // HEAD_DIM
N_KV_HEADS = N_HEADS // 4
GROUP = N_HEADS // N_KV_HEADS
ROT_DIMS = HEAD_DIM // 4
ROPE_THETA = 500000.0
WINDOW = 128
BLOCK = 128
POOL_WIDTH = D_MODEL // 2
POOL_WINDOWS = (2, 4, 8, 16)
POOL_GC = POOL_WIDTH // len(POOL_WINDOWS)
POOL_STATE = max(POOL_WINDOWS) - 1
FFN_HIDDEN = -(-8 * D_MODEL // (3 * 256)) * 256
PLE_DIM = 256
EPS = 1e-6
NEG_INF = -1e30
PAST_LEN = 16384

Q_W = N_HEADS * HEAD_DIM
KV_W = N_KV_HEADS * HEAD_DIM
C_U, C_Q, C_K, C_V, C_GP, C_GA, C_END = 0, POOL_WIDTH, POOL_WIDTH + Q_W, POOL_WIDTH + Q_W + KV_W, \
    POOL_WIDTH + Q_W + 2 * KV_W, POOL_WIDTH + Q_W + 2 * KV_W + D_MODEL, POOL_WIDTH + Q_W + 2 * KV_W + 2 * D_MODEL

LANES = 128
S_AHEAD = 16
S_SLOTS = 16
GATE_COLS = 256
U_HALO = 24
SUBLANES = 8
VMEM_LIMIT = 56 * 1024 * 1024

BF16 = jnp.bfloat16
F32 = jnp.float32


def _dot(a, b):
    return jnp.dot(a, b, preferred_element_type=F32)


def _dot_nt(a, b):
    return lax.dot_general(a, b, (((1,), (1,)), ((), ())), preferred_element_type=F32)


def _rms(x, g):
    y = x * lax.rsqrt(jnp.mean(x * x, axis=-1, keepdims=True) + EPS)
    return y * g


def _rope(x, cos, sg, lo8):
    outs = []
    for c in range(x.shape[1] // LANES):
        xc = x[:, c * LANES:(c + 1) * LANES]
        partner = jnp.where(lo8, pltpu.roll(xc, LANES - ROT_DIMS // 2, 1), pltpu.roll(xc, ROT_DIMS // 2, 1))
        outs.append(xc * cos + partner * sg)
    return jnp.concatenate(outs, axis=1)


def _lane_masks():
    lane = lax.broadcasted_iota(jnp.int32, (1, LANES), 1)
    lo8 = (lane % HEAD_DIM) < (ROT_DIMS // 2)
    lo64 = lane < HEAD_DIM
    return lo8, lo64


def _pool_mix(u, win_sum_fn, cnt_fn, gw_ref, pscale_ref):
    mixed = []
    for g, w in enumerate(POOL_WINDOWS):
        cols = slice(g * POOL_GC, (g + 1) * POOL_GC)
        ug = u[:, cols]
        m = win_sum_fn(g, w, ug) / cnt_fn(w) - ug
        mixed.append(_dot(m.astype(BF16), gw_ref[g]) * pscale_ref[:, cols])
    return jnp.concatenate(mixed, axis=1)


def _mixer_kernel(sinks_ref, x_ref, cos_ref, sg_ref, ln1_ref, w_in_ref, gw_ref, pscale_ref, w_pb_ref, w_ab_ref,
                  w_out_ref, *rest, tm, n_cast):
    cast_src = rest[:n_cast]
    h_ref, ko_ref, vo_ref, po_ref = rest[n_cast:n_cast + 4]
    cast_dst = rest[n_cast + 4:2 * n_cast + 4]
    kl_scr, kh_scr, vl_scr, vh_scr, u_scr, lvl_scr, attn_scr, q_scr, s_scr, g_scr = rest[2 * n_cast + 4:]
    t = pl.program_id(1)
    for src, dst in zip(cast_src, cast_dst):
        dst[...] = src[...].astype(BF16)
    lo8, lo64 = _lane_masks()

    @pl.when(t == 0)
    def _():
        for scr in (kl_scr, kh_scr, vl_scr, vh_scr):
            scr[:, 0:BLOCK, :] = jnp.zeros((N_KV_HEADS, BLOCK, LANES), BF16)
        u_scr[0:U_HALO, :] = jnp.zeros((U_HALO, POOL_WIDTH), F32)
        lvl_scr[:, 0:SUBLANES, :] = jnp.zeros((len(POOL_WINDOWS), SUBLANES, POOL_GC), F32)

    x = x_ref[0]
    xn = _rms(x, ln1_ref[...]).astype(BF16)
    cos = cos_ref[...]
    sg = sg_ref[...]

    k = _rope(_dot(xn, w_in_ref[:, C_K:C_V]), cos, sg, lo8)
    v = _dot(xn, w_in_ref[:, C_V:C_GP])
    q = (_rope(_dot(xn, w_in_ref[:, C_Q:C_K]), cos, sg, lo8) * (HEAD_DIM ** -0.5)).astype(BF16)
    q_scr[...] = q
    u = _dot(xn, w_in_ref[:, C_U:C_Q])
    ko_ref[0] = k[tm - WINDOW:, :].T
    vo_ref[0] = v[tm - WINDOW:, :].T

    zero = jnp.zeros((tm, LANES), F32)
    for src, lo_scr, hi_scr in ((k, kl_scr, kh_scr), (v, vl_scr, vh_scr)):
        for p in range(KV_W // LANES):
            xp = src[:, p * LANES:(p + 1) * LANES]
            xs = pltpu.roll(xp, HEAD_DIM, 1)
            lo_scr[2 * p, BLOCK:, :] = jnp.where(lo64, xp, zero).astype(BF16)
            hi_scr[2 * p, BLOCK:, :] = jnp.where(lo64, zero, xs).astype(BF16)
            lo_scr[2 * p + 1, BLOCK:, :] = jnp.where(lo64, xs, zero).astype(BF16)
            hi_scr[2 * p + 1, BLOCK:, :] = jnp.where(lo64, zero, xp).astype(BF16)

    qi = lax.broadcasted_iota(jnp.int32, (BLOCK, BLOCK), 0)
    ci = lax.broadcasted_iota(jnp.int32, (BLOCK, BLOCK), 1)
    from_prev = ci > qi
    bias0 = jnp.where(jnp.logical_and(t == 0, from_prev), NEG_INF, 0.0).astype(F32)

    ones_lo = jnp.broadcast_to(jnp.where(lo64, 1.0, 0.0).astype(BF16), (2 * BLOCK, LANES))
    ones_hi = jnp.broadcast_to(jnp.where(lo64, 0.0, 1.0).astype(BF16), (2 * BLOCK, LANES))

    units = [(j, kh) for j in range(tm // BLOCK) for kh in range(N_KV_HEADS)]

    def scores(i):
        j, kh = units[i]
        rows = slice(j * BLOCK, (j + 1) * BLOCK)
        win = slice(j * BLOCK, (j + 2) * BLOCK)
        qq = jnp.concatenate([q_scr[rows, (2 * kh + a) * LANES:(2 * kh + a + 1) * LANES] for a in range(2)], axis=0)
        kcat = jnp.concatenate([kl_scr[kh, win, :], kh_scr[kh, win, :]], axis=0)
        s = _dot_nt(qq, kcat)
        for a in range(2):
            for half in range(2):
                sa = s[a * BLOCK:(a + 1) * BLOCK, half * 2 * BLOCK:(half + 1) * 2 * BLOCK]
                folded = jnp.where(from_prev, sa[:, :BLOCK], sa[:, BLOCK:])
                if j == 0:
                    folded = folded + bias0
                s_scr[i % S_SLOTS, a * BLOCK:(a + 1) * BLOCK, half * BLOCK:(half + 1) * BLOCK] = folded

    units_per_gate = len(units) * GATE_COLS // (2 * D_MODEL)
    for i in range(S_AHEAD):
        scores(i)
    for i, (j, kh) in enumerate(units):
        if i + S_AHEAD < len(units):
            scores(i + S_AHEAD)
        rows = slice(j * BLOCK, (j + 1) * BLOCK)
        win = slice(j * BLOCK, (j + 2) * BLOCK)
        vcat = jnp.concatenate([
            jnp.concatenate([vl_scr[kh, win, :], ones_lo], axis=1),
            jnp.concatenate([vh_scr[kh, win, :], ones_hi], axis=1)], axis=0)
        ps, sink_terms = [], []
        for a in range(2):
            es, st = [], []
            for half in range(2):
                sh = s_scr[i % S_SLOTS, a * BLOCK:(a + 1) * BLOCK, half * BLOCK:(half + 1) * BLOCK]
                sink = sinks_ref[4 * kh + 2 * a + half]
                m = jnp.maximum(jnp.max(sh, axis=1, keepdims=True), sink)
                e = jnp.exp(sh - m)
                es.append(jnp.where(from_prev, e, 0.0).astype(BF16))
                es.append(jnp.where(from_prev, 0.0, e).astype(BF16))
                st.append(jnp.exp(sink - m))
            ps.append(jnp.concatenate(es, axis=1))
            sink_terms.append(jnp.where(lo64, st[0], st[1]))
        o = _dot(jnp.concatenate(ps, axis=0), vcat)
        for a in range(2):
            oa = o[a * BLOCK:(a + 1) * BLOCK]
            attn_scr[rows, (2 * kh + a) * LANES:(2 * kh + a + 1) * LANES] = (
                oa[:, :LANES] / (oa[:, LANES:] + sink_terms[a])).astype(BF16)
        if i % units_per_gate == units_per_gate - 1:
            gcols = slice((i // units_per_gate) * GATE_COLS, (i // units_per_gate + 1) * GATE_COLS)
            g_scr[:, gcols] = jax.nn.sigmoid(_dot(xn, w_in_ref[:, C_GP + gcols.start:C_GP + gcols.stop]))

    u_scr[U_HALO:, :] = u
    po_ref[0] = u_scr[U_HALO + tm - POOL_STATE:U_HALO + tm, :]
    pos = t * tm + lax.broadcasted_iota(jnp.int32, (tm, 1), 0)

    def win_sum(g, w, ug):
        cols = slice(g * POOL_GC, (g + 1) * POOL_GC)
        n = U_HALO - SUBLANES + tm
        src, span = u_scr, 1
        while span < w:
            lvl = src[SUBLANES:SUBLANES + n, cols] + src[SUBLANES - span:SUBLANES - span + n, cols]
            span *= 2
            if span < w:
                lvl_scr[g, SUBLANES:SUBLANES + n, :] = lvl
                src, cols = lvl_scr.at[g], slice(None)
        return lvl[U_HALO - SUBLANES:, :]

    pooled = _pool_mix(u, win_sum, lambda w: jnp.minimum(w, pos + 1).astype(F32), gw_ref, pscale_ref)

    for scr in (kl_scr, kh_scr, vl_scr, vh_scr):
        scr[:, 0:BLOCK, :] = scr[:, tm:tm + BLOCK, :]
    u_scr[0:U_HALO, :] = u_scr[tm:tm + U_HALO, :]

    merged = g_scr[:, 0:D_MODEL] * _dot(pooled.astype(BF16), w_pb_ref[...])
    merged = merged + g_scr[:, D_MODEL:2 * D_MODEL] * _dot(attn_scr[...], w_ab_ref[...])
    h_ref[0] = x + _dot(merged.astype(BF16), w_out_ref[...])


def _const_spec(shape):
    nd = len(shape)
    return pl.BlockSpec(shape, lambda *_: (0,) * nd, pipeline_mode=pl.Buffered(1))


def _cast_block_rows(rows, steps):
    br = 2 * SUBLANES
    while rows % br or rows // br > steps:
        br *= 2
    return br


def _prompt_mixer(x, cos, sg, sinks, ln1, w_in, gw, pscale, w_pb, w_ab, w_out, to_cast, tm):
    b, s, d = x.shape
    nt = s // tm
    cast_specs = []
    for w in to_cast:
        br = _cast_block_rows(w.shape[0], b * nt)
        last = w.shape[0] // br - 1
        cast_specs.append(pl.BlockSpec(
            (br, w.shape[1]), lambda bi, ti, *_, last=last: (jnp.minimum(bi * nt + ti, last), 0)))
    grid_spec = pltpu.PrefetchScalarGridSpec(
        num_scalar_prefetch=1,
        grid=(b, nt),
        in_specs=[
            pl.BlockSpec((1, tm, d), lambda bi, ti, *_: (bi, ti, 0)),
            pl.BlockSpec((tm, LANES), lambda bi, ti, *_: (ti, 0)),
            pl.BlockSpec((tm, LANES), lambda bi, ti, *_: (ti, 0)),
            _const_spec((1, d)),
            _const_spec(w_in.shape),
            _const_spec(gw.shape),
            _const_spec((1, POOL_WIDTH)),
            _const_spec(w_pb.shape),
            _const_spec(w_ab.shape),
            _const_spec(w_out.shape),
        ] + cast_specs,
        out_specs=[
            pl.BlockSpec((1, tm, d), lambda bi, ti, *_: (bi, ti, 0)),
            pl.BlockSpec((1, KV_W, WINDOW), lambda bi, ti, *_: (bi, 0, 0)),
            pl.BlockSpec((1, KV_W, WINDOW), lambda bi, ti, *_: (bi, 0, 0)),
            pl.BlockSpec((1, POOL_STATE, POOL_WIDTH), lambda bi, ti, *_: (bi, 0, 0)),
        ] + cast_specs,
        scratch_shapes=[
            pltpu.VMEM((N_KV_HEADS, BLOCK + tm, LANES), BF16),
            pltpu.VMEM((N_KV_HEADS, BLOCK + tm, LANES), BF16),
            pltpu.VMEM((N_KV_HEADS, BLOCK + tm, LANES), BF16),
            pltpu.VMEM((N_KV_HEADS, BLOCK + tm, LANES), BF16),
            pltpu.VMEM((U_HALO + tm, POOL_WIDTH), F32),
            pltpu.VMEM((len(POOL_WINDOWS), U_HALO + tm, POOL_GC), F32),
            pltpu.VMEM((tm, Q_W), BF16),
            pltpu.VMEM((tm, Q_W), BF16),
            pltpu.VMEM((S_SLOTS, 2 * BLOCK, 2 * BLOCK), F32),
            pltpu.VMEM((tm, 2 * D_MODEL), F32),
        ],
    )
    return pl.pallas_call(
        functools.partial(_mixer_kernel, tm=tm, n_cast=len(to_cast)),
        grid_spec=grid_spec,
        out_shape=[
            jax.ShapeDtypeStruct((b, s, d), F32),
            jax.ShapeDtypeStruct((b, KV_W, WINDOW), F32),
            jax.ShapeDtypeStruct((b, KV_W, WINDOW), F32),
            jax.ShapeDtypeStruct((b, POOL_STATE, POOL_WIDTH), F32),
        ] + [jax.ShapeDtypeStruct(w.shape, BF16) for w in to_cast],
        compiler_params=pltpu.CompilerParams(
            dimension_semantics=("arbitrary", "arbitrary"), vmem_limit_bytes=VMEM_LIMIT),
        name="prompt_mixer",
    )(sinks, x, cos, sg, ln1, w_in, gw, pscale, w_pb, w_ab, w_out, *to_cast)


FFN_CHUNKS = ((0, 1024), (1024, 2048), (2048, FFN_HIDDEN))


def _ffn_kernel(h_ref, p_ref, ln2_ref, w1_ref, w2_ref, w_pp_ref, pn_ref, w_pg_ref, fn_ref, y_ref, *, tm):
    def tile_stages(r0):
        rows = slice(r0, r0 + tm)
        st = {}

        def up_proj(c):
            lo, hi = FFN_CHUNKS[c]
            if c == 0:
                st['h'] = h_ref[rows, :]
                st['hn'] = _rms(st['h'], ln2_ref[...]).astype(BF16)
                st['acc'] = st['h']
            st['gate', c] = _dot(st['hn'], w1_ref[:, lo:hi])
            st['up', c] = _dot(st['hn'], w1_ref[:, FFN_HIDDEN + lo:FFN_HIDDEN + hi])
            if c == len(FFN_CHUNKS) - 1:
                st['e'] = _rms(_dot(p_ref[rows, :].astype(BF16), w_pp_ref[...]), pn_ref[...])

        def down_proj(c):
            lo, hi = FFN_CHUNKS[c]
            gate = st.pop(('gate', c))
            act = (gate * jax.nn.sigmoid(gate) * st.pop(('up', c))).astype(BF16)
            st['acc'] = st['acc'] + _dot(act, w2_ref[lo:hi, :])

        def ple_gate():
            st['g'] = _dot(st['acc'].astype(BF16), w_pg_ref[...])

        def finish():
            h3 = st['acc'] + jax.nn.sigmoid(st['g']) * st['e']
            y_ref[rows, :] = _rms(h3, fn_ref[...])

        n = len(FFN_CHUNKS)
        steps = [functools.partial(up_proj, 0)]
        for c in range(1, n):
            steps += [functools.partial(up_proj, c), functools.partial(down_proj, c - 1)]
        return steps + [functools.partial(down_proj, n - 1), ple_gate, finish]

    tail = 3
    order = []
    for r0 in range(0, h_ref.shape[0], tm):
        steps = tile_stages(r0)
        held, order = order[len(order) - tail:] if order else [], order[:len(order) - tail] if order else []
        for k in range(max(len(held), tail)):
            order += steps[k:k + 1] + held[k:k + 1]
        order += steps[tail:]
    for step in order:
        step()


def _ffn(h, p, ln2, w1, w2, w_pp, pn, w_pg, fn, tm, nsub):
    n, d = h.shape
    blk = tm * nsub
    return pl.pallas_call(
        functools.partial(_ffn_kernel, tm=tm),
        grid=(n // blk,),
        in_specs=[
            pl.BlockSpec((blk, d), lambda i: (i, 0)),
            pl.BlockSpec((blk, PLE_DIM), lambda i: (i, 0)),
            _const_spec((1, d)),
            _const_spec(w1.shape),
            _const_spec(w2.shape),
            _const_spec(w_pp.shape),
            _const_spec((1, d)),
            _const_spec(w_pg.shape),
            _const_spec((1, d)),
        ],
        out_specs=pl.BlockSpec((blk, d), lambda i: (i, 0)),
        out_shape=jax.ShapeDtypeStruct((n, d), F32),
        compiler_params=pltpu.CompilerParams(dimension_semantics=("arbitrary",), vmem_limit_bytes=VMEM_LIMIT),
        name="ffn_ple_norm",
    )(h, p, ln2, w1, w2, w_pp, pn, w_pg, fn)


def _sample_pre_kernel(x_ref, cos_ref, sg_ref, ln1_ref, w_in_ref, qe_ref, knt_ref, vnt_ref, u_ref):
    n = x_ref.shape[0]
    lo8, _ = _lane_masks()
    xn = _rms(x_ref[...], ln1_ref[...]).astype(BF16)
    cos = cos_ref[...]
    sg = sg_ref[...]
    u_ref[...] = _dot(xn, w_in_ref[:, C_U:C_Q])
    q = (_rope(_dot(xn, w_in_ref[:, C_Q:C_K]), cos, sg, lo8) * (HEAD_DIM ** -0.5)).astype(BF16)
    knt_ref[...] = _rope(_dot(xn, w_in_ref[:, C_K:C_V]), cos, sg, lo8).T
    vnt_ref[...] = _dot(xn, w_in_ref[:, C_V:C_GP]).T
    ii = lax.broadcasted_iota(jnp.int32, (Q_W, KV_W), 0)
    jj = lax.broadcasted_iota(jnp.int32, (Q_W, KV_W), 1)
    for r in range(N_HEADS):
        kh = r // GROUP
        sel = ((ii - r * HEAD_DIM) == (jj - kh * HEAD_DIM)) & (jj >= kh * HEAD_DIM) & (jj < (kh + 1) * HEAD_DIM)
        qr = _dot(q, jnp.where(sel, 1.0, 0.0).astype(BF16))
        for c in range(KV_W // LANES):
            qe_ref[c, pl.ds(r, n, stride=N_HEADS), :] = qr[:, c * LANES:(c + 1) * LANES]


def _sample_pre(x, cos, sg, ln1, w_in):
    n, d = x.shape
    return pl.pallas_call(
        _sample_pre_kernel,
        grid=(1,),
        in_specs=[
            _const_spec((n, d)),
            _const_spec((1, LANES)),
            _const_spec((1, LANES)),
            _const_spec((1, d)),
            pl.BlockSpec((d, C_GP), lambda i: (0, 0), pipeline_mode=pl.Buffered(1)),
        ],
        out_specs=[
            pl.BlockSpec((KV_W // LANES, n * N_HEADS, LANES), lambda i: (0, 0, 0)),
            pl.BlockSpec((KV_W, n), lambda i: (0, 0)),
            pl.BlockSpec((KV_W, n), lambda i: (0, 0)),
            pl.BlockSpec((n, POOL_WIDTH), lambda i: (0, 0)),
        ],
        out_shape=[
            jax.ShapeDtypeStruct((KV_W // LANES, n * N_HEADS, LANES), F32),
            jax.ShapeDtypeStruct((KV_W, n), F32),
            jax.ShapeDtypeStruct((KV_W, n), F32),
            jax.ShapeDtypeStruct((n, POOL_WIDTH), F32),
        ],
        compiler_params=pltpu.CompilerParams(dimension_semantics=("arbitrary",), vmem_limit_bytes=VMEM_LIMIT),
        name="sample_pre",
    )(x, cos, sg, ln1, w_in)


def _sample_attn_kernel(qe_ref, knt_ref, vnt_ref, ckt_ref, cvt_ref, sink_ref, o_ref, nkt_ref, nvt_ref, *, bb):
    n = knt_ref.shape[1]
    w_cache = ckt_ref.shape[2]
    i = pl.program_id(0)
    newest = lax.broadcasted_iota(jnp.int32, (1, w_cache), 1) == w_cache - 1
    shift = lax.rem(n - i * bb, n)
    kcols = pltpu.roll(knt_ref[...], shift, 1)
    vcols = pltpu.roll(vnt_ref[...], shift, 1)
    sink = sink_ref[...]
    scores = []
    for bl in range(bb):
        newk = jnp.where(newest, kcols[:, bl:bl + 1], pltpu.roll(ckt_ref[bl], w_cache - 1, 1))
        nkt_ref[bl] = newk
        nvt_ref[bl] = jnp.where(newest, vcols[:, bl:bl + 1], pltpu.roll(cvt_ref[bl], w_cache - 1, 1))
        rows = slice(bl * N_HEADS, (bl + 1) * N_HEADS)
        qb = jnp.concatenate([qe_ref[c, rows, :] for c in range(KV_W // LANES)], axis=1)
        scores.append(_dot(qb.astype(BF16), newk.astype(BF16)))
    probs, denoms = [], []
    for s in scores:
        m = jnp.maximum(jnp.max(s, axis=-1, keepdims=True), sink)
        e = jnp.exp(s - m)
        probs.append(e.astype(BF16))
        denoms.append(jnp.sum(e, axis=-1, keepdims=True) + jnp.exp(sink - m))
    for bl in range(bb):
        rows = slice(bl * N_HEADS, (bl + 1) * N_HEADS)
        o = _dot_nt(probs[bl], nvt_ref[bl].astype(BF16)) / denoms[bl]
        for c in range(KV_W // LANES):
            o_ref[c, rows, :] = o[:, c * LANES:(c + 1) * LANES]


def _sample_attn(qe, knt, vnt, ckt, cvt, sink, bb):
    n, _, w_cache = ckt.shape
    cache_spec = pl.BlockSpec((bb, KV_W, w_cache), lambda i: (i, 0, 0))
    head_spec = pl.BlockSpec((KV_W // LANES, bb * N_HEADS, LANES), lambda i: (0, i, 0))
    return pl.pallas_call(
        functools.partial(_sample_attn_kernel, bb=bb),
        grid=(n // bb,),
        in_specs=[head_spec, _const_spec((KV_W, n)), _const_spec((KV_W, n)), cache_spec, cache_spec,
                  _const_spec((N_HEADS, 1))],
        out_specs=[head_spec, cache_spec, cache_spec],
        out_shape=[
            jax.ShapeDtypeStruct((KV_W // LANES, n * N_HEADS, LANES), F32),
            jax.ShapeDtypeStruct((n, KV_W, w_cache), F32),
            jax.ShapeDtypeStruct((n, KV_W, w_cache), F32),
        ],
        compiler_params=pltpu.CompilerParams(dimension_semantics=("arbitrary",), vmem_limit_bytes=VMEM_LIMIT),
        name="sample_attn",
    )(qe, knt, vnt, ckt, cvt, sink)


def _sample_post_kernel(x_ref, u_ref, st_ref, o_ref, ln1_ref, w_g_ref, gw_ref, pscale_ref, w_pb_ref, w_ab_ref,
                        w_out_ref, h_ref, nst_ref):
    x = x_ref[...]
    n = x.shape[0]
    xn = _rms(x, ln1_ref[...]).astype(BF16)
    u = u_ref[...]
    nst_ref[0:POOL_STATE - 1] = st_ref[1:POOL_STATE]
    nst_ref[POOL_STATE - 1] = u

    def win_sum(g, w, ug):
        acc = ug
        for i in range(1, w):
            acc = acc + st_ref[POOL_STATE - i, :, g * POOL_GC:(g + 1) * POOL_GC]
        return acc

    pooled = _pool_mix(u, win_sum, lambda w: jnp.float32(min(w, PAST_LEN + 1)), gw_ref, pscale_ref)
    merged = jax.nn.sigmoid(_dot(xn, w_g_ref[:, 0:D_MODEL])) * _dot(pooled.astype(BF16), w_pb_ref[...])
    ab = jnp.zeros((n, D_MODEL), F32)
    for r in range(N_HEADS):
        kh = r // GROUP
        parts = []
        if kh > 0:
            parts.append(jnp.zeros((kh * HEAD_DIM, D_MODEL), BF16))
        parts.append(w_ab_ref[r * HEAD_DIM:(r + 1) * HEAD_DIM, :])
        if kh < N_KV_HEADS - 1:
            parts.append(jnp.zeros(((N_KV_HEADS - 1 - kh) * HEAD_DIM, D_MODEL), BF16))
        o_r = jnp.concatenate([o_ref[c, pl.ds(r, n, stride=N_HEADS), :] for c in range(KV_W // LANES)], axis=1)
        ab = ab + _dot(o_r.astype(BF16), jnp.concatenate(parts, axis=0))
    merged = merged + jax.nn.sigmoid(_dot(xn, w_g_ref[:, D_MODEL:2 * D_MODEL])) * ab
    h_ref[...] = x + _dot(merged.astype(BF16), w_out_ref[...])


def _sample_post(x, u, st, o, ln1, w_in, gw, pscale, w_pb, w_ab, w_out):
    n, d = x.shape
    return pl.pallas_call(
        _sample_post_kernel,
        grid=(1,),
        in_specs=[
            _const_spec((n, d)),
            _const_spec((n, POOL_WIDTH)),
            _const_spec(st.shape),
            _const_spec((KV_W // LANES, n * N_HEADS, LANES)),
            _const_spec((1, d)),
            pl.BlockSpec((d, 2 * D_MODEL), lambda i: (0, 1), pipeline_mode=pl.Buffered(1)),
            _const_spec(gw.shape),
            _const_spec((1, POOL_WIDTH)),
            _const_spec(w_pb.shape),
            _const_spec(w_ab.shape),
            _const_spec(w_out.shape),
        ],
        out_specs=[pl.BlockSpec((n, d), lambda i: (0, 0)), pl.BlockSpec(st.shape, lambda i: (0, 0, 0))],
        out_shape=[jax.ShapeDtypeStruct((n, d), F32), jax.ShapeDtypeStruct(st.shape, F32)],
        compiler_params=pltpu.CompilerParams(dimension_semantics=("arbitrary",), vmem_limit_bytes=VMEM_LIMIT),
        name="sample_post",
    )(x, u, st, o, ln1, w_in, gw, pscale, w_pb, w_ab, w_out)


def _rope_tables(first_pos, n):
    half = ROT_DIMS // 2
    inv = ROPE_THETA ** (-(np.arange(0, ROT_DIMS, 2, dtype=np.float64) / ROT_DIMS))
    ang = np.arange(first_pos, first_pos + n, dtype=np.float64)[:, None] * inv[None, :]
    cos, sin = np.cos(ang), np.sin(ang)
    rest = HEAD_DIM - 2 * half
    c64 = np.concatenate([cos, cos, np.ones((n, rest))], axis=1)
    s64 = np.concatenate([-sin, sin, np.zeros((n, rest))], axis=1)
    reps = LANES // HEAD_DIM
    return jnp.asarray(np.tile(c64, (1, reps)), F32), jnp.asarray(np.tile(s64, (1, reps)), F32)


def kernel(x_prompt, x_sample, p_prompt, p_sample, cache_k, cache_v, state_pool, ln1, w_in, pool_group_w, pool_scale,
           attn_sinks, w_pool_branch, w_attn_branch, w_out, ln2, w_ffn_in, w_ffn_out, w_ple_proj, ple_norm,
           w_ple_gate, final_norm):
    depth = ln1.shape[0]
    b, s, d = x_prompt.shape
    bd, t_dec, _ = x_sample.shape
    w_cache = cache_k.shape[2]
    assert depth == 1 and t_dec == 1 and w_cache == WINDOW and s % BLOCK == 0 and d == D_MODEL
    tm = 512
    assert s % tm == 0

    cos_p, sg_p = _rope_tables(0, s)
    cos_s, sg_s = _rope_tables(PAST_LEN, t_dec)

    hp = x_prompt
    hs = x_sample.reshape(bd, d)
    row = lambda a: a.reshape(1, -1)
    nkp, nvp, npp, nks, nvs, nps = [], [], [], [], [], []
    for i in range(depth):
        wi = w_in[i].astype(BF16)
        gw = pool_group_w[i].astype(BF16)
        wpb = w_pool_branch[i].astype(BF16)
        wab = w_attn_branch[i].astype(BF16)
        wo = w_out[i].astype(BF16)
        h1, kp, vp, pp, w1, w2, wpp, wpg = _prompt_mixer(
            hp, cos_p, sg_p, attn_sinks[i], row(ln1[i]), wi, gw, row(pool_scale[i]), wpb, wab, wo,
            (w_ffn_in[i], w_ffn_out[i], w_ple_proj[i], w_ple_gate[i]), tm)
        ffn_args = (row(ln2[i]), w1, w2, wpp, row(ple_norm[i]), wpg)
        hp = _ffn(h1.reshape(b * s, d), p_prompt[i].reshape(b * s, PLE_DIM), *ffn_args, row(final_norm),
                  tm, 2).reshape(b, s, d)
        from_fm_p = lambda c: jnp.transpose(c.reshape(b, N_KV_HEADS, HEAD_DIM, w_cache), (0, 3, 1, 2))
        nkp.append(from_fm_p(kp))
        nvp.append(from_fm_p(vp))
        npp.append(pp)

        to_fm = lambda c: jnp.transpose(c, (0, 2, 3, 1)).reshape(bd, KV_W, w_cache)
        from_fm = lambda c: jnp.transpose(c.reshape(bd, N_KV_HEADS, HEAD_DIM, w_cache), (0, 3, 1, 2))
        qe, knt, vnt, un = _sample_pre(hs, cos_s, sg_s, row(ln1[i]), wi)
        o, nkt, nvt = _sample_attn(qe, knt, vnt, to_fm(cache_k[i]), to_fm(cache_v[i]),
                                   attn_sinks[i].reshape(N_HEADS, 1), 8)
        h1s, nst = _sample_post(hs, un, jnp.transpose(state_pool[i], (1, 0, 2)), o, row(ln1[i]), wi, gw,
                                row(pool_scale[i]), wpb, wab, wo)
        hs = _ffn(h1s, p_sample[i].reshape(bd * t_dec, PLE_DIM), *ffn_args, row(final_norm), bd, 1)
        nks.append(from_fm(nkt))
        nvs.append(from_fm(nvt))
        nps.append(jnp.transpose(nst, (1, 0, 2)))

    return (hp, hs.reshape(bd, t_dec, d), jnp.stack(nkp), jnp.stack(nvp), jnp.stack(npp),
            jnp.stack(nks), jnp.stack(nvs), jnp.stack(nps))
```

```python
import functools

import jax
import jax.numpy as jnp
import numpy as np
from jax import lax
from jax.experimental import pallas as pl
from jax.experimental.pallas import tpu as pltpu

D_MODEL = 1024
HEAD_DIM = 64
N_HEADS = D_MODEL // HEAD_DIM
N_KV_HEADS = N_HEADS // 4
GROUP = N_HEADS // N_KV_HEADS
ROT_DIMS = HEAD_DIM // 4
ROPE_THETA = 500000.0
WINDOW = 128
BLOCK = 128
POOL_WIDTH = D_MODEL // 2
POOL_WINDOWS = (2, 4, 8, 16)
POOL_GC = POOL_WIDTH // len(POOL_WINDOWS)
POOL_STATE = max(POOL_WINDOWS) - 1
FFN_HIDDEN = -(-8 * D_MODEL // (3 * 256)) * 256
PLE_DIM = 256
EPS = 1e-6
NEG_INF = -1e30
PAST_LEN = 16384

Q_W = N_HEADS * HEAD_DIM
KV_W = N_KV_HEADS * HEAD_DIM
C_U, C_Q, C_K, C_V, C_GP, C_GA, C_END = 0, POOL_WIDTH, POOL_WIDTH + Q_W, POOL_WIDTH + Q_W + KV_W, \
    POOL_WIDTH + Q_W + 2 * KV_W, POOL_WIDTH + Q_W + 2 * KV_W + D_MODEL, POOL_WIDTH + Q_W + 2 * KV_W + 2 * D_MODEL

LANES = 128
S_AHEAD = 16
S_SLOTS = 16
GATE_COLS = 256
U_HALO = 24
SUBLANES = 8
VMEM_LIMIT = 56 * 1024 * 1024

BF16 = jnp.bfloat16
F32 = jnp.float32


def _dot(a, b):
    return jnp.dot(a, b, preferred_element_type=F32)


def _dot_nt(a, b):
    return lax.dot_general(a, b, (((1,), (1,)), ((), ())), preferred_element_type=F32)


def _rms(x, g):
    y = x * lax.rsqrt(jnp.mean(x * x, axis=-1, keepdims=True) + EPS)
    return y * g


def _rope(x, cos, sg, lo8):
    outs = []
    for c in range(x.shape[1] // LANES):
        xc = x[:, c * LANES:(c + 1) * LANES]
        partner = jnp.where(lo8, pltpu.roll(xc, LANES - ROT_DIMS // 2, 1), pltpu.roll(xc, ROT_DIMS // 2, 1))
        outs.append(xc * cos + partner * sg)
    return jnp.concatenate(outs, axis=1)


def _lane_masks():
    lane = lax.broadcasted_iota(jnp.int32, (1, LANES), 1)
    lo8 = (lane % HEAD_DIM) < (ROT_DIMS // 2)
    lo64 = lane < HEAD_DIM
    return lo8, lo64


def _pool_mix(u, win_sum_fn, cnt_fn, gw_ref, pscale_ref):
    mixed = []
    for g, w in enumerate(POOL_WINDOWS):
        cols = slice(g * POOL_GC, (g + 1) * POOL_GC)
        ug = u[:, cols]
        m = win_sum_fn(g, w, ug) / cnt_fn(w) - ug
        mixed.append(_dot(m.astype(BF16), gw_ref[g]) * pscale_ref[:, cols])
    return jnp.concatenate(mixed, axis=1)


def _mixer_kernel(sinks_ref, x_ref, cos_ref, sg_ref, ln1_ref, w_in_ref, gw_ref, pscale_ref, w_pb_ref, w_ab_ref,
                  w_out_ref, *rest, tm, n_cast):
    cast_src = rest[:n_cast]
    h_ref, ko_ref, vo_ref, po_ref = rest[n_cast:n_cast + 4]
    cast_dst = rest[n_cast + 4:2 * n_cast + 4]
    kl_scr, kh_scr, vl_scr, vh_scr, u_scr, lvl_scr, attn_scr, q_scr, s_scr, g_scr = rest[2 * n_cast + 4:]
    t = pl.program_id(1)
    for src, dst in zip(cast_src, cast_dst):
        dst[...] = src[...].astype(BF16)
    lo8, lo64 = _lane_masks()

    @pl.when(t == 0)
    def _():
        for scr in (kl_scr, kh_scr, vl_scr, vh_scr):
            scr[:, 0:BLOCK, :] = jnp.zeros((N_KV_HEADS, BLOCK, LANES), BF16)
        u_scr[0:U_HALO, :] = jnp.zeros((U_HALO, POOL_WIDTH), F32)
        lvl_scr[:, 0:SUBLANES, :] = jnp.zeros((len(POOL_WINDOWS), SUBLANES, POOL_GC), F32)

    x = x_ref[0]
    xn = _rms(x, ln1_ref[...]).astype(BF16)
    cos = cos_ref[...]
    sg = sg_ref[...]

    k = _rope(_dot(xn, w_in_ref[:, C_K:C_V]), cos, sg, lo8)
    v = _dot(xn, w_in_ref[:, C_V:C_GP])
    q = (_rope(_dot(xn, w_in_ref[:, C_Q:C_K]), cos, sg, lo8) * (HEAD_DIM ** -0.5)).astype(BF16)
    q_scr[...] = q
    u = _dot(xn, w_in_ref[:, C_U:C_Q])
    ko_ref[0] = k[tm - WINDOW:, :].T
    vo_ref[0] = v[tm - WINDOW:, :].T

    zero = jnp.zeros((tm, LANES), F32)
    for src, lo_scr, hi_scr in ((k, kl_scr, kh_scr), (v, vl_scr, vh_scr)):
        for p in range(KV_W // LANES):
            xp = src[:, p * LANES:(p + 1) * LANES]
            xs = pltpu.roll(xp, HEAD_DIM, 1)
            lo_scr[2 * p, BLOCK:, :] = jnp.where(lo64, xp, zero).astype(BF16)
            hi_scr[2 * p, BLOCK:, :] = jnp.where(lo64, zero, xs).astype(BF16)
            lo_scr[2 * p + 1, BLOCK:, :] = jnp.where(lo64, xs, zero).astype(BF16)
            hi_scr[2 * p + 1, BLOCK:, :] = jnp.where(lo64, zero, xp).astype(BF16)

    qi = lax.broadcasted_iota(jnp.int32, (BLOCK, BLOCK), 0)
    ci = lax.broadcasted_iota(jnp.int32, (BLOCK, BLOCK), 1)
    from_prev = ci > qi
    bias0 = jnp.where(jnp.logical_and(t == 0, from_prev), NEG_INF, 0.0).astype(F32)

    ones_lo = jnp.broadcast_to(jnp.where(lo64, 1.0, 0.0).astype(BF16), (2 * BLOCK, LANES))
    ones_hi = jnp.broadcast_to(jnp.where(lo64, 0.0, 1.0).astype(BF16), (2 * BLOCK, LANES))

    units = [(j, kh) for j in range(tm // BLOCK) for kh in range(N_KV_HEADS)]

    def scores(i):
        j, kh = units[i]
        rows = slice(j * BLOCK, (j + 1) * BLOCK)
        win = slice(j * BLOCK, (j + 2) * BLOCK)
        qq = jnp.concatenate([q_scr[rows, (2 * kh + a) * LANES:(2 * kh + a + 1) * LANES] for a in range(2)], axis=0)
        kcat = jnp.concatenate([kl_scr[kh, win, :], kh_scr[kh, win, :]], axis=0)
        s = _dot_nt(qq, kcat)
        for a in range(2):
            for half in range(2):
                sa = s[a * BLOCK:(a + 1) * BLOCK, half * 2 * BLOCK:(half + 1) * 2 * BLOCK]
                folded = jnp.where(from_prev, sa[:, :BLOCK], sa[:, BLOCK:])
                if j == 0:
                    folded = folded + bias0
                s_scr[i % S_SLOTS, a * BLOCK:(a + 1) * BLOCK, half * BLOCK:(half + 1) * BLOCK] = folded

    units_per_gate = len(units) * GATE_COLS // (2 * D_MODEL)
    for i in range(S_AHEAD):
        scores(i)
    for i, (j, kh) in enumerate(units):
        if i + S_AHEAD < len(units):
            scores(i + S_AHEAD)
        rows = slice(j * BLOCK, (j + 1) * BLOCK)
        win = slice(j * BLOCK, (j + 2) * BLOCK)
        vcat = jnp.concatenate([
            jnp.concatenate([vl_scr[kh, win, :], ones_lo], axis=1),
            jnp.concatenate([vh_scr[kh, win, :], ones_hi], axis=1)], axis=0)
        ps, sink_terms = [], []
        for a in range(2):
            es, st = [], []
            for half in range(2):
                sh = s_scr[i % S_SLOTS, a * BLOCK:(a + 1) * BLOCK, half * BLOCK:(half + 1) * BLOCK]
                sink = sinks_ref[4 * kh + 2 * a + half]
                m = jnp.maximum(jnp.max(sh, axis=1, keepdims=True), sink)
                e = jnp.exp(sh - m)
                es.append(jnp.where(from_prev, e, 0.0).astype(BF16))
                es.append(jnp.where(from_prev, 0.0, e).astype(BF16))
                st.append(jnp.exp(sink - m))
            ps.append(jnp.concatenate(es, axis=1))
            sink_terms.append(jnp.where(lo64, st[0], st[1]))
        o = _dot(jnp.concatenate(ps, axis=0), vcat)
        for a in range(2):
            oa = o[a * BLOCK:(a + 1) * BLOCK]
            attn_scr[rows, (2 * kh + a) * LANES:(2 * kh + a + 1) * LANES] = (
                oa[:, :LANES] / (oa[:, LANES:] + sink_terms[a])).astype(BF16)
        if i % units_per_gate == units_per_gate - 1:
            gcols = slice((i // units_per_gate) * GATE_COLS, (i // units_per_gate + 1) * GATE_COLS)
            g_scr[:, gcols] = jax.nn.sigmoid(_dot(xn, w_in_ref[:, C_GP + gcols.start:C_GP + gcols.stop]))

    u_scr[U_HALO:, :] = u
    po_ref[0] = u_scr[U_HALO + tm - POOL_STATE:U_HALO + tm, :]
    pos = t * tm + lax.broadcasted_iota(jnp.int32, (tm, 1), 0)

    def win_sum(g, w, ug):
        cols = slice(g * POOL_GC, (g + 1) * POOL_GC)
        n = U_HALO - SUBLANES + tm
        src, span = u_scr, 1
        while span < w:
            lvl = src[SUBLANES:SUBLANES + n, cols] + src[SUBLANES - span:SUBLANES - span + n, cols]
            span *= 2
            if span < w:
                lvl_scr[g, SUBLANES:SUBLANES + n, :] = lvl
                src, cols = lvl_scr.at[g], slice(None)
        return lvl[U_HALO - SUBLANES:, :]

    pooled = _pool_mix(u, win_sum, lambda w: jnp.minimum(w, pos + 1).astype(F32), gw_ref, pscale_ref)

    for scr in (kl_scr, kh_scr, vl_scr, vh_scr):
        scr[:, 0:BLOCK, :] = scr[:, tm:tm + BLOCK, :]
    u_scr[0:U_HALO, :] = u_scr[tm:tm + U_HALO, :]

    merged = g_scr[:, 0:D_MODEL] * _dot(pooled.astype(BF16), w_pb_ref[...])
    merged = merged + g_scr[:, D_MODEL:2 * D_MODEL] * _dot(attn_scr[...], w_ab_ref[...])
    h_ref[0] = x + _dot(merged.astype(BF16), w_out_ref[...])


def _const_spec(shape):
    nd = len(shape)
    return pl.BlockSpec(shape, lambda *_: (0,) * nd, pipeline_mode=pl.Buffered(1))


def _cast_block_rows(rows, steps):
    br = 2 * SUBLANES
    while rows % br or rows // br > steps:
        br *= 2
    return br


def _cast_specs(to_cast, steps, step_of):
    specs = []
    for w in to_cast:
        br = _cast_block_rows(w.shape[0], steps)
        last = w.shape[0] // br - 1
        specs.append(pl.BlockSpec(
            (br, w.shape[1]), lambda *idx, last=last: (jnp.minimum(step_of(*idx), last), 0)))
    return specs


def _prompt_mixer(x, cos, sg, sinks, ln1, w_in, gw, pscale, w_pb, w_ab, w_out, to_cast, tm):
    b, s, d = x.shape
    nt = s // tm
    cast_specs = _cast_specs(to_cast, b * nt, lambda bi, ti, *_: bi * nt + ti)
    grid_spec = pltpu.PrefetchScalarGridSpec(
        num_scalar_prefetch=1,
        grid=(b, nt),
        in_specs=[
            pl.BlockSpec((1, tm, d), lambda bi, ti, *_: (bi, ti, 0)),
            pl.BlockSpec((tm, LANES), lambda bi, ti, *_: (ti, 0)),
            pl.BlockSpec((tm, LANES), lambda bi, ti, *_: (ti, 0)),
            _const_spec((1, d)),
            _const_spec(w_in.shape),
            _const_spec(gw.shape),
            _const_spec((1, POOL_WIDTH)),
            _const_spec(w_pb.shape),
            _const_spec(w_ab.shape),
            _const_spec(w_out.shape),
        ] + cast_specs,
        out_specs=[
            pl.BlockSpec((1, tm, d), lambda bi, ti, *_: (bi, ti, 0)),
            pl.BlockSpec((1, KV_W, WINDOW), lambda bi, ti, *_: (bi, 0, 0)),
            pl.BlockSpec((1, KV_W, WINDOW), lambda bi, ti, *_: (bi, 0, 0)),
            pl.BlockSpec((1, POOL_STATE, POOL_WIDTH), lambda bi, ti, *_: (bi, 0, 0)),
        ] + cast_specs,
        scratch_shapes=[
            pltpu.VMEM((N_KV_HEADS, BLOCK + tm, LANES), BF16),
            pltpu.VMEM((N_KV_HEADS, BLOCK + tm, LANES), BF16),
            pltpu.VMEM((N_KV_HEADS, BLOCK + tm, LANES), BF16),
            pltpu.VMEM((N_KV_HEADS, BLOCK + tm, LANES), BF16),
            pltpu.VMEM((U_HALO + tm, POOL_WIDTH), F32),
            pltpu.VMEM((len(POOL_WINDOWS), U_HALO + tm, POOL_GC), F32),
            pltpu.VMEM((tm, Q_W), BF16),
            pltpu.VMEM((tm, Q_W), BF16),
            pltpu.VMEM((S_SLOTS, 2 * BLOCK, 2 * BLOCK), F32),
            pltpu.VMEM((tm, 2 * D_MODEL), F32),
        ],
    )
    return pl.pallas_call(
        functools.partial(_mixer_kernel, tm=tm, n_cast=len(to_cast)),
        grid_spec=grid_spec,
        out_shape=[
            jax.ShapeDtypeStruct((b, s, d), F32),
            jax.ShapeDtypeStruct((b, KV_W, WINDOW), F32),
            jax.ShapeDtypeStruct((b, KV_W, WINDOW), F32),
            jax.ShapeDtypeStruct((b, POOL_STATE, POOL_WIDTH), F32),
        ] + [jax.ShapeDtypeStruct(w.shape, BF16) for w in to_cast],
        compiler_params=pltpu.CompilerParams(
            dimension_semantics=("arbitrary", "arbitrary"), vmem_limit_bytes=VMEM_LIMIT),
        name="prompt_mixer",
    )(sinks, x, cos, sg, ln1, w_in, gw, pscale, w_pb, w_ab, w_out, *to_cast)


FFN_CHUNKS = ((0, 1024), (1024, 2048), (2048, FFN_HIDDEN))


def _ffn_kernel(h_ref, p_ref, ln2_ref, w1_ref, w2_ref, w_pp_ref, pn_ref, w_pg_ref, fn_ref, y_ref, *, tm):
    def tile_stages(r0):
        rows = slice(r0, r0 + tm)
        st = {}

        def up_proj(c):
            lo, hi = FFN_CHUNKS[c]
            if c == 0:
                st['h'] = h_ref[rows, :]
                st['hn'] = _rms(st['h'], ln2_ref[...]).astype(BF16)
                st['acc'] = st['h']
            st['gate', c] = _dot(st['hn'], w1_ref[:, lo:hi])
            st['up', c] = _dot(st['hn'], w1_ref[:, FFN_HIDDEN + lo:FFN_HIDDEN + hi])
            if c == len(FFN_CHUNKS) - 1:
                st['e'] = _rms(_dot(p_ref[rows, :].astype(BF16), w_pp_ref[...]), pn_ref[...])

        def down_proj(c):
            lo, hi = FFN_CHUNKS[c]
            gate = st.pop(('gate', c))
            act = (gate * jax.nn.sigmoid(gate) * st.pop(('up', c))).astype(BF16)
            st['acc'] = st['acc'] + _dot(act, w2_ref[lo:hi, :])

        def ple_gate():
            st['g'] = _dot(st['acc'].astype(BF16), w_pg_ref[...])

        def finish():
            h3 = st['acc'] + jax.nn.sigmoid(st['g']) * st['e']
            y_ref[rows, :] = _rms(h3, fn_ref[...])

        n = len(FFN_CHUNKS)
        steps = [functools.partial(up_proj, 0)]
        for c in range(1, n):
            steps += [functools.partial(up_proj, c), functools.partial(down_proj, c - 1)]
        return steps + [functools.partial(down_proj, n - 1), ple_gate, finish]

    tail = 3
    order = []
    for r0 in range(0, h_ref.shape[0], tm):
        steps = tile_stages(r0)
        held, order = order[len(order) - tail:] if order else [], order[:len(order) - tail] if order else []
        for k in range(max(len(held), tail)):
            order += steps[k:k + 1] + held[k:k + 1]
        order += steps[tail:]
    for step in order:
        step()


def _ffn(h, p, ln2, w1, w2, w_pp, pn, w_pg, fn, tm, nsub):
    n, d = h.shape
    blk = tm * nsub
    return pl.pallas_call(
        functools.partial(_ffn_kernel, tm=tm),
        grid=(n // blk,),
        in_specs=[
            pl.BlockSpec((blk, d), lambda i: (i, 0)),
            pl.BlockSpec((blk, PLE_DIM), lambda i: (i, 0)),
            _const_spec((1, d)),
            _const_spec(w1.shape),
            _const_spec(w2.shape),
            _const_spec(w_pp.shape),
            _const_spec((1, d)),
            _const_spec(w_pg.shape),
            _const_spec((1, d)),
        ],
        out_specs=pl.BlockSpec((blk, d), lambda i: (i, 0)),
        out_shape=jax.ShapeDtypeStruct((n, d), F32),
        compiler_params=pltpu.CompilerParams(dimension_semantics=("arbitrary",), vmem_limit_bytes=VMEM_LIMIT),
        name="ffn_ple_norm",
    )(h, p, ln2, w1, w2, w_pp, pn, w_pg, fn)


def _sample_pre_kernel(x_ref, cos_ref, sg_ref, ln1_ref, w_in_ref, qe_ref, knt_ref, vnt_ref, u_ref):
    n = x_ref.shape[0]
    lo8, _ = _lane_masks()
    xn = _rms(x_ref[...], ln1_ref[...]).astype(BF16)
    cos = cos_ref[...]
    sg = sg_ref[...]
    w = lambda lo, hi: w_in_ref[:, lo:hi].astype(BF16)
    u_ref[...] = _dot(xn, w(C_U, C_Q))
    q = (_rope(_dot(xn, w(C_Q, C_K)), cos, sg, lo8) * (HEAD_DIM ** -0.5)).astype(BF16)
    knt_ref[...] = _rope(_dot(xn, w(C_K, C_V)), cos, sg, lo8).T
    vnt_ref[...] = _dot(xn, w(C_V, C_GP)).T
    ii = lax.broadcasted_iota(jnp.int32, (Q_W, KV_W), 0)
    jj = lax.broadcasted_iota(jnp.int32, (Q_W, KV_W), 1)
    for r in range(N_HEADS):
        kh = r // GROUP
        sel = ((ii - r * HEAD_DIM) == (jj - kh * HEAD_DIM)) & (jj >= kh * HEAD_DIM) & (jj < (kh + 1) * HEAD_DIM)
        qr = _dot(q, jnp.where(sel, 1.0, 0.0).astype(BF16))
        for c in range(KV_W // LANES):
            qe_ref[c, pl.ds(r, n, stride=N_HEADS), :] = qr[:, c * LANES:(c + 1) * LANES]


def _sample_pre(x, cos, sg, ln1, w_in):
    n, d = x.shape
    return pl.pallas_call(
        _sample_pre_kernel,
        grid=(1,),
        in_specs=[
            _const_spec((n, d)),
            _const_spec((1, LANES)),
            _const_spec((1, LANES)),
            _const_spec((1, d)),
            pl.BlockSpec((d, C_GP), lambda i: (0, 0), pipeline_mode=pl.Buffered(1)),
        ],
        out_specs=[
            pl.BlockSpec((KV_W // LANES, n * N_HEADS, LANES), lambda i: (0, 0, 0)),
            pl.BlockSpec((KV_W, n), lambda i: (0, 0)),
            pl.BlockSpec((KV_W, n), lambda i: (0, 0)),
            pl.BlockSpec((n, POOL_WIDTH), lambda i: (0, 0)),
        ],
        out_shape=[
            jax.ShapeDtypeStruct((KV_W // LANES, n * N_HEADS, LANES), F32),
            jax.ShapeDtypeStruct((KV_W, n), F32),
            jax.ShapeDtypeStruct((KV_W, n), F32),
            jax.ShapeDtypeStruct((n, POOL_WIDTH), F32),
        ],
        compiler_params=pltpu.CompilerParams(dimension_semantics=("arbitrary",), vmem_limit_bytes=VMEM_LIMIT),
        name="sample_pre",
    )(x, cos, sg, ln1, w_in)


def _sample_attn_kernel(qe_ref, knt_ref, vnt_ref, ckt_ref, cvt_ref, sink_ref, *rest, bb, n_cast):
    cast_src = rest[:n_cast]
    o_ref, nkt_ref, nvt_ref = rest[n_cast:n_cast + 3]
    cast_dst = rest[n_cast + 3:]
    for src, dst in zip(cast_src, cast_dst):
        dst[...] = src[...].astype(BF16)
    n = knt_ref.shape[1]
    w_cache = ckt_ref.shape[2]
    i = pl.program_id(0)
    newest = lax.broadcasted_iota(jnp.int32, (1, w_cache), 1) == w_cache - 1
    shift = lax.rem(n - i * bb, n)
    kcols = pltpu.roll(knt_ref[...], shift, 1)
    vcols = pltpu.roll(vnt_ref[...], shift, 1)
    sink = sink_ref[...]
    scores = []
    for bl in range(bb):
        newk = jnp.where(newest, kcols[:, bl:bl + 1], pltpu.roll(ckt_ref[bl], w_cache - 1, 1))
        nkt_ref[bl] = newk
        nvt_ref[bl] = jnp.where(newest, vcols[:, bl:bl + 1], pltpu.roll(cvt_ref[bl], w_cache - 1, 1))
        rows = slice(bl * N_HEADS, (bl + 1) * N_HEADS)
        qb = jnp.concatenate([qe_ref[c, rows, :] for c in range(KV_W // LANES)], axis=1)
        scores.append(_dot(qb.astype(BF16), newk.astype(BF16)))
    probs, denoms = [], []
    for s in scores:
        m = jnp.maximum(jnp.max(s, axis=-1, keepdims=True), sink)
        e = jnp.exp(s - m)
        probs.append(e.astype(BF16))
        denoms.append(jnp.sum(e, axis=-1, keepdims=True) + jnp.exp(sink - m))
    for bl in range(bb):
        rows = slice(bl * N_HEADS, (bl + 1) * N_HEADS)
        o = _dot_nt(probs[bl], nvt_ref[bl].astype(BF16)) / denoms[bl]
        for c in range(KV_W // LANES):
            o_ref[c, rows, :] = o[:, c * LANES:(c + 1) * LANES]


def _sample_attn(qe, knt, vnt, ckt, cvt, sink, to_cast, bb):
    n, _, w_cache = ckt.shape
    cast_specs = _cast_specs(to_cast, n // bb, lambda i: i)
    cache_spec = pl.BlockSpec((bb, KV_W, w_cache), lambda i: (i, 0, 0))
    head_spec = pl.BlockSpec((KV_W // LANES, bb * N_HEADS, LANES), lambda i: (0, i, 0))
    return pl.pallas_call(
        functools.partial(_sample_attn_kernel, bb=bb, n_cast=len(to_cast)),
        grid=(n // bb,),
        in_specs=[head_spec, _const_spec((KV_W, n)), _const_spec((KV_W, n)), cache_spec, cache_spec,
                  _const_spec((N_HEADS, 1))] + cast_specs,
        out_specs=[head_spec, cache_spec, cache_spec] + cast_specs,
        out_shape=[
            jax.ShapeDtypeStruct((KV_W // LANES, n * N_HEADS, LANES), F32),
            jax.ShapeDtypeStruct((n, KV_W, w_cache), F32),
            jax.ShapeDtypeStruct((n, KV_W, w_cache), F32),
        ] + [jax.ShapeDtypeStruct(w.shape, BF16) for w in to_cast],
        compiler_params=pltpu.CompilerParams(dimension_semantics=("arbitrary",), vmem_limit_bytes=VMEM_LIMIT),
        name="sample_attn",
    )(qe, knt, vnt, ckt, cvt, sink, *to_cast)


def _sample_post_kernel(x_ref, u_ref, st_ref, o_ref, ln1_ref, w_g_ref, gw_ref, pscale_ref, w_pb_ref, w_ab_ref,
                        w_out_ref, h_ref, nst_ref):
    x = x_ref[...]
    n = x.shape[0]
    xn = _rms(x, ln1_ref[...]).astype(BF16)
    u = u_ref[...]
    nst_ref[0:POOL_STATE - 1] = st_ref[1:POOL_STATE]
    nst_ref[POOL_STATE - 1] = u

    def win_sum(g, w, ug):
        acc = ug
        for i in range(1, w):
            acc = acc + st_ref[POOL_STATE - i, :, g * POOL_GC:(g + 1) * POOL_GC]
        return acc

    pooled = _pool_mix(u, win_sum, lambda w: jnp.float32(min(w, PAST_LEN + 1)), gw_ref, pscale_ref)
    merged = jax.nn.sigmoid(_dot(xn, w_g_ref[:, 0:D_MODEL])) * _dot(pooled.astype(BF16), w_pb_ref[...])
    ab = jnp.zeros((n, D_MODEL), F32)
    for r in range(N_HEADS):
        kh = r // GROUP
        parts = []
        if kh > 0:
            parts.append(jnp.zeros((kh * HEAD_DIM, D_MODEL), BF16))
        parts.append(w_ab_ref[r * HEAD_DIM:(r + 1) * HEAD_DIM, :])
        if kh < N_KV_HEADS - 1:
            parts.append(jnp.zeros(((N_KV_HEADS - 1 - kh) * HEAD_DIM, D_MODEL), BF16))
        o_r = jnp.concatenate([o_ref[c, pl.ds(r, n, stride=N_HEADS), :] for c in range(KV_W // LANES)], axis=1)
        ab = ab + _dot(o_r.astype(BF16), jnp.concatenate(parts, axis=0))
    merged = merged + jax.nn.sigmoid(_dot(xn, w_g_ref[:, D_MODEL:2 * D_MODEL])) * ab
    h_ref[...] = x + _dot(merged.astype(BF16), w_out_ref[...])


def _sample_post(x, u, st, o, ln1, w_in, gw, pscale, w_pb, w_ab, w_out):
    n, d = x.shape
    return pl.pallas_call(
        _sample_post_kernel,
        grid=(1,),
        in_specs=[
            _const_spec((n, d)),
            _const_spec((n, POOL_WIDTH)),
            _const_spec(st.shape),
            _const_spec((KV_W // LANES, n * N_HEADS, LANES)),
            _const_spec((1, d)),
            pl.BlockSpec((d, 2 * D_MODEL), lambda i: (0, 1), pipeline_mode=pl.Buffered(1)),
            _const_spec(gw.shape),
            _const_spec((1, POOL_WIDTH)),
            _const_spec(w_pb.shape),
            _const_spec(w_ab.shape),
            _const_spec(w_out.shape),
        ],
        out_specs=[pl.BlockSpec((n, d), lambda i: (0, 0)), pl.BlockSpec(st.shape, lambda i: (0, 0, 0))],
        out_shape=[jax.ShapeDtypeStruct((n, d), F32), jax.ShapeDtypeStruct(st.shape, F32)],
        compiler_params=pltpu.CompilerParams(dimension_semantics=("arbitrary",), vmem_limit_bytes=VMEM_LIMIT),
        name="sample_post",
    )(x, u, st, o, ln1, w_in, gw, pscale, w_pb, w_ab, w_out)


def _rope_tables(first_pos, n):
    half = ROT_DIMS // 2
    inv = ROPE_THETA ** (-(np.arange(0, ROT_DIMS, 2, dtype=np.float64) / ROT_DIMS))
    ang = np.arange(first_pos, first_pos + n, dtype=np.float64)[:, None] * inv[None, :]
    cos, sin = np.cos(ang), np.sin(ang)
    rest = HEAD_DIM - 2 * half
    c64 = np.concatenate([cos, cos, np.ones((n, rest))], axis=1)
    s64 = np.concatenate([-sin, sin, np.zeros((n, rest))], axis=1)
    reps = LANES // HEAD_DIM
    return jnp.asarray(np.tile(c64, (1, reps)), F32), jnp.asarray(np.tile(s64, (1, reps)), F32)


def kernel(x_prompt, x_sample, p_prompt, p_sample, cache_k, cache_v, state_pool, ln1, w_in, pool_group_w, pool_scale,
           attn_sinks, w_pool_branch, w_attn_branch, w_out, ln2, w_ffn_in, w_ffn_out, w_ple_proj, ple_norm,
           w_ple_gate, final_norm):
    depth = ln1.shape[0]
    b, s, d = x_prompt.shape
    bd, t_dec, _ = x_sample.shape
    w_cache = cache_k.shape[2]
    assert depth == 1 and t_dec == 1 and w_cache == WINDOW and s % BLOCK == 0 and d == D_MODEL
    tm = 512
    assert s % tm == 0

    cos_p, sg_p = _rope_tables(0, s)
    cos_s, sg_s = _rope_tables(PAST_LEN, t_dec)

    hp = x_prompt
    hs = x_sample.reshape(bd, d)
    row = lambda a: a.reshape(1, -1)
    nkp, nvp, npp, nks, nvs, nps = [], [], [], [], [], []
    for i in range(depth):
        to_fm = lambda c: jnp.transpose(c, (0, 2, 3, 1)).reshape(bd, KV_W, w_cache)
        from_fm = lambda c: jnp.transpose(c.reshape(bd, N_KV_HEADS, HEAD_DIM, w_cache), (0, 3, 1, 2))
        qe, knt, vnt, un = _sample_pre(hs, cos_s, sg_s, row(ln1[i]), w_in[i])
        o, nkt, nvt, wi, gw, wpb, wab, wo = _sample_attn(
            qe, knt, vnt, to_fm(cache_k[i]), to_fm(cache_v[i]), attn_sinks[i].reshape(N_HEADS, 1),
            (w_in[i], pool_group_w[i].reshape(POOL_WIDTH, POOL_GC), w_pool_branch[i], w_attn_branch[i], w_out[i]), 8)
        gw = gw.reshape(len(POOL_WINDOWS), POOL_GC, POOL_GC)
        h1, kp, vp, pp, w1, w2, wpp, wpg = _prompt_mixer(
            hp, cos_p, sg_p, attn_sinks[i], row(ln1[i]), wi, gw, row(pool_scale[i]), wpb, wab, wo,
            (w_ffn_in[i], w_ffn_out[i], w_ple_proj[i], w_ple_gate[i]), tm)
        ffn_args = (row(ln2[i]), w1, w2, wpp, row(ple_norm[i]), wpg)
        hp = _ffn(h1.reshape(b * s, d), p_prompt[i].reshape(b * s, PLE_DIM), *ffn_args, row(final_norm),
                  tm, 2).reshape(b, s, d)
        from_fm_p = lambda c: jnp.transpose(c.reshape(b, N_KV_HEADS, HEAD_DIM, w_cache), (0, 3, 1, 2))
        nkp.append(from_fm_p(kp))
        nvp.append(from_fm_p(vp))
        npp.append(pp)

        h1s, nst = _sample_post(hs, un, jnp.transpose(state_pool[i], (1, 0, 2)), o, row(ln1[i]), wi, gw,
                                row(pool_scale[i]), wpb, wab, wo)
        hs = _ffn(h1s, p_sample[i].reshape(bd * t_dec, PLE_DIM), *ffn_args, row(final_norm), bd, 1)
        nks.append(from_fm(nkt))
        nvs.append(from_fm(nvt))
        nps.append(jnp.transpose(nst, (1, 0, 2)))

    return (hp, hs.reshape(bd, t_dec, d), jnp.stack(nkp), jnp.stack(nvp), jnp.stack(npp),
            jnp.stack(nks), jnp.stack(nvs), jnp.stack(nps))
```

```python
import functools

import jax
import jax.numpy as jnp
import numpy as np
from jax import lax
from jax.experimental import pallas as pl
from jax.experimental.pallas import tpu as pltpu

D_MODEL = 1024
HEAD_DIM = 64
N_HEADS = D_MODEL // HEAD_DIM
N_KV_HEADS = N_HEADS // 4
GROUP = N_HEADS // N_KV_HEADS
ROT_DIMS = HEAD_DIM // 4
ROPE_THETA = 500000.0
WINDOW = 128
BLOCK = 128
POOL_WIDTH = D_MODEL // 2
POOL_WINDOWS = (2, 4, 8, 16)
POOL_GC = POOL_WIDTH // len(POOL_WINDOWS)
POOL_STATE = max(POOL_WINDOWS) - 1
FFN_HIDDEN = -(-8 * D_MODEL // (3 * 256)) * 256
PLE_DIM = 256
EPS = 1e-6
NEG_INF = -1e30
PAST_LEN = 16384

Q_W = N_HEADS * HEAD_DIM
KV_W = N_KV_HEADS * HEAD_DIM
C_U, C_Q, C_K, C_V, C_GP, C_GA, C_END = 0, POOL_WIDTH, POOL_WIDTH + Q_W, POOL_WIDTH + Q_W + KV_W, \
    POOL_WIDTH + Q_W + 2 * KV_W, POOL_WIDTH + Q_W + 2 * KV_W + D_MODEL, POOL_WIDTH + Q_W + 2 * KV_W + 2 * D_MODEL

LANES = 128
S_AHEAD = 16
S_SLOTS = 16
GATE_COLS = 256
U_HALO = 24
SUBLANES = 8
VMEM_LIMIT = 56 * 1024 * 1024

BF16 = jnp.bfloat16
F32 = jnp.float32


def _dot(a, b):
    return jnp.dot(a, b, preferred_element_type=F32)


def _dot_nt(a, b):
    return lax.dot_general(a, b, (((1,), (1,)), ((), ())), preferred_element_type=F32)


def _rms(x, g):
    y = x * lax.rsqrt(jnp.mean(x * x, axis=-1, keepdims=True) + EPS)
    return y * g


def _rope(x, cos, sg, lo8):
    outs = []
    for c in range(x.shape[1] // LANES):
        xc = x[:, c * LANES:(c + 1) * LANES]
        partner = jnp.where(lo8, pltpu.roll(xc, LANES - ROT_DIMS // 2, 1), pltpu.roll(xc, ROT_DIMS // 2, 1))
        outs.append(xc * cos + partner * sg)
    return jnp.concatenate(outs, axis=1)


def _lane_masks():
    lane = lax.broadcasted_iota(jnp.int32, (1, LANES), 1)
    lo8 = (lane % HEAD_DIM) < (ROT_DIMS // 2)
    lo64 = lane < HEAD_DIM
    return lo8, lo64


def _pool_mix(u, win_sum_fn, cnt_fn, gw_ref, pscale_ref):
    mixed = []
    for g, w in enumerate(POOL_WINDOWS):
        cols = slice(g * POOL_GC, (g + 1) * POOL_GC)
        ug = u[:, cols]
        m = win_sum_fn(g, w, ug) / cnt_fn(w) - ug
        mixed.append(_dot(m.astype(BF16), gw_ref[g]) * pscale_ref[:, cols])
    return jnp.concatenate(mixed, axis=1)


def _mixer_kernel(sinks_ref, x_ref, cos_ref, sg_ref, ln1_ref, w_in_ref, gw_ref, pscale_ref, w_pb_ref, w_ab_ref,
                  w_out_ref, knt_ref, vnt_ref, ckt_ref, cvt_ref, *rest, tm, n_cast):
    cast_src = rest[:n_cast]
    h_ref, ko_ref, vo_ref, po_ref, nkt_ref, nvt_ref = rest[n_cast:n_cast + 6]
    cast_dst = rest[n_cast + 6:2 * n_cast + 6]
    kl_scr, kh_scr, vl_scr, vh_scr, u_scr, lvl_scr, attn_scr, q_scr, s_scr, g_scr = rest[2 * n_cast + 6:]
    t = pl.program_id(1)
    _shift_caches(pl.program_id(0) * pl.num_programs(1) + t, ckt_ref.shape[0], knt_ref, vnt_ref, ckt_ref, cvt_ref,
                  nkt_ref, nvt_ref)
    for src, dst in zip(cast_src, cast_dst):
        dst[...] = src[...].astype(BF16)
    lo8, lo64 = _lane_masks()

    @pl.when(t == 0)
    def _():
        for scr in (kl_scr, kh_scr, vl_scr, vh_scr):
            scr[:, 0:BLOCK, :] = jnp.zeros((N_KV_HEADS, BLOCK, LANES), BF16)
        u_scr[0:U_HALO, :] = jnp.zeros((U_HALO, POOL_WIDTH), F32)
        lvl_scr[:, 0:SUBLANES, :] = jnp.zeros((len(POOL_WINDOWS), SUBLANES, POOL_GC), F32)

    x = x_ref[0]
    xn = _rms(x, ln1_ref[...]).astype(BF16)
    cos = cos_ref[...]
    sg = sg_ref[...]

    k = _rope(_dot(xn, w_in_ref[:, C_K:C_V]), cos, sg, lo8)
    v = _dot(xn, w_in_ref[:, C_V:C_GP])
    q = (_rope(_dot(xn, w_in_ref[:, C_Q:C_K]), cos, sg, lo8) * (HEAD_DIM ** -0.5)).astype(BF16)
    q_scr[...] = q
    u = _dot(xn, w_in_ref[:, C_U:C_Q])
    ko_ref[0] = k[tm - WINDOW:, :].T
    vo_ref[0] = v[tm - WINDOW:, :].T

    zero = jnp.zeros((tm, LANES), F32)
    for src, lo_scr, hi_scr in ((k, kl_scr, kh_scr), (v, vl_scr, vh_scr)):
        for p in range(KV_W // LANES):
            xp = src[:, p * LANES:(p + 1) * LANES]
            xs = pltpu.roll(xp, HEAD_DIM, 1)
            lo_scr[2 * p, BLOCK:, :] = jnp.where(lo64, xp, zero).astype(BF16)
            hi_scr[2 * p, BLOCK:, :] = jnp.where(lo64, zero, xs).astype(BF16)
            lo_scr[2 * p + 1, BLOCK:, :] = jnp.where(lo64, xs, zero).astype(BF16)
            hi_scr[2 * p + 1, BLOCK:, :] = jnp.where(lo64, zero, xp).astype(BF16)

    qi = lax.broadcasted_iota(jnp.int32, (BLOCK, BLOCK), 0)
    ci = lax.broadcasted_iota(jnp.int32, (BLOCK, BLOCK), 1)
    from_prev = ci > qi
    bias0 = jnp.where(jnp.logical_and(t == 0, from_prev), NEG_INF, 0.0).astype(F32)

    ones_lo = jnp.broadcast_to(jnp.where(lo64, 1.0, 0.0).astype(BF16), (2 * BLOCK, LANES))
    ones_hi = jnp.broadcast_to(jnp.where(lo64, 0.0, 1.0).astype(BF16), (2 * BLOCK, LANES))

    units = [(j, kh) for j in range(tm // BLOCK) for kh in range(N_KV_HEADS)]

    def scores(i):
        j, kh = units[i]
        rows = slice(j * BLOCK, (j + 1) * BLOCK)
        win = slice(j * BLOCK, (j + 2) * BLOCK)
        qq = jnp.concatenate([q_scr[rows, (2 * kh + a) * LANES:(2 * kh + a + 1) * LANES] for a in range(2)], axis=0)
        kcat = jnp.concatenate([kl_scr[kh, win, :], kh_scr[kh, win, :]], axis=0)
        s = _dot_nt(qq, kcat)
        for a in range(2):
            for half in range(2):
                sa = s[a * BLOCK:(a + 1) * BLOCK, half * 2 * BLOCK:(half + 1) * 2 * BLOCK]
                folded = jnp.where(from_prev, sa[:, :BLOCK], sa[:, BLOCK:])
                if j == 0:
                    folded = folded + bias0
                s_scr[i % S_SLOTS, a * BLOCK:(a + 1) * BLOCK, half * BLOCK:(half + 1) * BLOCK] = folded

    units_per_gate = len(units) * GATE_COLS // (2 * D_MODEL)
    for i in range(S_AHEAD):
        scores(i)
    for i, (j, kh) in enumerate(units):
        if i + S_AHEAD < len(units):
            scores(i + S_AHEAD)
        rows = slice(j * BLOCK, (j + 1) * BLOCK)
        win = slice(j * BLOCK, (j + 2) * BLOCK)
        vcat = jnp.concatenate([
            jnp.concatenate([vl_scr[kh, win, :], ones_lo], axis=1),
            jnp.concatenate([vh_scr[kh, win, :], ones_hi], axis=1)], axis=0)
        ps, sink_terms = [], []
        for a in range(2):
            es, st = [], []
            for half in range(2):
                sh = s_scr[i % S_SLOTS, a * BLOCK:(a + 1) * BLOCK, half * BLOCK:(half + 1) * BLOCK]
                sink = sinks_ref[4 * kh + 2 * a + half]
                m = jnp.maximum(jnp.max(sh, axis=1, keepdims=True), sink)
                e = jnp.exp(sh - m)
                es.append(jnp.where(from_prev, e, 0.0).astype(BF16))
                es.append(jnp.where(from_prev, 0.0, e).astype(BF16))
                st.append(jnp.exp(sink - m))
            ps.append(jnp.concatenate(es, axis=1))
            sink_terms.append(jnp.where(lo64, st[0], st[1]))
        o = _dot(jnp.concatenate(ps, axis=0), vcat)
        for a in range(2):
            oa = o[a * BLOCK:(a + 1) * BLOCK]
            attn_scr[rows, (2 * kh + a) * LANES:(2 * kh + a + 1) * LANES] = (
                oa[:, :LANES] / (oa[:, LANES:] + sink_terms[a])).astype(BF16)
        if i % units_per_gate == units_per_gate - 1:
            gcols = slice((i // units_per_gate) * GATE_COLS, (i // units_per_gate + 1) * GATE_COLS)
            g_scr[:, gcols] = jax.nn.sigmoid(_dot(xn, w_in_ref[:, C_GP + gcols.start:C_GP + gcols.stop]))

    u_scr[U_HALO:, :] = u
    po_ref[0] = u_scr[U_HALO + tm - POOL_STATE:U_HALO + tm, :]
    pos = t * tm + lax.broadcasted_iota(jnp.int32, (tm, 1), 0)

    def win_sum(g, w, ug):
        cols = slice(g * POOL_GC, (g + 1) * POOL_GC)
        n = U_HALO - SUBLANES + tm
        src, span = u_scr, 1
        while span < w:
            lvl = src[SUBLANES:SUBLANES + n, cols] + src[SUBLANES - span:SUBLANES - span + n, cols]
            span *= 2
            if span < w:
                lvl_scr[g, SUBLANES:SUBLANES + n, :] = lvl
                src, cols = lvl_scr.at[g], slice(None)
        return lvl[U_HALO - SUBLANES:, :]

    pooled = _pool_mix(u, win_sum, lambda w: jnp.minimum(w, pos + 1).astype(F32), gw_ref, pscale_ref)

    for scr in (kl_scr, kh_scr, vl_scr, vh_scr):
        scr[:, 0:BLOCK, :] = scr[:, tm:tm + BLOCK, :]
    u_scr[0:U_HALO, :] = u_scr[tm:tm + U_HALO, :]

    merged = g_scr[:, 0:D_MODEL] * _dot(pooled.astype(BF16), w_pb_ref[...])
    merged = merged + g_scr[:, D_MODEL:2 * D_MODEL] * _dot(attn_scr[...], w_ab_ref[...])
    h_ref[0] = x + _dot(merged.astype(BF16), w_out_ref[...])


def _const_spec(shape):
    nd = len(shape)
    return pl.BlockSpec(shape, lambda *_: (0,) * nd, pipeline_mode=pl.Buffered(1))


def _cast_block_rows(rows, steps):
    br = 2 * SUBLANES
    while rows % br or rows // br > steps:
        br *= 2
    return br


def _cast_specs(to_cast, steps, step_of):
    specs = []
    for w in to_cast:
        br = _cast_block_rows(w.shape[0], steps)
        last = w.shape[0] // br - 1
        specs.append(pl.BlockSpec(
            (br, w.shape[1]), lambda *idx, last=last: (jnp.minimum(step_of(*idx), last), 0)))
    return specs


def _prompt_mixer(x, cos, sg, sinks, ln1, w_in, gw, pscale, w_pb, w_ab, w_out, knt, vnt, ckt, cvt, to_cast, tm):
    b, s, d = x.shape
    nt = s // tm
    cast_specs = _cast_specs(to_cast, b * nt, lambda bi, ti, *_: bi * nt + ti)
    nb = ckt.shape[0] // (b * nt)
    assert nb * b * nt == ckt.shape[0]
    cache_spec = pl.BlockSpec((nb,) + ckt.shape[1:], lambda bi, ti, *_: (bi * nt + ti, 0, 0))
    grid_spec = pltpu.PrefetchScalarGridSpec(
        num_scalar_prefetch=1,
        grid=(b, nt),
        in_specs=[
            pl.BlockSpec((1, tm, d), lambda bi, ti, *_: (bi, ti, 0)),
            pl.BlockSpec((tm, LANES), lambda bi, ti, *_: (ti, 0)),
            pl.BlockSpec((tm, LANES), lambda bi, ti, *_: (ti, 0)),
            _const_spec((1, d)),
            _const_spec(w_in.shape),
            _const_spec(gw.shape),
            _const_spec((1, POOL_WIDTH)),
            _const_spec(w_pb.shape),
            _const_spec(w_ab.shape),
            _const_spec(w_out.shape),
            _const_spec(knt.shape),
            _const_spec(vnt.shape),
            cache_spec,
            cache_spec,
        ] + cast_specs,
        out_specs=[
            pl.BlockSpec((1, tm, d), lambda bi, ti, *_: (bi, ti, 0)),
            pl.BlockSpec((1, KV_W, WINDOW), lambda bi, ti, *_: (bi, 0, 0)),
            pl.BlockSpec((1, KV_W, WINDOW), lambda bi, ti, *_: (bi, 0, 0)),
            pl.BlockSpec((1, POOL_STATE, POOL_WIDTH), lambda bi, ti, *_: (bi, 0, 0)),
            cache_spec,
            cache_spec,
        ] + cast_specs,
        scratch_shapes=[
            pltpu.VMEM((N_KV_HEADS, BLOCK + tm, LANES), BF16),
            pltpu.VMEM((N_KV_HEADS, BLOCK + tm, LANES), BF16),
            pltpu.VMEM((N_KV_HEADS, BLOCK + tm, LANES), BF16),
            pltpu.VMEM((N_KV_HEADS, BLOCK + tm, LANES), BF16),
            pltpu.VMEM((U_HALO + tm, POOL_WIDTH), F32),
            pltpu.VMEM((len(POOL_WINDOWS), U_HALO + tm, POOL_GC), F32),
            pltpu.VMEM((tm, Q_W), BF16),
            pltpu.VMEM((tm, Q_W), BF16),
            pltpu.VMEM((S_SLOTS, 2 * BLOCK, 2 * BLOCK), F32),
            pltpu.VMEM((tm, 2 * D_MODEL), F32),
        ],
    )
    return pl.pallas_call(
        functools.partial(_mixer_kernel, tm=tm, n_cast=len(to_cast)),
        grid_spec=grid_spec,
        out_shape=[
            jax.ShapeDtypeStruct((b, s, d), F32),
            jax.ShapeDtypeStruct((b, KV_W, WINDOW), F32),
            jax.ShapeDtypeStruct((b, KV_W, WINDOW), F32),
            jax.ShapeDtypeStruct((b, POOL_STATE, POOL_WIDTH), F32),
            jax.ShapeDtypeStruct(ckt.shape, F32),
            jax.ShapeDtypeStruct(cvt.shape, F32),
        ] + [jax.ShapeDtypeStruct(w.shape, BF16) for w in to_cast],
        compiler_params=pltpu.CompilerParams(
            dimension_semantics=("arbitrary", "arbitrary"), vmem_limit_bytes=VMEM_LIMIT),
        name="prompt_mixer",
    )(sinks, x, cos, sg, ln1, w_in, gw, pscale, w_pb, w_ab, w_out, knt, vnt, ckt, cvt, *to_cast)


FFN_CHUNKS = ((0, 1024), (1024, 2048), (2048, FFN_HIDDEN))


def _ffn_kernel(h_ref, p_ref, ln2_ref, w1_ref, w2_ref, w_pp_ref, pn_ref, w_pg_ref, fn_ref, y_ref, *, tm):
    def tile_stages(r0):
        rows = slice(r0, r0 + tm)
        st = {}

        def up_proj(c):
            lo, hi = FFN_CHUNKS[c]
            if c == 0:
                st['h'] = h_ref[rows, :]
                st['hn'] = _rms(st['h'], ln2_ref[...]).astype(BF16)
                st['acc'] = st['h']
            st['gate', c] = _dot(st['hn'], w1_ref[:, lo:hi])
            st['up', c] = _dot(st['hn'], w1_ref[:, FFN_HIDDEN + lo:FFN_HIDDEN + hi])
            if c == len(FFN_CHUNKS) - 1:
                st['e'] = _rms(_dot(p_ref[rows, :].astype(BF16), w_pp_ref[...]), pn_ref[...])

        def down_proj(c):
            lo, hi = FFN_CHUNKS[c]
            gate = st.pop(('gate', c))
            act = (gate * jax.nn.sigmoid(gate) * st.pop(('up', c))).astype(BF16)
            st['acc'] = st['acc'] + _dot(act, w2_ref[lo:hi, :])

        def ple_gate():
            st['g'] = _dot(st['acc'].astype(BF16), w_pg_ref[...])

        def finish():
            h3 = st['acc'] + jax.nn.sigmoid(st['g']) * st['e']
            y_ref[rows, :] = _rms(h3, fn_ref[...])

        n = len(FFN_CHUNKS)
        steps = [functools.partial(up_proj, 0)]
        for c in range(1, n):
            steps += [functools.partial(up_proj, c), functools.partial(down_proj, c - 1)]
        return steps + [functools.partial(down_proj, n - 1), ple_gate, finish]

    tail = 3
    order = []
    for r0 in range(0, h_ref.shape[0], tm):
        steps = tile_stages(r0)
        held, order = order[len(order) - tail:] if order else [], order[:len(order) - tail] if order else []
        for k in range(max(len(held), tail)):
            order += steps[k:k + 1] + held[k:k + 1]
        order += steps[tail:]
    for step in order:
        step()


def _ffn(h, p, ln2, w1, w2, w_pp, pn, w_pg, fn, tm, nsub):
    n, d = h.shape
    blk = tm * nsub
    return pl.pallas_call(
        functools.partial(_ffn_kernel, tm=tm),
        grid=(n // blk,),
        in_specs=[
            pl.BlockSpec((blk, d), lambda i: (i, 0)),
            pl.BlockSpec((blk, PLE_DIM), lambda i: (i, 0)),
            _const_spec((1, d)),
            _const_spec(w1.shape),
            _const_spec(w2.shape),
            _const_spec(w_pp.shape),
            _const_spec((1, d)),
            _const_spec(w_pg.shape),
            _const_spec((1, d)),
        ],
        out_specs=pl.BlockSpec((blk, d), lambda i: (i, 0)),
        out_shape=jax.ShapeDtypeStruct((n, d), F32),
        compiler_params=pltpu.CompilerParams(dimension_semantics=("arbitrary",), vmem_limit_bytes=VMEM_LIMIT),
        name="ffn_ple_norm",
    )(h, p, ln2, w1, w2, w_pp, pn, w_pg, fn)


def _sample_pre_kernel(x_ref, cos_ref, sg_ref, ln1_ref, w_in_ref, qe_ref, knt_ref, vnt_ref, kn_ref, vn_ref, u_ref):
    n = x_ref.shape[0]
    lo8, _ = _lane_masks()
    xn = _rms(x_ref[...], ln1_ref[...]).astype(BF16)
    cos = cos_ref[...]
    sg = sg_ref[...]
    w = lambda lo, hi: w_in_ref[:, lo:hi].astype(BF16)
    u_ref[...] = _dot(xn, w(C_U, C_Q))
    q = (_rope(_dot(xn, w(C_Q, C_K)), cos, sg, lo8) * (HEAD_DIM ** -0.5)).astype(BF16)
    kn = _rope(_dot(xn, w(C_K, C_V)), cos, sg, lo8)
    vn = _dot(xn, w(C_V, C_GP))
    kn_ref[...] = kn
    vn_ref[...] = vn
    knt_ref[...] = kn.T
    vnt_ref[...] = vn.T
    ii = lax.broadcasted_iota(jnp.int32, (Q_W, KV_W), 0)
    jj = lax.broadcasted_iota(jnp.int32, (Q_W, KV_W), 1)
    for r in range(N_HEADS):
        kh = r // GROUP
        sel = ((ii - r * HEAD_DIM) == (jj - kh * HEAD_DIM)) & (jj >= kh * HEAD_DIM) & (jj < (kh + 1) * HEAD_DIM)
        qr = _dot(q, jnp.where(sel, 1.0, 0.0).astype(BF16))
        for c in range(KV_W // LANES):
            qe_ref[c, pl.ds(r, n, stride=N_HEADS), :] = qr[:, c * LANES:(c + 1) * LANES]


def _sample_pre(x, cos, sg, ln1, w_in):
    n, d = x.shape
    return pl.pallas_call(
        _sample_pre_kernel,
        grid=(1,),
        in_specs=[
            _const_spec((n, d)),
            _const_spec((1, LANES)),
            _const_spec((1, LANES)),
            _const_spec((1, d)),
            pl.BlockSpec((d, C_GP), lambda i: (0, 0), pipeline_mode=pl.Buffered(1)),
        ],
        out_specs=[
            pl.BlockSpec((KV_W // LANES, n * N_HEADS, LANES), lambda i: (0, 0, 0)),
            pl.BlockSpec((KV_W, n), lambda i: (0, 0)),
            pl.BlockSpec((KV_W, n), lambda i: (0, 0)),
            pl.BlockSpec((n, KV_W), lambda i: (0, 0)),
            pl.BlockSpec((n, KV_W), lambda i: (0, 0)),
            pl.BlockSpec((n, POOL_WIDTH), lambda i: (0, 0)),
        ],
        out_shape=[
            jax.ShapeDtypeStruct((KV_W // LANES, n * N_HEADS, LANES), F32),
            jax.ShapeDtypeStruct((KV_W, n), F32),
            jax.ShapeDtypeStruct((KV_W, n), F32),
            jax.ShapeDtypeStruct((n, KV_W), F32),
            jax.ShapeDtypeStruct((n, KV_W), F32),
            jax.ShapeDtypeStruct((n, POOL_WIDTH), F32),
        ],
        compiler_params=pltpu.CompilerParams(dimension_semantics=("arbitrary",), vmem_limit_bytes=VMEM_LIMIT),
        name="sample_pre",
    )(x, cos, sg, ln1, w_in)


def _shift_caches(step, nb, knt_ref, vnt_ref, ckt_ref, cvt_ref, nkt_ref, nvt_ref):
    n = knt_ref.shape[1]
    w_cache = ckt_ref.shape[2]
    newest = lax.broadcasted_iota(jnp.int32, (1, w_cache), 1) == w_cache - 1
    shift = lax.rem(n - lax.rem(step * nb, n), n)
    kcols = pltpu.roll(knt_ref[...], shift, 1)
    vcols = pltpu.roll(vnt_ref[...], shift, 1)
    for bl in range(nb):
        nkt_ref[bl] = jnp.where(newest, kcols[:, bl:bl + 1], pltpu.roll(ckt_ref[bl], w_cache - 1, 1))
        nvt_ref[bl] = jnp.where(newest, vcols[:, bl:bl + 1], pltpu.roll(cvt_ref[bl], w_cache - 1, 1))


def _sample_attn_kernel(qe_ref, kn_ref, vn_ref, ckt_ref, cvt_ref, sink_ref, *rest, bb, n_cast):
    cast_src = rest[:n_cast]
    o_ref = rest[n_cast]
    cast_dst = rest[n_cast + 1:]
    for src, dst in zip(cast_src, cast_dst):
        dst[...] = src[...].astype(BF16)
    w_cache = ckt_ref.shape[2]
    oldest = lax.broadcasted_iota(jnp.int32, (1, w_cache), 1) == 0
    sink = sink_ref[...]
    rounded = lambda a: a.astype(BF16).astype(F32)
    scores = []
    for bl in range(bb):
        rows = slice(bl * N_HEADS, (bl + 1) * N_HEADS)
        qb = jnp.concatenate([qe_ref[c, rows, :] for c in range(KV_W // LANES)], axis=1).astype(BF16)
        s_old = jnp.where(oldest, NEG_INF, _dot(qb, ckt_ref[bl].astype(BF16)))
        s_new = jnp.sum(qb.astype(F32) * rounded(kn_ref[bl:bl + 1, :]), axis=1, keepdims=True)
        scores.append((s_old, s_new))
    probs = []
    for s_old, s_new in scores:
        m = jnp.maximum(jnp.maximum(jnp.max(s_old, axis=-1, keepdims=True), s_new), sink)
        e_old = jnp.exp(s_old - m)
        e_new = jnp.exp(s_new - m)
        denom = jnp.sum(e_old, axis=-1, keepdims=True) + e_new + jnp.exp(sink - m)
        probs.append((e_old.astype(BF16), rounded(e_new), denom))
    for bl, (e_old, e_new, denom) in enumerate(probs):
        rows = slice(bl * N_HEADS, (bl + 1) * N_HEADS)
        o = (_dot_nt(e_old, cvt_ref[bl].astype(BF16)) + e_new * rounded(vn_ref[bl:bl + 1, :])) / denom
        for c in range(KV_W // LANES):
            o_ref[c, rows, :] = o[:, c * LANES:(c + 1) * LANES]


def _sample_attn(qe, kn, vn, ckt, cvt, sink, to_cast, bb):
    n, _, w_cache = ckt.shape
    cast_specs = _cast_specs(to_cast, n // bb, lambda i: i)
    cache_spec = pl.BlockSpec((bb, KV_W, w_cache), lambda i: (i, 0, 0))
    new_spec = pl.BlockSpec((bb, KV_W), lambda i: (i, 0))
    head_spec = pl.BlockSpec((KV_W // LANES, bb * N_HEADS, LANES), lambda i: (0, i, 0))
    return pl.pallas_call(
        functools.partial(_sample_attn_kernel, bb=bb, n_cast=len(to_cast)),
        grid=(n // bb,),
        in_specs=[head_spec, new_spec, new_spec, cache_spec, cache_spec, _const_spec((N_HEADS, 1))] + cast_specs,
        out_specs=[head_spec] + cast_specs,
        out_shape=[jax.ShapeDtypeStruct((KV_W // LANES, n * N_HEADS, LANES), F32)]
        + [jax.ShapeDtypeStruct(w.shape, BF16) for w in to_cast],
        compiler_params=pltpu.CompilerParams(dimension_semantics=("arbitrary",), vmem_limit_bytes=VMEM_LIMIT),
        name="sample_attn",
    )(qe, kn, vn, ckt, cvt, sink, *to_cast)


def _sample_post_kernel(x_ref, u_ref, st_ref, o_ref, ln1_ref, w_g_ref, gw_ref, pscale_ref, w_pb_ref, w_ab_ref,
                        w_out_ref, h_ref, nst_ref):
    x = x_ref[...]
    n = x.shape[0]
    xn = _rms(x, ln1_ref[...]).astype(BF16)
    u = u_ref[...]
    nst_ref[0:POOL_STATE - 1] = st_ref[1:POOL_STATE]
    nst_ref[POOL_STATE - 1] = u

    def win_sum(g, w, ug):
        acc = ug
        for i in range(1, w):
            acc = acc + st_ref[POOL_STATE - i, :, g * POOL_GC:(g + 1) * POOL_GC]
        return acc

    pooled = _pool_mix(u, win_sum, lambda w: jnp.float32(min(w, PAST_LEN + 1)), gw_ref, pscale_ref)
    merged = jax.nn.sigmoid(_dot(xn, w_g_ref[:, 0:D_MODEL])) * _dot(pooled.astype(BF16), w_pb_ref[...])
    ab = jnp.zeros((n, D_MODEL), F32)
    for r in range(N_HEADS):
        kh = r // GROUP
        parts = []
        if kh > 0:
            parts.append(jnp.zeros((kh * HEAD_DIM, D_MODEL), BF16))
        parts.append(w_ab_ref[r * HEAD_DIM:(r + 1) * HEAD_DIM, :])
        if kh < N_KV_HEADS - 1:
            parts.append(jnp.zeros(((N_KV_HEADS - 1 - kh) * HEAD_DIM, D_MODEL), BF16))
        o_r = jnp.concatenate([o_ref[c, pl.ds(r, n, stride=N_HEADS), :] for c in range(KV_W // LANES)], axis=1)
        ab = ab + _dot(o_r.astype(BF16), jnp.concatenate(parts, axis=0))
    merged = merged + jax.nn.sigmoid(_dot(xn, w_g_ref[:, D_MODEL:2 * D_MODEL])) * ab
    h_ref[...] = x + _dot(merged.astype(BF16), w_out_ref[...])


def _sample_post(x, u, st, o, ln1, w_in, gw, pscale, w_pb, w_ab, w_out):
    n, d = x.shape
    return pl.pallas_call(
        _sample_post_kernel,
        grid=(1,),
        in_specs=[
            _const_spec((n, d)),
            _const_spec((n, POOL_WIDTH)),
            _const_spec(st.shape),
            _const_spec((KV_W // LANES, n * N_HEADS, LANES)),
            _const_spec((1, d)),
            pl.BlockSpec((d, 2 * D_MODEL), lambda i: (0, 1), pipeline_mode=pl.Buffered(1)),
            _const_spec(gw.shape),
            _const_spec((1, POOL_WIDTH)),
            _const_spec(w_pb.shape),
            _const_spec(w_ab.shape),
            _const_spec(w_out.shape),
        ],
        out_specs=[pl.BlockSpec((n, d), lambda i: (0, 0)), pl.BlockSpec(st.shape, lambda i: (0, 0, 0))],
        out_shape=[jax.ShapeDtypeStruct((n, d), F32), jax.ShapeDtypeStruct(st.shape, F32)],
        compiler_params=pltpu.CompilerParams(dimension_semantics=("arbitrary",), vmem_limit_bytes=VMEM_LIMIT),
        name="sample_post",
    )(x, u, st, o, ln1, w_in, gw, pscale, w_pb, w_ab, w_out)


def _rope_tables(first_pos, n):
    half = ROT_DIMS // 2
    inv = ROPE_THETA ** (-(np.arange(0, ROT_DIMS, 2, dtype=np.float64) / ROT_DIMS))
    ang = np.arange(first_pos, first_pos + n, dtype=np.float64)[:, None] * inv[None, :]
    cos, sin = np.cos(ang), np.sin(ang)
    rest = HEAD_DIM - 2 * half
    c64 = np.concatenate([cos, cos, np.ones((n, rest))], axis=1)
    s64 = np.concatenate([-sin, sin, np.zeros((n, rest))], axis=1)
    reps = LANES // HEAD_DIM
    return jnp.asarray(np.tile(c64, (1, reps)), F32), jnp.asarray(np.tile(s64, (1, reps)), F32)


def kernel(x_prompt, x_sample, p_prompt, p_sample, cache_k, cache_v, state_pool, ln1, w_in, pool_group_w, pool_scale,
           attn_sinks, w_pool_branch, w_attn_branch, w_out, ln2, w_ffn_in, w_ffn_out, w_ple_proj, ple_norm,
           w_ple_gate, final_norm):
    depth = ln1.shape[0]
    b, s, d = x_prompt.shape
    bd, t_dec, _ = x_sample.shape
    w_cache = cache_k.shape[2]
    assert depth == 1 and t_dec == 1 and w_cache == WINDOW and s % BLOCK == 0 and d == D_MODEL
    tm = 512
    assert s % tm == 0

    cos_p, sg_p = _rope_tables(0, s)
    cos_s, sg_s = _rope_tables(PAST_LEN, t_dec)

    hp = x_prompt
    hs = x_sample.reshape(bd, d)
    row = lambda a: a.reshape(1, -1)
    nkp, nvp, npp, nks, nvs, nps = [], [], [], [], [], []
    for i in range(depth):
        to_fm = lambda c: jnp.transpose(c, (0, 2, 3, 1)).reshape(bd, KV_W, w_cache)
        from_fm = lambda c: jnp.transpose(c.reshape(bd, N_KV_HEADS, HEAD_DIM, w_cache), (0, 3, 1, 2))
        ckt, cvt = to_fm(cache_k[i]), to_fm(cache_v[i])
        qe, knt, vnt, kn, vn, un = _sample_pre(hs, cos_s, sg_s, row(ln1[i]), w_in[i])
        o, wi, gw, wpb, wab, wo = _sample_attn(
            qe, kn, vn, ckt, cvt, attn_sinks[i].reshape(N_HEADS, 1),
            (w_in[i], pool_group_w[i].reshape(POOL_WIDTH, POOL_GC), w_pool_branch[i], w_attn_branch[i], w_out[i]), 8)
        gw = gw.reshape(len(POOL_WINDOWS), POOL_GC, POOL_GC)
        h1, kp, vp, pp, nkt, nvt, w1, w2, wpp, wpg = _prompt_mixer(
            hp, cos_p, sg_p, attn_sinks[i], row(ln1[i]), wi, gw, row(pool_scale[i]), wpb, wab, wo,
            knt, vnt, ckt, cvt, (w_ffn_in[i], w_ffn_out[i], w_ple_proj[i], w_ple_gate[i]), tm)
        ffn_args = (row(ln2[i]), w1, w2, wpp, row(ple_norm[i]), wpg)
        hp = _ffn(h1.reshape(b * s, d), p_prompt[i].reshape(b * s, PLE_DIM), *ffn_args, row(final_norm),
                  tm, 2).reshape(b, s, d)
        from_fm_p = lambda c: jnp.transpose(c.reshape(b, N_KV_HEADS, HEAD_DIM, w_cache), (0, 3, 1, 2))
        nkp.append(from_fm_p(kp))
        nvp.append(from_fm_p(vp))
        npp.append(pp)

        h1s, nst = _sample_post(hs, un, jnp.transpose(state_pool[i], (1, 0, 2)), o, row(ln1[i]), wi, gw,
                                row(pool_scale[i]), wpb, wab, wo)
        hs = _ffn(h1s, p_sample[i].reshape(bd * t_dec, PLE_DIM), *ffn_args, row(final_norm), bd, 1)
        nks.append(from_fm(nkt))
        nvs.append(from_fm(nvt))
        nps.append(jnp.transpose(nst, (1, 0, 2)))

    return (hp, hs.reshape(bd, t_dec, d), jnp.stack(nkp), jnp.stack(nvp), jnp.stack(npp),
            jnp.stack(nks), jnp.stack(nvs), jnp.stack(nps))
```

```python
import functools

import jax
import jax.numpy as jnp
import numpy as np
from jax import lax
from jax.experimental import pallas as pl
from jax.experimental.pallas import tpu as pltpu

D_MODEL = 1024
HEAD_DIM = 64
N_HEADS = D_MODEL // HEAD_DIM
N_KV_HEADS = N_HEADS // 4
GROUP = N_HEADS // N_KV_HEADS
ROT_DIMS = HEAD_DIM // 4
ROPE_THETA = 500000.0
WINDOW = 128
BLOCK = 128
POOL_WIDTH = D_MODEL // 2
POOL_WINDOWS = (2, 4, 8, 16)
POOL_GC = POOL_WIDTH // len(POOL_WINDOWS)
POOL_STATE = max(POOL_WINDOWS) - 1
FFN_HIDDEN = -(-8 * D_MODEL // (3 * 256)) * 256
PLE_DIM = 256
EPS = 1e-6
NEG_INF = -1e30
PAST_LEN = 16384

Q_W = N_HEADS * HEAD_DIM
KV_W = N_KV_HEADS * HEAD_DIM
C_U, C_Q, C_K, C_V, C_GP, C_GA, C_END = 0, POOL_WIDTH, POOL_WIDTH + Q_W, POOL_WIDTH + Q_W + KV_W, \
    POOL_WIDTH + Q_W + 2 * KV_W, POOL_WIDTH + Q_W + 2 * KV_W + D_MODEL, POOL_WIDTH + Q_W + 2 * KV_W + 2 * D_MODEL

LANES = 128
S_AHEAD = 16
S_SLOTS = 16
GATE_COLS = 256
U_HALO = 24
SUBLANES = 8
VMEM_LIMIT = 56 * 1024 * 1024

BF16 = jnp.bfloat16
F32 = jnp.float32


def _dot(a, b):
    return jnp.dot(a, b, preferred_element_type=F32)


def _dot_nt(a, b):
    return lax.dot_general(a, b, (((1,), (1,)), ((), ())), preferred_element_type=F32)


def _rms(x, g):
    y = x * lax.rsqrt(jnp.mean(x * x, axis=-1, keepdims=True) + EPS)
    return y * g


def _rope(x, cos, sg, lo8):
    outs = []
    for c in range(x.shape[1] // LANES):
        xc = x[:, c * LANES:(c + 1) * LANES]
        partner = jnp.where(lo8, pltpu.roll(xc, LANES - ROT_DIMS // 2, 1), pltpu.roll(xc, ROT_DIMS // 2, 1))
        outs.append(xc * cos + partner * sg)
    return jnp.concatenate(outs, axis=1)


def _lane_masks():
    lane = lax.broadcasted_iota(jnp.int32, (1, LANES), 1)
    lo8 = (lane % HEAD_DIM) < (ROT_DIMS // 2)
    lo64 = lane < HEAD_DIM
    return lo8, lo64


def _pool_mix(u, win_sum_fn, cnt_fn, gw_ref, pscale_ref):
    mixed = []
    for g, w in enumerate(POOL_WINDOWS):
        cols = slice(g * POOL_GC, (g + 1) * POOL_GC)
        ug = u[:, cols]
        m = win_sum_fn(g, w, ug) / cnt_fn(w) - ug
        mixed.append(_dot(m.astype(BF16), gw_ref[g]) * pscale_ref[:, cols])
    return jnp.concatenate(mixed, axis=1)


def _mixer_kernel(sinks_ref, x_ref, cos_ref, sg_ref, ln1_ref, w_in_ref, gw_ref, pscale_ref, w_pb_ref, w_ab_ref,
                  w_out_ref, knt_ref, vnt_ref, ckt_ref, cvt_ref, *rest, tm, n_cast):
    cast_src = rest[:n_cast]
    h_ref, ko_ref, vo_ref, po_ref, nkt_ref, nvt_ref = rest[n_cast:n_cast + 6]
    cast_dst = rest[n_cast + 6:2 * n_cast + 6]
    kl_scr, kh_scr, vl_scr, vh_scr, u_scr, lvl_scr, attn_scr, q_scr, s_scr, g_scr = rest[2 * n_cast + 6:]
    t = pl.program_id(1)
    for src, dst in zip(cast_src, cast_dst):
        dst[...] = src[...].astype(BF16)
    lo8, lo64 = _lane_masks()

    @pl.when(t == 0)
    def _():
        for scr in (kl_scr, kh_scr, vl_scr, vh_scr):
            scr[:, 0:BLOCK, :] = jnp.zeros((N_KV_HEADS, BLOCK, LANES), BF16)
        u_scr[0:U_HALO, :] = jnp.zeros((U_HALO, POOL_WIDTH), F32)
        lvl_scr[:, 0:SUBLANES, :] = jnp.zeros((len(POOL_WINDOWS), SUBLANES, POOL_GC), F32)

    x = x_ref[0]
    xn = _rms(x, ln1_ref[...]).astype(BF16)
    cos = cos_ref[...]
    sg = sg_ref[...]

    k = _rope(_dot(xn, w_in_ref[:, C_K:C_V]), cos, sg, lo8)
    v = _dot(xn, w_in_ref[:, C_V:C_GP])
    q = (_rope(_dot(xn, w_in_ref[:, C_Q:C_K]), cos, sg, lo8) * (HEAD_DIM ** -0.5)).astype(BF16)
    q_scr[...] = q
    u = _dot(xn, w_in_ref[:, C_U:C_Q])
    ko_ref[0] = k[tm - WINDOW:, :].T
    vo_ref[0] = v[tm - WINDOW:, :].T

    zero = jnp.zeros((tm, LANES), F32)
    for src, lo_scr, hi_scr in ((k, kl_scr, kh_scr), (v, vl_scr, vh_scr)):
        for p in range(KV_W // LANES):
            xp = src[:, p * LANES:(p + 1) * LANES]
            xs = pltpu.roll(xp, HEAD_DIM, 1)
            lo_scr[2 * p, BLOCK:, :] = jnp.where(lo64, xp, zero).astype(BF16)
            hi_scr[2 * p, BLOCK:, :] = jnp.where(lo64, zero, xs).astype(BF16)
            lo_scr[2 * p + 1, BLOCK:, :] = jnp.where(lo64, xs, zero).astype(BF16)
            hi_scr[2 * p + 1, BLOCK:, :] = jnp.where(lo64, zero, xp).astype(BF16)

    qi = lax.broadcasted_iota(jnp.int32, (BLOCK, BLOCK), 0)
    ci = lax.broadcasted_iota(jnp.int32, (BLOCK, BLOCK), 1)
    from_prev = ci > qi
    bias0 = jnp.where(jnp.logical_and(t == 0, from_prev), NEG_INF, 0.0).astype(F32)

    ones_lo = jnp.broadcast_to(jnp.where(lo64, 1.0, 0.0).astype(BF16), (2 * BLOCK, LANES))
    ones_hi = jnp.broadcast_to(jnp.where(lo64, 0.0, 1.0).astype(BF16), (2 * BLOCK, LANES))

    units = [(j, kh) for j in range(tm // BLOCK) for kh in range(N_KV_HEADS)]

    def scores(i):
        j, kh = units[i]
        rows = slice(j * BLOCK, (j + 1) * BLOCK)
        win = slice(j * BLOCK, (j + 2) * BLOCK)
        qq = jnp.concatenate([q_scr[rows, (2 * kh + a) * LANES:(2 * kh + a + 1) * LANES] for a in range(2)], axis=0)
        kcat = jnp.concatenate([kl_scr[kh, win, :], kh_scr[kh, win, :]], axis=0)
        s = _dot_nt(qq, kcat)
        for a in range(2):
            for half in range(2):
                sa = s[a * BLOCK:(a + 1) * BLOCK, half * 2 * BLOCK:(half + 1) * 2 * BLOCK]
                folded = jnp.where(from_prev, sa[:, :BLOCK], sa[:, BLOCK:])
                if j == 0:
                    folded = folded + bias0
                s_scr[i % S_SLOTS, a * BLOCK:(a + 1) * BLOCK, half * BLOCK:(half + 1) * BLOCK] = folded

    units_per_gate = len(units) * GATE_COLS // (2 * D_MODEL)
    for i in range(S_AHEAD):
        scores(i)
    for i, (j, kh) in enumerate(units):
        if i + S_AHEAD < len(units):
            scores(i + S_AHEAD)
        rows = slice(j * BLOCK, (j + 1) * BLOCK)
        win = slice(j * BLOCK, (j + 2) * BLOCK)
        vcat = jnp.concatenate([
            jnp.concatenate([vl_scr[kh, win, :], ones_lo], axis=1),
            jnp.concatenate([vh_scr[kh, win, :], ones_hi], axis=1)], axis=0)
        ps, sink_terms = [], []
        for a in range(2):
            es, st = [], []
            for half in range(2):
                sh = s_scr[i % S_SLOTS, a * BLOCK:(a + 1) * BLOCK, half * BLOCK:(half + 1) * BLOCK]
                sink = sinks_ref[4 * kh + 2 * a + half]
                m = jnp.maximum(jnp.max(sh, axis=1, keepdims=True), sink)
                e = jnp.exp(sh - m)
                es.append(jnp.where(from_prev, e, 0.0).astype(BF16))
                es.append(jnp.where(from_prev, 0.0, e).astype(BF16))
                st.append(jnp.exp(sink - m))
            ps.append(jnp.concatenate(es, axis=1))
            sink_terms.append(jnp.where(lo64, st[0], st[1]))
        o = _dot(jnp.concatenate(ps, axis=0), vcat)
        for a in range(2):
            oa = o[a * BLOCK:(a + 1) * BLOCK]
            attn_scr[rows, (2 * kh + a) * LANES:(2 * kh + a + 1) * LANES] = (
                oa[:, :LANES] / (oa[:, LANES:] + sink_terms[a])).astype(BF16)
        if i % units_per_gate == units_per_gate - 1:
            gcols = slice((i // units_per_gate) * GATE_COLS, (i // units_per_gate + 1) * GATE_COLS)
            g_scr[:, gcols] = jax.nn.sigmoid(_dot(xn, w_in_ref[:, C_GP + gcols.start:C_GP + gcols.stop]))

    u_scr[U_HALO:, :] = u
    po_ref[0] = u_scr[U_HALO + tm - POOL_STATE:U_HALO + tm, :]
    pos = t * tm + lax.broadcasted_iota(jnp.int32, (tm, 1), 0)

    def win_sum(g, w, ug):
        cols = slice(g * POOL_GC, (g + 1) * POOL_GC)
        n = U_HALO - SUBLANES + tm
        src, span = u_scr, 1
        while span < w:
            lvl = src[SUBLANES:SUBLANES + n, cols] + src[SUBLANES - span:SUBLANES - span + n, cols]
            span *= 2
            if span < w:
                lvl_scr[g, SUBLANES:SUBLANES + n, :] = lvl
                src, cols = lvl_scr.at[g], slice(None)
        return lvl[U_HALO - SUBLANES:, :]

    pooled = _pool_mix(u, win_sum, lambda w: jnp.minimum(w, pos + 1).astype(F32), gw_ref, pscale_ref)

    for scr in (kl_scr, kh_scr, vl_scr, vh_scr):
        scr[:, 0:BLOCK, :] = scr[:, tm:tm + BLOCK, :]
    u_scr[0:U_HALO, :] = u_scr[tm:tm + U_HALO, :]

    _shift_caches(pl.program_id(0) * pl.num_programs(1) + t, ckt_ref.shape[0], knt_ref, vnt_ref, ckt_ref, cvt_ref,
                  nkt_ref, nvt_ref)

    merged = g_scr[:, 0:D_MODEL] * _dot(pooled.astype(BF16), w_pb_ref[...])
    merged = merged + g_scr[:, D_MODEL:2 * D_MODEL] * _dot(attn_scr[...], w_ab_ref[...])
    h_ref[0] = x + _dot(merged.astype(BF16), w_out_ref[...])


def _const_spec(shape):
    nd = len(shape)
    return pl.BlockSpec(shape, lambda *_: (0,) * nd, pipeline_mode=pl.Buffered(1))


def _cast_block_rows(rows, steps):
    br = 2 * SUBLANES
    while rows % br or rows // br > steps:
        br *= 2
    return br


def _cast_specs(to_cast, steps, step_of):
    specs = []
    for w in to_cast:
        br = _cast_block_rows(w.shape[0], steps)
        last = w.shape[0] // br - 1
        specs.append(pl.BlockSpec(
            (br, w.shape[1]), lambda *idx, last=last: (jnp.minimum(step_of(*idx), last), 0)))
    return specs


def _prompt_mixer(x, cos, sg, sinks, ln1, w_in, gw, pscale, w_pb, w_ab, w_out, knt, vnt, ckt, cvt, to_cast, tm):
    b, s, d = x.shape
    nt = s // tm
    cast_specs = _cast_specs(to_cast, b * nt, lambda bi, ti, *_: bi * nt + ti)
    nb = ckt.shape[0] // (b * nt)
    assert nb * b * nt == ckt.shape[0]
    cache_spec = pl.BlockSpec((nb,) + ckt.shape[1:], lambda bi, ti, *_: (bi * nt + ti, 0, 0))
    grid_spec = pltpu.PrefetchScalarGridSpec(
        num_scalar_prefetch=1,
        grid=(b, nt),
        in_specs=[
            pl.BlockSpec((1, tm, d), lambda bi, ti, *_: (bi, ti, 0)),
            pl.BlockSpec((tm, LANES), lambda bi, ti, *_: (ti, 0)),
            pl.BlockSpec((tm, LANES), lambda bi, ti, *_: (ti, 0)),
            _const_spec((1, d)),
            _const_spec(w_in.shape),
            _const_spec(gw.shape),
            _const_spec((1, POOL_WIDTH)),
            _const_spec(w_pb.shape),
            _const_spec(w_ab.shape),
            _const_spec(w_out.shape),
            _const_spec(knt.shape),
            _const_spec(vnt.shape),
            cache_spec,
            cache_spec,
        ] + cast_specs,
        out_specs=[
            pl.BlockSpec((1, tm, d), lambda bi, ti, *_: (bi, ti, 0)),
            pl.BlockSpec((1, KV_W, WINDOW), lambda bi, ti, *_: (bi, 0, 0)),
            pl.BlockSpec((1, KV_W, WINDOW), lambda bi, ti, *_: (bi, 0, 0)),
            pl.BlockSpec((1, POOL_STATE, POOL_WIDTH), lambda bi, ti, *_: (bi, 0, 0)),
            cache_spec,
            cache_spec,
        ] + cast_specs,
        scratch_shapes=[
            pltpu.VMEM((N_KV_HEADS, BLOCK + tm, LANES), BF16),
            pltpu.VMEM((N_KV_HEADS, BLOCK + tm, LANES), BF16),
            pltpu.VMEM((N_KV_HEADS, BLOCK + tm, LANES), BF16),
            pltpu.VMEM((N_KV_HEADS, BLOCK + tm, LANES), BF16),
            pltpu.VMEM((U_HALO + tm, POOL_WIDTH), F32),
            pltpu.VMEM((len(POOL_WINDOWS), U_HALO + tm, POOL_GC), F32),
            pltpu.VMEM((tm, Q_W), BF16),
            pltpu.VMEM((tm, Q_W), BF16),
            pltpu.VMEM((S_SLOTS, 2 * BLOCK, 2 * BLOCK), F32),
            pltpu.VMEM((tm, 2 * D_MODEL), F32),
        ],
    )
    return pl.pallas_call(
        functools.partial(_mixer_kernel, tm=tm, n_cast=len(to_cast)),
        grid_spec=grid_spec,
        out_shape=[
            jax.ShapeDtypeStruct((b, s, d), F32),
            jax.ShapeDtypeStruct((b, KV_W, WINDOW), F32),
            jax.ShapeDtypeStruct((b, KV_W, WINDOW), F32),
            jax.ShapeDtypeStruct((b, POOL_STATE, POOL_WIDTH), F32),
            jax.ShapeDtypeStruct(ckt.shape, F32),
            jax.ShapeDtypeStruct(cvt.shape, F32),
        ] + [jax.ShapeDtypeStruct(w.shape, BF16) for w in to_cast],
        compiler_params=pltpu.CompilerParams(
            dimension_semantics=("arbitrary", "arbitrary"), vmem_limit_bytes=VMEM_LIMIT),
        name="prompt_mixer",
    )(sinks, x, cos, sg, ln1, w_in, gw, pscale, w_pb, w_ab, w_out, knt, vnt, ckt, cvt, *to_cast)


FFN_CHUNKS = ((0, 1024), (1024, 2048), (2048, FFN_HIDDEN))


def _ffn_kernel(h_ref, p_ref, ln2_ref, w1_ref, w2_ref, w_pp_ref, pn_ref, w_pg_ref, fn_ref, y_ref, *, tm):
    def tile_stages(r0):
        rows = slice(r0, r0 + tm)
        st = {}

        def up_proj(c):
            lo, hi = FFN_CHUNKS[c]
            if c == 0:
                st['h'] = h_ref[rows, :]
                st['hn'] = _rms(st['h'], ln2_ref[...]).astype(BF16)
                st['acc'] = st['h']
            st['gate', c] = _dot(st['hn'], w1_ref[:, lo:hi])
            st['up', c] = _dot(st['hn'], w1_ref[:, FFN_HIDDEN + lo:FFN_HIDDEN + hi])
            if c == len(FFN_CHUNKS) - 1:
                st['e'] = _rms(_dot(p_ref[rows, :].astype(BF16), w_pp_ref[...]), pn_ref[...])

        def down_proj(c):
            lo, hi = FFN_CHUNKS[c]
            gate = st.pop(('gate', c))
            act = (gate * jax.nn.sigmoid(gate) * st.pop(('up', c))).astype(BF16)
            st['acc'] = st['acc'] + _dot(act, w2_ref[lo:hi, :])

        def ple_gate():
            st['g'] = _dot(st['acc'].astype(BF16), w_pg_ref[...])

        def finish():
            h3 = st['acc'] + jax.nn.sigmoid(st['g']) * st['e']
            y_ref[rows, :] = _rms(h3, fn_ref[...])

        n = len(FFN_CHUNKS)
        steps = [functools.partial(up_proj, 0)]
        for c in range(1, n):
            steps += [functools.partial(up_proj, c), functools.partial(down_proj, c - 1)]
        return steps + [functools.partial(down_proj, n - 1), ple_gate, finish]

    tail = 3
    order = []
    for r0 in range(0, h_ref.shape[0], tm):
        steps = tile_stages(r0)
        held, order = order[len(order) - tail:] if order else [], order[:len(order) - tail] if order else []
        for k in range(max(len(held), tail)):
            order += steps[k:k + 1] + held[k:k + 1]
        order += steps[tail:]
    for step in order:
        step()


def _ffn(h, p, ln2, w1, w2, w_pp, pn, w_pg, fn, tm, nsub):
    n, d = h.shape
    blk = tm * nsub
    return pl.pallas_call(
        functools.partial(_ffn_kernel, tm=tm),
        grid=(n // blk,),
        in_specs=[
            pl.BlockSpec((blk, d), lambda i: (i, 0)),
            pl.BlockSpec((blk, PLE_DIM), lambda i: (i, 0)),
            _const_spec((1, d)),
            _const_spec(w1.shape),
            _const_spec(w2.shape),
            _const_spec(w_pp.shape),
            _const_spec((1, d)),
            _const_spec(w_pg.shape),
            _const_spec((1, d)),
        ],
        out_specs=pl.BlockSpec((blk, d), lambda i: (i, 0)),
        out_shape=jax.ShapeDtypeStruct((n, d), F32),
        compiler_params=pltpu.CompilerParams(dimension_semantics=("arbitrary",), vmem_limit_bytes=VMEM_LIMIT),
        name="ffn_ple_norm",
    )(h, p, ln2, w1, w2, w_pp, pn, w_pg, fn)


def _sample_pre_kernel(x_ref, cos_ref, sg_ref, ln1_ref, w_in_ref, qe_ref, knt_ref, vnt_ref, kn_ref, vn_ref, u_ref):
    n = x_ref.shape[0]
    lo8, _ = _lane_masks()
    xn = _rms(x_ref[...], ln1_ref[...]).astype(BF16)
    cos = cos_ref[...]
    sg = sg_ref[...]
    w = lambda lo, hi: w_in_ref[:, lo:hi].astype(BF16)
    u_ref[...] = _dot(xn, w(C_U, C_Q))
    q = (_rope(_dot(xn, w(C_Q, C_K)), cos, sg, lo8) * (HEAD_DIM ** -0.5)).astype(BF16)
    kn = _rope(_dot(xn, w(C_K, C_V)), cos, sg, lo8)
    vn = _dot(xn, w(C_V, C_GP))
    kn_ref[...] = kn
    vn_ref[...] = vn
    knt_ref[...] = kn.T
    vnt_ref[...] = vn.T
    ii = lax.broadcasted_iota(jnp.int32, (Q_W, KV_W), 0)
    jj = lax.broadcasted_iota(jnp.int32, (Q_W, KV_W), 1)
    for r in range(N_HEADS):
        kh = r // GROUP
        sel = ((ii - r * HEAD_DIM) == (jj - kh * HEAD_DIM)) & (jj >= kh * HEAD_DIM) & (jj < (kh + 1) * HEAD_DIM)
        qr = _dot(q, jnp.where(sel, 1.0, 0.0).astype(BF16))
        for c in range(KV_W // LANES):
            qe_ref[c, pl.ds(r, n, stride=N_HEADS), :] = qr[:, c * LANES:(c + 1) * LANES]


def _sample_pre(x, cos, sg, ln1, w_in):
    n, d = x.shape
    return pl.pallas_call(
        _sample_pre_kernel,
        grid=(1,),
        in_specs=[
            _const_spec((n, d)),
            _const_spec((1, LANES)),
            _const_spec((1, LANES)),
            _const_spec((1, d)),
            pl.BlockSpec((d, C_GP), lambda i: (0, 0), pipeline_mode=pl.Buffered(1)),
        ],
        out_specs=[
            pl.BlockSpec((KV_W // LANES, n * N_HEADS, LANES), lambda i: (0, 0, 0)),
            pl.BlockSpec((KV_W, n), lambda i: (0, 0)),
            pl.BlockSpec((KV_W, n), lambda i: (0, 0)),
            pl.BlockSpec((n, KV_W), lambda i: (0, 0)),
            pl.BlockSpec((n, KV_W), lambda i: (0, 0)),
            pl.BlockSpec((n, POOL_WIDTH), lambda i: (0, 0)),
        ],
        out_shape=[
            jax.ShapeDtypeStruct((KV_W // LANES, n * N_HEADS, LANES), F32),
            jax.ShapeDtypeStruct((KV_W, n), F32),
            jax.ShapeDtypeStruct((KV_W, n), F32),
            jax.ShapeDtypeStruct((n, KV_W), F32),
            jax.ShapeDtypeStruct((n, KV_W), F32),
            jax.ShapeDtypeStruct((n, POOL_WIDTH), F32),
        ],
        compiler_params=pltpu.CompilerParams(dimension_semantics=("arbitrary",), vmem_limit_bytes=VMEM_LIMIT),
        name="sample_pre",
    )(x, cos, sg, ln1, w_in)


def _shift_caches(step, nb, knt_ref, vnt_ref, ckt_ref, cvt_ref, nkt_ref, nvt_ref):
    n = knt_ref.shape[1]
    w_cache = ckt_ref.shape[2]
    newest = lax.broadcasted_iota(jnp.int32, (1, w_cache), 1) == w_cache - 1
    shift = lax.rem(n - lax.rem(step * nb, n), n)
    kcols = pltpu.roll(knt_ref[...], shift, 1)
    vcols = pltpu.roll(vnt_ref[...], shift, 1)
    for bl in range(nb):
        nkt_ref[bl] = jnp.where(newest, kcols[:, bl:bl + 1], pltpu.roll(ckt_ref[bl], w_cache - 1, 1))
        nvt_ref[bl] = jnp.where(newest, vcols[:, bl:bl + 1], pltpu.roll(cvt_ref[bl], w_cache - 1, 1))


def _sample_attn_kernel(qe_ref, kn_ref, vn_ref, ckt_ref, cvt_ref, sink_ref, *rest, bb, n_cast):
    cast_src = rest[:n_cast]
    o_ref = rest[n_cast]
    cast_dst = rest[n_cast + 1:]
    for src, dst in zip(cast_src, cast_dst):
        dst[...] = src[...].astype(BF16)
    w_cache = ckt_ref.shape[2]
    oldest = lax.broadcasted_iota(jnp.int32, (1, w_cache), 1) == 0
    sink = sink_ref[...]
    rounded = lambda a: a.astype(BF16).astype(F32)
    scores = []
    for bl in range(bb):
        rows = slice(bl * N_HEADS, (bl + 1) * N_HEADS)
        qb = jnp.concatenate([qe_ref[c, rows, :] for c in range(KV_W // LANES)], axis=1).astype(BF16)
        s_old = jnp.where(oldest, NEG_INF, _dot(qb, ckt_ref[bl].astype(BF16)))
        s_new = jnp.sum(qb.astype(F32) * rounded(kn_ref[bl:bl + 1, :]), axis=1, keepdims=True)
        scores.append((s_old, s_new))
    probs = []
    for s_old, s_new in scores:
        m = jnp.maximum(jnp.maximum(jnp.max(s_old, axis=-1, keepdims=True), s_new), sink)
        e_old = jnp.exp(s_old - m)
        e_new = jnp.exp(s_new - m)
        denom = jnp.sum(e_old, axis=-1, keepdims=True) + e_new + jnp.exp(sink - m)
        probs.append((e_old.astype(BF16), rounded(e_new), denom))
    for bl, (e_old, e_new, denom) in enumerate(probs):
        rows = slice(bl * N_HEADS, (bl + 1) * N_HEADS)
        o = (_dot_nt(e_old, cvt_ref[bl].astype(BF16)) + e_new * rounded(vn_ref[bl:bl + 1, :])) / denom
        for c in range(KV_W // LANES):
            o_ref[c, rows, :] = o[:, c * LANES:(c + 1) * LANES]


def _sample_attn(qe, kn, vn, ckt, cvt, sink, to_cast, bb):
    n, _, w_cache = ckt.shape
    cast_specs = _cast_specs(to_cast, n // bb, lambda i: i)
    cache_spec = pl.BlockSpec((bb, KV_W, w_cache), lambda i: (i, 0, 0))
    new_spec = pl.BlockSpec((bb, KV_W), lambda i: (i, 0))
    head_spec = pl.BlockSpec((KV_W // LANES, bb * N_HEADS, LANES), lambda i: (0, i, 0))
    return pl.pallas_call(
        functools.partial(_sample_attn_kernel, bb=bb, n_cast=len(to_cast)),
        grid=(n // bb,),
        in_specs=[head_spec, new_spec, new_spec, cache_spec, cache_spec, _const_spec((N_HEADS, 1))] + cast_specs,
        out_specs=[head_spec] + cast_specs,
        out_shape=[jax.ShapeDtypeStruct((KV_W // LANES, n * N_HEADS, LANES), F32)]
        + [jax.ShapeDtypeStruct(w.shape, BF16) for w in to_cast],
        compiler_params=pltpu.CompilerParams(dimension_semantics=("arbitrary",), vmem_limit_bytes=VMEM_LIMIT),
        name="sample_attn",
    )(qe, kn, vn, ckt, cvt, sink, *to_cast)


def _sample_post_kernel(x_ref, u_ref, st_ref, o_ref, ln1_ref, w_g_ref, gw_ref, pscale_ref, w_pb_ref, w_ab_ref,
                        w_out_ref, h_ref, nst_ref):
    x = x_ref[...]
    n = x.shape[0]
    xn = _rms(x, ln1_ref[...]).astype(BF16)
    u = u_ref[...]
    nst_ref[0:POOL_STATE - 1] = st_ref[1:POOL_STATE]
    nst_ref[POOL_STATE - 1] = u

    def win_sum(g, w, ug):
        acc = ug
        for i in range(1, w):
            acc = acc + st_ref[POOL_STATE - i, :, g * POOL_GC:(g + 1) * POOL_GC]
        return acc

    pooled = _pool_mix(u, win_sum, lambda w: jnp.float32(min(w, PAST_LEN + 1)), gw_ref, pscale_ref)
    merged = jax.nn.sigmoid(_dot(xn, w_g_ref[:, 0:D_MODEL])) * _dot(pooled.astype(BF16), w_pb_ref[...])
    ab = jnp.zeros((n, D_MODEL), F32)
    for r in range(N_HEADS):
        kh = r // GROUP
        parts = []
        if kh > 0:
            parts.append(jnp.zeros((kh * HEAD_DIM, D_MODEL), BF16))
        parts.append(w_ab_ref[r * HEAD_DIM:(r + 1) * HEAD_DIM, :])
        if kh < N_KV_HEADS - 1:
            parts.append(jnp.zeros(((N_KV_HEADS - 1 - kh) * HEAD_DIM, D_MODEL), BF16))
        o_r = jnp.concatenate([o_ref[c, pl.ds(r, n, stride=N_HEADS), :] for c in range(KV_W // LANES)], axis=1)
        ab = ab + _dot(o_r.astype(BF16), jnp.concatenate(parts, axis=0))
    merged = merged + jax.nn.sigmoid(_dot(xn, w_g_ref[:, D_MODEL:2 * D_MODEL])) * ab
    h_ref[...] = x + _dot(merged.astype(BF16), w_out_ref[...])


def _sample_post(x, u, st, o, ln1, w_in, gw, pscale, w_pb, w_ab, w_out):
    n, d = x.shape
    return pl.pallas_call(
        _sample_post_kernel,
        grid=(1,),
        in_specs=[
            _const_spec((n, d)),
            _const_spec((n, POOL_WIDTH)),
            _const_spec(st.shape),
            _const_spec((KV_W // LANES, n * N_HEADS, LANES)),
            _const_spec((1, d)),
            pl.BlockSpec((d, 2 * D_MODEL), lambda i: (0, 1), pipeline_mode=pl.Buffered(1)),
            _const_spec(gw.shape),
            _const_spec((1, POOL_WIDTH)),
            _const_spec(w_pb.shape),
            _const_spec(w_ab.shape),
            _const_spec(w_out.shape),
        ],
        out_specs=[pl.BlockSpec((n, d), lambda i: (0, 0)), pl.BlockSpec(st.shape, lambda i: (0, 0, 0))],
        out_shape=[jax.ShapeDtypeStruct((n, d), F32), jax.ShapeDtypeStruct(st.shape, F32)],
        compiler_params=pltpu.CompilerParams(dimension_semantics=("arbitrary",), vmem_limit_bytes=VMEM_LIMIT),
        name="sample_post",
    )(x, u, st, o, ln1, w_in, gw, pscale, w_pb, w_ab, w_out)


def _rope_tables(first_pos, n):
    half = ROT_DIMS // 2
    inv = ROPE_THETA ** (-(np.arange(0, ROT_DIMS, 2, dtype=np.float64) / ROT_DIMS))
    ang = np.arange(first_pos, first_pos + n, dtype=np.float64)[:, None] * inv[None, :]
    cos, sin = np.cos(ang), np.sin(ang)
    rest = HEAD_DIM - 2 * half
    c64 = np.concatenate([cos, cos, np.ones((n, rest))], axis=1)
    s64 = np.concatenate([-sin, sin, np.zeros((n, rest))], axis=1)
    reps = LANES // HEAD_DIM
    return jnp.asarray(np.tile(c64, (1, reps)), F32), jnp.asarray(np.tile(s64, (1, reps)), F32)


def kernel(x_prompt, x_sample, p_prompt, p_sample, cache_k, cache_v, state_pool, ln1, w_in, pool_group_w, pool_scale,
           attn_sinks, w_pool_branch, w_attn_branch, w_out, ln2, w_ffn_in, w_ffn_out, w_ple_proj, ple_norm,
           w_ple_gate, final_norm):
    depth = ln1.shape[0]
    b, s, d = x_prompt.shape
    bd, t_dec, _ = x_sample.shape
    w_cache = cache_k.shape[2]
    assert depth == 1 and t_dec == 1 and w_cache == WINDOW and s % BLOCK == 0 and d == D_MODEL
    tm = 512
    assert s % tm == 0

    cos_p, sg_p = _rope_tables(0, s)
    cos_s, sg_s = _rope_tables(PAST_LEN, t_dec)

    hp = x_prompt
    hs = x_sample.reshape(bd, d)
    row = lambda a: a.reshape(1, -1)
    nkp, nvp, npp, nks, nvs, nps = [], [], [], [], [], []
    for i in range(depth):
        to_fm = lambda c: jnp.transpose(c, (0, 2, 3, 1)).reshape(bd, KV_W, w_cache)
        from_fm = lambda c: jnp.transpose(c.reshape(bd, N_KV_HEADS, HEAD_DIM, w_cache), (0, 3, 1, 2))
        ckt, cvt = to_fm(cache_k[i]), to_fm(cache_v[i])
        qe, knt, vnt, kn, vn, un = _sample_pre(hs, cos_s, sg_s, row(ln1[i]), w_in[i])
        o, wi, gw, wpb, wab, wo = _sample_attn(
            qe, kn, vn, ckt, cvt, attn_sinks[i].reshape(N_HEADS, 1),
            (w_in[i], pool_group_w[i].reshape(POOL_WIDTH, POOL_GC), w_pool_branch[i], w_attn_branch[i], w_out[i]), 8)
        gw = gw.reshape(len(POOL_WINDOWS), POOL_GC, POOL_GC)
        h1, kp, vp, pp, nkt, nvt, w1, w2, wpp, wpg = _prompt_mixer(
            hp, cos_p, sg_p, attn_sinks[i], row(ln1[i]), wi, gw, row(pool_scale[i]), wpb, wab, wo,
            knt, vnt, ckt, cvt, (w_ffn_in[i], w_ffn_out[i], w_ple_proj[i], w_ple_gate[i]), tm)
        ffn_args = (row(ln2[i]), w1, w2, wpp, row(ple_norm[i]), wpg)
        hp = _ffn(h1.reshape(b * s, d), p_prompt[i].reshape(b * s, PLE_DIM), *ffn_args, row(final_norm),
                  tm, 2).reshape(b, s, d)
        from_fm_p = lambda c: jnp.transpose(c.reshape(b, N_KV_HEADS, HEAD_DIM, w_cache), (0, 3, 1, 2))
        nkp.append(from_fm_p(kp))
        nvp.append(from_fm_p(vp))
        npp.append(pp)

        h1s, nst = _sample_post(hs, un, jnp.transpose(state_pool[i], (1, 0, 2)), o, row(ln1[i]), wi, gw,
                                row(pool_scale[i]), wpb, wab, wo)
        hs = _ffn(h1s, p_sample[i].reshape(bd * t_dec, PLE_DIM), *ffn_args, row(final_norm), bd, 1)
        nks.append(from_fm(nkt))
        nvs.append(from_fm(nvt))
        nps.append(jnp.transpose(nst, (1, 0, 2)))

    return (hp, hs.reshape(bd, t_dec, d), jnp.stack(nkp), jnp.stack(nvp), jnp.stack(npp),
            jnp.stack(nks), jnp.stack(nvs), jnp.stack(nps))
```

```python
import functools

import jax
import jax.numpy as jnp
import numpy as np
from jax import lax
from jax.experimental import pallas as pl
from jax.experimental.pallas import tpu as pltpu

D_MODEL = 1024
HEAD_DIM = 64
N_HEADS = D_MODEL // HEAD_DIM
N_KV_HEADS = N_HEADS // 4
GROUP = N_HEADS // N_KV_HEADS
ROT_DIMS = HEAD_DIM // 4
ROPE_THETA = 500000.0
WINDOW = 128
BLOCK = 128
POOL_WIDTH = D_MODEL // 2
POOL_WINDOWS = (2, 4, 8, 16)
POOL_GC = POOL_WIDTH // len(POOL_WINDOWS)
POOL_STATE = max(POOL_WINDOWS) - 1
FFN_HIDDEN = -(-8 * D_MODEL // (3 * 256)) * 256
PLE_DIM = 256
EPS = 1e-6
NEG_INF = -1e30
PAST_LEN = 16384

Q_W = N_HEADS * HEAD_DIM
KV_W = N_KV_HEADS * HEAD_DIM
C_U, C_Q, C_K, C_V, C_GP, C_GA, C_END = 0, POOL_WIDTH, POOL_WIDTH + Q_W, POOL_WIDTH + Q_W + KV_W, \
    POOL_WIDTH + Q_W + 2 * KV_W, POOL_WIDTH + Q_W + 2 * KV_W + D_MODEL, POOL_WIDTH + Q_W + 2 * KV_W + 2 * D_MODEL

LANES = 128
S_AHEAD = 16
S_SLOTS = 16
GATE_COLS = 256
U_HALO = 24
SUBLANES = 8
VMEM_LIMIT = 56 * 1024 * 1024

BF16 = jnp.bfloat16
F32 = jnp.float32


def _dot(a, b):
    return jnp.dot(a, b, preferred_element_type=F32)


def _dot_nt(a, b):
    return lax.dot_general(a, b, (((1,), (1,)), ((), ())), preferred_element_type=F32)


def _rms(x, g):
    y = x * lax.rsqrt(jnp.mean(x * x, axis=-1, keepdims=True) + EPS)
    return y * g


def _rope(x, cos, sg, lo8):
    outs = []
    for c in range(x.shape[1] // LANES):
        xc = x[:, c * LANES:(c + 1) * LANES]
        partner = jnp.where(lo8, pltpu.roll(xc, LANES - ROT_DIMS // 2, 1), pltpu.roll(xc, ROT_DIMS // 2, 1))
        outs.append(xc * cos + partner * sg)
    return jnp.concatenate(outs, axis=1)


def _lane_masks():
    lane = lax.broadcasted_iota(jnp.int32, (1, LANES), 1)
    lo8 = (lane % HEAD_DIM) < (ROT_DIMS // 2)
    lo64 = lane < HEAD_DIM
    return lo8, lo64


def _pool_mix(u, win_sum_fn, cnt_fn, gw_ref, pscale_ref):
    mixed = []
    for g, w in enumerate(POOL_WINDOWS):
        cols = slice(g * POOL_GC, (g + 1) * POOL_GC)
        ug = u[:, cols]
        m = win_sum_fn(g, w, ug) / cnt_fn(w) - ug
        mixed.append(_dot(m.astype(BF16), gw_ref[g]) * pscale_ref[:, cols])
    return jnp.concatenate(mixed, axis=1)


def _mixer_kernel(sinks_ref, x_ref, cos_ref, sg_ref, ln1_ref, w_in_ref, gw_ref, pscale_ref, w_pb_ref, w_ab_ref,
                  w_out_ref, knt_ref, vnt_ref, ckt_ref, cvt_ref, *rest, tm, n_cast):
    cast_src = rest[:n_cast]
    h_ref, ko_ref, vo_ref, po_ref, nkt_ref, nvt_ref = rest[n_cast:n_cast + 6]
    cast_dst = rest[n_cast + 6:2 * n_cast + 6]
    kl_scr, kh_scr, vl_scr, vh_scr, u_scr, lvl_scr, attn_scr, q_scr, s_scr, g_scr = rest[2 * n_cast + 6:]
    t = pl.program_id(1)
    step = pl.program_id(0) * pl.num_programs(1) + t
    for src, dst in zip(cast_src, cast_dst):
        dst[...] = src[...].astype(BF16)
    lo8, lo64 = _lane_masks()

    @pl.when(t == 0)
    def _():
        for scr in (kl_scr, kh_scr, vl_scr, vh_scr):
            scr[:, 0:BLOCK, :] = jnp.zeros((N_KV_HEADS, BLOCK, LANES), BF16)
        u_scr[0:U_HALO, :] = jnp.zeros((U_HALO, POOL_WIDTH), F32)
        lvl_scr[:, 0:SUBLANES, :] = jnp.zeros((len(POOL_WINDOWS), SUBLANES, POOL_GC), F32)

    x = x_ref[0]
    xn = _rms(x, ln1_ref[...]).astype(BF16)
    cos = cos_ref[...]
    sg = sg_ref[...]

    k = _rope(_dot(xn, w_in_ref[:, C_K:C_V]), cos, sg, lo8)
    v = _dot(xn, w_in_ref[:, C_V:C_GP])
    q = (_rope(_dot(xn, w_in_ref[:, C_Q:C_K]), cos, sg, lo8) * (HEAD_DIM ** -0.5)).astype(BF16)
    q_scr[...] = q
    u = _dot(xn, w_in_ref[:, C_U:C_Q])
    ko_ref[0] = k[tm - WINDOW:, :].T
    vo_ref[0] = v[tm - WINDOW:, :].T

    zero = jnp.zeros((tm, LANES), F32)
    for src, lo_scr, hi_scr in ((k, kl_scr, kh_scr), (v, vl_scr, vh_scr)):
        for p in range(KV_W // LANES):
            xp = src[:, p * LANES:(p + 1) * LANES]
            xs = pltpu.roll(xp, HEAD_DIM, 1)
            lo_scr[2 * p, BLOCK:, :] = jnp.where(lo64, xp, zero).astype(BF16)
            hi_scr[2 * p, BLOCK:, :] = jnp.where(lo64, zero, xs).astype(BF16)
            lo_scr[2 * p + 1, BLOCK:, :] = jnp.where(lo64, xs, zero).astype(BF16)
            hi_scr[2 * p + 1, BLOCK:, :] = jnp.where(lo64, zero, xp).astype(BF16)

    qi = lax.broadcasted_iota(jnp.int32, (BLOCK, BLOCK), 0)
    ci = lax.broadcasted_iota(jnp.int32, (BLOCK, BLOCK), 1)
    from_prev = ci > qi
    bias0 = jnp.where(jnp.logical_and(t == 0, from_prev), NEG_INF, 0.0).astype(F32)

    ones_lo = jnp.broadcast_to(jnp.where(lo64, 1.0, 0.0).astype(BF16), (2 * BLOCK, LANES))
    ones_hi = jnp.broadcast_to(jnp.where(lo64, 0.0, 1.0).astype(BF16), (2 * BLOCK, LANES))

    units = [(j, kh) for j in range(tm // BLOCK) for kh in range(N_KV_HEADS)]

    def scores(i):
        j, kh = units[i]
        rows = slice(j * BLOCK, (j + 1) * BLOCK)
        win = slice(j * BLOCK, (j + 2) * BLOCK)
        qq = jnp.concatenate([q_scr[rows, (2 * kh + a) * LANES:(2 * kh + a + 1) * LANES] for a in range(2)], axis=0)
        kcat = jnp.concatenate([kl_scr[kh, win, :], kh_scr[kh, win, :]], axis=0)
        s = _dot_nt(qq, kcat)
        for a in range(2):
            for half in range(2):
                sa = s[a * BLOCK:(a + 1) * BLOCK, half * 2 * BLOCK:(half + 1) * 2 * BLOCK]
                folded = jnp.where(from_prev, sa[:, :BLOCK], sa[:, BLOCK:])
                if j == 0:
                    folded = folded + bias0
                s_scr[i % S_SLOTS, a * BLOCK:(a + 1) * BLOCK, half * BLOCK:(half + 1) * BLOCK] = folded

    units_per_gate = len(units) * GATE_COLS // (2 * D_MODEL)
    for i in range(S_AHEAD):
        scores(i)
    def softmax(i):
        kh = units[i][1]
        ps, sink_terms = [], []
        for a in range(2):
            es, st = [], []
            for half in range(2):
                sh = s_scr[i % S_SLOTS, a * BLOCK:(a + 1) * BLOCK, half * BLOCK:(half + 1) * BLOCK]
                sink = sinks_ref[4 * kh + 2 * a + half]
                m = jnp.maximum(jnp.max(sh, axis=1, keepdims=True), sink)
                e = jnp.exp(sh - m)
                es.append(jnp.where(from_prev, e, 0.0).astype(BF16))
                es.append(jnp.where(from_prev, 0.0, e).astype(BF16))
                st.append(jnp.exp(sink - m))
            ps.append(jnp.concatenate(es, axis=1))
            sink_terms.append(jnp.where(lo64, st[0], st[1]))
        return jnp.concatenate(ps, axis=0), sink_terms

    def weighted_values(i, p, sink_terms):
        j, kh = units[i]
        rows = slice(j * BLOCK, (j + 1) * BLOCK)
        win = slice(j * BLOCK, (j + 2) * BLOCK)
        vcat = jnp.concatenate([
            jnp.concatenate([vl_scr[kh, win, :], ones_lo], axis=1),
            jnp.concatenate([vh_scr[kh, win, :], ones_hi], axis=1)], axis=0)
        o = _dot(p, vcat)
        for a in range(2):
            oa = o[a * BLOCK:(a + 1) * BLOCK]
            attn_scr[rows, (2 * kh + a) * LANES:(2 * kh + a + 1) * LANES] = (
                oa[:, :LANES] / (oa[:, LANES:] + sink_terms[a])).astype(BF16)

    pending = None
    for i in range(len(units)):
        if i + S_AHEAD < len(units):
            scores(i + S_AHEAD)
        current = (i,) + softmax(i)
        if pending is not None:
            weighted_values(*pending)
        pending = current
        if i % units_per_gate == units_per_gate - 1:
            gcols = slice((i // units_per_gate) * GATE_COLS, (i // units_per_gate + 1) * GATE_COLS)
            g_scr[:, gcols] = jax.nn.sigmoid(_dot(xn, w_in_ref[:, C_GP + gcols.start:C_GP + gcols.stop]))
    weighted_values(*pending)

    u_scr[U_HALO:, :] = u
    po_ref[0] = u_scr[U_HALO + tm - POOL_STATE:U_HALO + tm, :]
    pos = t * tm + lax.broadcasted_iota(jnp.int32, (tm, 1), 0)

    def win_sum(g, w, ug):
        cols = slice(g * POOL_GC, (g + 1) * POOL_GC)
        n = U_HALO - SUBLANES + tm
        src, span = u_scr, 1
        while span < w:
            lvl = src[SUBLANES:SUBLANES + n, cols] + src[SUBLANES - span:SUBLANES - span + n, cols]
            span *= 2
            if span < w:
                lvl_scr[g, SUBLANES:SUBLANES + n, :] = lvl
                src, cols = lvl_scr.at[g], slice(None)
        return lvl[U_HALO - SUBLANES:, :]

    pooled = _pool_mix(u, win_sum, lambda w: jnp.minimum(w, pos + 1).astype(F32), gw_ref, pscale_ref)

    for scr in (kl_scr, kh_scr, vl_scr, vh_scr):
        scr[:, 0:BLOCK, :] = scr[:, tm:tm + BLOCK, :]
    u_scr[0:U_HALO, :] = u_scr[tm:tm + U_HALO, :]

    _shift_caches(step, ckt_ref.shape[0], knt_ref, vnt_ref, ckt_ref, cvt_ref, nkt_ref, nvt_ref)

    merged = g_scr[:, 0:D_MODEL] * _dot(pooled.astype(BF16), w_pb_ref[...])
    merged = merged + g_scr[:, D_MODEL:2 * D_MODEL] * _dot(attn_scr[...], w_ab_ref[...])
    h_ref[0] = x + _dot(merged.astype(BF16), w_out_ref[...])


def _const_spec(shape):
    nd = len(shape)
    return pl.BlockSpec(shape, lambda *_: (0,) * nd, pipeline_mode=pl.Buffered(1))


def _cast_block_rows(rows, steps):
    br = 2 * SUBLANES
    while rows % br or rows // br > steps:
        br *= 2
    return br


def _cast_specs(to_cast, steps, step_of):
    specs = []
    for w in to_cast:
        br = _cast_block_rows(w.shape[0], steps)
        last = w.shape[0] // br - 1
        specs.append(pl.BlockSpec(
            (br, w.shape[1]), lambda *idx, last=last: (jnp.minimum(step_of(*idx), last), 0)))
    return specs


def _prompt_mixer(x, cos, sg, sinks, ln1, w_in, gw, pscale, w_pb, w_ab, w_out, knt, vnt, ckt, cvt, to_cast, tm):
    b, s, d = x.shape
    nt = s // tm
    cast_specs = _cast_specs(to_cast, b * nt, lambda bi, ti, *_: bi * nt + ti)
    nb = ckt.shape[0] // (b * nt)
    assert nb * b * nt == ckt.shape[0]
    cache_spec = pl.BlockSpec((nb,) + ckt.shape[1:], lambda bi, ti, *_: (bi * nt + ti, 0, 0))
    grid_spec = pltpu.PrefetchScalarGridSpec(
        num_scalar_prefetch=1,
        grid=(b, nt),
        in_specs=[
            pl.BlockSpec((1, tm, d), lambda bi, ti, *_: (bi, ti, 0)),
            pl.BlockSpec((tm, LANES), lambda bi, ti, *_: (ti, 0)),
            pl.BlockSpec((tm, LANES), lambda bi, ti, *_: (ti, 0)),
            _const_spec((1, d)),
            _const_spec(w_in.shape),
            _const_spec(gw.shape),
            _const_spec((1, POOL_WIDTH)),
            _const_spec(w_pb.shape),
            _const_spec(w_ab.shape),
            _const_spec(w_out.shape),
            _const_spec(knt.shape),
            _const_spec(vnt.shape),
            cache_spec,
            cache_spec,
        ] + cast_specs,
        out_specs=[
            pl.BlockSpec((1, tm, d), lambda bi, ti, *_: (bi, ti, 0)),
            pl.BlockSpec((1, KV_W, WINDOW), lambda bi, ti, *_: (bi, 0, 0)),
            pl.BlockSpec((1, KV_W, WINDOW), lambda bi, ti, *_: (bi, 0, 0)),
            pl.BlockSpec((1, POOL_STATE, POOL_WIDTH), lambda bi, ti, *_: (bi, 0, 0)),
            cache_spec,
            cache_spec,
        ] + cast_specs,
        scratch_shapes=[
            pltpu.VMEM((N_KV_HEADS, BLOCK + tm, LANES), BF16),
            pltpu.VMEM((N_KV_HEADS, BLOCK + tm, LANES), BF16),
            pltpu.VMEM((N_KV_HEADS, BLOCK + tm, LANES), BF16),
            pltpu.VMEM((N_KV_HEADS, BLOCK + tm, LANES), BF16),
            pltpu.VMEM((U_HALO + tm, POOL_WIDTH), F32),
            pltpu.VMEM((len(POOL_WINDOWS), U_HALO + tm, POOL_GC), F32),
            pltpu.VMEM((tm, Q_W), BF16),
            pltpu.VMEM((tm, Q_W), BF16),
            pltpu.VMEM((S_SLOTS, 2 * BLOCK, 2 * BLOCK), F32),
            pltpu.VMEM((tm, 2 * D_MODEL), F32),
        ],
    )
    return pl.pallas_call(
        functools.partial(_mixer_kernel, tm=tm, n_cast=len(to_cast)),
        grid_spec=grid_spec,
        out_shape=[
            jax.ShapeDtypeStruct((b, s, d), F32),
            jax.ShapeDtypeStruct((b, KV_W, WINDOW), F32),
            jax.ShapeDtypeStruct((b, KV_W, WINDOW), F32),
            jax.ShapeDtypeStruct((b, POOL_STATE, POOL_WIDTH), F32),
            jax.ShapeDtypeStruct(ckt.shape, F32),
            jax.ShapeDtypeStruct(cvt.shape, F32),
        ] + [jax.ShapeDtypeStruct(w.shape, BF16) for w in to_cast],
        compiler_params=pltpu.CompilerParams(
            dimension_semantics=("arbitrary", "arbitrary"), vmem_limit_bytes=VMEM_LIMIT),
        name="prompt_mixer",
    )(sinks, x, cos, sg, ln1, w_in, gw, pscale, w_pb, w_ab, w_out, knt, vnt, ckt, cvt, *to_cast)


FFN_CHUNKS = ((0, 1024), (1024, 2048), (2048, FFN_HIDDEN))


def _ffn_kernel(h_ref, p_ref, ln2_ref, w1_ref, w2_ref, w_pp_ref, pn_ref, w_pg_ref, fn_ref, y_ref, *, tm):
    def tile_stages(r0):
        rows = slice(r0, r0 + tm)
        st = {}

        def up_proj(c):
            lo, hi = FFN_CHUNKS[c]
            if c == 0:
                st['h'] = h_ref[rows, :]
                st['hn'] = _rms(st['h'], ln2_ref[...]).astype(BF16)
                st['acc'] = st['h']
            st['gate', c] = _dot(st['hn'], w1_ref[:, lo:hi])
            st['up', c] = _dot(st['hn'], w1_ref[:, FFN_HIDDEN + lo:FFN_HIDDEN + hi])
            if c == len(FFN_CHUNKS) - 1:
                st['e'] = _rms(_dot(p_ref[rows, :].astype(BF16), w_pp_ref[...]), pn_ref[...])

        def down_proj(c):
            lo, hi = FFN_CHUNKS[c]
            gate = st.pop(('gate', c))
            act = (gate * jax.nn.sigmoid(gate) * st.pop(('up', c))).astype(BF16)
            st['acc'] = st['acc'] + _dot(act, w2_ref[lo:hi, :])

        def ple_gate():
            st['g'] = _dot(st['acc'].astype(BF16), w_pg_ref[...])

        def finish():
            h3 = st['acc'] + jax.nn.sigmoid(st['g']) * st['e']
            y_ref[rows, :] = _rms(h3, fn_ref[...])

        n = len(FFN_CHUNKS)
        steps = [functools.partial(up_proj, 0)]
        for c in range(1, n):
            steps += [functools.partial(up_proj, c), functools.partial(down_proj, c - 1)]
        return steps + [functools.partial(down_proj, n - 1), ple_gate, finish]

    tail = 3
    order = []
    for r0 in range(0, h_ref.shape[0], tm):
        steps = tile_stages(r0)
        held, order = order[len(order) - tail:] if order else [], order[:len(order) - tail] if order else []
        for k in range(max(len(held), tail)):
            order += steps[k:k + 1] + held[k:k + 1]
        order += steps[tail:]
    for step in order:
        step()


def _ffn(h, p, ln2, w1, w2, w_pp, pn, w_pg, fn, tm, nsub):
    n, d = h.shape
    blk = tm * nsub
    return pl.pallas_call(
        functools.partial(_ffn_kernel, tm=tm),
        grid=(n // blk,),
        in_specs=[
            pl.BlockSpec((blk, d), lambda i: (i, 0)),
            pl.BlockSpec((blk, PLE_DIM), lambda i: (i, 0)),
            _const_spec((1, d)),
            _const_spec(w1.shape),
            _const_spec(w2.shape),
            _const_spec(w_pp.shape),
            _const_spec((1, d)),
            _const_spec(w_pg.shape),
            _const_spec((1, d)),
        ],
        out_specs=pl.BlockSpec((blk, d), lambda i: (i, 0)),
        out_shape=jax.ShapeDtypeStruct((n, d), F32),
        compiler_params=pltpu.CompilerParams(dimension_semantics=("arbitrary",), vmem_limit_bytes=VMEM_LIMIT),
        name="ffn_ple_norm",
    )(h, p, ln2, w1, w2, w_pp, pn, w_pg, fn)


def _sample_pre_kernel(x_ref, cos_ref, sg_ref, ln1_ref, w_in_ref, qe_ref, knt_ref, vnt_ref, kn_ref, vn_ref, u_ref):
    n = x_ref.shape[0]
    lo8, _ = _lane_masks()
    xn = _rms(x_ref[...], ln1_ref[...]).astype(BF16)
    cos = cos_ref[...]
    sg = sg_ref[...]
    w = lambda lo, hi: w_in_ref[:, lo:hi].astype(BF16)
    u_ref[...] = _dot(xn, w(C_U, C_Q))
    q = (_rope(_dot(xn, w(C_Q, C_K)), cos, sg, lo8) * (HEAD_DIM ** -0.5)).astype(BF16)
    kn = _rope(_dot(xn, w(C_K, C_V)), cos, sg, lo8)
    vn = _dot(xn, w(C_V, C_GP))
    kn_ref[...] = kn
    vn_ref[...] = vn
    knt_ref[...] = kn.T
    vnt_ref[...] = vn.T
    ii = lax.broadcasted_iota(jnp.int32, (Q_W, KV_W), 0)
    jj = lax.broadcasted_iota(jnp.int32, (Q_W, KV_W), 1)
    for r in range(N_HEADS):
        kh = r // GROUP
        sel = ((ii - r * HEAD_DIM) == (jj - kh * HEAD_DIM)) & (jj >= kh * HEAD_DIM) & (jj < (kh + 1) * HEAD_DIM)
        qr = _dot(q, jnp.where(sel, 1.0, 0.0).astype(BF16))
        for c in range(KV_W // LANES):
            qe_ref[c, pl.ds(r, n, stride=N_HEADS), :] = qr[:, c * LANES:(c + 1) * LANES]


def _sample_pre(x, cos, sg, ln1, w_in):
    n, d = x.shape
    return pl.pallas_call(
        _sample_pre_kernel,
        grid=(1,),
        in_specs=[
            _const_spec((n, d)),
            _const_spec((1, LANES)),
            _const_spec((1, LANES)),
            _const_spec((1, d)),
            pl.BlockSpec((d, C_GP), lambda i: (0, 0), pipeline_mode=pl.Buffered(1)),
        ],
        out_specs=[
            pl.BlockSpec((KV_W // LANES, n * N_HEADS, LANES), lambda i: (0, 0, 0)),
            pl.BlockSpec((KV_W, n), lambda i: (0, 0)),
            pl.BlockSpec((KV_W, n), lambda i: (0, 0)),
            pl.BlockSpec((n, KV_W), lambda i: (0, 0)),
            pl.BlockSpec((n, KV_W), lambda i: (0, 0)),
            pl.BlockSpec((n, POOL_WIDTH), lambda i: (0, 0)),
        ],
        out_shape=[
            jax.ShapeDtypeStruct((KV_W // LANES, n * N_HEADS, LANES), F32),
            jax.ShapeDtypeStruct((KV_W, n), F32),
            jax.ShapeDtypeStruct((KV_W, n), F32),
            jax.ShapeDtypeStruct((n, KV_W), F32),
            jax.ShapeDtypeStruct((n, KV_W), F32),
            jax.ShapeDtypeStruct((n, POOL_WIDTH), F32),
        ],
        compiler_params=pltpu.CompilerParams(dimension_semantics=("arbitrary",), vmem_limit_bytes=VMEM_LIMIT),
        name="sample_pre",
    )(x, cos, sg, ln1, w_in)


def _shift_caches(step, nb, knt_ref, vnt_ref, ckt_ref, cvt_ref, nkt_ref, nvt_ref):
    n = knt_ref.shape[1]
    w_cache = ckt_ref.shape[2]
    newest = lax.broadcasted_iota(jnp.int32, (1, w_cache), 1) == w_cache - 1
    shift = lax.rem(n - lax.rem(step * nb, n), n)
    kcols = pltpu.roll(knt_ref[...], shift, 1)
    vcols = pltpu.roll(vnt_ref[...], shift, 1)
    for bl in range(nb):
        nkt_ref[bl] = jnp.where(newest, kcols[:, bl:bl + 1], pltpu.roll(ckt_ref[bl], w_cache - 1, 1))
        nvt_ref[bl] = jnp.where(newest, vcols[:, bl:bl + 1], pltpu.roll(cvt_ref[bl], w_cache - 1, 1))


def _sample_attn_kernel(qe_ref, kn_ref, vn_ref, ckt_ref, cvt_ref, sink_ref, *rest, bb, n_cast):
    cast_src = rest[:n_cast]
    o_ref = rest[n_cast]
    cast_dst = rest[n_cast + 1:]
    for src, dst in zip(cast_src, cast_dst):
        dst[...] = src[...].astype(BF16)
    w_cache = ckt_ref.shape[2]
    oldest = lax.broadcasted_iota(jnp.int32, (1, w_cache), 1) == 0
    sink = sink_ref[...]
    rounded = lambda a: a.astype(BF16).astype(F32)
    scores = []
    for bl in range(bb):
        rows = slice(bl * N_HEADS, (bl + 1) * N_HEADS)
        qb = jnp.concatenate([qe_ref[c, rows, :] for c in range(KV_W // LANES)], axis=1).astype(BF16)
        s_old = jnp.where(oldest, NEG_INF, _dot(qb, ckt_ref[bl].astype(BF16)))
        s_new = jnp.sum(qb.astype(F32) * rounded(kn_ref[bl:bl + 1, :]), axis=1, keepdims=True)
        scores.append((s_old, s_new))
    probs = []
    for s_old, s_new in scores:
        m = jnp.maximum(jnp.maximum(jnp.max(s_old, axis=-1, keepdims=True), s_new), sink)
        e_old = jnp.exp(s_old - m)
        e_new = jnp.exp(s_new - m)
        denom = jnp.sum(e_old, axis=-1, keepdims=True) + e_new + jnp.exp(sink - m)
        probs.append((e_old.astype(BF16), rounded(e_new), denom))
    for bl, (e_old, e_new, denom) in enumerate(probs):
        rows = slice(bl * N_HEADS, (bl + 1) * N_HEADS)
        o = (_dot_nt(e_old, cvt_ref[bl].astype(BF16)) + e_new * rounded(vn_ref[bl:bl + 1, :])) / denom
        for c in range(KV_W // LANES):
            o_ref[c, rows, :] = o[:, c * LANES:(c + 1) * LANES]


def _sample_attn(qe, kn, vn, ckt, cvt, sink, to_cast, bb):
    n, _, w_cache = ckt.shape
    cast_specs = _cast_specs(to_cast, n // bb, lambda i: i)
    cache_spec = pl.BlockSpec((bb, KV_W, w_cache), lambda i: (i, 0, 0))
    new_spec = pl.BlockSpec((bb, KV_W), lambda i: (i, 0))
    head_spec = pl.BlockSpec((KV_W // LANES, bb * N_HEADS, LANES), lambda i: (0, i, 0))
    return pl.pallas_call(
        functools.partial(_sample_attn_kernel, bb=bb, n_cast=len(to_cast)),
        grid=(n // bb,),
        in_specs=[head_spec, new_spec, new_spec, cache_spec, cache_spec, _const_spec((N_HEADS, 1))] + cast_specs,
        out_specs=[head_spec] + cast_specs,
        out_shape=[jax.ShapeDtypeStruct((KV_W // LANES, n * N_HEADS, LANES), F32)]
        + [jax.ShapeDtypeStruct(w.shape, BF16) for w in to_cast],
        compiler_params=pltpu.CompilerParams(dimension_semantics=("arbitrary",), vmem_limit_bytes=VMEM_LIMIT),
        name="sample_attn",
    )(qe, kn, vn, ckt, cvt, sink, *to_cast)


def _sample_post_kernel(x_ref, u_ref, st_ref, o_ref, ln1_ref, w_g_ref, gw_ref, pscale_ref, w_pb_ref, w_ab_ref,
                        w_out_ref, h_ref, nst_ref):
    x = x_ref[...]
    n = x.shape[0]
    xn = _rms(x, ln1_ref[...]).astype(BF16)
    u = u_ref[...]
    nst_ref[0:POOL_STATE - 1] = st_ref[1:POOL_STATE]
    nst_ref[POOL_STATE - 1] = u

    def win_sum(g, w, ug):
        acc = ug
        for i in range(1, w):
            acc = acc + st_ref[POOL_STATE - i, :, g * POOL_GC:(g + 1) * POOL_GC]
        return acc

    pooled = _pool_mix(u, win_sum, lambda w: jnp.float32(min(w, PAST_LEN + 1)), gw_ref, pscale_ref)
    merged = jax.nn.sigmoid(_dot(xn, w_g_ref[:, 0:D_MODEL])) * _dot(pooled.astype(BF16), w_pb_ref[...])
    kv_of_lane = lax.broadcasted_iota(jnp.int32, (1, KV_W), 1) // HEAD_DIM
    ab = jnp.zeros((n, D_MODEL), F32)
    for g in range(GROUP):
        row_g = jnp.zeros((n, KV_W), F32)
        for kh in range(N_KV_HEADS):
            r = kh * GROUP + g
            o_r = jnp.concatenate([o_ref[c, pl.ds(r, n, stride=N_HEADS), :] for c in range(KV_W // LANES)], axis=1)
            row_g = jnp.where(kv_of_lane == kh, o_r, row_g)
        w_g = jnp.concatenate([w_ab_ref[(kh * GROUP + g) * HEAD_DIM:(kh * GROUP + g + 1) * HEAD_DIM, :]
                               for kh in range(N_KV_HEADS)], axis=0)
        ab = ab + _dot(row_g.astype(BF16), w_g)
    merged = merged + jax.nn.sigmoid(_dot(xn, w_g_ref[:, D_MODEL:2 * D_MODEL])) * ab
    h_ref[...] = x + _dot(merged.astype(BF16), w_out_ref[...])


def _sample_post(x, u, st, o, ln1, w_in, gw, pscale, w_pb, w_ab, w_out):
    n, d = x.shape
    return pl.pallas_call(
        _sample_post_kernel,
        grid=(1,),
        in_specs=[
            _const_spec((n, d)),
            _const_spec((n, POOL_WIDTH)),
            _const_spec(st.shape),
            _const_spec((KV_W // LANES, n * N_HEADS, LANES)),
            _const_spec((1, d)),
            pl.BlockSpec((d, 2 * D_MODEL), lambda i: (0, 1), pipeline_mode=pl.Buffered(1)),
            _const_spec(gw.shape),
            _const_spec((1, POOL_WIDTH)),
            _const_spec(w_pb.shape),
            _const_spec(w_ab.shape),
            _const_spec(w_out.shape),
        ],
        out_specs=[pl.BlockSpec((n, d), lambda i: (0, 0)), pl.BlockSpec(st.shape, lambda i: (0, 0, 0))],
        out_shape=[jax.ShapeDtypeStruct((n, d), F32), jax.ShapeDtypeStruct(st.shape, F32)],
        compiler_params=pltpu.CompilerParams(dimension_semantics=("arbitrary",), vmem_limit_bytes=VMEM_LIMIT),
        name="sample_post",
    )(x, u, st, o, ln1, w_in, gw, pscale, w_pb, w_ab, w_out)


def _rope_tables(first_pos, n):
    half = ROT_DIMS // 2
    inv = ROPE_THETA ** (-(np.arange(0, ROT_DIMS, 2, dtype=np.float64) / ROT_DIMS))
    ang = np.arange(first_pos, first_pos + n, dtype=np.float64)[:, None] * inv[None, :]
    cos, sin = np.cos(ang), np.sin(ang)
    rest = HEAD_DIM - 2 * half
    c64 = np.concatenate([cos, cos, np.ones((n, rest))], axis=1)
    s64 = np.concatenate([-sin, sin, np.zeros((n, rest))], axis=1)
    reps = LANES // HEAD_DIM
    return jnp.asarray(np.tile(c64, (1, reps)), F32), jnp.asarray(np.tile(s64, (1, reps)), F32)


def kernel(x_prompt, x_sample, p_prompt, p_sample, cache_k, cache_v, state_pool, ln1, w_in, pool_group_w, pool_scale,
           attn_sinks, w_pool_branch, w_attn_branch, w_out, ln2, w_ffn_in, w_ffn_out, w_ple_proj, ple_norm,
           w_ple_gate, final_norm):
    depth = ln1.shape[0]
    b, s, d = x_prompt.shape
    bd, t_dec, _ = x_sample.shape
    w_cache = cache_k.shape[2]
    assert depth == 1 and t_dec == 1 and w_cache == WINDOW and s % BLOCK == 0 and d == D_MODEL
    tm = 512
    assert s % tm == 0

    cos_p, sg_p = _rope_tables(0, s)
    cos_s, sg_s = _rope_tables(PAST_LEN, t_dec)

    hp = x_prompt
    hs = x_sample.reshape(bd, d)
    row = lambda a: a.reshape(1, -1)
    nkp, nvp, npp, nks, nvs, nps = [], [], [], [], [], []
    for i in range(depth):
        to_fm = lambda c: jnp.transpose(c, (0, 2, 3, 1)).reshape(bd, KV_W, w_cache)
        from_fm = lambda c: jnp.transpose(c.reshape(bd, N_KV_HEADS, HEAD_DIM, w_cache), (0, 3, 1, 2))
        ckt, cvt = to_fm(cache_k[i]), to_fm(cache_v[i])
        qe, knt, vnt, kn, vn, un = _sample_pre(hs, cos_s, sg_s, row(ln1[i]), w_in[i])
        o, wi, gw, wpb, wab, wo = _sample_attn(
            qe, kn, vn, ckt, cvt, attn_sinks[i].reshape(N_HEADS, 1),
            (w_in[i], pool_group_w[i].reshape(POOL_WIDTH, POOL_GC), w_pool_branch[i], w_attn_branch[i], w_out[i]), 8)
        gw = gw.reshape(len(POOL_WINDOWS), POOL_GC, POOL_GC)
        h1, kp, vp, pp, nkt, nvt, w1, w2, wpp, wpg = _prompt_mixer(
            hp, cos_p, sg_p, attn_sinks[i], row(ln1[i]), wi, gw, row(pool_scale[i]), wpb, wab, wo,
            knt, vnt, ckt, cvt, (w_ffn_in[i], w_ffn_out[i], w_ple_proj[i], w_ple_gate[i]), tm)
        ffn_args = (row(ln2[i]), w1, w2, wpp, row(ple_norm[i]), wpg)
        hp = _ffn(h1.reshape(b * s, d), p_prompt[i].reshape(b * s, PLE_DIM), *ffn_args, row(final_norm),
                  tm, 2).reshape(b, s, d)
        from_fm_p = lambda c: jnp.transpose(c.reshape(b, N_KV_HEADS, HEAD_DIM, w_cache), (0, 3, 1, 2))
        nkp.append(from_fm_p(kp))
        nvp.append(from_fm_p(vp))
        npp.append(pp)

        h1s, nst = _sample_post(hs, un, jnp.transpose(state_pool[i], (1, 0, 2)), o, row(ln1[i]), wi, gw,
                                row(pool_scale[i]), wpb, wab, wo)
        hs = _ffn(h1s, p_sample[i].reshape(bd * t_dec, PLE_DIM), *ffn_args, row(final_norm), bd, 1)
        nks.append(from_fm(nkt))
        nvs.append(from_fm(nvt))
        nps.append(jnp.transpose(nst, (1, 0, 2)))

    return (hp, hs.reshape(bd, t_dec, d), jnp.stack(nkp), jnp.stack(nvp), jnp.stack(npp),
            jnp.stack(nks), jnp.stack(nvs), jnp.stack(nps))
```

```python
import functools

import jax
import jax.numpy as jnp
import numpy as np
from jax import lax
from jax.experimental import pallas as pl
from jax.experimental.pallas import tpu as pltpu

D_MODEL = 1024
HEAD_DIM = 64
N_HEADS = D_MODEL // HEAD_DIM
N_KV_HEADS = N_HEADS // 4
GROUP = N_HEADS // N_KV_HEADS
ROT_DIMS = HEAD_DIM // 4
ROPE_THETA = 500000.0
WINDOW = 128
BLOCK = 128
POOL_WIDTH = D_MODEL // 2
POOL_WINDOWS = (2, 4, 8, 16)
POOL_GC = POOL_WIDTH // len(POOL_WINDOWS)
POOL_STATE = max(POOL_WINDOWS) - 1
FFN_HIDDEN = -(-8 * D_MODEL // (3 * 256)) * 256
PLE_DIM = 256
EPS = 1e-6
NEG_INF = -1e30
PAST_LEN = 16384

Q_W = N_HEADS * HEAD_DIM
KV_W = N_KV_HEADS * HEAD_DIM
C_U, C_Q, C_K, C_V, C_GP, C_GA, C_END = 0, POOL_WIDTH, POOL_WIDTH + Q_W, POOL_WIDTH + Q_W + KV_W, \
    POOL_WIDTH + Q_W + 2 * KV_W, POOL_WIDTH + Q_W + 2 * KV_W + D_MODEL, POOL_WIDTH + Q_W + 2 * KV_W + 2 * D_MODEL

LANES = 128
S_AHEAD = 16
S_SLOTS = 16
GATE_COLS = 256
U_HALO = 24
SUBLANES = 8
VMEM_LIMIT = 56 * 1024 * 1024

BF16 = jnp.bfloat16
F32 = jnp.float32


def _dot(a, b):
    return jnp.dot(a, b, preferred_element_type=F32)


def _dot_nt(a, b):
    return lax.dot_general(a, b, (((1,), (1,)), ((), ())), preferred_element_type=F32)


def _sigmoid(x):
    return 0.5 * jnp.tanh(0.5 * x) + 0.5


def _rms(x, g):
    y = x * lax.rsqrt(jnp.mean(x * x, axis=-1, keepdims=True) + EPS)
    return y * g


def _rope(x, cos, sg, lo8):
    outs = []
    for c in range(x.shape[1] // LANES):
        xc = x[:, c * LANES:(c + 1) * LANES]
        partner = jnp.where(lo8, pltpu.roll(xc, LANES - ROT_DIMS // 2, 1), pltpu.roll(xc, ROT_DIMS // 2, 1))
        outs.append(xc * cos + partner * sg)
    return jnp.concatenate(outs, axis=1)


def _lane_masks():
    lane = lax.broadcasted_iota(jnp.int32, (1, LANES), 1)
    lo8 = (lane % HEAD_DIM) < (ROT_DIMS // 2)
    lo64 = lane < HEAD_DIM
    return lo8, lo64


def _pool_mix(u, win_sum_fn, cnt_fn, gw_ref, pscale_ref):
    mixed = []
    for g, w in enumerate(POOL_WINDOWS):
        cols = slice(g * POOL_GC, (g + 1) * POOL_GC)
        ug = u[:, cols]
        m = win_sum_fn(g, w, ug) / cnt_fn(w) - ug
        mixed.append(_dot(m.astype(BF16), gw_ref[g]) * pscale_ref[:, cols])
    return jnp.concatenate(mixed, axis=1)


def _mixer_kernel(sinks_ref, x_ref, cos_ref, sg_ref, ln1_ref, w_in_ref, gw_ref, pscale_ref, w_pb_ref, w_ab_ref,
                  w_out_ref, knt_ref, vnt_ref, ckt_ref, cvt_ref, *rest, tm, n_cast):
    cast_src = rest[:n_cast]
    h_ref, ko_ref, vo_ref, po_ref, nkt_ref, nvt_ref = rest[n_cast:n_cast + 6]
    cast_dst = rest[n_cast + 6:2 * n_cast + 6]
    kl_scr, kh_scr, vl_scr, vh_scr, u_scr, lvl_scr, attn_scr, q_scr, s_scr, g_scr = rest[2 * n_cast + 6:]
    t = pl.program_id(1)
    step = pl.program_id(0) * pl.num_programs(1) + t
    for src, dst in zip(cast_src, cast_dst):
        dst[...] = src[...].astype(BF16)
    lo8, lo64 = _lane_masks()

    @pl.when(t == 0)
    def _():
        for scr in (kl_scr, kh_scr, vl_scr, vh_scr):
            scr[:, 0:BLOCK, :] = jnp.zeros((N_KV_HEADS, BLOCK, LANES), BF16)
        u_scr[0:U_HALO, :] = jnp.zeros((U_HALO, POOL_WIDTH), F32)
        lvl_scr[:, 0:SUBLANES, :] = jnp.zeros((len(POOL_WINDOWS), SUBLANES, POOL_GC), F32)

    x = x_ref[0]
    xn = _rms(x, ln1_ref[...]).astype(BF16)
    cos = cos_ref[...]
    sg = sg_ref[...]

    k = _rope(_dot(xn, w_in_ref[:, C_K:C_V]), cos, sg, lo8)
    v = _dot(xn, w_in_ref[:, C_V:C_GP])
    q = (_rope(_dot(xn, w_in_ref[:, C_Q:C_K]), cos, sg, lo8) * (HEAD_DIM ** -0.5)).astype(BF16)
    q_scr[...] = q
    u = _dot(xn, w_in_ref[:, C_U:C_Q])
    ko_ref[0] = k[tm - WINDOW:, :].T
    vo_ref[0] = v[tm - WINDOW:, :].T

    zero = jnp.zeros((tm, LANES), F32)
    for src, lo_scr, hi_scr in ((k, kl_scr, kh_scr), (v, vl_scr, vh_scr)):
        for p in range(KV_W // LANES):
            xp = src[:, p * LANES:(p + 1) * LANES]
            xs = pltpu.roll(xp, HEAD_DIM, 1)
            lo_scr[2 * p, BLOCK:, :] = jnp.where(lo64, xp, zero).astype(BF16)
            hi_scr[2 * p, BLOCK:, :] = jnp.where(lo64, zero, xs).astype(BF16)
            lo_scr[2 * p + 1, BLOCK:, :] = jnp.where(lo64, xs, zero).astype(BF16)
            hi_scr[2 * p + 1, BLOCK:, :] = jnp.where(lo64, zero, xp).astype(BF16)

    qi = lax.broadcasted_iota(jnp.int32, (BLOCK, BLOCK), 0)
    ci = lax.broadcasted_iota(jnp.int32, (BLOCK, BLOCK), 1)
    from_prev = ci > qi
    bias0 = jnp.where(jnp.logical_and(t == 0, from_prev), NEG_INF, 0.0).astype(F32)

    ones_lo = jnp.broadcast_to(jnp.where(lo64, 1.0, 0.0).astype(BF16), (2 * BLOCK, LANES))
    ones_hi = jnp.broadcast_to(jnp.where(lo64, 0.0, 1.0).astype(BF16), (2 * BLOCK, LANES))

    units = [(j, kh) for j in range(tm // BLOCK) for kh in range(N_KV_HEADS)]

    def scores(i):
        j, kh = units[i]
        rows = slice(j * BLOCK, (j + 1) * BLOCK)
        win = slice(j * BLOCK, (j + 2) * BLOCK)
        qq = jnp.concatenate([q_scr[rows, (2 * kh + a) * LANES:(2 * kh + a + 1) * LANES] for a in range(2)], axis=0)
        kcat = jnp.concatenate([kl_scr[kh, win, :], kh_scr[kh, win, :]], axis=0)
        s = _dot_nt(qq, kcat)
        for a in range(2):
            for half in range(2):
                sa = s[a * BLOCK:(a + 1) * BLOCK, half * 2 * BLOCK:(half + 1) * 2 * BLOCK]
                folded = jnp.where(from_prev, sa[:, :BLOCK], sa[:, BLOCK:])
                if j == 0:
                    folded = folded + bias0
                s_scr[i % S_SLOTS, a * BLOCK:(a + 1) * BLOCK, half * BLOCK:(half + 1) * BLOCK] = folded

    units_per_gate = len(units) * GATE_COLS // (2 * D_MODEL)
    for i in range(S_AHEAD):
        scores(i)
    def softmax(i):
        kh = units[i][1]
        ps, sink_terms = [], []
        for a in range(2):
            es, st = [], []
            for half in range(2):
                sh = s_scr[i % S_SLOTS, a * BLOCK:(a + 1) * BLOCK, half * BLOCK:(half + 1) * BLOCK]
                sink = sinks_ref[4 * kh + 2 * a + half]
                m = jnp.maximum(jnp.max(sh, axis=1, keepdims=True), sink)
                e = jnp.exp(sh - m)
                es.append(jnp.where(from_prev, e, 0.0).astype(BF16))
                es.append(jnp.where(from_prev, 0.0, e).astype(BF16))
                st.append(jnp.exp(sink - m))
            ps.append(jnp.concatenate(es, axis=1))
            sink_terms.append(jnp.where(lo64, st[0], st[1]))
        return jnp.concatenate(ps, axis=0), sink_terms

    def weighted_values(i, p, sink_terms):
        j, kh = units[i]
        rows = slice(j * BLOCK, (j + 1) * BLOCK)
        win = slice(j * BLOCK, (j + 2) * BLOCK)
        vcat = jnp.concatenate([
            jnp.concatenate([vl_scr[kh, win, :], ones_lo], axis=1),
            jnp.concatenate([vh_scr[kh, win, :], ones_hi], axis=1)], axis=0)
        o = _dot(p, vcat)
        for a in range(2):
            oa = o[a * BLOCK:(a + 1) * BLOCK]
            attn_scr[rows, (2 * kh + a) * LANES:(2 * kh + a + 1) * LANES] = (
                oa[:, :LANES] / (oa[:, LANES:] + sink_terms[a])).astype(BF16)

    pending = None
    for i in range(len(units)):
        if i + S_AHEAD < len(units):
            scores(i + S_AHEAD)
        current = (i,) + softmax(i)
        if pending is not None:
            weighted_values(*pending)
        pending = current
        if i % units_per_gate == units_per_gate - 1:
            gcols = slice((i // units_per_gate) * GATE_COLS, (i // units_per_gate + 1) * GATE_COLS)
            g_scr[:, gcols] = _sigmoid(_dot(xn, w_in_ref[:, C_GP + gcols.start:C_GP + gcols.stop]))
    weighted_values(*pending)

    u_scr[U_HALO:, :] = u
    po_ref[0] = u_scr[U_HALO + tm - POOL_STATE:U_HALO + tm, :]
    pos = t * tm + lax.broadcasted_iota(jnp.int32, (tm, 1), 0)

    def win_sum(g, w, ug):
        cols = slice(g * POOL_GC, (g + 1) * POOL_GC)
        n = U_HALO - SUBLANES + tm
        src, span = u_scr, 1
        while span < w:
            lvl = src[SUBLANES:SUBLANES + n, cols] + src[SUBLANES - span:SUBLANES - span + n, cols]
            span *= 2
            if span < w:
                lvl_scr[g, SUBLANES:SUBLANES + n, :] = lvl
                src, cols = lvl_scr.at[g], slice(None)
        return lvl[U_HALO - SUBLANES:, :]

    pooled = _pool_mix(u, win_sum, lambda w: jnp.minimum(w, pos + 1).astype(F32), gw_ref, pscale_ref)

    for scr in (kl_scr, kh_scr, vl_scr, vh_scr):
        scr[:, 0:BLOCK, :] = scr[:, tm:tm + BLOCK, :]
    u_scr[0:U_HALO, :] = u_scr[tm:tm + U_HALO, :]

    _shift_caches(step, ckt_ref.shape[0], knt_ref, vnt_ref, ckt_ref, cvt_ref, nkt_ref, nvt_ref)

    merged = g_scr[:, 0:D_MODEL] * _dot(pooled.astype(BF16), w_pb_ref[...])
    merged = merged + g_scr[:, D_MODEL:2 * D_MODEL] * _dot(attn_scr[...], w_ab_ref[...])
    h_ref[0] = x + _dot(merged.astype(BF16), w_out_ref[...])


def _const_spec(shape):
    nd = len(shape)
    return pl.BlockSpec(shape, lambda *_: (0,) * nd, pipeline_mode=pl.Buffered(1))


def _cast_block_rows(rows, steps):
    br = 2 * SUBLANES
    while rows % br or rows // br > steps:
        br *= 2
    return br


def _cast_specs(to_cast, steps, step_of):
    specs = []
    for w in to_cast:
        br = _cast_block_rows(w.shape[0], steps)
        last = w.shape[0] // br - 1
        specs.append(pl.BlockSpec(
            (br, w.shape[1]), lambda *idx, last=last: (jnp.minimum(step_of(*idx), last), 0)))
    return specs


def _prompt_mixer(x, cos, sg, sinks, ln1, w_in, gw, pscale, w_pb, w_ab, w_out, knt, vnt, ckt, cvt, to_cast, tm):
    b, s, d = x.shape
    nt = s // tm
    cast_specs = _cast_specs(to_cast, b * nt, lambda bi, ti, *_: bi * nt + ti)
    nb = ckt.shape[0] // (b * nt)
    assert nb * b * nt == ckt.shape[0]
    cache_spec = pl.BlockSpec((nb,) + ckt.shape[1:], lambda bi, ti, *_: (bi * nt + ti, 0, 0))
    grid_spec = pltpu.PrefetchScalarGridSpec(
        num_scalar_prefetch=1,
        grid=(b, nt),
        in_specs=[
            pl.BlockSpec((1, tm, d), lambda bi, ti, *_: (bi, ti, 0)),
            pl.BlockSpec((tm, LANES), lambda bi, ti, *_: (ti, 0)),
            pl.BlockSpec((tm, LANES), lambda bi, ti, *_: (ti, 0)),
            _const_spec((1, d)),
            _const_spec(w_in.shape),
            _const_spec(gw.shape),
            _const_spec((1, POOL_WIDTH)),
            _const_spec(w_pb.shape),
            _const_spec(w_ab.shape),
            _const_spec(w_out.shape),
            _const_spec(knt.shape),
            _const_spec(vnt.shape),
            cache_spec,
            cache_spec,
        ] + cast_specs,
        out_specs=[
            pl.BlockSpec((1, tm, d), lambda bi, ti, *_: (bi, ti, 0)),
            pl.BlockSpec((1, KV_W, WINDOW), lambda bi, ti, *_: (bi, 0, 0)),
            pl.BlockSpec((1, KV_W, WINDOW), lambda bi, ti, *_: (bi, 0, 0)),
            pl.BlockSpec((1, POOL_STATE, POOL_WIDTH), lambda bi, ti, *_: (bi, 0, 0)),
            cache_spec,
            cache_spec,
        ] + cast_specs,
        scratch_shapes=[
            pltpu.VMEM((N_KV_HEADS, BLOCK + tm, LANES), BF16),
            pltpu.VMEM((N_KV_HEADS, BLOCK + tm, LANES), BF16),
            pltpu.VMEM((N_KV_HEADS, BLOCK + tm, LANES), BF16),
            pltpu.VMEM((N_KV_HEADS, BLOCK + tm, LANES), BF16),
            pltpu.VMEM((U_HALO + tm, POOL_WIDTH), F32),
            pltpu.VMEM((len(POOL_WINDOWS), U_HALO + tm, POOL_GC), F32),
            pltpu.VMEM((tm, Q_W), BF16),
            pltpu.VMEM((tm, Q_W), BF16),
            pltpu.VMEM((S_SLOTS, 2 * BLOCK, 2 * BLOCK), F32),
            pltpu.VMEM((tm, 2 * D_MODEL), F32),
        ],
    )
    return pl.pallas_call(
        functools.partial(_mixer_kernel, tm=tm, n_cast=len(to_cast)),
        grid_spec=grid_spec,
        out_shape=[
            jax.ShapeDtypeStruct((b, s, d), F32),
            jax.ShapeDtypeStruct((b, KV_W, WINDOW), F32),
            jax.ShapeDtypeStruct((b, KV_W, WINDOW), F32),
            jax.ShapeDtypeStruct((b, POOL_STATE, POOL_WIDTH), F32),
            jax.ShapeDtypeStruct(ckt.shape, F32),
            jax.ShapeDtypeStruct(cvt.shape, F32),
        ] + [jax.ShapeDtypeStruct(w.shape, BF16) for w in to_cast],
        compiler_params=pltpu.CompilerParams(
            dimension_semantics=("arbitrary", "arbitrary"), vmem_limit_bytes=VMEM_LIMIT),
        name="prompt_mixer",
    )(sinks, x, cos, sg, ln1, w_in, gw, pscale, w_pb, w_ab, w_out, knt, vnt, ckt, cvt, *to_cast)


FFN_CHUNKS = ((0, 1024), (1024, 2048), (2048, FFN_HIDDEN))


def _ffn_kernel(h_ref, p_ref, ln2_ref, w1_ref, w2_ref, w_pp_ref, pn_ref, w_pg_ref, fn_ref, y_ref, *, tm):
    def tile_stages(r0):
        rows = slice(r0, r0 + tm)
        st = {}

        def up_proj(c):
            lo, hi = FFN_CHUNKS[c]
            if c == 0:
                st['h'] = h_ref[rows, :]
                st['hn'] = _rms(st['h'], ln2_ref[...]).astype(BF16)
                st['acc'] = st['h']
            st['gate', c] = _dot(st['hn'], w1_ref[:, lo:hi])
            st['up', c] = _dot(st['hn'], w1_ref[:, FFN_HIDDEN + lo:FFN_HIDDEN + hi])
            if c == len(FFN_CHUNKS) - 1:
                st['e'] = _rms(_dot(p_ref[rows, :].astype(BF16), w_pp_ref[...]), pn_ref[...])

        def down_proj(c):
            lo, hi = FFN_CHUNKS[c]
            gate = st.pop(('gate', c))
            act = (gate * _sigmoid(gate) * st.pop(('up', c))).astype(BF16)
            st['acc'] = st['acc'] + _dot(act, w2_ref[lo:hi, :])

        def ple_gate():
            st['g'] = _dot(st['acc'].astype(BF16), w_pg_ref[...])

        def finish():
            h3 = st['acc'] + _sigmoid(st['g']) * st['e']
            y_ref[rows, :] = _rms(h3, fn_ref[...])

        n = len(FFN_CHUNKS)
        steps = [functools.partial(up_proj, 0)]
        for c in range(1, n):
            steps += [functools.partial(up_proj, c), functools.partial(down_proj, c - 1)]
        return steps + [functools.partial(down_proj, n - 1), ple_gate, finish]

    tail = 3
    order = []
    for r0 in range(0, h_ref.shape[0], tm):
        steps = tile_stages(r0)
        held, order = order[len(order) - tail:] if order else [], order[:len(order) - tail] if order else []
        for k in range(max(len(held), tail)):
            order += steps[k:k + 1] + held[k:k + 1]
        order += steps[tail:]
    for step in order:
        step()


def _ffn(h, p, ln2, w1, w2, w_pp, pn, w_pg, fn, tm, nsub):
    n, d = h.shape
    blk = tm * nsub
    return pl.pallas_call(
        functools.partial(_ffn_kernel, tm=tm),
        grid=(n // blk,),
        in_specs=[
            pl.BlockSpec((blk, d), lambda i: (i, 0)),
            pl.BlockSpec((blk, PLE_DIM), lambda i: (i, 0)),
            _const_spec((1, d)),
            _const_spec(w1.shape),
            _const_spec(w2.shape),
            _const_spec(w_pp.shape),
            _const_spec((1, d)),
            _const_spec(w_pg.shape),
            _const_spec((1, d)),
        ],
        out_specs=pl.BlockSpec((blk, d), lambda i: (i, 0)),
        out_shape=jax.ShapeDtypeStruct((n, d), F32),
        compiler_params=pltpu.CompilerParams(dimension_semantics=("arbitrary",), vmem_limit_bytes=VMEM_LIMIT),
        name="ffn_ple_norm",
    )(h, p, ln2, w1, w2, w_pp, pn, w_pg, fn)


def _sample_pre_kernel(x_ref, cos_ref, sg_ref, ln1_ref, w_in_ref, qe_ref, knt_ref, vnt_ref, kn_ref, vn_ref, u_ref):
    n = x_ref.shape[0]
    lo8, _ = _lane_masks()
    xn = _rms(x_ref[...], ln1_ref[...]).astype(BF16)
    cos = cos_ref[...]
    sg = sg_ref[...]
    w = lambda lo, hi: w_in_ref[:, lo:hi].astype(BF16)
    u_ref[...] = _dot(xn, w(C_U, C_Q))
    q = (_rope(_dot(xn, w(C_Q, C_K)), cos, sg, lo8) * (HEAD_DIM ** -0.5)).astype(BF16)
    kn = _rope(_dot(xn, w(C_K, C_V)), cos, sg, lo8)
    vn = _dot(xn, w(C_V, C_GP))
    kn_ref[...] = kn
    vn_ref[...] = vn
    knt_ref[...] = kn.T
    vnt_ref[...] = vn.T
    ii = lax.broadcasted_iota(jnp.int32, (Q_W, KV_W), 0)
    jj = lax.broadcasted_iota(jnp.int32, (Q_W, KV_W), 1)
    for r in range(N_HEADS):
        kh = r // GROUP
        sel = ((ii - r * HEAD_DIM) == (jj - kh * HEAD_DIM)) & (jj >= kh * HEAD_DIM) & (jj < (kh + 1) * HEAD_DIM)
        qr = _dot(q, jnp.where(sel, 1.0, 0.0).astype(BF16))
        for c in range(KV_W // LANES):
            qe_ref[c, pl.ds(r, n, stride=N_HEADS), :] = qr[:, c * LANES:(c + 1) * LANES]


def _sample_pre(x, cos, sg, ln1, w_in):
    n, d = x.shape
    return pl.pallas_call(
        _sample_pre_kernel,
        grid=(1,),
        in_specs=[
            _const_spec((n, d)),
            _const_spec((1, LANES)),
            _const_spec((1, LANES)),
            _const_spec((1, d)),
            pl.BlockSpec((d, C_GP), lambda i: (0, 0), pipeline_mode=pl.Buffered(1)),
        ],
        out_specs=[
            pl.BlockSpec((KV_W // LANES, n * N_HEADS, LANES), lambda i: (0, 0, 0)),
            pl.BlockSpec((KV_W, n), lambda i: (0, 0)),
            pl.BlockSpec((KV_W, n), lambda i: (0, 0)),
            pl.BlockSpec((n, KV_W), lambda i: (0, 0)),
            pl.BlockSpec((n, KV_W), lambda i: (0, 0)),
            pl.BlockSpec((n, POOL_WIDTH), lambda i: (0, 0)),
        ],
        out_shape=[
            jax.ShapeDtypeStruct((KV_W // LANES, n * N_HEADS, LANES), F32),
            jax.ShapeDtypeStruct((KV_W, n), F32),
            jax.ShapeDtypeStruct((KV_W, n), F32),
            jax.ShapeDtypeStruct((n, KV_W), F32),
            jax.ShapeDtypeStruct((n, KV_W), F32),
            jax.ShapeDtypeStruct((n, POOL_WIDTH), F32),
        ],
        compiler_params=pltpu.CompilerParams(dimension_semantics=("arbitrary",), vmem_limit_bytes=VMEM_LIMIT),
        name="sample_pre",
    )(x, cos, sg, ln1, w_in)


def _shift_caches(step, nb, knt_ref, vnt_ref, ckt_ref, cvt_ref, nkt_ref, nvt_ref):
    n = knt_ref.shape[1]
    w_cache = ckt_ref.shape[2]
    newest = lax.broadcasted_iota(jnp.int32, (1, w_cache), 1) == w_cache - 1
    shift = lax.rem(n - lax.rem(step * nb, n), n)
    kcols = pltpu.roll(knt_ref[...], shift, 1)
    vcols = pltpu.roll(vnt_ref[...], shift, 1)
    for bl in range(nb):
        nkt_ref[bl] = jnp.where(newest, kcols[:, bl:bl + 1], pltpu.roll(ckt_ref[bl], w_cache - 1, 1))
        nvt_ref[bl] = jnp.where(newest, vcols[:, bl:bl + 1], pltpu.roll(cvt_ref[bl], w_cache - 1, 1))


def _sample_attn_kernel(qe_ref, kn_ref, vn_ref, ckt_ref, cvt_ref, sink_ref, *rest, bb, n_cast):
    cast_src = rest[:n_cast]
    o_ref = rest[n_cast]
    cast_dst = rest[n_cast + 1:]
    for src, dst in zip(cast_src, cast_dst):
        dst[...] = src[...].astype(BF16)
    w_cache = ckt_ref.shape[2]
    oldest = lax.broadcasted_iota(jnp.int32, (1, w_cache), 1) == 0
    sink = sink_ref[...]
    rounded = lambda a: a.astype(BF16).astype(F32)
    scores = []
    for bl in range(bb):
        rows = slice(bl * N_HEADS, (bl + 1) * N_HEADS)
        qb = jnp.concatenate([qe_ref[c, rows, :] for c in range(KV_W // LANES)], axis=1).astype(BF16)
        s_old = jnp.where(oldest, NEG_INF, _dot(qb, ckt_ref[bl].astype(BF16)))
        s_new = jnp.sum(qb.astype(F32) * rounded(kn_ref[bl:bl + 1, :]), axis=1, keepdims=True)
        scores.append((s_old, s_new))
    probs = []
    for s_old, s_new in scores:
        m = jnp.maximum(jnp.maximum(jnp.max(s_old, axis=-1, keepdims=True), s_new), sink)
        e_old = jnp.exp(s_old - m)
        e_new = jnp.exp(s_new - m)
        denom = jnp.sum(e_old, axis=-1, keepdims=True) + e_new + jnp.exp(sink - m)
        probs.append((e_old.astype(BF16), rounded(e_new), denom))
    for bl, (e_old, e_new, denom) in enumerate(probs):
        rows = slice(bl * N_HEADS, (bl + 1) * N_HEADS)
        o = (_dot_nt(e_old, cvt_ref[bl].astype(BF16)) + e_new * rounded(vn_ref[bl:bl + 1, :])) / denom
        for c in range(KV_W // LANES):
            o_ref[c, rows, :] = o[:, c * LANES:(c + 1) * LANES]


def _sample_attn(qe, kn, vn, ckt, cvt, sink, to_cast, bb):
    n, _, w_cache = ckt.shape
    cast_specs = _cast_specs(to_cast, n // bb, lambda i: i)
    cache_spec = pl.BlockSpec((bb, KV_W, w_cache), lambda i: (i, 0, 0))
    new_spec = pl.BlockSpec((bb, KV_W), lambda i: (i, 0))
    head_spec = pl.BlockSpec((KV_W // LANES, bb * N_HEADS, LANES), lambda i: (0, i, 0))
    return pl.pallas_call(
        functools.partial(_sample_attn_kernel, bb=bb, n_cast=len(to_cast)),
        grid=(n // bb,),
        in_specs=[head_spec, new_spec, new_spec, cache_spec, cache_spec, _const_spec((N_HEADS, 1))] + cast_specs,
        out_specs=[head_spec] + cast_specs,
        out_shape=[jax.ShapeDtypeStruct((KV_W // LANES, n * N_HEADS, LANES), F32)]
        + [jax.ShapeDtypeStruct(w.shape, BF16) for w in to_cast],
        compiler_params=pltpu.CompilerParams(dimension_semantics=("arbitrary",), vmem_limit_bytes=VMEM_LIMIT),
        name="sample_attn",
    )(qe, kn, vn, ckt, cvt, sink, *to_cast)


def _sample_post_kernel(x_ref, u_ref, st_ref, o_ref, ln1_ref, w_g_ref, gw_ref, pscale_ref, w_pb_ref, w_ab_ref,
                        w_out_ref, h_ref, nst_ref):
    x = x_ref[...]
    n = x.shape[0]
    xn = _rms(x, ln1_ref[...]).astype(BF16)
    u = u_ref[...]
    nst_ref[0:POOL_STATE - 1] = st_ref[1:POOL_STATE]
    nst_ref[POOL_STATE - 1] = u

    def win_sum(g, w, ug):
        acc = ug
        for i in range(1, w):
            acc = acc + st_ref[POOL_STATE - i, :, g * POOL_GC:(g + 1) * POOL_GC]
        return acc

    pooled = _pool_mix(u, win_sum, lambda w: jnp.float32(min(w, PAST_LEN + 1)), gw_ref, pscale_ref)
    merged = _sigmoid(_dot(xn, w_g_ref[:, 0:D_MODEL])) * _dot(pooled.astype(BF16), w_pb_ref[...])
    kv_of_lane = lax.broadcasted_iota(jnp.int32, (1, KV_W), 1) // HEAD_DIM
    ab = jnp.zeros((n, D_MODEL), F32)
    for g in range(GROUP):
        row_g = jnp.zeros((n, KV_W), F32)
        for kh in range(N_KV_HEADS):
            r = kh * GROUP + g
            o_r = jnp.concatenate([o_ref[c, pl.ds(r, n, stride=N_HEADS), :] for c in range(KV_W // LANES)], axis=1)
            row_g = jnp.where(kv_of_lane == kh, o_r, row_g)
        w_g = jnp.concatenate([w_ab_ref[(kh * GROUP + g) * HEAD_DIM:(kh * GROUP + g + 1) * HEAD_DIM, :]
                               for kh in range(N_KV_HEADS)], axis=0)
        ab = ab + _dot(row_g.astype(BF16), w_g)
    merged = merged + _sigmoid(_dot(xn, w_g_ref[:, D_MODEL:2 * D_MODEL])) * ab
    h_ref[...] = x + _dot(merged.astype(BF16), w_out_ref[...])


def _sample_post(x, u, st, o, ln1, w_in, gw, pscale, w_pb, w_ab, w_out):
    n, d = x.shape
    return pl.pallas_call(
        _sample_post_kernel,
        grid=(1,),
        in_specs=[
            _const_spec((n, d)),
            _const_spec((n, POOL_WIDTH)),
            _const_spec(st.shape),
            _const_spec((KV_W // LANES, n * N_HEADS, LANES)),
            _const_spec((1, d)),
            pl.BlockSpec((d, 2 * D_MODEL), lambda i: (0, 1), pipeline_mode=pl.Buffered(1)),
            _const_spec(gw.shape),
            _const_spec((1, POOL_WIDTH)),
            _const_spec(w_pb.shape),
            _const_spec(w_ab.shape),
            _const_spec(w_out.shape),
        ],
        out_specs=[pl.BlockSpec((n, d), lambda i: (0, 0)), pl.BlockSpec(st.shape, lambda i: (0, 0, 0))],
        out_shape=[jax.ShapeDtypeStruct((n, d), F32), jax.ShapeDtypeStruct(st.shape, F32)],
        compiler_params=pltpu.CompilerParams(dimension_semantics=("arbitrary",), vmem_limit_bytes=VMEM_LIMIT),
        name="sample_post",
    )(x, u, st, o, ln1, w_in, gw, pscale, w_pb, w_ab, w_out)


def _rope_tables(first_pos, n):
    half = ROT_DIMS // 2
    inv = ROPE_THETA ** (-(np.arange(0, ROT_DIMS, 2, dtype=np.float64) / ROT_DIMS))
    ang = np.arange(first_pos, first_pos + n, dtype=np.float64)[:, None] * inv[None, :]
    cos, sin = np.cos(ang), np.sin(ang)
    rest = HEAD_DIM - 2 * half
    c64 = np.concatenate([cos, cos, np.ones((n, rest))], axis=1)
    s64 = np.concatenate([-sin, sin, np.zeros((n, rest))], axis=1)
    reps = LANES // HEAD_DIM
    return jnp.asarray(np.tile(c64, (1, reps)), F32), jnp.asarray(np.tile(s64, (1, reps)), F32)


def kernel(x_prompt, x_sample, p_prompt, p_sample, cache_k, cache_v, state_pool, ln1, w_in, pool_group_w, pool_scale,
           attn_sinks, w_pool_branch, w_attn_branch, w_out, ln2, w_ffn_in, w_ffn_out, w_ple_proj, ple_norm,
           w_ple_gate, final_norm):
    depth = ln1.shape[0]
    b, s, d = x_prompt.shape
    bd, t_dec, _ = x_sample.shape
    w_cache = cache_k.shape[2]
    assert depth == 1 and t_dec == 1 and w_cache == WINDOW and s % BLOCK == 0 and d == D_MODEL
    tm = 512
    assert s % tm == 0

    cos_p, sg_p = _rope_tables(0, s)
    cos_s, sg_s = _rope_tables(PAST_LEN, t_dec)

    hp = x_prompt
    hs = x_sample.reshape(bd, d)
    row = lambda a: a.reshape(1, -1)
    nkp, nvp, npp, nks, nvs, nps = [], [], [], [], [], []
    for i in range(depth):
        to_fm = lambda c: jnp.transpose(c, (0, 2, 3, 1)).reshape(bd, KV_W, w_cache)
        from_fm = lambda c: jnp.transpose(c.reshape(bd, N_KV_HEADS, HEAD_DIM, w_cache), (0, 3, 1, 2))
        ckt, cvt = to_fm(cache_k[i]), to_fm(cache_v[i])
        qe, knt, vnt, kn, vn, un = _sample_pre(hs, cos_s, sg_s, row(ln1[i]), w_in[i])
        o, wi, gw, wpb, wab, wo = _sample_attn(
            qe, kn, vn, ckt, cvt, attn_sinks[i].reshape(N_HEADS, 1),
            (w_in[i], pool_group_w[i].reshape(POOL_WIDTH, POOL_GC), w_pool_branch[i], w_attn_branch[i], w_out[i]), 16)
        gw = gw.reshape(len(POOL_WINDOWS), POOL_GC, POOL_GC)
        h1, kp, vp, pp, nkt, nvt, w1, w2, wpp, wpg = _prompt_mixer(
            hp, cos_p, sg_p, attn_sinks[i], row(ln1[i]), wi, gw, row(pool_scale[i]), wpb, wab, wo,
            knt, vnt, ckt, cvt, (w_ffn_in[i], w_ffn_out[i], w_ple_proj[i], w_ple_gate[i]), tm)
        ffn_args = (row(ln2[i]), w1, w2, wpp, row(ple_norm[i]), wpg)
        hp = _ffn(h1.reshape(b * s, d), p_prompt[i].reshape(b * s, PLE_DIM), *ffn_args, row(final_norm),
                  tm, 2).reshape(b, s, d)
        from_fm_p = lambda c: jnp.transpose(c.reshape(b, N_KV_HEADS, HEAD_DIM, w_cache), (0, 3, 1, 2))
        nkp.append(from_fm_p(kp))
        nvp.append(from_fm_p(vp))
        npp.append(pp)

        h1s, nst = _sample_post(hs, un, jnp.transpose(state_pool[i], (1, 0, 2)), o, row(ln1[i]), wi, gw,
                                row(pool_scale[i]), wpb, wab, wo)
        hs = _ffn(h1s, p_sample[i].reshape(bd * t_dec, PLE_DIM), *ffn_args, row(final_norm), bd, 1)
        nks.append(from_fm(nkt))
        nvs.append(from_fm(nvt))
        nps.append(jnp.transpose(nst, (1, 0, 2)))

    return (hp, hs.reshape(bd, t_dec, d), jnp.stack(nkp), jnp.stack(nvp), jnp.stack(npp),
            jnp.stack(nks), jnp.stack(nvs), jnp.stack(nps))
```

```python
import functools

import jax
import jax.numpy as jnp
import numpy as np
from jax import lax
from jax.experimental import pallas as pl
from jax.experimental.pallas import tpu as pltpu

D_MODEL = 1024
HEAD_DIM = 64
N_HEADS = D_MODEL // HEAD_DIM
N_KV_HEADS = N_HEADS // 4
GROUP = N_HEADS // N_KV_HEADS
ROT_DIMS = HEAD_DIM // 4
ROPE_THETA = 500000.0
WINDOW = 128
BLOCK = 128
POOL_WIDTH = D_MODEL // 2
POOL_WINDOWS = (2, 4, 8, 16)
POOL_GC = POOL_WIDTH // len(POOL_WINDOWS)
POOL_STATE = max(POOL_WINDOWS) - 1
FFN_HIDDEN = -(-8 * D_MODEL // (3 * 256)) * 256
PLE_DIM = 256
EPS = 1e-6
NEG_INF = -1e30
PAST_LEN = 16384

Q_W = N_HEADS * HEAD_DIM
KV_W = N_KV_HEADS * HEAD_DIM
C_U, C_Q, C_K, C_V, C_GP, C_GA, C_END = 0, POOL_WIDTH, POOL_WIDTH + Q_W, POOL_WIDTH + Q_W + KV_W, \
    POOL_WIDTH + Q_W + 2 * KV_W, POOL_WIDTH + Q_W + 2 * KV_W + D_MODEL, POOL_WIDTH + Q_W + 2 * KV_W + 2 * D_MODEL

LANES = 128
S_AHEAD = 16
S_SLOTS = 16
GATE_COLS = 256
U_HALO = 24
SUBLANES = 8
VMEM_LIMIT = 56 * 1024 * 1024

BF16 = jnp.bfloat16
F32 = jnp.float32


def _dot(a, b):
    return jnp.dot(a, b, preferred_element_type=F32)


def _dot_nt(a, b):
    return lax.dot_general(a, b, (((1,), (1,)), ((), ())), preferred_element_type=F32)


def _sigmoid(x):
    return 0.5 * jnp.tanh(0.5 * x) + 0.5


def _rms(x, g):
    y = x * lax.rsqrt(jnp.mean(x * x, axis=-1, keepdims=True) + EPS)
    return y * g


def _rope(x, cos, sg, lo8):
    outs = []
    for c in range(x.shape[1] // LANES):
        xc = x[:, c * LANES:(c + 1) * LANES]
        partner = jnp.where(lo8, pltpu.roll(xc, LANES - ROT_DIMS // 2, 1), pltpu.roll(xc, ROT_DIMS // 2, 1))
        outs.append(xc * cos + partner * sg)
    return jnp.concatenate(outs, axis=1)


def _lane_masks():
    lane = lax.broadcasted_iota(jnp.int32, (1, LANES), 1)
    lo8 = (lane % HEAD_DIM) < (ROT_DIMS // 2)
    lo64 = lane < HEAD_DIM
    return lo8, lo64


def _pool_mix(u, win_sum_fn, cnt_fn, gw_ref, pscale_ref):
    mixed = []
    for g, w in enumerate(POOL_WINDOWS):
        cols = slice(g * POOL_GC, (g + 1) * POOL_GC)
        ug = u[:, cols]
        m = win_sum_fn(g, w, ug) / cnt_fn(w) - ug
        mixed.append(_dot(m.astype(BF16), gw_ref[g]) * pscale_ref[:, cols])
    return jnp.concatenate(mixed, axis=1)


def _mixer_kernel(sinks_ref, x_ref, cos_ref, sg_ref, ln1_ref, w_in_ref, gw_ref, pscale_ref, w_pb_ref, w_ab_ref,
                  w_out_ref, knt_ref, vnt_ref, ckt_ref, cvt_ref, *rest, tm, n_cast):
    cast_src = rest[:n_cast]
    h_ref, ko_ref, vo_ref, po_ref, nkt_ref, nvt_ref = rest[n_cast:n_cast + 6]
    cast_dst = rest[n_cast + 6:2 * n_cast + 6]
    kl_scr, kh_scr, vl_scr, vh_scr, u_scr, lvl_scr, attn_scr, q_scr, s_scr, g_scr = rest[2 * n_cast + 6:]
    t = pl.program_id(1)
    step = pl.program_id(0) * pl.num_programs(1) + t
    for src, dst in zip(cast_src, cast_dst):
        dst[...] = src[...].astype(BF16)
    lo8, lo64 = _lane_masks()

    @pl.when(t == 0)
    def _():
        for scr in (kl_scr, kh_scr, vl_scr, vh_scr):
            scr[:, 0:BLOCK, :] = jnp.zeros((N_KV_HEADS, BLOCK, LANES), BF16)
        u_scr[0:U_HALO, :] = jnp.zeros((U_HALO, POOL_WIDTH), F32)
        lvl_scr[:, 0:SUBLANES, :] = jnp.zeros((len(POOL_WINDOWS), SUBLANES, POOL_GC), F32)

    x = x_ref[0]
    xn = _rms(x, ln1_ref[...]).astype(BF16)
    cos = cos_ref[...]
    sg = sg_ref[...]

    k = _rope(_dot(xn, w_in_ref[:, C_K:C_V]), cos, sg, lo8)
    v = _dot(xn, w_in_ref[:, C_V:C_GP])
    q = (_rope(_dot(xn, w_in_ref[:, C_Q:C_K]), cos, sg, lo8) * (HEAD_DIM ** -0.5)).astype(BF16)
    q_scr[...] = q
    u = _dot(xn, w_in_ref[:, C_U:C_Q])
    ko_ref[0] = k[tm - WINDOW:, :].T
    vo_ref[0] = v[tm - WINDOW:, :].T

    zero = jnp.zeros((tm, LANES), F32)
    for src, lo_scr, hi_scr in ((k, kl_scr, kh_scr), (v, vl_scr, vh_scr)):
        for p in range(KV_W // LANES):
            xp = src[:, p * LANES:(p + 1) * LANES]
            xs = pltpu.roll(xp, HEAD_DIM, 1)
            lo_scr[2 * p, BLOCK:, :] = jnp.where(lo64, xp, zero).astype(BF16)
            hi_scr[2 * p, BLOCK:, :] = jnp.where(lo64, zero, xs).astype(BF16)
            lo_scr[2 * p + 1, BLOCK:, :] = jnp.where(lo64, xs, zero).astype(BF16)
            hi_scr[2 * p + 1, BLOCK:, :] = jnp.where(lo64, zero, xp).astype(BF16)

    qi = lax.broadcasted_iota(jnp.int32, (BLOCK, BLOCK), 0)
    ci = lax.broadcasted_iota(jnp.int32, (BLOCK, BLOCK), 1)
    from_prev = ci > qi
    bias0 = jnp.where(jnp.logical_and(t == 0, from_prev), NEG_INF, 0.0).astype(F32)

    ones_lo = jnp.broadcast_to(jnp.where(lo64, 1.0, 0.0).astype(BF16), (2 * BLOCK, LANES))
    ones_hi = jnp.broadcast_to(jnp.where(lo64, 0.0, 1.0).astype(BF16), (2 * BLOCK, LANES))

    units = [(j, kh) for j in range(tm // BLOCK) for kh in range(N_KV_HEADS)]

    def scores(i):
        j, kh = units[i]
        rows = slice(j * BLOCK, (j + 1) * BLOCK)
        win = slice(j * BLOCK, (j + 2) * BLOCK)
        qq = jnp.concatenate([q_scr[rows, (2 * kh + a) * LANES:(2 * kh + a + 1) * LANES] for a in range(2)], axis=0)
        kcat = jnp.concatenate([kl_scr[kh, win, :], kh_scr[kh, win, :]], axis=0)
        s = _dot_nt(qq, kcat)
        for a in range(2):
            for half in range(2):
                sa = s[a * BLOCK:(a + 1) * BLOCK, half * 2 * BLOCK:(half + 1) * 2 * BLOCK]
                folded = jnp.where(from_prev, sa[:, :BLOCK], sa[:, BLOCK:])
                if j == 0:
                    folded = folded + bias0
                s_scr[i % S_SLOTS, a * BLOCK:(a + 1) * BLOCK, half * BLOCK:(half + 1) * BLOCK] = folded

    units_per_gate = len(units) * GATE_COLS // (2 * D_MODEL)
    for i in range(S_AHEAD):
        scores(i)
    def softmax(i):
        kh = units[i][1]
        ps, sink_terms = [], []
        for a in range(2):
            es, st = [], []
            for half in range(2):
                sh = s_scr[i % S_SLOTS, a * BLOCK:(a + 1) * BLOCK, half * BLOCK:(half + 1) * BLOCK]
                sink = sinks_ref[4 * kh + 2 * a + half]
                m = jnp.maximum(jnp.max(sh, axis=1, keepdims=True), sink)
                e = jnp.exp(sh - m)
                es.append(jnp.where(from_prev, e, 0.0).astype(BF16))
                es.append(jnp.where(from_prev, 0.0, e).astype(BF16))
                st.append(sink - m)
            ps.append(jnp.concatenate(es, axis=1))
            sink_terms.append(jnp.exp(jnp.where(lo64, st[0], st[1])))
        return jnp.concatenate(ps, axis=0), sink_terms

    def weighted_values(i, p, sink_terms):
        j, kh = units[i]
        rows = slice(j * BLOCK, (j + 1) * BLOCK)
        win = slice(j * BLOCK, (j + 2) * BLOCK)
        vcat = jnp.concatenate([
            jnp.concatenate([vl_scr[kh, win, :], ones_lo], axis=1),
            jnp.concatenate([vh_scr[kh, win, :], ones_hi], axis=1)], axis=0)
        o = _dot(p, vcat)
        for a in range(2):
            oa = o[a * BLOCK:(a + 1) * BLOCK]
            attn_scr[rows, (2 * kh + a) * LANES:(2 * kh + a + 1) * LANES] = (
                oa[:, :LANES] / (oa[:, LANES:] + sink_terms[a])).astype(BF16)

    pending = None
    for i in range(len(units)):
        if i + S_AHEAD < len(units):
            scores(i + S_AHEAD)
        current = (i,) + softmax(i)
        if pending is not None:
            weighted_values(*pending)
        pending = current
        if i % units_per_gate == units_per_gate - 1:
            gcols = slice((i // units_per_gate) * GATE_COLS, (i // units_per_gate + 1) * GATE_COLS)
            g_scr[:, gcols] = _sigmoid(_dot(xn, w_in_ref[:, C_GP + gcols.start:C_GP + gcols.stop]))
    weighted_values(*pending)

    u_scr[U_HALO:, :] = u
    po_ref[0] = u_scr[U_HALO + tm - POOL_STATE:U_HALO + tm, :]
    pos = t * tm + lax.broadcasted_iota(jnp.int32, (tm, 1), 0)

    def win_sum(g, w, ug):
        cols = slice(g * POOL_GC, (g + 1) * POOL_GC)
        n = U_HALO - SUBLANES + tm
        src, span = u_scr, 1
        while span < w:
            lvl = src[SUBLANES:SUBLANES + n, cols] + src[SUBLANES - span:SUBLANES - span + n, cols]
            span *= 2
            if span < w:
                lvl_scr[g, SUBLANES:SUBLANES + n, :] = lvl
                src, cols = lvl_scr.at[g], slice(None)
        return lvl[U_HALO - SUBLANES:, :]

    pooled = _pool_mix(u, win_sum, lambda w: jnp.minimum(w, pos + 1).astype(F32), gw_ref, pscale_ref)

    for scr in (kl_scr, kh_scr, vl_scr, vh_scr):
        scr[:, 0:BLOCK, :] = scr[:, tm:tm + BLOCK, :]
    u_scr[0:U_HALO, :] = u_scr[tm:tm + U_HALO, :]

    _shift_caches(step, ckt_ref.shape[0], knt_ref, vnt_ref, ckt_ref, cvt_ref, nkt_ref, nvt_ref)

    merged = g_scr[:, 0:D_MODEL] * _dot(pooled.astype(BF16), w_pb_ref[...])
    merged = merged + g_scr[:, D_MODEL:2 * D_MODEL] * _dot(attn_scr[...], w_ab_ref[...])
    h_ref[0] = x + _dot(merged.astype(BF16), w_out_ref[...])


def _const_spec(shape):
    nd = len(shape)
    return pl.BlockSpec(shape, lambda *_: (0,) * nd, pipeline_mode=pl.Buffered(1))


def _cast_block_rows(rows, steps):
    br = 2 * SUBLANES
    while rows % br or rows // br > steps:
        br *= 2
    return br


def _cast_specs(to_cast, steps, step_of):
    specs = []
    for w in to_cast:
        br = _cast_block_rows(w.shape[0], steps)
        last = w.shape[0] // br - 1
        specs.append(pl.BlockSpec(
            (br, w.shape[1]), lambda *idx, last=last: (jnp.minimum(step_of(*idx), last), 0)))
    return specs


def _prompt_mixer(x, cos, sg, sinks, ln1, w_in, gw, pscale, w_pb, w_ab, w_out, knt, vnt, ckt, cvt, to_cast, tm):
    b, s, d = x.shape
    nt = s // tm
    cast_specs = _cast_specs(to_cast, b * nt, lambda bi, ti, *_: bi * nt + ti)
    nb = ckt.shape[0] // (b * nt)
    assert nb * b * nt == ckt.shape[0]
    cache_spec = pl.BlockSpec((nb,) + ckt.shape[1:], lambda bi, ti, *_: (bi * nt + ti, 0, 0))
    grid_spec = pltpu.PrefetchScalarGridSpec(
        num_scalar_prefetch=1,
        grid=(b, nt),
        in_specs=[
            pl.BlockSpec((1, tm, d), lambda bi, ti, *_: (bi, ti, 0)),
            pl.BlockSpec((tm, LANES), lambda bi, ti, *_: (ti, 0)),
            pl.BlockSpec((tm, LANES), lambda bi, ti, *_: (ti, 0)),
            _const_spec((1, d)),
            _const_spec(w_in.shape),
            _const_spec(gw.shape),
            _const_spec((1, POOL_WIDTH)),
            _const_spec(w_pb.shape),
            _const_spec(w_ab.shape),
            _const_spec(w_out.shape),
            _const_spec(knt.shape),
            _const_spec(vnt.shape),
            cache_spec,
            cache_spec,
        ] + cast_specs,
        out_specs=[
            pl.BlockSpec((1, tm, d), lambda bi, ti, *_: (bi, ti, 0)),
            pl.BlockSpec((1, KV_W, WINDOW), lambda bi, ti, *_: (bi, 0, 0)),
            pl.BlockSpec((1, KV_W, WINDOW), lambda bi, ti, *_: (bi, 0, 0)),
            pl.BlockSpec((1, POOL_STATE, POOL_WIDTH), lambda bi, ti, *_: (bi, 0, 0)),
            cache_spec,
            cache_spec,
        ] + cast_specs,
        scratch_shapes=[
            pltpu.VMEM((N_KV_HEADS, BLOCK + tm, LANES), BF16),
            pltpu.VMEM((N_KV_HEADS, BLOCK + tm, LANES), BF16),
            pltpu.VMEM((N_KV_HEADS, BLOCK + tm, LANES), BF16),
            pltpu.VMEM((N_KV_HEADS, BLOCK + tm, LANES), BF16),
            pltpu.VMEM((U_HALO + tm, POOL_WIDTH), F32),
            pltpu.VMEM((len(POOL_WINDOWS), U_HALO + tm, POOL_GC), F32),
            pltpu.VMEM((tm, Q_W), BF16),
            pltpu.VMEM((tm, Q_W), BF16),
            pltpu.VMEM((S_SLOTS, 2 * BLOCK, 2 * BLOCK), F32),
            pltpu.VMEM((tm, 2 * D_MODEL), F32),
        ],
    )
    return pl.pallas_call(
        functools.partial(_mixer_kernel, tm=tm, n_cast=len(to_cast)),
        grid_spec=grid_spec,
        out_shape=[
            jax.ShapeDtypeStruct((b, s, d), F32),
            jax.ShapeDtypeStruct((b, KV_W, WINDOW), F32),
            jax.ShapeDtypeStruct((b, KV_W, WINDOW), F32),
            jax.ShapeDtypeStruct((b, POOL_STATE, POOL_WIDTH), F32),
            jax.ShapeDtypeStruct(ckt.shape, F32),
            jax.ShapeDtypeStruct(cvt.shape, F32),
        ] + [jax.ShapeDtypeStruct(w.shape, BF16) for w in to_cast],
        compiler_params=pltpu.CompilerParams(
            dimension_semantics=("arbitrary", "arbitrary"), vmem_limit_bytes=VMEM_LIMIT),
        name="prompt_mixer",
    )(sinks, x, cos, sg, ln1, w_in, gw, pscale, w_pb, w_ab, w_out, knt, vnt, ckt, cvt, *to_cast)


FFN_CHUNKS = ((0, 1024), (1024, 2048), (2048, FFN_HIDDEN))


def _ffn_kernel(h_ref, p_ref, ln2_ref, w1_ref, w2_ref, w_pp_ref, pn_ref, w_pg_ref, fn_ref, y_ref, *, tm):
    def tile_stages(r0):
        rows = slice(r0, r0 + tm)
        st = {}

        def up_proj(c):
            lo, hi = FFN_CHUNKS[c]
            if c == 0:
                st['h'] = h_ref[rows, :]
                st['hn'] = _rms(st['h'], ln2_ref[...]).astype(BF16)
                st['acc'] = st['h']
            st['gate', c] = _dot(st['hn'], w1_ref[:, lo:hi])
            st['up', c] = _dot(st['hn'], w1_ref[:, FFN_HIDDEN + lo:FFN_HIDDEN + hi])
            if c == len(FFN_CHUNKS) - 1:
                st['e'] = _rms(_dot(p_ref[rows, :].astype(BF16), w_pp_ref[...]), pn_ref[...])

        def down_proj(c):
            lo, hi = FFN_CHUNKS[c]
            gate = st.pop(('gate', c))
            act = (gate * _sigmoid(gate) * st.pop(('up', c))).astype(BF16)
            st['acc'] = st['acc'] + _dot(act, w2_ref[lo:hi, :])

        def ple_gate():
            st['g'] = _dot(st['acc'].astype(BF16), w_pg_ref[...])

        def finish():
            h3 = st['acc'] + _sigmoid(st['g']) * st['e']
            y_ref[rows, :] = _rms(h3, fn_ref[...])

        n = len(FFN_CHUNKS)
        steps = [functools.partial(up_proj, 0)]
        for c in range(1, n):
            steps += [functools.partial(up_proj, c), functools.partial(down_proj, c - 1)]
        return steps + [functools.partial(down_proj, n - 1), ple_gate, finish]

    tail = 3
    order = []
    for r0 in range(0, h_ref.shape[0], tm):
        steps = tile_stages(r0)
        held, order = order[len(order) - tail:] if order else [], order[:len(order) - tail] if order else []
        for k in range(max(len(held), tail)):
            order += steps[k:k + 1] + held[k:k + 1]
        order += steps[tail:]
    for step in order:
        step()


def _ffn(h, p, ln2, w1, w2, w_pp, pn, w_pg, fn, tm, nsub):
    n, d = h.shape
    blk = tm * nsub
    return pl.pallas_call(
        functools.partial(_ffn_kernel, tm=tm),
        grid=(n // blk,),
        in_specs=[
            pl.BlockSpec((blk, d), lambda i: (i, 0)),
            pl.BlockSpec((blk, PLE_DIM), lambda i: (i, 0)),
            _const_spec((1, d)),
            _const_spec(w1.shape),
            _const_spec(w2.shape),
            _const_spec(w_pp.shape),
            _const_spec((1, d)),
            _const_spec(w_pg.shape),
            _const_spec((1, d)),
        ],
        out_specs=pl.BlockSpec((blk, d), lambda i: (i, 0)),
        out_shape=jax.ShapeDtypeStruct((n, d), F32),
        compiler_params=pltpu.CompilerParams(dimension_semantics=("arbitrary",), vmem_limit_bytes=VMEM_LIMIT),
        name="ffn_ple_norm",
    )(h, p, ln2, w1, w2, w_pp, pn, w_pg, fn)


def _sample_pre_kernel(x_ref, cos_ref, sg_ref, ln1_ref, w_in_ref, qe_ref, knt_ref, vnt_ref, kn_ref, vn_ref, u_ref):
    n = x_ref.shape[0]
    lo8, _ = _lane_masks()
    xn = _rms(x_ref[...], ln1_ref[...]).astype(BF16)
    cos = cos_ref[...]
    sg = sg_ref[...]
    w = lambda lo, hi: w_in_ref[:, lo:hi].astype(BF16)
    u_ref[...] = _dot(xn, w(C_U, C_Q))
    q = (_rope(_dot(xn, w(C_Q, C_K)), cos, sg, lo8) * (HEAD_DIM ** -0.5)).astype(BF16)
    kn = _rope(_dot(xn, w(C_K, C_V)), cos, sg, lo8)
    vn = _dot(xn, w(C_V, C_GP))
    kn_ref[...] = kn
    vn_ref[...] = vn
    knt_ref[...] = kn.T
    vnt_ref[...] = vn.T
    ii = lax.broadcasted_iota(jnp.int32, (Q_W, KV_W), 0)
    jj = lax.broadcasted_iota(jnp.int32, (Q_W, KV_W), 1)
    for r in range(N_HEADS):
        kh = r // GROUP
        sel = ((ii - r * HEAD_DIM) == (jj - kh * HEAD_DIM)) & (jj >= kh * HEAD_DIM) & (jj < (kh + 1) * HEAD_DIM)
        qr = _dot(q, jnp.where(sel, 1.0, 0.0).astype(BF16))
        for c in range(KV_W // LANES):
            qe_ref[c, pl.ds(r, n, stride=N_HEADS), :] = qr[:, c * LANES:(c + 1) * LANES]


def _sample_pre(x, cos, sg, ln1, w_in):
    n, d = x.shape
    return pl.pallas_call(
        _sample_pre_kernel,
        grid=(1,),
        in_specs=[
            _const_spec((n, d)),
            _const_spec((1, LANES)),
            _const_spec((1, LANES)),
            _const_spec((1, d)),
            pl.BlockSpec((d, C_GP), lambda i: (0, 0), pipeline_mode=pl.Buffered(1)),
        ],
        out_specs=[
            pl.BlockSpec((KV_W // LANES, n * N_HEADS, LANES), lambda i: (0, 0, 0)),
            pl.BlockSpec((KV_W, n), lambda i: (0, 0)),
            pl.BlockSpec((KV_W, n), lambda i: (0, 0)),
            pl.BlockSpec((n, KV_W), lambda i: (0, 0)),
            pl.BlockSpec((n, KV_W), lambda i: (0, 0)),
            pl.BlockSpec((n, POOL_WIDTH), lambda i: (0, 0)),
        ],
        out_shape=[
            jax.ShapeDtypeStruct((KV_W // LANES, n * N_HEADS, LANES), F32),
            jax.ShapeDtypeStruct((KV_W, n), F32),
            jax.ShapeDtypeStruct((KV_W, n), F32),
            jax.ShapeDtypeStruct((n, KV_W), F32),
            jax.ShapeDtypeStruct((n, KV_W), F32),
            jax.ShapeDtypeStruct((n, POOL_WIDTH), F32),
        ],
        compiler_params=pltpu.CompilerParams(dimension_semantics=("arbitrary",), vmem_limit_bytes=VMEM_LIMIT),
        name="sample_pre",
    )(x, cos, sg, ln1, w_in)


def _shift_caches(step, nb, knt_ref, vnt_ref, ckt_ref, cvt_ref, nkt_ref, nvt_ref):
    n = knt_ref.shape[1]
    w_cache = ckt_ref.shape[2]
    newest = lax.broadcasted_iota(jnp.int32, (1, w_cache), 1) == w_cache - 1
    shift = lax.rem(n - lax.rem(step * nb, n), n)
    kcols = pltpu.roll(knt_ref[...], shift, 1)
    vcols = pltpu.roll(vnt_ref[...], shift, 1)
    for bl in range(nb):
        nkt_ref[bl] = jnp.where(newest, kcols[:, bl:bl + 1], pltpu.roll(ckt_ref[bl], w_cache - 1, 1))
        nvt_ref[bl] = jnp.where(newest, vcols[:, bl:bl + 1], pltpu.roll(cvt_ref[bl], w_cache - 1, 1))


def _sample_attn_kernel(qe_ref, kn_ref, vn_ref, ckt_ref, cvt_ref, sink_ref, *rest, bb, n_cast):
    cast_src = rest[:n_cast]
    o_ref = rest[n_cast]
    cast_dst = rest[n_cast + 1:]
    for src, dst in zip(cast_src, cast_dst):
        dst[...] = src[...].astype(BF16)
    w_cache = ckt_ref.shape[2]
    oldest = lax.broadcasted_iota(jnp.int32, (1, w_cache), 1) == 0
    sink = sink_ref[...]
    rounded = lambda a: a.astype(BF16).astype(F32)
    scores = []
    for bl in range(bb):
        rows = slice(bl * N_HEADS, (bl + 1) * N_HEADS)
        qb = jnp.concatenate([qe_ref[c, rows, :] for c in range(KV_W // LANES)], axis=1).astype(BF16)
        s_old = jnp.where(oldest, NEG_INF, _dot(qb, ckt_ref[bl].astype(BF16)))
        s_new = jnp.sum(qb.astype(F32) * rounded(kn_ref[bl:bl + 1, :]), axis=1, keepdims=True)
        scores.append((s_old, s_new))
    probs = []
    for s_old, s_new in scores:
        m = jnp.maximum(jnp.maximum(jnp.max(s_old, axis=-1, keepdims=True), s_new), sink)
        e_old = jnp.exp(s_old - m)
        e_new = jnp.exp(s_new - m)
        denom = jnp.sum(e_old, axis=-1, keepdims=True) + e_new + jnp.exp(sink - m)
        probs.append((e_old.astype(BF16), rounded(e_new), denom))
    for bl, (e_old, e_new, denom) in enumerate(probs):
        rows = slice(bl * N_HEADS, (bl + 1) * N_HEADS)
        o = (_dot_nt(e_old, cvt_ref[bl].astype(BF16)) + e_new * rounded(vn_ref[bl:bl + 1, :])) / denom
        for c in range(KV_W // LANES):
            o_ref[c, rows, :] = o[:, c * LANES:(c + 1) * LANES]


def _sample_attn(qe, kn, vn, ckt, cvt, sink, to_cast, bb):
    n, _, w_cache = ckt.shape
    cast_specs = _cast_specs(to_cast, n // bb, lambda i: i)
    cache_spec = pl.BlockSpec((bb, KV_W, w_cache), lambda i: (i, 0, 0))
    new_spec = pl.BlockSpec((bb, KV_W), lambda i: (i, 0))
    head_spec = pl.BlockSpec((KV_W // LANES, bb * N_HEADS, LANES), lambda i: (0, i, 0))
    return pl.pallas_call(
        functools.partial(_sample_attn_kernel, bb=bb, n_cast=len(to_cast)),
        grid=(n // bb,),
        in_specs=[head_spec, new_spec, new_spec, cache_spec, cache_spec, _const_spec((N_HEADS, 1))] + cast_specs,
        out_specs=[head_spec] + cast_specs,
        out_shape=[jax.ShapeDtypeStruct((KV_W // LANES, n * N_HEADS, LANES), F32)]
        + [jax.ShapeDtypeStruct(w.shape, BF16) for w in to_cast],
        compiler_params=pltpu.CompilerParams(dimension_semantics=("arbitrary",), vmem_limit_bytes=VMEM_LIMIT),
        name="sample_attn",
    )(qe, kn, vn, ckt, cvt, sink, *to_cast)


def _sample_post_kernel(x_ref, u_ref, st_ref, o_ref, ln1_ref, w_g_ref, gw_ref, pscale_ref, w_pb_ref, w_ab_ref,
                        w_out_ref, h_ref, nst_ref):
    x = x_ref[...]
    n = x.shape[0]
    xn = _rms(x, ln1_ref[...]).astype(BF16)
    u = u_ref[...]
    nst_ref[0:POOL_STATE - 1] = st_ref[1:POOL_STATE]
    nst_ref[POOL_STATE - 1] = u

    def win_sum(g, w, ug):
        acc = ug
        for i in range(1, w):
            acc = acc + st_ref[POOL_STATE - i, :, g * POOL_GC:(g + 1) * POOL_GC]
        return acc

    pooled = _pool_mix(u, win_sum, lambda w: jnp.float32(min(w, PAST_LEN + 1)), gw_ref, pscale_ref)
    merged = _sigmoid(_dot(xn, w_g_ref[:, 0:D_MODEL])) * _dot(pooled.astype(BF16), w_pb_ref[...])
    kv_of_lane = lax.broadcasted_iota(jnp.int32, (1, KV_W), 1) // HEAD_DIM
    ab = jnp.zeros((n, D_MODEL), F32)
    for g in range(GROUP):
        row_g = jnp.zeros((n, KV_W), F32)
        for kh in range(N_KV_HEADS):
            r = kh * GROUP + g
            o_r = jnp.concatenate([o_ref[c, pl.ds(r, n, stride=N_HEADS), :] for c in range(KV_W // LANES)], axis=1)
            row_g = jnp.where(kv_of_lane == kh, o_r, row_g)
        w_g = jnp.concatenate([w_ab_ref[(kh * GROUP + g) * HEAD_DIM:(kh * GROUP + g + 1) * HEAD_DIM, :]
                               for kh in range(N_KV_HEADS)], axis=0)
        ab = ab + _dot(row_g.astype(BF16), w_g)
    merged = merged + _sigmoid(_dot(xn, w_g_ref[:, D_MODEL:2 * D_MODEL])) * ab
    h_ref[...] = x + _dot(merged.astype(BF16), w_out_ref[...])


def _sample_post(x, u, st, o, ln1, w_in, gw, pscale, w_pb, w_ab, w_out):
    n, d = x.shape
    return pl.pallas_call(
        _sample_post_kernel,
        grid=(1,),
        in_specs=[
            _const_spec((n, d)),
            _const_spec((n, POOL_WIDTH)),
            _const_spec(st.shape),
            _const_spec((KV_W // LANES, n * N_HEADS, LANES)),
            _const_spec((1, d)),
            pl.BlockSpec((d, 2 * D_MODEL), lambda i: (0, 1), pipeline_mode=pl.Buffered(1)),
            _const_spec(gw.shape),
            _const_spec((1, POOL_WIDTH)),
            _const_spec(w_pb.shape),
            _const_spec(w_ab.shape),
            _const_spec(w_out.shape),
        ],
        out_specs=[pl.BlockSpec((n, d), lambda i: (0, 0)), pl.BlockSpec(st.shape, lambda i: (0, 0, 0))],
        out_shape=[jax.ShapeDtypeStruct((n, d), F32), jax.ShapeDtypeStruct(st.shape, F32)],
        compiler_params=pltpu.CompilerParams(dimension_semantics=("arbitrary",), vmem_limit_bytes=VMEM_LIMIT),
        name="sample_post",
    )(x, u, st, o, ln1, w_in, gw, pscale, w_pb, w_ab, w_out)


def _rope_tables(first_pos, n):
    half = ROT_DIMS // 2
    inv = ROPE_THETA ** (-(np.arange(0, ROT_DIMS, 2, dtype=np.float64) / ROT_DIMS))
    ang = np.arange(first_pos, first_pos + n, dtype=np.float64)[:, None] * inv[None, :]
    cos, sin = np.cos(ang), np.sin(ang)
    rest = HEAD_DIM - 2 * half
    c64 = np.concatenate([cos, cos, np.ones((n, rest))], axis=1)
    s64 = np.concatenate([-sin, sin, np.zeros((n, rest))], axis=1)
    reps = LANES // HEAD_DIM
    return jnp.asarray(np.tile(c64, (1, reps)), F32), jnp.asarray(np.tile(s64, (1, reps)), F32)


def kernel(x_prompt, x_sample, p_prompt, p_sample, cache_k, cache_v, state_pool, ln1, w_in, pool_group_w, pool_scale,
           attn_sinks, w_pool_branch, w_attn_branch, w_out, ln2, w_ffn_in, w_ffn_out, w_ple_proj, ple_norm,
           w_ple_gate, final_norm):
    depth = ln1.shape[0]
    b, s, d = x_prompt.shape
    bd, t_dec, _ = x_sample.shape
    w_cache = cache_k.shape[2]
    assert depth == 1 and t_dec == 1 and w_cache == WINDOW and s % BLOCK == 0 and d == D_MODEL
    tm = 512
    assert s % tm == 0

    cos_p, sg_p = _rope_tables(0, s)
    cos_s, sg_s = _rope_tables(PAST_LEN, t_dec)

    hp = x_prompt
    hs = x_sample.reshape(bd, d)
    row = lambda a: a.reshape(1, -1)
    nkp, nvp, npp, nks, nvs, nps = [], [], [], [], [], []
    for i in range(depth):
        to_fm = lambda c: jnp.transpose(c, (0, 2, 3, 1)).reshape(bd, KV_W, w_cache)
        from_fm = lambda c: jnp.transpose(c.reshape(bd, N_KV_HEADS, HEAD_DIM, w_cache), (0, 3, 1, 2))
        ckt, cvt = to_fm(cache_k[i]), to_fm(cache_v[i])
        qe, knt, vnt, kn, vn, un = _sample_pre(hs, cos_s, sg_s, row(ln1[i]), w_in[i])
        o, wi, gw, wpb, wab, wo = _sample_attn(
            qe, kn, vn, ckt, cvt, attn_sinks[i].reshape(N_HEADS, 1),
            (w_in[i], pool_group_w[i].reshape(POOL_WIDTH, POOL_GC), w_pool_branch[i], w_attn_branch[i], w_out[i]), 16)
        gw = gw.reshape(len(POOL_WINDOWS), POOL_GC, POOL_GC)
        h1, kp, vp, pp, nkt, nvt, w1, w2, wpp, wpg = _prompt_mixer(
            hp, cos_p, sg_p, attn_sinks[i], row(ln1[i]), wi, gw, row(pool_scale[i]), wpb, wab, wo,
            knt, vnt, ckt, cvt, (w_ffn_in[i], w_ffn_out[i], w_ple_proj[i], w_ple_gate[i]), tm)
        ffn_args = (row(ln2[i]), w1, w2, wpp, row(ple_norm[i]), wpg)
        hp = _ffn(h1.reshape(b * s, d), p_prompt[i].reshape(b * s, PLE_DIM), *ffn_args, row(final_norm),
                  tm, 2).reshape(b, s, d)
        from_fm_p = lambda c: jnp.transpose(c.reshape(b, N_KV_HEADS, HEAD_DIM, w_cache), (0, 3, 1, 2))
        nkp.append(from_fm_p(kp))
        nvp.append(from_fm_p(vp))
        npp.append(pp)

        h1s, nst = _sample_post(hs, un, jnp.transpose(state_pool[i], (1, 0, 2)), o, row(ln1[i]), wi, gw,
                                row(pool_scale[i]), wpb, wab, wo)
        hs = _ffn(h1s, p_sample[i].reshape(bd * t_dec, PLE_DIM), *ffn_args, row(final_norm), bd, 1)
        nks.append(from_fm(nkt))
        nvs.append(from_fm(nvt))
        nps.append(jnp.transpose(nst, (1, 0, 2)))

    return (hp, hs.reshape(bd, t_dec, d), jnp.stack(nkp), jnp.stack(nvp), jnp.stack(npp),
            jnp.stack(nks), jnp.stack(nvs), jnp.stack(nps))
```

```python
import functools

import jax
import jax.numpy as jnp
import numpy as np
from jax import lax
from jax.experimental import pallas as pl
from jax.experimental.pallas import tpu as pltpu

D_MODEL = 1024
HEAD_DIM = 64
N_HEADS = D_MODEL // HEAD_DIM
N_KV_HEADS = N_HEADS // 4
GROUP = N_HEADS // N_KV_HEADS
ROT_DIMS = HEAD_DIM // 4
ROPE_THETA = 500000.0
WINDOW = 128
BLOCK = 128
POOL_WIDTH = D_MODEL // 2
POOL_WINDOWS = (2, 4, 8, 16)
POOL_GC = POOL_WIDTH // len(POOL_WINDOWS)
POOL_STATE = max(POOL_WINDOWS) - 1
FFN_HIDDEN = -(-8 * D_MODEL // (3 * 256)) * 256
PLE_DIM = 256
EPS = 1e-6
NEG_INF = -1e30
PAST_LEN = 16384

Q_W = N_HEADS * HEAD_DIM
KV_W = N_KV_HEADS * HEAD_DIM
C_U, C_Q, C_K, C_V, C_GP, C_GA, C_END = 0, POOL_WIDTH, POOL_WIDTH + Q_W, POOL_WIDTH + Q_W + KV_W, \
    POOL_WIDTH + Q_W + 2 * KV_W, POOL_WIDTH + Q_W + 2 * KV_W + D_MODEL, POOL_WIDTH + Q_W + 2 * KV_W + 2 * D_MODEL

LANES = 128
S_AHEAD = 16
S_SLOTS = 16
GATE_COLS = 256
U_HALO = 24
SUBLANES = 8
VMEM_LIMIT = 56 * 1024 * 1024

BF16 = jnp.bfloat16
F32 = jnp.float32


def _dot(a, b):
    return jnp.dot(a, b, preferred_element_type=F32)


def _dot_nt(a, b):
    return lax.dot_general(a, b, (((1,), (1,)), ((), ())), preferred_element_type=F32)


def _sigmoid(x):
    return 0.5 * jnp.tanh(0.5 * x) + 0.5


def _rms(x, g):
    y = x * lax.rsqrt(jnp.mean(x * x, axis=-1, keepdims=True) + EPS)
    return y * g


def _rope(x, cos, sg, lo8):
    outs = []
    for c in range(x.shape[1] // LANES):
        xc = x[:, c * LANES:(c + 1) * LANES]
        partner = jnp.where(lo8, pltpu.roll(xc, LANES - ROT_DIMS // 2, 1), pltpu.roll(xc, ROT_DIMS // 2, 1))
        outs.append(xc * cos + partner * sg)
    return jnp.concatenate(outs, axis=1)


def _lane_masks():
    lane = lax.broadcasted_iota(jnp.int32, (1, LANES), 1)
    lo8 = (lane % HEAD_DIM) < (ROT_DIMS // 2)
    lo64 = lane < HEAD_DIM
    return lo8, lo64


def _pool_mix(u, win_sum_fn, cnt_fn, gw_ref, pscale_ref):
    mixed = []
    for g, w in enumerate(POOL_WINDOWS):
        cols = slice(g * POOL_GC, (g + 1) * POOL_GC)
        ug = u[:, cols]
        m = win_sum_fn(g, w, ug) / cnt_fn(w) - ug
        mixed.append(_dot(m.astype(BF16), gw_ref[g]) * pscale_ref[:, cols])
    return jnp.concatenate(mixed, axis=1)


def _mixer_kernel(sinks_ref, x_ref, cos_ref, sg_ref, ln1_ref, w_in_ref, gw_ref, pscale_ref, w_pb_ref, w_ab_ref,
                  w_out_ref, knt_ref, vnt_ref, ckt_ref, cvt_ref, *rest, tm, n_cast):
    cast_src = rest[:n_cast]
    h_ref, ko_ref, vo_ref, po_ref, nkt_ref, nvt_ref = rest[n_cast:n_cast + 6]
    cast_dst = rest[n_cast + 6:2 * n_cast + 6]
    kl_scr, kh_scr, vl_scr, vh_scr, u_scr, lvl_scr, attn_scr, q_scr, s_scr, g_scr = rest[2 * n_cast + 6:]
    t = pl.program_id(1)
    step = pl.program_id(0) * pl.num_programs(1) + t
    for src, dst in zip(cast_src, cast_dst):
        dst[...] = src[...].astype(BF16)
    lo8, lo64 = _lane_masks()

    @pl.when(t == 0)
    def _():
        for scr in (kl_scr, kh_scr, vl_scr, vh_scr):
            scr[:, 0:BLOCK, :] = jnp.zeros((N_KV_HEADS, BLOCK, LANES), BF16)
        u_scr[0:U_HALO, :] = jnp.zeros((U_HALO, POOL_WIDTH), F32)
        lvl_scr[:, 0:SUBLANES, :] = jnp.zeros((len(POOL_WINDOWS), SUBLANES, POOL_GC), F32)

    x = x_ref[0]
    xn = _rms(x, ln1_ref[...]).astype(BF16)
    cos = cos_ref[...]
    sg = sg_ref[...]

    k = _rope(_dot(xn, w_in_ref[:, C_K:C_V]), cos, sg, lo8)
    v = _dot(xn, w_in_ref[:, C_V:C_GP])
    q = (_rope(_dot(xn, w_in_ref[:, C_Q:C_K]), cos, sg, lo8) * (HEAD_DIM ** -0.5)).astype(BF16)
    q_scr[...] = q
    u = _dot(xn, w_in_ref[:, C_U:C_Q])
    ko_ref[0] = k[tm - WINDOW:, :].T
    vo_ref[0] = v[tm - WINDOW:, :].T

    zero = jnp.zeros((tm, LANES), F32)
    for src, lo_scr, hi_scr in ((k, kl_scr, kh_scr), (v, vl_scr, vh_scr)):
        for p in range(KV_W // LANES):
            xp = src[:, p * LANES:(p + 1) * LANES]
            xs = pltpu.roll(xp, HEAD_DIM, 1)
            lo_scr[2 * p, BLOCK:, :] = jnp.where(lo64, xp, zero).astype(BF16)
            hi_scr[2 * p, BLOCK:, :] = jnp.where(lo64, zero, xs).astype(BF16)
            lo_scr[2 * p + 1, BLOCK:, :] = jnp.where(lo64, xs, zero).astype(BF16)
            hi_scr[2 * p + 1, BLOCK:, :] = jnp.where(lo64, zero, xp).astype(BF16)

    qi = lax.broadcasted_iota(jnp.int32, (BLOCK, BLOCK), 0)
    ci = lax.broadcasted_iota(jnp.int32, (BLOCK, BLOCK), 1)
    from_prev = ci > qi
    bias0 = jnp.where(jnp.logical_and(t == 0, from_prev), NEG_INF, 0.0).astype(F32)

    ones_lo = jnp.broadcast_to(jnp.where(lo64, 1.0, 0.0).astype(BF16), (2 * BLOCK, LANES))
    ones_hi = jnp.broadcast_to(jnp.where(lo64, 0.0, 1.0).astype(BF16), (2 * BLOCK, LANES))

    units = [(j, kh) for j in range(tm // BLOCK) for kh in range(N_KV_HEADS)]

    def scores(i):
        j, kh = units[i]
        rows = slice(j * BLOCK, (j + 1) * BLOCK)
        win = slice(j * BLOCK, (j + 2) * BLOCK)
        qq = jnp.concatenate([q_scr[rows, (2 * kh + a) * LANES:(2 * kh + a + 1) * LANES] for a in range(2)], axis=0)
        kcat = jnp.concatenate([kl_scr[kh, win, :], kh_scr[kh, win, :]], axis=0)
        s = _dot_nt(qq, kcat)
        for a in range(2):
            for half in range(2):
                sa = s[a * BLOCK:(a + 1) * BLOCK, half * 2 * BLOCK:(half + 1) * 2 * BLOCK]
                folded = jnp.where(from_prev, sa[:, :BLOCK], sa[:, BLOCK:])
                if j == 0:
                    folded = folded + bias0
                s_scr[i % S_SLOTS, a * BLOCK:(a + 1) * BLOCK, half * BLOCK:(half + 1) * BLOCK] = folded

    units_per_gate = len(units) * GATE_COLS // (2 * D_MODEL)
    for i in range(S_AHEAD):
        scores(i)
    def softmax(i):
        kh = units[i][1]
        ps, sink_terms = [], []
        for a in range(2):
            es, st = [], []
            for half in range(2):
                sh = s_scr[i % S_SLOTS, a * BLOCK:(a + 1) * BLOCK, half * BLOCK:(half + 1) * BLOCK]
                sink = sinks_ref[4 * kh + 2 * a + half]
                m = jnp.maximum(jnp.max(sh, axis=1, keepdims=True), sink)
                e = jnp.exp(sh - m)
                es.append(jnp.where(from_prev, e, 0.0).astype(BF16))
                es.append(jnp.where(from_prev, 0.0, e).astype(BF16))
                st.append(jnp.exp(sink - m))
            ps.append(jnp.concatenate(es, axis=1))
            sink_terms.append(jnp.where(lo64, st[0], st[1]))
        return jnp.concatenate(ps, axis=0), sink_terms

    def weighted_values(i, p, sink_terms):
        j, kh = units[i]
        rows = slice(j * BLOCK, (j + 1) * BLOCK)
        win = slice(j * BLOCK, (j + 2) * BLOCK)
        vcat = jnp.concatenate([
            jnp.concatenate([vl_scr[kh, win, :], ones_lo], axis=1),
            jnp.concatenate([vh_scr[kh, win, :], ones_hi], axis=1)], axis=0)
        o = _dot(p, vcat)
        for a in range(2):
            oa = o[a * BLOCK:(a + 1) * BLOCK]
            attn_scr[rows, (2 * kh + a) * LANES:(2 * kh + a + 1) * LANES] = (
                oa[:, :LANES] / (oa[:, LANES:] + sink_terms[a])).astype(BF16)

    pending = None
    for i in range(len(units)):
        if i + S_AHEAD < len(units):
            scores(i + S_AHEAD)
        current = (i,) + softmax(i)
        if pending is not None:
            weighted_values(*pending)
        pending = current
        if i % units_per_gate == units_per_gate - 1:
            gcols = slice((i // units_per_gate) * GATE_COLS, (i // units_per_gate + 1) * GATE_COLS)
            g_scr[:, gcols] = _sigmoid(_dot(xn, w_in_ref[:, C_GP + gcols.start:C_GP + gcols.stop]))
    weighted_values(*pending)

    u_scr[U_HALO:, :] = u
    po_ref[0] = u_scr[U_HALO + tm - POOL_STATE:U_HALO + tm, :]
    pos = t * tm + lax.broadcasted_iota(jnp.int32, (tm, 1), 0)

    def win_sum(g, w, ug):
        cols = slice(g * POOL_GC, (g + 1) * POOL_GC)
        n = U_HALO - SUBLANES + tm
        src, span = u_scr, 1
        while span < w:
            lvl = src[SUBLANES:SUBLANES + n, cols] + src[SUBLANES - span:SUBLANES - span + n, cols]
            span *= 2
            if span < w:
                lvl_scr[g, SUBLANES:SUBLANES + n, :] = lvl
                src, cols = lvl_scr.at[g], slice(None)
        return lvl[U_HALO - SUBLANES:, :]

    pooled = _pool_mix(u, win_sum, lambda w: jnp.minimum(w, pos + 1).astype(F32), gw_ref, pscale_ref)

    for scr in (kl_scr, kh_scr, vl_scr, vh_scr):
        scr[:, 0:BLOCK, :] = scr[:, tm:tm + BLOCK, :]
    u_scr[0:U_HALO, :] = u_scr[tm:tm + U_HALO, :]

    _shift_caches(step, ckt_ref.shape[0], knt_ref, vnt_ref, ckt_ref, cvt_ref, nkt_ref, nvt_ref)

    merged = g_scr[:, 0:D_MODEL] * _dot(pooled.astype(BF16), w_pb_ref[...])
    merged = merged + g_scr[:, D_MODEL:2 * D_MODEL] * _dot(attn_scr[...], w_ab_ref[...])
    h_ref[0] = x + _dot(merged.astype(BF16), w_out_ref[...])


def _const_spec(shape):
    nd = len(shape)
    return pl.BlockSpec(shape, lambda *_: (0,) * nd, pipeline_mode=pl.Buffered(1))


def _cast_block_rows(rows, steps):
    br = 2 * SUBLANES
    while rows % br or rows // br > steps:
        br *= 2
    return br


def _cast_specs(to_cast, steps, step_of):
    specs = []
    for w in to_cast:
        br = _cast_block_rows(w.shape[0], steps)
        last = w.shape[0] // br - 1
        specs.append(pl.BlockSpec(
            (br, w.shape[1]), lambda *idx, last=last: (jnp.minimum(step_of(*idx), last), 0)))
    return specs


def _prompt_mixer(x, cos, sg, sinks, ln1, w_in, gw, pscale, w_pb, w_ab, w_out, knt, vnt, ckt, cvt, to_cast, tm):
    b, s, d = x.shape
    nt = s // tm
    cast_specs = _cast_specs(to_cast, b * nt, lambda bi, ti, *_: bi * nt + ti)
    nb = ckt.shape[0] // (b * nt)
    assert nb * b * nt == ckt.shape[0]
    cache_spec = pl.BlockSpec((nb,) + ckt.shape[1:], lambda bi, ti, *_: (bi * nt + ti, 0, 0))
    grid_spec = pltpu.PrefetchScalarGridSpec(
        num_scalar_prefetch=1,
        grid=(b, nt),
        in_specs=[
            pl.BlockSpec((1, tm, d), lambda bi, ti, *_: (bi, ti, 0)),
            pl.BlockSpec((tm, LANES), lambda bi, ti, *_: (ti, 0)),
            pl.BlockSpec((tm, LANES), lambda bi, ti, *_: (ti, 0)),
            _const_spec((1, d)),
            _const_spec(w_in.shape),
            _const_spec(gw.shape),
            _const_spec((1, POOL_WIDTH)),
            _const_spec(w_pb.shape),
            _const_spec(w_ab.shape),
            _const_spec(w_out.shape),
            _const_spec(knt.shape),
            _const_spec(vnt.shape),
            cache_spec,
            cache_spec,
        ] + cast_specs,
        out_specs=[
            pl.BlockSpec((1, tm, d), lambda bi, ti, *_: (bi, ti, 0)),
            pl.BlockSpec((1, KV_W, WINDOW), lambda bi, ti, *_: (bi, 0, 0)),
            pl.BlockSpec((1, KV_W, WINDOW), lambda bi, ti, *_: (bi, 0, 0)),
            pl.BlockSpec((1, POOL_STATE, POOL_WIDTH), lambda bi, ti, *_: (bi, 0, 0)),
            cache_spec,
            cache_spec,
        ] + cast_specs,
        scratch_shapes=[
            pltpu.VMEM((N_KV_HEADS, BLOCK + tm, LANES), BF16),
            pltpu.VMEM((N_KV_HEADS, BLOCK + tm, LANES), BF16),
            pltpu.VMEM((N_KV_HEADS, BLOCK + tm, LANES), BF16),
            pltpu.VMEM((N_KV_HEADS, BLOCK + tm, LANES), BF16),
            pltpu.VMEM((U_HALO + tm, POOL_WIDTH), F32),
            pltpu.VMEM((len(POOL_WINDOWS), U_HALO + tm, POOL_GC), F32),
            pltpu.VMEM((tm, Q_W), BF16),
            pltpu.VMEM((tm, Q_W), BF16),
            pltpu.VMEM((S_SLOTS, 2 * BLOCK, 2 * BLOCK), F32),
            pltpu.VMEM((tm, 2 * D_MODEL), F32),
        ],
    )
    return pl.pallas_call(
        functools.partial(_mixer_kernel, tm=tm, n_cast=len(to_cast)),
        grid_spec=grid_spec,
        out_shape=[
            jax.ShapeDtypeStruct((b, s, d), F32),
            jax.ShapeDtypeStruct((b, KV_W, WINDOW), F32),
            jax.ShapeDtypeStruct((b, KV_W, WINDOW), F32),
            jax.ShapeDtypeStruct((b, POOL_STATE, POOL_WIDTH), F32),
            jax.ShapeDtypeStruct(ckt.shape, F32),
            jax.ShapeDtypeStruct(cvt.shape, F32),
        ] + [jax.ShapeDtypeStruct(w.shape, BF16) for w in to_cast],
        compiler_params=pltpu.CompilerParams(
            dimension_semantics=("arbitrary", "arbitrary"), vmem_limit_bytes=VMEM_LIMIT),
        name="prompt_mixer",
    )(sinks, x, cos, sg, ln1, w_in, gw, pscale, w_pb, w_ab, w_out, knt, vnt, ckt, cvt, *to_cast)


FFN_CHUNKS = ((0, 1024), (1024, 2048), (2048, FFN_HIDDEN))


def _ffn_kernel(h_ref, p_ref, hs_ref, ps_ref, ln2_ref, w1_ref, w2_ref, w_pp_ref, pn_ref, w_pg_ref, fn_ref, y_ref,
                ys_ref, *, tm):
    def tile_stages(r0, with_sample):
        rows = slice(r0, r0 + tm)
        st = {}
        gather = lambda ref, ref_s: (jnp.concatenate([ref[rows, :], ref_s[...]], axis=0) if with_sample
                                     else ref[rows, :])

        def up_proj(c):
            lo, hi = FFN_CHUNKS[c]
            if c == 0:
                st['h'] = gather(h_ref, hs_ref)
                st['hn'] = _rms(st['h'], ln2_ref[...]).astype(BF16)
                st['acc'] = st['h']
            st['gate', c] = _dot(st['hn'], w1_ref[:, lo:hi])
            st['up', c] = _dot(st['hn'], w1_ref[:, FFN_HIDDEN + lo:FFN_HIDDEN + hi])
            if c == len(FFN_CHUNKS) - 1:
                st['e'] = _rms(_dot(gather(p_ref, ps_ref).astype(BF16), w_pp_ref[...]), pn_ref[...])

        def down_proj(c):
            lo, hi = FFN_CHUNKS[c]
            gate = st.pop(('gate', c))
            act = (gate * _sigmoid(gate) * st.pop(('up', c))).astype(BF16)
            st['acc'] = st['acc'] + _dot(act, w2_ref[lo:hi, :])

        def ple_gate():
            st['g'] = _dot(st['acc'].astype(BF16), w_pg_ref[...])

        def finish():
            h3 = st['acc'] + _sigmoid(st['g']) * st['e']
            y = _rms(h3, fn_ref[...])
            y_ref[rows, :] = y[:tm]
            if with_sample:
                ys_ref[...] = y[tm:]

        n = len(FFN_CHUNKS)
        steps = [functools.partial(up_proj, 0)]
        for c in range(1, n):
            steps += [functools.partial(up_proj, c), functools.partial(down_proj, c - 1)]
        return steps + [functools.partial(down_proj, n - 1), ple_gate, finish]

    def run(sample_rides):
        tail = 3
        order = []
        starts = list(range(0, h_ref.shape[0], tm))
        for r0 in starts:
            steps = tile_stages(r0, sample_rides and r0 == starts[-1])
            held, order = order[len(order) - tail:] if order else [], order[:len(order) - tail] if order else []
            for k in range(max(len(held), tail)):
                order += steps[k:k + 1] + held[k:k + 1]
            order += steps[tail:]
        for step in order:
            step()

    last = pl.program_id(0) == pl.num_programs(0) - 1
    pl.when(last)(functools.partial(run, True))
    pl.when(jnp.logical_not(last))(functools.partial(run, False))


def _ffn(h, p, hs, ps, ln2, w1, w2, w_pp, pn, w_pg, fn, tm, nsub):
    n, d = h.shape
    blk = tm * nsub
    return pl.pallas_call(
        functools.partial(_ffn_kernel, tm=tm),
        grid=(n // blk,),
        in_specs=[
            pl.BlockSpec((blk, d), lambda i: (i, 0)),
            pl.BlockSpec((blk, PLE_DIM), lambda i: (i, 0)),
            _const_spec(hs.shape),
            _const_spec(ps.shape),
            _const_spec((1, d)),
            _const_spec(w1.shape),
            _const_spec(w2.shape),
            _const_spec(w_pp.shape),
            _const_spec((1, d)),
            _const_spec(w_pg.shape),
            _const_spec((1, d)),
        ],
        out_specs=[pl.BlockSpec((blk, d), lambda i: (i, 0)), pl.BlockSpec(hs.shape, lambda i: (0, 0))],
        out_shape=[jax.ShapeDtypeStruct((n, d), F32), jax.ShapeDtypeStruct(hs.shape, F32)],
        compiler_params=pltpu.CompilerParams(dimension_semantics=("arbitrary",), vmem_limit_bytes=VMEM_LIMIT),
        name="ffn_ple_norm",
    )(h, p, hs, ps, ln2, w1, w2, w_pp, pn, w_pg, fn)


def _sample_pre_kernel(x_ref, cos_ref, sg_ref, ln1_ref, w_in_ref, qe_ref, knt_ref, vnt_ref, kn_ref, vn_ref, u_ref):
    n = x_ref.shape[0]
    lo8, _ = _lane_masks()
    xn = _rms(x_ref[...], ln1_ref[...]).astype(BF16)
    cos = cos_ref[...]
    sg = sg_ref[...]
    w = lambda lo, hi: w_in_ref[:, lo:hi].astype(BF16)
    u_ref[...] = _dot(xn, w(C_U, C_Q))
    q = (_rope(_dot(xn, w(C_Q, C_K)), cos, sg, lo8) * (HEAD_DIM ** -0.5)).astype(BF16)
    kn = _rope(_dot(xn, w(C_K, C_V)), cos, sg, lo8)
    vn = _dot(xn, w(C_V, C_GP))
    kn_ref[...] = kn
    vn_ref[...] = vn
    knt_ref[...] = kn.T
    vnt_ref[...] = vn.T
    _, lo64 = _lane_masks()
    qf = q.astype(F32)
    zero = jnp.zeros((n, LANES), F32)
    for r in range(N_HEADS):
        kh = r // GROUP
        chunk = qf[:, (r // 2) * LANES:(r // 2 + 1) * LANES]
        if r % 2 != kh % 2:
            chunk = pltpu.roll(chunk, HEAD_DIM, 1)
        placed = jnp.where(lo64, chunk, zero) if kh % 2 == 0 else jnp.where(lo64, zero, chunk)
        for c in range(KV_W // LANES):
            qe_ref[c, pl.ds(r, n, stride=N_HEADS), :] = placed if c == kh // 2 else zero


def _sample_pre(x, cos, sg, ln1, w_in):
    n, d = x.shape
    return pl.pallas_call(
        _sample_pre_kernel,
        grid=(1,),
        in_specs=[
            _const_spec((n, d)),
            _const_spec((1, LANES)),
            _const_spec((1, LANES)),
            _const_spec((1, d)),
            pl.BlockSpec((d, C_GP), lambda i: (0, 0), pipeline_mode=pl.Buffered(1)),
        ],
        out_specs=[
            pl.BlockSpec((KV_W // LANES, n * N_HEADS, LANES), lambda i: (0, 0, 0)),
            pl.BlockSpec((KV_W, n), lambda i: (0, 0)),
            pl.BlockSpec((KV_W, n), lambda i: (0, 0)),
            pl.BlockSpec((n, KV_W), lambda i: (0, 0)),
            pl.BlockSpec((n, KV_W), lambda i: (0, 0)),
            pl.BlockSpec((n, POOL_WIDTH), lambda i: (0, 0)),
        ],
        out_shape=[
            jax.ShapeDtypeStruct((KV_W // LANES, n * N_HEADS, LANES), F32),
            jax.ShapeDtypeStruct((KV_W, n), F32),
            jax.ShapeDtypeStruct((KV_W, n), F32),
            jax.ShapeDtypeStruct((n, KV_W), F32),
            jax.ShapeDtypeStruct((n, KV_W), F32),
            jax.ShapeDtypeStruct((n, POOL_WIDTH), F32),
        ],
        compiler_params=pltpu.CompilerParams(dimension_semantics=("arbitrary",), vmem_limit_bytes=VMEM_LIMIT),
        name="sample_pre",
    )(x, cos, sg, ln1, w_in)


def _shift_caches(step, nb, knt_ref, vnt_ref, ckt_ref, cvt_ref, nkt_ref, nvt_ref):
    n = knt_ref.shape[1]
    w_cache = ckt_ref.shape[2]
    newest = lax.broadcasted_iota(jnp.int32, (1, w_cache), 1) == w_cache - 1
    shift = lax.rem(n - lax.rem(step * nb, n), n)
    kcols = pltpu.roll(knt_ref[...], shift, 1)
    vcols = pltpu.roll(vnt_ref[...], shift, 1)
    for bl in range(nb):
        nkt_ref[bl] = jnp.where(newest, kcols[:, bl:bl + 1], pltpu.roll(ckt_ref[bl], w_cache - 1, 1))
        nvt_ref[bl] = jnp.where(newest, vcols[:, bl:bl + 1], pltpu.roll(cvt_ref[bl], w_cache - 1, 1))


def _sample_attn_kernel(qe_ref, kn_ref, vn_ref, ckt_ref, cvt_ref, sink_ref, *rest, bb, n_cast):
    cast_src = rest[:n_cast]
    o_ref = rest[n_cast]
    cast_dst = rest[n_cast + 1:]
    for src, dst in zip(cast_src, cast_dst):
        dst[...] = src[...].astype(BF16)
    w_cache = ckt_ref.shape[2]
    oldest = lax.broadcasted_iota(jnp.int32, (1, w_cache), 1) == 0
    sink = sink_ref[...]
    rounded = lambda a: a.astype(BF16).astype(F32)
    scores = []
    for bl in range(bb):
        rows = slice(bl * N_HEADS, (bl + 1) * N_HEADS)
        qb = jnp.concatenate([qe_ref[c, rows, :] for c in range(KV_W // LANES)], axis=1).astype(BF16)
        s_old = jnp.where(oldest, NEG_INF, _dot(qb, ckt_ref[bl].astype(BF16)))
        s_new = jnp.sum(qb.astype(F32) * rounded(kn_ref[bl:bl + 1, :]), axis=1, keepdims=True)
        scores.append((s_old, s_new))
    probs = []
    for s_old, s_new in scores:
        m = jnp.maximum(jnp.maximum(jnp.max(s_old, axis=-1, keepdims=True), s_new), sink)
        e_old = jnp.exp(s_old - m)
        e_new = jnp.exp(s_new - m)
        denom = jnp.sum(e_old, axis=-1, keepdims=True) + e_new + jnp.exp(sink - m)
        probs.append((e_old.astype(BF16), rounded(e_new), denom))
    for bl, (e_old, e_new, denom) in enumerate(probs):
        rows = slice(bl * N_HEADS, (bl + 1) * N_HEADS)
        o = (_dot_nt(e_old, cvt_ref[bl].astype(BF16)) + e_new * rounded(vn_ref[bl:bl + 1, :])) / denom
        for c in range(KV_W // LANES):
            o_ref[c, rows, :] = o[:, c * LANES:(c + 1) * LANES]


def _sample_attn(qe, kn, vn, ckt, cvt, sink, to_cast, bb):
    n, _, w_cache = ckt.shape
    cast_specs = _cast_specs(to_cast, n // bb, lambda i: i)
    cache_spec = pl.BlockSpec((bb, KV_W, w_cache), lambda i: (i, 0, 0))
    new_spec = pl.BlockSpec((bb, KV_W), lambda i: (i, 0))
    head_spec = pl.BlockSpec((KV_W // LANES, bb * N_HEADS, LANES), lambda i: (0, i, 0))
    return pl.pallas_call(
        functools.partial(_sample_attn_kernel, bb=bb, n_cast=len(to_cast)),
        grid=(n // bb,),
        in_specs=[head_spec, new_spec, new_spec, cache_spec, cache_spec, _const_spec((N_HEADS, 1))] + cast_specs,
        out_specs=[head_spec] + cast_specs,
        out_shape=[jax.ShapeDtypeStruct((KV_W // LANES, n * N_HEADS, LANES), F32)]
        + [jax.ShapeDtypeStruct(w.shape, BF16) for w in to_cast],
        compiler_params=pltpu.CompilerParams(dimension_semantics=("arbitrary",), vmem_limit_bytes=VMEM_LIMIT),
        name="sample_attn",
    )(qe, kn, vn, ckt, cvt, sink, *to_cast)


def _sample_post_kernel(x_ref, u_ref, st_ref, o_ref, ln1_ref, w_g_ref, gw_ref, pscale_ref, w_pb_ref, w_ab_ref,
                        w_out_ref, h_ref, nst_ref):
    x = x_ref[...]
    n = x.shape[0]
    xn = _rms(x, ln1_ref[...]).astype(BF16)
    u = u_ref[...]
    nst_ref[0:POOL_STATE - 1] = st_ref[1:POOL_STATE]
    nst_ref[POOL_STATE - 1] = u

    def win_sum(g, w, ug):
        acc = ug
        for i in range(1, w):
            acc = acc + st_ref[POOL_STATE - i, :, g * POOL_GC:(g + 1) * POOL_GC]
        return acc

    pooled = _pool_mix(u, win_sum, lambda w: jnp.float32(min(w, PAST_LEN + 1)), gw_ref, pscale_ref)
    merged = _sigmoid(_dot(xn, w_g_ref[:, 0:D_MODEL])) * _dot(pooled.astype(BF16), w_pb_ref[...])
    kv_of_lane = lax.broadcasted_iota(jnp.int32, (1, KV_W), 1) // HEAD_DIM
    ab = jnp.zeros((n, D_MODEL), F32)
    for g in range(GROUP):
        row_g = jnp.zeros((n, KV_W), F32)
        for kh in range(N_KV_HEADS):
            r = kh * GROUP + g
            o_r = jnp.concatenate([o_ref[c, pl.ds(r, n, stride=N_HEADS), :] for c in range(KV_W // LANES)], axis=1)
            row_g = jnp.where(kv_of_lane == kh, o_r, row_g)
        w_g = jnp.concatenate([w_ab_ref[(kh * GROUP + g) * HEAD_DIM:(kh * GROUP + g + 1) * HEAD_DIM, :]
                               for kh in range(N_KV_HEADS)], axis=0)
        ab = ab + _dot(row_g.astype(BF16), w_g)
    merged = merged + _sigmoid(_dot(xn, w_g_ref[:, D_MODEL:2 * D_MODEL])) * ab
    h_ref[...] = x + _dot(merged.astype(BF16), w_out_ref[...])


def _sample_post(x, u, st, o, ln1, w_in, gw, pscale, w_pb, w_ab, w_out):
    n, d = x.shape
    return pl.pallas_call(
        _sample_post_kernel,
        grid=(1,),
        in_specs=[
            _const_spec((n, d)),
            _const_spec((n, POOL_WIDTH)),
            _const_spec(st.shape),
            _const_spec((KV_W // LANES, n * N_HEADS, LANES)),
            _const_spec((1, d)),
            pl.BlockSpec((d, 2 * D_MODEL), lambda i: (0, 1), pipeline_mode=pl.Buffered(1)),
            _const_spec(gw.shape),
            _const_spec((1, POOL_WIDTH)),
            _const_spec(w_pb.shape),
            _const_spec(w_ab.shape),
            _const_spec(w_out.shape),
        ],
        out_specs=[pl.BlockSpec((n, d), lambda i: (0, 0)), pl.BlockSpec(st.shape, lambda i: (0, 0, 0))],
        out_shape=[jax.ShapeDtypeStruct((n, d), F32), jax.ShapeDtypeStruct(st.shape, F32)],
        compiler_params=pltpu.CompilerParams(dimension_semantics=("arbitrary",), vmem_limit_bytes=VMEM_LIMIT),
        name="sample_post",
    )(x, u, st, o, ln1, w_in, gw, pscale, w_pb, w_ab, w_out)


def _rope_tables(first_pos, n):
    half = ROT_DIMS // 2
    inv = ROPE_THETA ** (-(np.arange(0, ROT_DIMS, 2, dtype=np.float64) / ROT_DIMS))
    ang = np.arange(first_pos, first_pos + n, dtype=np.float64)[:, None] * inv[None, :]
    cos, sin = np.cos(ang), np.sin(ang)
    rest = HEAD_DIM - 2 * half
    c64 = np.concatenate([cos, cos, np.ones((n, rest))], axis=1)
    s64 = np.concatenate([-sin, sin, np.zeros((n, rest))], axis=1)
    reps = LANES // HEAD_DIM
    return jnp.asarray(np.tile(c64, (1, reps)), F32), jnp.asarray(np.tile(s64, (1, reps)), F32)


def kernel(x_prompt, x_sample, p_prompt, p_sample, cache_k, cache_v, state_pool, ln1, w_in, pool_group_w, pool_scale,
           attn_sinks, w_pool_branch, w_attn_branch, w_out, ln2, w_ffn_in, w_ffn_out, w_ple_proj, ple_norm,
           w_ple_gate, final_norm):
    depth = ln1.shape[0]
    b, s, d = x_prompt.shape
    bd, t_dec, _ = x_sample.shape
    w_cache = cache_k.shape[2]
    assert depth == 1 and t_dec == 1 and w_cache == WINDOW and s % BLOCK == 0 and d == D_MODEL
    tm = 512
    assert s % tm == 0

    cos_p, sg_p = _rope_tables(0, s)
    cos_s, sg_s = _rope_tables(PAST_LEN, t_dec)

    hp = x_prompt
    hs = x_sample.reshape(bd, d)
    row = lambda a: a.reshape(1, -1)
    nkp, nvp, npp, nks, nvs, nps = [], [], [], [], [], []
    for i in range(depth):
        to_fm = lambda c: jnp.transpose(c, (0, 2, 3, 1)).reshape(bd, KV_W, w_cache)
        from_fm = lambda c: jnp.transpose(c.reshape(bd, N_KV_HEADS, HEAD_DIM, w_cache), (0, 3, 1, 2))
        ckt, cvt = to_fm(cache_k[i]), to_fm(cache_v[i])
        qe, knt, vnt, kn, vn, un = _sample_pre(hs, cos_s, sg_s, row(ln1[i]), w_in[i])
        o, wi, gw, wpb, wab, wo = _sample_attn(
            qe, kn, vn, ckt, cvt, attn_sinks[i].reshape(N_HEADS, 1),
            (w_in[i], pool_group_w[i].reshape(POOL_WIDTH, POOL_GC), w_pool_branch[i], w_attn_branch[i], w_out[i]), 16)
        gw = gw.reshape(len(POOL_WINDOWS), POOL_GC, POOL_GC)
        h1, kp, vp, pp, nkt, nvt, w1, w2, wpp, wpg = _prompt_mixer(
            hp, cos_p, sg_p, attn_sinks[i], row(ln1[i]), wi, gw, row(pool_scale[i]), wpb, wab, wo,
            knt, vnt, ckt, cvt, (w_ffn_in[i], w_ffn_out[i], w_ple_proj[i], w_ple_gate[i]), tm)
        h1s, nst = _sample_post(hs, un, jnp.transpose(state_pool[i], (1, 0, 2)), o, row(ln1[i]), wi, gw,
                                row(pool_scale[i]), wpb, wab, wo)
        hp, hs = _ffn(h1.reshape(b * s, d), p_prompt[i].reshape(b * s, PLE_DIM), h1s,
                      p_sample[i].reshape(bd * t_dec, PLE_DIM), row(ln2[i]), w1, w2, wpp, row(ple_norm[i]), wpg,
                      row(final_norm), tm, 2)
        hp = hp.reshape(b, s, d)
        from_fm_p = lambda c: jnp.transpose(c.reshape(b, N_KV_HEADS, HEAD_DIM, w_cache), (0, 3, 1, 2))
        nkp.append(from_fm_p(kp))
        nvp.append(from_fm_p(vp))
        npp.append(pp)
        nks.append(from_fm(nkt))
        nvs.append(from_fm(nvt))
        nps.append(jnp.transpose(nst, (1, 0, 2)))

    return (hp, hs.reshape(bd, t_dec, d), jnp.stack(nkp), jnp.stack(nvp), jnp.stack(npp),
            jnp.stack(nks), jnp.stack(nvs), jnp.stack(nps))
```

```python
import functools

import jax
import jax.numpy as jnp
import numpy as np
from jax import lax
from jax.experimental import pallas as pl
from jax.experimental.pallas import tpu as pltpu

D_MODEL = 1024
HEAD_DIM = 64
N_HEADS = D_MODEL // HEAD_DIM
N_KV_HEADS = N_HEADS // 4
GROUP = N_HEADS // N_KV_HEADS
ROT_DIMS = HEAD_DIM // 4
ROPE_THETA = 500000.0
WINDOW = 128
BLOCK = 128
POOL_WIDTH = D_MODEL // 2
POOL_WINDOWS = (2, 4, 8, 16)
POOL_GC = POOL_WIDTH // len(POOL_WINDOWS)
POOL_STATE = max(POOL_WINDOWS) - 1
FFN_HIDDEN = -(-8 * D_MODEL // (3 * 256)) * 256
PLE_DIM = 256
EPS = 1e-6
NEG_INF = -1e30
PAST_LEN = 16384

Q_W = N_HEADS * HEAD_DIM
KV_W = N_KV_HEADS * HEAD_DIM
C_U, C_Q, C_K, C_V, C_GP, C_GA, C_END = 0, POOL_WIDTH, POOL_WIDTH + Q_W, POOL_WIDTH + Q_W + KV_W, \
    POOL_WIDTH + Q_W + 2 * KV_W, POOL_WIDTH + Q_W + 2 * KV_W + D_MODEL, POOL_WIDTH + Q_W + 2 * KV_W + 2 * D_MODEL

LANES = 128
S_AHEAD = 16
S_SLOTS = 16
GATE_COLS = 256
SUBLANES = 8
U_HALO = 24
VMEM_LIMIT = 56 * 1024 * 1024

BF16 = jnp.bfloat16
F32 = jnp.float32


def _dot(a, b):
    return jnp.dot(a, b, preferred_element_type=F32)


def _dot_nt(a, b):
    return lax.dot_general(a, b, (((1,), (1,)), ((), ())), preferred_element_type=F32)


def _sigmoid(x):
    return 0.5 * jnp.tanh(0.5 * x) + 0.5


def _rms(x, g):
    y = x * lax.rsqrt(jnp.mean(x * x, axis=-1, keepdims=True) + EPS)
    return y * g


def _rope(x, cos, sg, lo8):
    outs = []
    for c in range(x.shape[1] // LANES):
        xc = x[:, c * LANES:(c + 1) * LANES]
        partner = jnp.where(lo8, pltpu.roll(xc, LANES - ROT_DIMS // 2, 1), pltpu.roll(xc, ROT_DIMS // 2, 1))
        outs.append(xc * cos + partner * sg)
    return jnp.concatenate(outs, axis=1)


def _lane_masks():
    lane = lax.broadcasted_iota(jnp.int32, (1, LANES), 1)
    lo8 = (lane % HEAD_DIM) < (ROT_DIMS // 2)
    lo64 = lane < HEAD_DIM
    return lo8, lo64


def _pool_mix(u, win_sum_fn, cnt_fn, gw_ref, pscale_ref):
    mixed = []
    for g, w in enumerate(POOL_WINDOWS):
        cols = slice(g * POOL_GC, (g + 1) * POOL_GC)
        ug = u[:, cols]
        m = win_sum_fn(g, w, ug) / cnt_fn(w) - ug
        mixed.append(_dot(m.astype(BF16), gw_ref[g]) * pscale_ref[:, cols])
    return jnp.concatenate(mixed, axis=1)


def _mixer_kernel(sinks_ref, x_ref, cos_ref, sg_ref, ln1_ref, w_in_ref, gw_ref, pscale_ref, w_pb_ref, w_ab_ref,
                  w_out_ref, knt_ref, vnt_ref, ckt_ref, cvt_ref, *rest, tm, n_cast):
    cast_src = rest[:n_cast]
    h_ref, ko_ref, vo_ref, po_ref, nkt_ref, nvt_ref = rest[n_cast:n_cast + 6]
    cast_dst = rest[n_cast + 6:2 * n_cast + 6]
    kl_scr, kh_scr, vl_scr, vh_scr, u_scr, lvl_scr, attn_scr, q_scr, s_scr, g_scr = rest[2 * n_cast + 6:]
    t = pl.program_id(1)
    step = pl.program_id(0) * pl.num_programs(1) + t
    for src, dst in zip(cast_src, cast_dst):
        dst[...] = src[...].astype(BF16)
    lo8, lo64 = _lane_masks()

    @pl.when(t == 0)
    def _():
        for scr in (kl_scr, kh_scr, vl_scr, vh_scr):
            scr[:, 0:BLOCK, :] = jnp.zeros((N_KV_HEADS, BLOCK, LANES), BF16)
        u_scr[0:U_HALO, :] = jnp.zeros((U_HALO, POOL_WIDTH), F32)
        lvl_scr[:, 0:SUBLANES, :] = jnp.zeros((len(POOL_WINDOWS), SUBLANES, POOL_GC), F32)

    x = x_ref[0]
    xn = _rms(x, ln1_ref[...]).astype(BF16)
    cos = cos_ref[...]
    sg = sg_ref[...]

    k = _rope(_dot(xn, w_in_ref[:, C_K:C_V]), cos, sg, lo8)
    v = _dot(xn, w_in_ref[:, C_V:C_GP])
    q = (_rope(_dot(xn, w_in_ref[:, C_Q:C_K]), cos, sg, lo8) * (HEAD_DIM ** -0.5)).astype(BF16)
    q_scr[...] = q
    u = _dot(xn, w_in_ref[:, C_U:C_Q])
    ko_ref[0] = k[tm - WINDOW:, :].T
    vo_ref[0] = v[tm - WINDOW:, :].T

    zero = jnp.zeros((tm, LANES), F32)
    for src, lo_scr, hi_scr in ((k, kl_scr, kh_scr), (v, vl_scr, vh_scr)):
        for p in range(KV_W // LANES):
            xp = src[:, p * LANES:(p + 1) * LANES]
            xs = pltpu.roll(xp, HEAD_DIM, 1)
            lo_scr[2 * p, BLOCK:, :] = jnp.where(lo64, xp, zero).astype(BF16)
            hi_scr[2 * p, BLOCK:, :] = jnp.where(lo64, zero, xs).astype(BF16)
            lo_scr[2 * p + 1, BLOCK:, :] = jnp.where(lo64, xs, zero).astype(BF16)
            hi_scr[2 * p + 1, BLOCK:, :] = jnp.where(lo64, zero, xp).astype(BF16)

    qi = lax.broadcasted_iota(jnp.int32, (BLOCK, BLOCK), 0)
    ci = lax.broadcasted_iota(jnp.int32, (BLOCK, BLOCK), 1)
    from_prev = ci > qi
    bias0 = jnp.where(jnp.logical_and(t == 0, from_prev), NEG_INF, 0.0).astype(F32)

    ones_lo = jnp.broadcast_to(jnp.where(lo64, 1.0, 0.0).astype(BF16), (2 * BLOCK, LANES))
    ones_hi = jnp.broadcast_to(jnp.where(lo64, 0.0, 1.0).astype(BF16), (2 * BLOCK, LANES))

    units = [(j, kh) for j in range(tm // BLOCK) for kh in range(N_KV_HEADS)]

    def scores(i):
        j, kh = units[i]
        rows = slice(j * BLOCK, (j + 1) * BLOCK)
        win = slice(j * BLOCK, (j + 2) * BLOCK)
        qq = jnp.concatenate([q_scr[rows, (2 * kh + a) * LANES:(2 * kh + a + 1) * LANES] for a in range(2)], axis=0)
        kcat = jnp.concatenate([kl_scr[kh, win, :], kh_scr[kh, win, :]], axis=0)
        s = _dot_nt(qq, kcat)
        for a in range(2):
            for half in range(2):
                sa = s[a * BLOCK:(a + 1) * BLOCK, half * 2 * BLOCK:(half + 1) * 2 * BLOCK]
                folded = jnp.where(from_prev, sa[:, :BLOCK], sa[:, BLOCK:])
                if j == 0:
                    folded = folded + bias0
                s_scr[i % S_SLOTS, a * BLOCK:(a + 1) * BLOCK, half * BLOCK:(half + 1) * BLOCK] = folded

    units_per_gate = len(units) * GATE_COLS // (2 * D_MODEL)
    for i in range(S_AHEAD):
        scores(i)

    def softmax(i):
        kh = units[i][1]
        ps, sink_terms = [], []
        for a in range(2):
            es, st = [], []
            for half in range(2):
                sh = s_scr[i % S_SLOTS, a * BLOCK:(a + 1) * BLOCK, half * BLOCK:(half + 1) * BLOCK]
                sink = sinks_ref[4 * kh + 2 * a + half]
                m = jnp.maximum(jnp.max(sh, axis=1, keepdims=True), sink)
                e = jnp.exp(sh - m)
                es.append(jnp.where(from_prev, e, 0.0).astype(BF16))
                es.append(jnp.where(from_prev, 0.0, e).astype(BF16))
                st.append(jnp.exp(sink - m))
            ps.append(jnp.concatenate(es, axis=1))
            sink_terms.append(jnp.where(lo64, st[0], st[1]))
        return jnp.concatenate(ps, axis=0), sink_terms

    def weighted_values(i, p, sink_terms):
        j, kh = units[i]
        rows = slice(j * BLOCK, (j + 1) * BLOCK)
        win = slice(j * BLOCK, (j + 2) * BLOCK)
        vcat = jnp.concatenate([
            jnp.concatenate([vl_scr[kh, win, :], ones_lo], axis=1),
            jnp.concatenate([vh_scr[kh, win, :], ones_hi], axis=1)], axis=0)
        o = _dot(p, vcat)
        for a in range(2):
            oa = o[a * BLOCK:(a + 1) * BLOCK]
            attn_scr[rows, (2 * kh + a) * LANES:(2 * kh + a + 1) * LANES] = (
                oa[:, :LANES] / (oa[:, LANES:] + sink_terms[a])).astype(BF16)

    pending = None
    for i in range(len(units)):
        if i + S_AHEAD < len(units):
            scores(i + S_AHEAD)
        current = (i,) + softmax(i)
        if pending is not None:
            weighted_values(*pending)
        pending = current
        if i % units_per_gate == units_per_gate - 1:
            gcols = slice((i // units_per_gate) * GATE_COLS, (i // units_per_gate + 1) * GATE_COLS)
            g_scr[:, gcols] = _sigmoid(_dot(xn, w_in_ref[:, C_GP + gcols.start:C_GP + gcols.stop]))
    weighted_values(*pending)

    u_scr[U_HALO:, :] = u
    po_ref[0] = u_scr[U_HALO + tm - POOL_STATE:U_HALO + tm, :]
    pos = t * tm + lax.broadcasted_iota(jnp.int32, (tm, 1), 0)

    def win_sum(g, w, ug):
        cols = slice(g * POOL_GC, (g + 1) * POOL_GC)
        n = U_HALO - SUBLANES + tm
        src, span = u_scr, 1
        while span < w:
            lvl = src[SUBLANES:SUBLANES + n, cols] + src[SUBLANES - span:SUBLANES - span + n, cols]
            span *= 2
            if span < w:
                lvl_scr[g, SUBLANES:SUBLANES + n, :] = lvl
                src, cols = lvl_scr.at[g], slice(None)
        return lvl[U_HALO - SUBLANES:, :]

    pooled = _pool_mix(u, win_sum, lambda w: jnp.minimum(w, pos + 1).astype(F32), gw_ref, pscale_ref)

    for scr in (kl_scr, kh_scr, vl_scr, vh_scr):
        scr[:, 0:BLOCK, :] = scr[:, tm:tm + BLOCK, :]
    u_scr[0:U_HALO, :] = u_scr[tm:tm + U_HALO, :]

    _shift_caches(step, ckt_ref.shape[0], knt_ref, vnt_ref, ckt_ref, cvt_ref, nkt_ref, nvt_ref)

    merged = g_scr[:, 0:D_MODEL] * _dot(pooled.astype(BF16), w_pb_ref[...])
    merged = merged + g_scr[:, D_MODEL:2 * D_MODEL] * _dot(attn_scr[...], w_ab_ref[...])
    h_ref[0] = x + _dot(merged.astype(BF16), w_out_ref[...])


def _const_spec(shape):
    nd = len(shape)
    return pl.BlockSpec(shape, lambda *_: (0,) * nd, pipeline_mode=pl.Buffered(1))


def _cast_block_rows(rows, steps):
    br = 2 * SUBLANES
    while rows % br or rows // br > steps:
        br *= 2
    return br


def _cast_specs(to_cast, steps, step_of):
    specs = []
    for w in to_cast:
        br = _cast_block_rows(w.shape[0], steps)
        last = w.shape[0] // br - 1
        specs.append(pl.BlockSpec(
            (br, w.shape[1]), lambda *idx, last=last: (jnp.minimum(step_of(*idx), last), 0)))
    return specs


def _prompt_mixer(x, cos, sg, sinks, ln1, w_in, gw, pscale, w_pb, w_ab, w_out, knt, vnt, ckt, cvt, to_cast, tm):
    b, s, d = x.shape
    nt = s // tm
    cast_specs = _cast_specs(to_cast, b * nt, lambda bi, ti, *_: bi * nt + ti)
    nb = ckt.shape[0] // (b * nt)
    assert nb * b * nt == ckt.shape[0]
    cache_spec = pl.BlockSpec((nb,) + ckt.shape[1:], lambda bi, ti, *_: (bi * nt + ti, 0, 0))
    grid_spec = pltpu.PrefetchScalarGridSpec(
        num_scalar_prefetch=1,
        grid=(b, nt),
        in_specs=[
            pl.BlockSpec((1, tm, d), lambda bi, ti, *_: (bi, ti, 0)),
            pl.BlockSpec((tm, LANES), lambda bi, ti, *_: (ti, 0)),
            pl.BlockSpec((tm, LANES), lambda bi, ti, *_: (ti, 0)),
            _const_spec((1, d)),
            _const_spec(w_in.shape),
            _const_spec(gw.shape),
            _const_spec((1, POOL_WIDTH)),
            _const_spec(w_pb.shape),
            _const_spec(w_ab.shape),
            _const_spec(w_out.shape),
            _const_spec(knt.shape),
            _const_spec(vnt.shape),
            cache_spec,
            cache_spec,
        ] + cast_specs,
        out_specs=[
            pl.BlockSpec((1, tm, d), lambda bi, ti, *_: (bi, ti, 0)),
            pl.BlockSpec((1, KV_W, WINDOW), lambda bi, ti, *_: (bi, 0, 0)),
            pl.BlockSpec((1, KV_W, WINDOW), lambda bi, ti, *_: (bi, 0, 0)),
            pl.BlockSpec((1, POOL_STATE, POOL_WIDTH), lambda bi, ti, *_: (bi, 0, 0)),
            cache_spec,
            cache_spec,
        ] + cast_specs,
        scratch_shapes=[
            pltpu.VMEM((N_KV_HEADS, BLOCK + tm, LANES), BF16),
            pltpu.VMEM((N_KV_HEADS, BLOCK + tm, LANES), BF16),
            pltpu.VMEM((N_KV_HEADS, BLOCK + tm, LANES), BF16),
            pltpu.VMEM((N_KV_HEADS, BLOCK + tm, LANES), BF16),
            pltpu.VMEM((U_HALO + tm, POOL_WIDTH), F32),
            pltpu.VMEM((len(POOL_WINDOWS), U_HALO + tm, POOL_GC), F32),
            pltpu.VMEM((tm, Q_W), BF16),
            pltpu.VMEM((tm, Q_W), BF16),
            pltpu.VMEM((S_SLOTS, 2 * BLOCK, 2 * BLOCK), F32),
            pltpu.VMEM((tm, 2 * D_MODEL), F32),
        ],
    )
    return pl.pallas_call(
        functools.partial(_mixer_kernel, tm=tm, n_cast=len(to_cast)),
        grid_spec=grid_spec,
        out_shape=[
            jax.ShapeDtypeStruct((b, s, d), F32),
            jax.ShapeDtypeStruct((b, KV_W, WINDOW), F32),
            jax.ShapeDtypeStruct((b, KV_W, WINDOW), F32),
            jax.ShapeDtypeStruct((b, POOL_STATE, POOL_WIDTH), F32),
            jax.ShapeDtypeStruct(ckt.shape, F32),
            jax.ShapeDtypeStruct(cvt.shape, F32),
        ] + [jax.ShapeDtypeStruct(w.shape, BF16) for w in to_cast],
        compiler_params=pltpu.CompilerParams(
            dimension_semantics=("arbitrary", "arbitrary"), vmem_limit_bytes=VMEM_LIMIT),
        name="prompt_mixer",
    )(sinks, x, cos, sg, ln1, w_in, gw, pscale, w_pb, w_ab, w_out, knt, vnt, ckt, cvt, *to_cast)


FFN_CHUNKS = ((0, 1024), (1024, 2048), (2048, FFN_HIDDEN))


def _ffn_kernel(h_ref, p_ref, ln2_ref, w1_ref, w2_ref, w_pp_ref, pn_ref, w_pg_ref, fn_ref, y_ref, *, tm):
    def tile_stages(r0):
        rows = slice(r0, r0 + tm)
        st = {}

        def up_proj(c):
            lo, hi = FFN_CHUNKS[c]
            if c == 0:
                st['h'] = h_ref[rows, :]
                st['hn'] = _rms(st['h'], ln2_ref[...]).astype(BF16)
                st['acc'] = st['h']
            st['gate', c] = _dot(st['hn'], w1_ref[:, lo:hi])
            st['up', c] = _dot(st['hn'], w1_ref[:, FFN_HIDDEN + lo:FFN_HIDDEN + hi])
            if c == len(FFN_CHUNKS) - 1:
                st['e'] = _rms(_dot(p_ref[rows, :].astype(BF16), w_pp_ref[...]), pn_ref[...])

        def down_proj(c):
            lo, hi = FFN_CHUNKS[c]
            gate = st.pop(('gate', c))
            act = (gate * _sigmoid(gate) * st.pop(('up', c))).astype(BF16)
            st['acc'] = st['acc'] + _dot(act, w2_ref[lo:hi, :])

        def ple_gate():
            st['g'] = _dot(st['acc'].astype(BF16), w_pg_ref[...])

        def finish():
            h3 = st['acc'] + _sigmoid(st['g']) * st['e']
            y_ref[rows, :] = _rms(h3, fn_ref[...])

        n = len(FFN_CHUNKS)
        steps = [functools.partial(up_proj, 0)]
        for c in range(1, n):
            steps += [functools.partial(up_proj, c), functools.partial(down_proj, c - 1)]
        return steps + [functools.partial(down_proj, n - 1), ple_gate, finish]

    tail = 3
    order = []
    for r0 in range(0, h_ref.shape[0], tm):
        steps = tile_stages(r0)
        held, order = order[len(order) - tail:] if order else [], order[:len(order) - tail] if order else []
        for k in range(max(len(held), tail)):
            order += steps[k:k + 1] + held[k:k + 1]
        order += steps[tail:]
    for step in order:
        step()


def _ffn(h, p, ln2, w1, w2, w_pp, pn, w_pg, fn, tm, nsub):
    n, d = h.shape
    blk = tm * nsub
    return pl.pallas_call(
        functools.partial(_ffn_kernel, tm=tm),
        grid=(n // blk,),
        in_specs=[
            pl.BlockSpec((blk, d), lambda i: (i, 0)),
            pl.BlockSpec((blk, PLE_DIM), lambda i: (i, 0)),
            _const_spec((1, d)),
            _const_spec(w1.shape),
            _const_spec(w2.shape),
            _const_spec(w_pp.shape),
            _const_spec((1, d)),
            _const_spec(w_pg.shape),
            _const_spec((1, d)),
        ],
        out_specs=pl.BlockSpec((blk, d), lambda i: (i, 0)),
        out_shape=jax.ShapeDtypeStruct((n, d), F32),
        compiler_params=pltpu.CompilerParams(dimension_semantics=("arbitrary",), vmem_limit_bytes=VMEM_LIMIT),
        name="ffn_ple_norm",
    )(h, p, ln2, w1, w2, w_pp, pn, w_pg, fn)


def _sample_pre_kernel(x_ref, cos_ref, sg_ref, ln1_ref, w_in_ref, qe_ref, knt_ref, vnt_ref, kn_ref, vn_ref, u_ref):
    n = x_ref.shape[0]
    lo8, _ = _lane_masks()
    xn = _rms(x_ref[...], ln1_ref[...]).astype(BF16)
    cos = cos_ref[...]
    sg = sg_ref[...]
    w = lambda lo, hi: w_in_ref[:, lo:hi].astype(BF16)
    u_ref[...] = _dot(xn, w(C_U, C_Q))
    q = (_rope(_dot(xn, w(C_Q, C_K)), cos, sg, lo8) * (HEAD_DIM ** -0.5)).astype(BF16)
    kn = _rope(_dot(xn, w(C_K, C_V)), cos, sg, lo8)
    vn = _dot(xn, w(C_V, C_GP))
    kn_ref[...] = kn
    vn_ref[...] = vn
    knt_ref[...] = kn.T
    vnt_ref[...] = vn.T
    ii = lax.broadcasted_iota(jnp.int32, (Q_W, KV_W), 0)
    jj = lax.broadcasted_iota(jnp.int32, (Q_W, KV_W), 1)
    for r in range(N_HEADS):
        kh = r // GROUP
        sel = ((ii - r * HEAD_DIM) == (jj - kh * HEAD_DIM)) & (jj >= kh * HEAD_DIM) & (jj < (kh + 1) * HEAD_DIM)
        qr = _dot(q, jnp.where(sel, 1.0, 0.0).astype(BF16))
        for c in range(KV_W // LANES):
            qe_ref[c, pl.ds(r, n, stride=N_HEADS), :] = qr[:, c * LANES:(c + 1) * LANES]


def _sample_pre(x, cos, sg, ln1, w_in):
    n, d = x.shape
    return pl.pallas_call(
        _sample_pre_kernel,
        grid=(1,),
        in_specs=[
            _const_spec((n, d)),
            _const_spec((1, LANES)),
            _const_spec((1, LANES)),
            _const_spec((1, d)),
            pl.BlockSpec((d, C_GP), lambda i: (0, 0), pipeline_mode=pl.Buffered(1)),
        ],
        out_specs=[
            pl.BlockSpec((KV_W // LANES, n * N_HEADS, LANES), lambda i: (0, 0, 0)),
            pl.BlockSpec((KV_W, n), lambda i: (0, 0)),
            pl.BlockSpec((KV_W, n), lambda i: (0, 0)),
            pl.BlockSpec((n, KV_W), lambda i: (0, 0)),
            pl.BlockSpec((n, KV_W), lambda i: (0, 0)),
            pl.BlockSpec((n, POOL_WIDTH), lambda i: (0, 0)),
        ],
        out_shape=[
            jax.ShapeDtypeStruct((KV_W // LANES, n * N_HEADS, LANES), F32),
            jax.ShapeDtypeStruct((KV_W, n), F32),
            jax.ShapeDtypeStruct((KV_W, n), F32),
            jax.ShapeDtypeStruct((n, KV_W), F32),
            jax.ShapeDtypeStruct((n, KV_W), F32),
            jax.ShapeDtypeStruct((n, POOL_WIDTH), F32),
        ],
        compiler_params=pltpu.CompilerParams(dimension_semantics=("arbitrary",), vmem_limit_bytes=VMEM_LIMIT),
        name="sample_pre",
    )(x, cos, sg, ln1, w_in)


def _shift_caches(step, nb, knt_ref, vnt_ref, ckt_ref, cvt_ref, nkt_ref, nvt_ref):
    n = knt_ref.shape[1]
    w_cache = ckt_ref.shape[2]
    newest = lax.broadcasted_iota(jnp.int32, (1, w_cache), 1) == w_cache - 1
    shift = lax.rem(n - lax.rem(step * nb, n), n)
    kcols = pltpu.roll(knt_ref[...], shift, 1)
    vcols = pltpu.roll(vnt_ref[...], shift, 1)
    for bl in range(nb):
        nkt_ref[bl] = jnp.where(newest, kcols[:, bl:bl + 1], pltpu.roll(ckt_ref[bl], w_cache - 1, 1))
        nvt_ref[bl] = jnp.where(newest, vcols[:, bl:bl + 1], pltpu.roll(cvt_ref[bl], w_cache - 1, 1))


def _sample_attn_kernel(qe_ref, kn_ref, vn_ref, ckt_ref, cvt_ref, sink_ref, *rest, bb, n_cast):
    cast_src = rest[:n_cast]
    o_ref = rest[n_cast]
    cast_dst = rest[n_cast + 1:]
    for src, dst in zip(cast_src, cast_dst):
        dst[...] = src[...].astype(BF16)
    w_cache = ckt_ref.shape[2]
    oldest = lax.broadcasted_iota(jnp.int32, (1, w_cache), 1) == 0
    sink = sink_ref[...]
    rounded = lambda a: a.astype(BF16).astype(F32)
    scores = []
    for bl in range(bb):
        rows = slice(bl * N_HEADS, (bl + 1) * N_HEADS)
        qb = jnp.concatenate([qe_ref[c, rows, :] for c in range(KV_W // LANES)], axis=1).astype(BF16)
        s_old = jnp.where(oldest, NEG_INF, _dot(qb, ckt_ref[bl].astype(BF16)))
        s_new = jnp.sum(qb.astype(F32) * rounded(kn_ref[bl:bl + 1, :]), axis=1, keepdims=True)
        scores.append((s_old, s_new))
    probs = []
    for s_old, s_new in scores:
        m = jnp.maximum(jnp.maximum(jnp.max(s_old, axis=-1, keepdims=True), s_new), sink)
        e_old = jnp.exp(s_old - m)
        e_new = jnp.exp(s_new - m)
        denom = jnp.sum(e_old, axis=-1, keepdims=True) + e_new + jnp.exp(sink - m)
        probs.append((e_old.astype(BF16), rounded(e_new), denom))
    for bl, (e_old, e_new, denom) in enumerate(probs):
        rows = slice(bl * N_HEADS, (bl + 1) * N_HEADS)
        o = (_dot_nt(e_old, cvt_ref[bl].astype(BF16)) + e_new * rounded(vn_ref[bl:bl + 1, :])) / denom
        for c in range(KV_W // LANES):
            o_ref[c, rows, :] = o[:, c * LANES:(c + 1) * LANES]


def _sample_attn(qe, kn, vn, ckt, cvt, sink, to_cast, bb):
    n, _, w_cache = ckt.shape
    cast_specs = _cast_specs(to_cast, n // bb, lambda i: i)
    cache_spec = pl.BlockSpec((bb, KV_W, w_cache), lambda i: (i, 0, 0))
    new_spec = pl.BlockSpec((bb, KV_W), lambda i: (i, 0))
    head_spec = pl.BlockSpec((KV_W // LANES, bb * N_HEADS, LANES), lambda i: (0, i, 0))
    return pl.pallas_call(
        functools.partial(_sample_attn_kernel, bb=bb, n_cast=len(to_cast)),
        grid=(n // bb,),
        in_specs=[head_spec, new_spec, new_spec, cache_spec, cache_spec, _const_spec((N_HEADS, 1))] + cast_specs,
        out_specs=[head_spec] + cast_specs,
        out_shape=[jax.ShapeDtypeStruct((KV_W // LANES, n * N_HEADS, LANES), F32)]
        + [jax.ShapeDtypeStruct(w.shape, BF16) for w in to_cast],
        compiler_params=pltpu.CompilerParams(dimension_semantics=("arbitrary",), vmem_limit_bytes=VMEM_LIMIT),
        name="sample_attn",
    )(qe, kn, vn, ckt, cvt, sink, *to_cast)


def _sample_post_kernel(x_ref, u_ref, st_ref, o_ref, ln1_ref, w_g_ref, gw_ref, pscale_ref, w_pb_ref, w_ab_ref,
                        w_out_ref, h_ref, nst_ref):
    x = x_ref[...]
    n = x.shape[0]
    xn = _rms(x, ln1_ref[...]).astype(BF16)
    u = u_ref[...]
    nst_ref[0:POOL_STATE - 1] = st_ref[1:POOL_STATE]
    nst_ref[POOL_STATE - 1] = u

    def win_sum(g, w, ug):
        acc = ug
        for i in range(1, w):
            acc = acc + st_ref[POOL_STATE - i, :, g * POOL_GC:(g + 1) * POOL_GC]
        return acc

    pooled = _pool_mix(u, win_sum, lambda w: jnp.float32(min(w, PAST_LEN + 1)), gw_ref, pscale_ref)
    merged = _sigmoid(_dot(xn, w_g_ref[:, 0:D_MODEL])) * _dot(pooled.astype(BF16), w_pb_ref[...])
    kv_of_lane = lax.broadcasted_iota(jnp.int32, (1, KV_W), 1) // HEAD_DIM
    ab = jnp.zeros((n, D_MODEL), F32)
    for g in range(GROUP):
        row_g = jnp.zeros((n, KV_W), F32)
        for kh in range(N_KV_HEADS):
            r = kh * GROUP + g
            o_r = jnp.concatenate([o_ref[c, pl.ds(r, n, stride=N_HEADS), :] for c in range(KV_W // LANES)], axis=1)
            row_g = jnp.where(kv_of_lane == kh, o_r, row_g)
        w_g = jnp.concatenate([w_ab_ref[(kh * GROUP + g) * HEAD_DIM:(kh * GROUP + g + 1) * HEAD_DIM, :]
                               for kh in range(N_KV_HEADS)], axis=0)
        ab = ab + _dot(row_g.astype(BF16), w_g)
    merged = merged + _sigmoid(_dot(xn, w_g_ref[:, D_MODEL:2 * D_MODEL])) * ab
    h_ref[...] = x + _dot(merged.astype(BF16), w_out_ref[...])


def _sample_post(x, u, st, o, ln1, w_in, gw, pscale, w_pb, w_ab, w_out):
    n, d = x.shape
    return pl.pallas_call(
        _sample_post_kernel,
        grid=(1,),
        in_specs=[
            _const_spec((n, d)),
            _const_spec((n, POOL_WIDTH)),
            _const_spec(st.shape),
            _const_spec((KV_W // LANES, n * N_HEADS, LANES)),
            _const_spec((1, d)),
            pl.BlockSpec((d, 2 * D_MODEL), lambda i: (0, 1), pipeline_mode=pl.Buffered(1)),
            _const_spec(gw.shape),
            _const_spec((1, POOL_WIDTH)),
            _const_spec(w_pb.shape),
            _const_spec(w_ab.shape),
            _const_spec(w_out.shape),
        ],
        out_specs=[pl.BlockSpec((n, d), lambda i: (0, 0)), pl.BlockSpec(st.shape, lambda i: (0, 0, 0))],
        out_shape=[jax.ShapeDtypeStruct((n, d), F32), jax.ShapeDtypeStruct(st.shape, F32)],
        compiler_params=pltpu.CompilerParams(dimension_semantics=("arbitrary",), vmem_limit_bytes=VMEM_LIMIT),
        name="sample_post",
    )(x, u, st, o, ln1, w_in, gw, pscale, w_pb, w_ab, w_out)


def _rope_tables(first_pos, n):
    half = ROT_DIMS // 2
    inv = ROPE_THETA ** (-(np.arange(0, ROT_DIMS, 2, dtype=np.float64) / ROT_DIMS))
    ang = np.arange(first_pos, first_pos + n, dtype=np.float64)[:, None] * inv[None, :]
    cos, sin = np.cos(ang), np.sin(ang)
    rest = HEAD_DIM - 2 * half
    c64 = np.concatenate([cos, cos, np.ones((n, rest))], axis=1)
    s64 = np.concatenate([-sin, sin, np.zeros((n, rest))], axis=1)
    reps = LANES // HEAD_DIM
    return jnp.asarray(np.tile(c64, (1, reps)), F32), jnp.asarray(np.tile(s64, (1, reps)), F32)


def kernel(x_prompt, x_sample, p_prompt, p_sample, cache_k, cache_v, state_pool, ln1, w_in, pool_group_w, pool_scale,
           attn_sinks, w_pool_branch, w_attn_branch, w_out, ln2, w_ffn_in, w_ffn_out, w_ple_proj, ple_norm,
           w_ple_gate, final_norm):
    depth = ln1.shape[0]
    b, s, d = x_prompt.shape
    bd, t_dec, _ = x_sample.shape
    w_cache = cache_k.shape[2]
    assert depth == 1 and t_dec == 1 and w_cache == WINDOW and s % BLOCK == 0 and d == D_MODEL
    tm = 512
    assert s % tm == 0

    cos_p, sg_p = _rope_tables(0, s)
    cos_s, sg_s = _rope_tables(PAST_LEN, t_dec)

    hp = x_prompt
    hs = x_sample.reshape(bd, d)
    row = lambda a: a.reshape(1, -1)
    nkp, nvp, npp, nks, nvs, nps = [], [], [], [], [], []
    for i in range(depth):
        to_fm = lambda c: jnp.transpose(c, (0, 2, 3, 1)).reshape(bd, KV_W, w_cache)
        from_fm = lambda c: jnp.transpose(c.reshape(bd, N_KV_HEADS, HEAD_DIM, w_cache), (0, 3, 1, 2))
        ckt, cvt = to_fm(cache_k[i]), to_fm(cache_v[i])
        qe, knt, vnt, kn, vn, un = _sample_pre(hs, cos_s, sg_s, row(ln1[i]), w_in[i])
        o, wi, gw, wpb, wab, wo = _sample_attn(
            qe, kn, vn, ckt, cvt, attn_sinks[i].reshape(N_HEADS, 1),
            (w_in[i], pool_group_w[i].reshape(POOL_WIDTH, POOL_GC), w_pool_branch[i], w_attn_branch[i], w_out[i]), 16)
        gw = gw.reshape(len(POOL_WINDOWS), POOL_GC, POOL_GC)
        h1, kp, vp, pp, nkt, nvt, w1, w2, wpp, wpg = _prompt_mixer(
            hp, cos_p, sg_p, attn_sinks[i], row(ln1[i]), wi, gw, row(pool_scale[i]), wpb, wab, wo,
            knt, vnt, ckt, cvt, (w_ffn_in[i], w_ffn_out[i], w_ple_proj[i], w_ple_gate[i]), tm)
        ffn_args = (row(ln2[i]), w1, w2, wpp, row(ple_norm[i]), wpg)
        hp = _ffn(h1.reshape(b * s, d), p_prompt[i].reshape(b * s, PLE_DIM), *ffn_args, row(final_norm),
                  tm, 2).reshape(b, s, d)
        from_fm_p = lambda c: jnp.transpose(c.reshape(b, N_KV_HEADS, HEAD_DIM, w_cache), (0, 3, 1, 2))
        nkp.append(from_fm_p(kp))
        nvp.append(from_fm_p(vp))
        npp.append(pp)

        h1s, nst = _sample_post(hs, un, jnp.transpose(state_pool[i], (1, 0, 2)), o, row(ln1[i]), wi, gw,
                                row(pool_scale[i]), wpb, wab, wo)
        hs = _ffn(h1s, p_sample[i].reshape(bd * t_dec, PLE_DIM), *ffn_args, row(final_norm), bd, 1)
        nks.append(from_fm(nkt))
        nvs.append(from_fm(nvt))
        nps.append(jnp.transpose(nst, (1, 0, 2)))

    return (hp, hs.reshape(bd, t_dec, d), jnp.stack(nkp), jnp.stack(nvp), jnp.stack(npp),
            jnp.stack(nks), jnp.stack(nvs), jnp.stack(nps))
```

```python
import functools

import jax
import jax.numpy as jnp
import numpy as np
from jax import lax
from jax.experimental import pallas as pl
from jax.experimental.pallas import tpu as pltpu

D_MODEL = 1024
HEAD_DIM = 64
N_HEADS = D_MODEL // HEAD_DIM
N_KV_HEADS = N_HEADS // 4
GROUP = N_HEADS // N_KV_HEADS
ROT_DIMS = HEAD_DIM // 4
ROPE_THETA = 500000.0
WINDOW = 128
BLOCK = 128
POOL_WIDTH = D_MODEL // 2
POOL_WINDOWS = (2, 4, 8, 16)
POOL_GC = POOL_WIDTH // len(POOL_WINDOWS)
POOL_STATE = max(POOL_WINDOWS) - 1
FFN_HIDDEN = -(-8 * D_MODEL // (3 * 256)) * 256
PLE_DIM = 256
EPS = 1e-6
NEG_INF = -1e30
PAST_LEN = 16384

Q_W = N_HEADS * HEAD_DIM
KV_W = N_KV_HEADS * HEAD_DIM
C_U, C_Q, C_K, C_V, C_GP, C_GA, C_END = 0, POOL_WIDTH, POOL_WIDTH + Q_W, POOL_WIDTH + Q_W + KV_W, \
    POOL_WIDTH + Q_W + 2 * KV_W, POOL_WIDTH + Q_W + 2 * KV_W + D_MODEL, POOL_WIDTH + Q_W + 2 * KV_W + 2 * D_MODEL

LANES = 128
S_AHEAD = 16
S_SLOTS = 16
GATE_COLS = 256
SUBLANES = 8
U_HALO = 24
VMEM_LIMIT = 56 * 1024 * 1024

BF16 = jnp.bfloat16
F32 = jnp.float32


def _dot(a, b):
    return jnp.dot(a, b, preferred_element_type=F32)


def _dot_nt(a, b):
    return lax.dot_general(a, b, (((1,), (1,)), ((), ())), preferred_element_type=F32)


def _sigmoid(x):
    return 0.5 * jnp.tanh(0.5 * x) + 0.5


def _rms(x, g):
    y = x * lax.rsqrt(jnp.mean(x * x, axis=-1, keepdims=True) + EPS)
    return y * g


def _rope(x, cos, sg, lo8):
    outs = []
    for c in range(x.shape[1] // LANES):
        xc = x[:, c * LANES:(c + 1) * LANES]
        partner = jnp.where(lo8, pltpu.roll(xc, LANES - ROT_DIMS // 2, 1), pltpu.roll(xc, ROT_DIMS // 2, 1))
        outs.append(xc * cos + partner * sg)
    return jnp.concatenate(outs, axis=1)


def _lane_masks():
    lane = lax.broadcasted_iota(jnp.int32, (1, LANES), 1)
    lo8 = (lane % HEAD_DIM) < (ROT_DIMS // 2)
    lo64 = lane < HEAD_DIM
    return lo8, lo64


def _pool_mix(u, win_sum_fn, cnt_fn, gw_ref, pscale_ref):
    mixed = []
    for g, w in enumerate(POOL_WINDOWS):
        cols = slice(g * POOL_GC, (g + 1) * POOL_GC)
        ug = u[:, cols]
        m = win_sum_fn(g, w, ug) / cnt_fn(w) - ug
        mixed.append(_dot(m.astype(BF16), gw_ref[g]) * pscale_ref[:, cols])
    return jnp.concatenate(mixed, axis=1)


def _mixer_kernel(sinks_ref, x_ref, cos_ref, sg_ref, ln1_ref, w_in_ref, gw_ref, pscale_ref, w_pb_ref, w_ab_ref,
                  w_out_ref, knt_ref, vnt_ref, ckt_ref, cvt_ref, *rest, tm, n_cast):
    cast_src = rest[:n_cast]
    h_ref, ko_ref, vo_ref, po_ref, nkt_ref, nvt_ref = rest[n_cast:n_cast + 6]
    cast_dst = rest[n_cast + 6:2 * n_cast + 6]
    kl_scr, kh_scr, vl_scr, vh_scr, u_scr, lvl_scr, attn_scr, q_scr, s_scr, g_scr = rest[2 * n_cast + 6:]
    t = pl.program_id(1)
    step = pl.program_id(0) * pl.num_programs(1) + t
    lo8, lo64 = _lane_masks()

    @pl.when(t == 0)
    def _():
        for scr in (kl_scr, kh_scr, vl_scr, vh_scr):
            scr[:, 0:BLOCK, :] = jnp.zeros((N_KV_HEADS, BLOCK, LANES), BF16)
        u_scr[0:U_HALO, :] = jnp.zeros((U_HALO, POOL_WIDTH), F32)
        lvl_scr[:, 0:SUBLANES, :] = jnp.zeros((len(POOL_WINDOWS), SUBLANES, POOL_GC), F32)

    x = x_ref[0]
    xn = _rms(x, ln1_ref[...]).astype(BF16)
    cos = cos_ref[...]
    sg = sg_ref[...]

    k = _rope(_dot(xn, w_in_ref[:, C_K:C_V]), cos, sg, lo8)
    v = _dot(xn, w_in_ref[:, C_V:C_GP])
    q = (_rope(_dot(xn, w_in_ref[:, C_Q:C_K]), cos, sg, lo8) * (HEAD_DIM ** -0.5)).astype(BF16)
    q_scr[...] = q
    u = _dot(xn, w_in_ref[:, C_U:C_Q])

    zero = jnp.zeros((tm, LANES), F32)
    for src, lo_scr, hi_scr in ((k, kl_scr, kh_scr), (v, vl_scr, vh_scr)):
        for p in range(KV_W // LANES):
            xp = src[:, p * LANES:(p + 1) * LANES]
            xs = pltpu.roll(xp, HEAD_DIM, 1)
            lo_scr[2 * p, BLOCK:, :] = jnp.where(lo64, xp, zero).astype(BF16)
            hi_scr[2 * p, BLOCK:, :] = jnp.where(lo64, zero, xs).astype(BF16)
            lo_scr[2 * p + 1, BLOCK:, :] = jnp.where(lo64, xs, zero).astype(BF16)
            hi_scr[2 * p + 1, BLOCK:, :] = jnp.where(lo64, zero, xp).astype(BF16)

    qi = lax.broadcasted_iota(jnp.int32, (BLOCK, BLOCK), 0)
    ci = lax.broadcasted_iota(jnp.int32, (BLOCK, BLOCK), 1)
    from_prev = ci > qi
    bias0 = jnp.where(jnp.logical_and(t == 0, from_prev), NEG_INF, 0.0).astype(F32)

    ones_lo = jnp.broadcast_to(jnp.where(lo64, 1.0, 0.0).astype(BF16), (2 * BLOCK, LANES))
    ones_hi = jnp.broadcast_to(jnp.where(lo64, 0.0, 1.0).astype(BF16), (2 * BLOCK, LANES))

    units = [(j, kh) for j in range(tm // BLOCK) for kh in range(N_KV_HEADS)]

    def scores(i):
        j, kh = units[i]
        rows = slice(j * BLOCK, (j + 1) * BLOCK)
        win = slice(j * BLOCK, (j + 2) * BLOCK)
        qq = jnp.concatenate([q_scr[rows, (2 * kh + a) * LANES:(2 * kh + a + 1) * LANES] for a in range(2)], axis=0)
        kcat = jnp.concatenate([kl_scr[kh, win, :], kh_scr[kh, win, :]], axis=0)
        s = _dot_nt(qq, kcat)
        for a in range(2):
            for half in range(2):
                sa = s[a * BLOCK:(a + 1) * BLOCK, half * 2 * BLOCK:(half + 1) * 2 * BLOCK]
                folded = jnp.where(from_prev, sa[:, :BLOCK], sa[:, BLOCK:])
                if j == 0:
                    folded = folded + bias0
                s_scr[i % S_SLOTS, a * BLOCK:(a + 1) * BLOCK, half * BLOCK:(half + 1) * BLOCK] = folded

    units_per_gate = len(units) * GATE_COLS // (2 * D_MODEL)
    for i in range(S_AHEAD):
        scores(i)

    def softmax(i):
        kh = units[i][1]
        ps, sink_terms = [], []
        for a in range(2):
            es, st = [], []
            for half in range(2):
                sh = s_scr[i % S_SLOTS, a * BLOCK:(a + 1) * BLOCK, half * BLOCK:(half + 1) * BLOCK]
                sink = sinks_ref[4 * kh + 2 * a + half]
                m = jnp.maximum(jnp.max(sh, axis=1, keepdims=True), sink)
                e = jnp.exp(sh - m)
                es.append(jnp.where(from_prev, e, 0.0).astype(BF16))
                es.append(jnp.where(from_prev, 0.0, e).astype(BF16))
                st.append(jnp.exp(sink - m))
            ps.append(jnp.concatenate(es, axis=1))
            sink_terms.append(jnp.where(lo64, st[0], st[1]))
        return jnp.concatenate(ps, axis=0), sink_terms

    def weighted_values(i, p, sink_terms):
        j, kh = units[i]
        rows = slice(j * BLOCK, (j + 1) * BLOCK)
        win = slice(j * BLOCK, (j + 2) * BLOCK)
        vcat = jnp.concatenate([
            jnp.concatenate([vl_scr[kh, win, :], ones_lo], axis=1),
            jnp.concatenate([vh_scr[kh, win, :], ones_hi], axis=1)], axis=0)
        o = _dot(p, vcat)
        for a in range(2):
            oa = o[a * BLOCK:(a + 1) * BLOCK]
            attn_scr[rows, (2 * kh + a) * LANES:(2 * kh + a + 1) * LANES] = (
                oa[:, :LANES] / (oa[:, LANES:] + sink_terms[a])).astype(BF16)

    pending = None
    for i in range(len(units)):
        if i + S_AHEAD < len(units):
            scores(i + S_AHEAD)
        current = (i,) + softmax(i)
        if pending is not None:
            weighted_values(*pending)
        pending = current
        if i % units_per_gate == units_per_gate - 1:
            gcols = slice((i // units_per_gate) * GATE_COLS, (i // units_per_gate + 1) * GATE_COLS)
            g_scr[:, gcols] = _sigmoid(_dot(xn, w_in_ref[:, C_GP + gcols.start:C_GP + gcols.stop]))
    weighted_values(*pending)

    u_scr[U_HALO:, :] = u
    po_ref[0] = u_scr[U_HALO + tm - POOL_STATE:U_HALO + tm, :]
    pos = t * tm + lax.broadcasted_iota(jnp.int32, (tm, 1), 0)

    def win_sum(g, w, ug):
        cols = slice(g * POOL_GC, (g + 1) * POOL_GC)
        n = U_HALO - SUBLANES + tm
        src, span = u_scr, 1
        while span < w:
            lvl = src[SUBLANES:SUBLANES + n, cols] + src[SUBLANES - span:SUBLANES - span + n, cols]
            span *= 2
            if span < w:
                lvl_scr[g, SUBLANES:SUBLANES + n, :] = lvl
                src, cols = lvl_scr.at[g], slice(None)
        return lvl[U_HALO - SUBLANES:, :]

    pooled = _pool_mix(u, win_sum, lambda w: jnp.minimum(w, pos + 1).astype(F32), gw_ref, pscale_ref)

    for scr in (kl_scr, kh_scr, vl_scr, vh_scr):
        scr[:, 0:BLOCK, :] = scr[:, tm:tm + BLOCK, :]
    u_scr[0:U_HALO, :] = u_scr[tm:tm + U_HALO, :]

    _shift_caches(step, ckt_ref.shape[0], knt_ref, vnt_ref, ckt_ref, cvt_ref, nkt_ref, nvt_ref)
    for src, dst in zip(cast_src, cast_dst):
        dst[...] = src[...].astype(BF16)
    ko_ref[0] = k[tm - WINDOW:, :].T
    vo_ref[0] = v[tm - WINDOW:, :].T

    merged = g_scr[:, 0:D_MODEL] * _dot(pooled.astype(BF16), w_pb_ref[...])
    merged = merged + g_scr[:, D_MODEL:2 * D_MODEL] * _dot(attn_scr[...], w_ab_ref[...])
    h_ref[0] = x + _dot(merged.astype(BF16), w_out_ref[...])


def _const_spec(shape):
    nd = len(shape)
    return pl.BlockSpec(shape, lambda *_: (0,) * nd, pipeline_mode=pl.Buffered(1))


def _cast_block_rows(rows, steps):
    br = 2 * SUBLANES
    while rows % br or rows // br > steps:
        br *= 2
    return br


def _cast_specs(to_cast, steps, step_of):
    specs = []
    for w in to_cast:
        br = _cast_block_rows(w.shape[0], steps)
        last = w.shape[0] // br - 1
        specs.append(pl.BlockSpec(
            (br, w.shape[1]), lambda *idx, last=last: (jnp.minimum(step_of(*idx), last), 0)))
    return specs


def _prompt_mixer(x, cos, sg, sinks, ln1, w_in, gw, pscale, w_pb, w_ab, w_out, knt, vnt, ckt, cvt, to_cast, tm):
    b, s, d = x.shape
    nt = s // tm
    cast_specs = _cast_specs(to_cast, b * nt, lambda bi, ti, *_: bi * nt + ti)
    nb = ckt.shape[0] // (b * nt)
    assert nb * b * nt == ckt.shape[0]
    cache_spec = pl.BlockSpec((nb,) + ckt.shape[1:], lambda bi, ti, *_: (bi * nt + ti, 0, 0))
    grid_spec = pltpu.PrefetchScalarGridSpec(
        num_scalar_prefetch=1,
        grid=(b, nt),
        in_specs=[
            pl.BlockSpec((1, tm, d), lambda bi, ti, *_: (bi, ti, 0)),
            pl.BlockSpec((tm, LANES), lambda bi, ti, *_: (ti, 0)),
            pl.BlockSpec((tm, LANES), lambda bi, ti, *_: (ti, 0)),
            _const_spec((1, d)),
            _const_spec(w_in.shape),
            _const_spec(gw.shape),
            _const_spec((1, POOL_WIDTH)),
            _const_spec(w_pb.shape),
            _const_spec(w_ab.shape),
            _const_spec(w_out.shape),
            _const_spec(knt.shape),
            _const_spec(vnt.shape),
            cache_spec,
            cache_spec,
        ] + cast_specs,
        out_specs=[
            pl.BlockSpec((1, tm, d), lambda bi, ti, *_: (bi, ti, 0)),
            pl.BlockSpec((1, KV_W, WINDOW), lambda bi, ti, *_: (bi, 0, 0)),
            pl.BlockSpec((1, KV_W, WINDOW), lambda bi, ti, *_: (bi, 0, 0)),
            pl.BlockSpec((1, POOL_STATE, POOL_WIDTH), lambda bi, ti, *_: (bi, 0, 0)),
            cache_spec,
            cache_spec,
        ] + cast_specs,
        scratch_shapes=[
            pltpu.VMEM((N_KV_HEADS, BLOCK + tm, LANES), BF16),
            pltpu.VMEM((N_KV_HEADS, BLOCK + tm, LANES), BF16),
            pltpu.VMEM((N_KV_HEADS, BLOCK + tm, LANES), BF16),
            pltpu.VMEM((N_KV_HEADS, BLOCK + tm, LANES), BF16),
            pltpu.VMEM((U_HALO + tm, POOL_WIDTH), F32),
            pltpu.VMEM((len(POOL_WINDOWS), U_HALO + tm, POOL_GC), F32),
            pltpu.VMEM((tm, Q_W), BF16),
            pltpu.VMEM((tm, Q_W), BF16),
            pltpu.VMEM((S_SLOTS, 2 * BLOCK, 2 * BLOCK), F32),
            pltpu.VMEM((tm, 2 * D_MODEL), F32),
        ],
    )
    return pl.pallas_call(
        functools.partial(_mixer_kernel, tm=tm, n_cast=len(to_cast)),
        grid_spec=grid_spec,
        out_shape=[
            jax.ShapeDtypeStruct((b, s, d), F32),
            jax.ShapeDtypeStruct((b, KV_W, WINDOW), F32),
            jax.ShapeDtypeStruct((b, KV_W, WINDOW), F32),
            jax.ShapeDtypeStruct((b, POOL_STATE, POOL_WIDTH), F32),
            jax.ShapeDtypeStruct(ckt.shape, F32),
            jax.ShapeDtypeStruct(cvt.shape, F32),
        ] + [jax.ShapeDtypeStruct(w.shape, BF16) for w in to_cast],
        compiler_params=pltpu.CompilerParams(
            dimension_semantics=("arbitrary", "arbitrary"), vmem_limit_bytes=VMEM_LIMIT),
        name="prompt_mixer",
    )(sinks, x, cos, sg, ln1, w_in, gw, pscale, w_pb, w_ab, w_out, knt, vnt, ckt, cvt, *to_cast)


FFN_CHUNKS = ((0, 1024), (1024, 2048), (2048, FFN_HIDDEN))


def _ffn_kernel(h_ref, p_ref, ln2_ref, w1_ref, w2_ref, w_pp_ref, pn_ref, w_pg_ref, fn_ref, y_ref, *, tm):
    def tile_stages(r0):
        rows = slice(r0, r0 + tm)
        st = {}

        def up_proj(c):
            lo, hi = FFN_CHUNKS[c]
            if c == 0:
                st['h'] = h_ref[rows, :]
                st['hn'] = _rms(st['h'], ln2_ref[...]).astype(BF16)
                st['acc'] = st['h']
            st['gate', c] = _dot(st['hn'], w1_ref[:, lo:hi])
            st['up', c] = _dot(st['hn'], w1_ref[:, FFN_HIDDEN + lo:FFN_HIDDEN + hi])
            if c == len(FFN_CHUNKS) - 1:
                st['e'] = _rms(_dot(p_ref[rows, :].astype(BF16), w_pp_ref[...]), pn_ref[...])

        def down_proj(c):
            lo, hi = FFN_CHUNKS[c]
            gate = st.pop(('gate', c))
            act = (gate * _sigmoid(gate) * st.pop(('up', c))).astype(BF16)
            st['acc'] = st['acc'] + _dot(act, w2_ref[lo:hi, :])

        def ple_gate():
            st['g'] = _dot(st['acc'].astype(BF16), w_pg_ref[...])

        def finish():
            h3 = st['acc'] + _sigmoid(st['g']) * st['e']
            y_ref[rows, :] = _rms(h3, fn_ref[...])

        n = len(FFN_CHUNKS)
        steps = [functools.partial(up_proj, 0)]
        for c in range(1, n):
            steps += [functools.partial(up_proj, c), functools.partial(down_proj, c - 1)]
        return steps + [functools.partial(down_proj, n - 1), ple_gate, finish]

    tail = 3
    order = []
    for r0 in range(0, h_ref.shape[0], tm):
        steps = tile_stages(r0)
        held, order = order[len(order) - tail:] if order else [], order[:len(order) - tail] if order else []
        for k in range(max(len(held), tail)):
            order += steps[k:k + 1] + held[k:k + 1]
        order += steps[tail:]
    for step in order:
        step()


def _ffn(h, p, ln2, w1, w2, w_pp, pn, w_pg, fn, tm, nsub):
    n, d = h.shape
    blk = tm * nsub
    return pl.pallas_call(
        functools.partial(_ffn_kernel, tm=tm),
        grid=(n // blk,),
        in_specs=[
            pl.BlockSpec((blk, d), lambda i: (i, 0)),
            pl.BlockSpec((blk, PLE_DIM), lambda i: (i, 0)),
            _const_spec((1, d)),
            _const_spec(w1.shape),
            _const_spec(w2.shape),
            _const_spec(w_pp.shape),
            _const_spec((1, d)),
            _const_spec(w_pg.shape),
            _const_spec((1, d)),
        ],
        out_specs=pl.BlockSpec((blk, d), lambda i: (i, 0)),
        out_shape=jax.ShapeDtypeStruct((n, d), F32),
        compiler_params=pltpu.CompilerParams(dimension_semantics=("arbitrary",), vmem_limit_bytes=VMEM_LIMIT),
        name="ffn_ple_norm",
    )(h, p, ln2, w1, w2, w_pp, pn, w_pg, fn)


def _sample_pre_kernel(x_ref, cos_ref, sg_ref, ln1_ref, w_in_ref, qe_ref, knt_ref, vnt_ref, kn_ref, vn_ref, u_ref):
    n = x_ref.shape[0]
    lo8, _ = _lane_masks()
    xn = _rms(x_ref[...], ln1_ref[...]).astype(BF16)
    cos = cos_ref[...]
    sg = sg_ref[...]
    w = lambda lo, hi: w_in_ref[:, lo:hi].astype(BF16)
    u_ref[...] = _dot(xn, w(C_U, C_Q))
    q = (_rope(_dot(xn, w(C_Q, C_K)), cos, sg, lo8) * (HEAD_DIM ** -0.5)).astype(BF16)
    kn = _rope(_dot(xn, w(C_K, C_V)), cos, sg, lo8)
    vn = _dot(xn, w(C_V, C_GP))
    kn_ref[...] = kn
    vn_ref[...] = vn
    knt_ref[...] = kn.T
    vnt_ref[...] = vn.T
    ii = lax.broadcasted_iota(jnp.int32, (Q_W, KV_W), 0)
    jj = lax.broadcasted_iota(jnp.int32, (Q_W, KV_W), 1)
    for r in range(N_HEADS):
        kh = r // GROUP
        sel = ((ii - r * HEAD_DIM) == (jj - kh * HEAD_DIM)) & (jj >= kh * HEAD_DIM) & (jj < (kh + 1) * HEAD_DIM)
        qr = _dot(q, jnp.where(sel, 1.0, 0.0).astype(BF16))
        for c in range(KV_W // LANES):
            qe_ref[c, pl.ds(r, n, stride=N_HEADS), :] = qr[:, c * LANES:(c + 1) * LANES]


def _sample_pre(x, cos, sg, ln1, w_in):
    n, d = x.shape
    return pl.pallas_call(
        _sample_pre_kernel,
        grid=(1,),
        in_specs=[
            _const_spec((n, d)),
            _const_spec((1, LANES)),
            _const_spec((1, LANES)),
            _const_spec((1, d)),
            pl.BlockSpec((d, C_GP), lambda i: (0, 0), pipeline_mode=pl.Buffered(1)),
        ],
        out_specs=[
            pl.BlockSpec((KV_W // LANES, n * N_HEADS, LANES), lambda i: (0, 0, 0)),
            pl.BlockSpec((KV_W, n), lambda i: (0, 0)),
            pl.BlockSpec((KV_W, n), lambda i: (0, 0)),
            pl.BlockSpec((n, KV_W), lambda i: (0, 0)),
            pl.BlockSpec((n, KV_W), lambda i: (0, 0)),
            pl.BlockSpec((n, POOL_WIDTH), lambda i: (0, 0)),
        ],
        out_shape=[
            jax.ShapeDtypeStruct((KV_W // LANES, n * N_HEADS, LANES), F32),
            jax.ShapeDtypeStruct((KV_W, n), F32),
            jax.ShapeDtypeStruct((KV_W, n), F32),
            jax.ShapeDtypeStruct((n, KV_W), F32),
            jax.ShapeDtypeStruct((n, KV_W), F32),
            jax.ShapeDtypeStruct((n, POOL_WIDTH), F32),
        ],
        compiler_params=pltpu.CompilerParams(dimension_semantics=("arbitrary",), vmem_limit_bytes=VMEM_LIMIT),
        name="sample_pre",
    )(x, cos, sg, ln1, w_in)


def _shift_caches(step, nb, knt_ref, vnt_ref, ckt_ref, cvt_ref, nkt_ref, nvt_ref):
    n = knt_ref.shape[1]
    w_cache = ckt_ref.shape[2]
    newest = lax.broadcasted_iota(jnp.int32, (1, w_cache), 1) == w_cache - 1
    shift = lax.rem(n - lax.rem(step * nb, n), n)
    kcols = pltpu.roll(knt_ref[...], shift, 1)
    vcols = pltpu.roll(vnt_ref[...], shift, 1)
    for bl in range(nb):
        nkt_ref[bl] = jnp.where(newest, kcols[:, bl:bl + 1], pltpu.roll(ckt_ref[bl], w_cache - 1, 1))
        nvt_ref[bl] = jnp.where(newest, vcols[:, bl:bl + 1], pltpu.roll(cvt_ref[bl], w_cache - 1, 1))


def _sample_attn_kernel(qe_ref, kn_ref, vn_ref, ckt_ref, cvt_ref, sink_ref, *rest, bb, n_cast):
    cast_src = rest[:n_cast]
    o_ref = rest[n_cast]
    cast_dst = rest[n_cast + 1:]
    for src, dst in zip(cast_src, cast_dst):
        dst[...] = src[...].astype(BF16)
    w_cache = ckt_ref.shape[2]
    oldest = lax.broadcasted_iota(jnp.int32, (1, w_cache), 1) == 0
    sink = sink_ref[...]
    rounded = lambda a: a.astype(BF16).astype(F32)
    scores = []
    for bl in range(bb):
        rows = slice(bl * N_HEADS, (bl + 1) * N_HEADS)
        qb = jnp.concatenate([qe_ref[c, rows, :] for c in range(KV_W // LANES)], axis=1).astype(BF16)
        s_old = jnp.where(oldest, NEG_INF, _dot(qb, ckt_ref[bl].astype(BF16)))
        s_new = jnp.sum(qb.astype(F32) * rounded(kn_ref[bl:bl + 1, :]), axis=1, keepdims=True)
        scores.append((s_old, s_new))
    probs = []
    for s_old, s_new in scores:
        m = jnp.maximum(jnp.maximum(jnp.max(s_old, axis=-1, keepdims=True), s_new), sink)
        e_old = jnp.exp(s_old - m)
        e_new = jnp.exp(s_new - m)
        denom = jnp.sum(e_old, axis=-1, keepdims=True) + e_new + jnp.exp(sink - m)
        probs.append((e_old.astype(BF16), rounded(e_new), denom))
    for bl, (e_old, e_new, denom) in enumerate(probs):
        rows = slice(bl * N_HEADS, (bl + 1) * N_HEADS)
        o = (_dot_nt(e_old, cvt_ref[bl].astype(BF16)) + e_new * rounded(vn_ref[bl:bl + 1, :])) / denom
        for c in range(KV_W // LANES):
            o_ref[c, rows, :] = o[:, c * LANES:(c + 1) * LANES]


def _sample_attn(qe, kn, vn, ckt, cvt, sink, to_cast, bb):
    n, _, w_cache = ckt.shape
    cast_specs = _cast_specs(to_cast, n // bb, lambda i: i)
    cache_spec = pl.BlockSpec((bb, KV_W, w_cache), lambda i: (i, 0, 0))
    new_spec = pl.BlockSpec((bb, KV_W), lambda i: (i, 0))
    head_spec = pl.BlockSpec((KV_W // LANES, bb * N_HEADS, LANES), lambda i: (0, i, 0))
    return pl.pallas_call(
        functools.partial(_sample_attn_kernel, bb=bb, n_cast=len(to_cast)),
        grid=(n // bb,),
        in_specs=[head_spec, new_spec, new_spec, cache_spec, cache_spec, _const_spec((N_HEADS, 1))] + cast_specs,
        out_specs=[head_spec] + cast_specs,
        out_shape=[jax.ShapeDtypeStruct((KV_W // LANES, n * N_HEADS, LANES), F32)]
        + [jax.ShapeDtypeStruct(w.shape, BF16) for w in to_cast],
        compiler_params=pltpu.CompilerParams(dimension_semantics=("arbitrary",), vmem_limit_bytes=VMEM_LIMIT),
        name="sample_attn",
    )(qe, kn, vn, ckt, cvt, sink, *to_cast)


def _sample_post_kernel(x_ref, u_ref, st_ref, o_ref, ln1_ref, w_g_ref, gw_ref, pscale_ref, w_pb_ref, w_ab_ref,
                        w_out_ref, h_ref, nst_ref):
    x = x_ref[...]
    n = x.shape[0]
    xn = _rms(x, ln1_ref[...]).astype(BF16)
    u = u_ref[...]
    nst_ref[0:POOL_STATE - 1] = st_ref[1:POOL_STATE]
    nst_ref[POOL_STATE - 1] = u

    def win_sum(g, w, ug):
        acc = ug
        for i in range(1, w):
            acc = acc + st_ref[POOL_STATE - i, :, g * POOL_GC:(g + 1) * POOL_GC]
        return acc

    pooled = _pool_mix(u, win_sum, lambda w: jnp.float32(min(w, PAST_LEN + 1)), gw_ref, pscale_ref)
    merged = _sigmoid(_dot(xn, w_g_ref[:, 0:D_MODEL])) * _dot(pooled.astype(BF16), w_pb_ref[...])
    kv_of_lane = lax.broadcasted_iota(jnp.int32, (1, KV_W), 1) // HEAD_DIM
    ab = jnp.zeros((n, D_MODEL), F32)
    for g in range(GROUP):
        row_g = jnp.zeros((n, KV_W), F32)
        for kh in range(N_KV_HEADS):
            r = kh * GROUP + g
            o_r = jnp.concatenate([o_ref[c, pl.ds(r, n, stride=N_HEADS), :] for c in range(KV_W // LANES)], axis=1)
            row_g = jnp.where(kv_of_lane == kh, o_r, row_g)
        w_g = jnp.concatenate([w_ab_ref[(kh * GROUP + g) * HEAD_DIM:(kh * GROUP + g + 1) * HEAD_DIM, :]
                               for kh in range(N_KV_HEADS)], axis=0)
        ab = ab + _dot(row_g.astype(BF16), w_g)
    merged = merged + _sigmoid(_dot(xn, w_g_ref[:, D_MODEL:2 * D_MODEL])) * ab
    h_ref[...] = x + _dot(merged.astype(BF16), w_out_ref[...])


def _sample_post(x, u, st, o, ln1, w_in, gw, pscale, w_pb, w_ab, w_out):
    n, d = x.shape
    return pl.pallas_call(
        _sample_post_kernel,
        grid=(1,),
        in_specs=[
            _const_spec((n, d)),
            _const_spec((n, POOL_WIDTH)),
            _const_spec(st.shape),
            _const_spec((KV_W // LANES, n * N_HEADS, LANES)),
            _const_spec((1, d)),
            pl.BlockSpec((d, 2 * D_MODEL), lambda i: (0, 1), pipeline_mode=pl.Buffered(1)),
            _const_spec(gw.shape),
            _const_spec((1, POOL_WIDTH)),
            _const_spec(w_pb.shape),
            _const_spec(w_ab.shape),
            _const_spec(w_out.shape),
        ],
        out_specs=[pl.BlockSpec((n, d), lambda i: (0, 0)), pl.BlockSpec(st.shape, lambda i: (0, 0, 0))],
        out_shape=[jax.ShapeDtypeStruct((n, d), F32), jax.ShapeDtypeStruct(st.shape, F32)],
        compiler_params=pltpu.CompilerParams(dimension_semantics=("arbitrary",), vmem_limit_bytes=VMEM_LIMIT),
        name="sample_post",
    )(x, u, st, o, ln1, w_in, gw, pscale, w_pb, w_ab, w_out)


def _rope_tables(first_pos, n):
    half = ROT_DIMS // 2
    inv = ROPE_THETA ** (-(np.arange(0, ROT_DIMS, 2, dtype=np.float64) / ROT_DIMS))
    ang = np.arange(first_pos, first_pos + n, dtype=np.float64)[:, None] * inv[None, :]
    cos, sin = np.cos(ang), np.sin(ang)
    rest = HEAD_DIM - 2 * half
    c64 = np.concatenate([cos, cos, np.ones((n, rest))], axis=1)
    s64 = np.concatenate([-sin, sin, np.zeros((n, rest))], axis=1)
    reps = LANES // HEAD_DIM
    return jnp.asarray(np.tile(c64, (1, reps)), F32), jnp.asarray(np.tile(s64, (1, reps)), F32)


def kernel(x_prompt, x_sample, p_prompt, p_sample, cache_k, cache_v, state_pool, ln1, w_in, pool_group_w, pool_scale,
           attn_sinks, w_pool_branch, w_attn_branch, w_out, ln2, w_ffn_in, w_ffn_out, w_ple_proj, ple_norm,
           w_ple_gate, final_norm):
    depth = ln1.shape[0]
    b, s, d = x_prompt.shape
    bd, t_dec, _ = x_sample.shape
    w_cache = cache_k.shape[2]
    assert depth == 1 and t_dec == 1 and w_cache == WINDOW and s % BLOCK == 0 and d == D_MODEL
    tm = 512
    assert s % tm == 0

    cos_p, sg_p = _rope_tables(0, s)
    cos_s, sg_s = _rope_tables(PAST_LEN, t_dec)

    hp = x_prompt
    hs = x_sample.reshape(bd, d)
    row = lambda a: a.reshape(1, -1)
    nkp, nvp, npp, nks, nvs, nps = [], [], [], [], [], []
    for i in range(depth):
        to_fm = lambda c: jnp.transpose(c, (0, 2, 3, 1)).reshape(bd, KV_W, w_cache)
        from_fm = lambda c: jnp.transpose(c.reshape(bd, N_KV_HEADS, HEAD_DIM, w_cache), (0, 3, 1, 2))
        ckt, cvt = to_fm(cache_k[i]), to_fm(cache_v[i])
        qe, knt, vnt, kn, vn, un = _sample_pre(hs, cos_s, sg_s, row(ln1[i]), w_in[i])
        o, wi, gw, wpb, wab, wo = _sample_attn(
            qe, kn, vn, ckt, cvt, attn_sinks[i].reshape(N_HEADS, 1),
            (w_in[i], pool_group_w[i].reshape(POOL_WIDTH, POOL_GC), w_pool_branch[i], w_attn_branch[i], w_out[i]), 16)
        gw = gw.reshape(len(POOL_WINDOWS), POOL_GC, POOL_GC)
        h1, kp, vp, pp, nkt, nvt, w1, w2, wpp, wpg = _prompt_mixer(
            hp, cos_p, sg_p, attn_sinks[i], row(ln1[i]), wi, gw, row(pool_scale[i]), wpb, wab, wo,
            knt, vnt, ckt, cvt, (w_ffn_in[i], w_ffn_out[i], w_ple_proj[i], w_ple_gate[i]), tm)
        ffn_args = (row(ln2[i]), w1, w2, wpp, row(ple_norm[i]), wpg)
        hp = _ffn(h1.reshape(b * s, d), p_prompt[i].reshape(b * s, PLE_DIM), *ffn_args, row(final_norm),
                  tm, 2).reshape(b, s, d)
        from_fm_p = lambda c: jnp.transpose(c.reshape(b, N_KV_HEADS, HEAD_DIM, w_cache), (0, 3, 1, 2))
        nkp.append(from_fm_p(kp))
        nvp.append(from_fm_p(vp))
        npp.append(pp)

        h1s, nst = _sample_post(hs, un, jnp.transpose(state_pool[i], (1, 0, 2)), o, row(ln1[i]), wi, gw,
                                row(pool_scale[i]), wpb, wab, wo)
        hs = _ffn(h1s, p_sample[i].reshape(bd * t_dec, PLE_DIM), *ffn_args, row(final_norm), bd, 1)
        nks.append(from_fm(nkt))
        nvs.append(from_fm(nvt))
        nps.append(jnp.transpose(nst, (1, 0, 2)))

    return (hp, hs.reshape(bd, t_dec, d), jnp.stack(nkp), jnp.stack(nvp), jnp.stack(npp),
            jnp.stack(nks), jnp.stack(nvs), jnp.stack(nps))
```

```python
import functools

import jax
import jax.numpy as jnp
import numpy as np
from jax import lax
from jax.experimental import pallas as pl
from jax.experimental.pallas import tpu as pltpu

D_MODEL = 1024
HEAD_DIM = 64
N_HEADS = D_MODEL // HEAD_DIM
N_KV_HEADS = N_HEADS // 4
GROUP = N_HEADS // N_KV_HEADS
ROT_DIMS = HEAD_DIM // 4
ROPE_THETA = 500000.0
WINDOW = 128
BLOCK = 128
POOL_WIDTH = D_MODEL // 2
POOL_WINDOWS = (2, 4, 8, 16)
POOL_GC = POOL_WIDTH // len(POOL_WINDOWS)
POOL_STATE = max(POOL_WINDOWS) - 1
FFN_HIDDEN = -(-8 * D_MODEL // (3 * 256)) * 256
PLE_DIM = 256
EPS = 1e-6
NEG_INF = -1e30
PAST_LEN = 16384

Q_W = N_HEADS * HEAD_DIM
KV_W = N_KV_HEADS * HEAD_DIM
C_U, C_Q, C_K, C_V, C_GP, C_GA, C_END = 0, POOL_WIDTH, POOL_WIDTH + Q_W, POOL_WIDTH + Q_W + KV_W, \
    POOL_WIDTH + Q_W + 2 * KV_W, POOL_WIDTH + Q_W + 2 * KV_W + D_MODEL, POOL_WIDTH + Q_W + 2 * KV_W + 2 * D_MODEL

LANES = 128
S_AHEAD = 16
S_SLOTS = 16
GATE_COLS = 256
SUBLANES = 8
U_HALO = 24
VMEM_LIMIT = 56 * 1024 * 1024

BF16 = jnp.bfloat16
F32 = jnp.float32


def _dot(a, b):
    return jnp.dot(a, b, preferred_element_type=F32)


def _dot_nt(a, b):
    return lax.dot_general(a, b, (((1,), (1,)), ((), ())), preferred_element_type=F32)


def _sigmoid(x):
    return 0.5 * jnp.tanh(0.5 * x) + 0.5


def _rms(x, g):
    y = x * lax.rsqrt(jnp.mean(x * x, axis=-1, keepdims=True) + EPS)
    return y * g


def _rope(x, cos, sg, lo8):
    outs = []
    for c in range(x.shape[1] // LANES):
        xc = x[:, c * LANES:(c + 1) * LANES]
        partner = jnp.where(lo8, pltpu.roll(xc, LANES - ROT_DIMS // 2, 1), pltpu.roll(xc, ROT_DIMS // 2, 1))
        outs.append(xc * cos + partner * sg)
    return jnp.concatenate(outs, axis=1)


def _lane_masks():
    lane = lax.broadcasted_iota(jnp.int32, (1, LANES), 1)
    lo8 = (lane % HEAD_DIM) < (ROT_DIMS // 2)
    lo64 = lane < HEAD_DIM
    return lo8, lo64


def _pool_mix(u, win_sum_fn, cnt_fn, gw_ref, pscale_ref):
    mixed = []
    for g, w in enumerate(POOL_WINDOWS):
        cols = slice(g * POOL_GC, (g + 1) * POOL_GC)
        ug = u[:, cols]
        m = win_sum_fn(g, w, ug) / cnt_fn(w) - ug
        mixed.append(_dot(m.astype(BF16), gw_ref[g]) * pscale_ref[:, cols])
    return jnp.concatenate(mixed, axis=1)


def _mixer_kernel(sinks_ref, x_ref, cos_ref, sg_ref, ln1_ref, w_in_ref, w_gate_ref, gw_ref, pscale_ref, w_pb_ref,
                  w_ab_ref, w_out_ref, knt_ref, vnt_ref, ckt_ref, cvt_ref, *rest, tm, n_cast):
    cast_src = rest[:n_cast]
    h_ref, ko_ref, vo_ref, po_ref, nkt_ref, nvt_ref = rest[n_cast:n_cast + 6]
    cast_dst = rest[n_cast + 6:2 * n_cast + 6]
    kl_scr, kh_scr, vl_scr, vh_scr, u_scr, lvl_scr, attn_scr, q_scr, s_scr, g_scr = rest[2 * n_cast + 6:]
    t = pl.program_id(1)
    step = pl.program_id(0) * pl.num_programs(1) + t
    lo8, lo64 = _lane_masks()

    @pl.when(t == 0)
    def _():
        for scr in (kl_scr, kh_scr, vl_scr, vh_scr):
            scr[:, 0:BLOCK, :] = jnp.zeros((N_KV_HEADS, BLOCK, LANES), BF16)
        u_scr[0:U_HALO, :] = jnp.zeros((U_HALO, POOL_WIDTH), F32)
        lvl_scr[:, 0:SUBLANES, :] = jnp.zeros((len(POOL_WINDOWS), SUBLANES, POOL_GC), F32)

    x = x_ref[0]
    xn = _rms(x, ln1_ref[...]).astype(BF16)
    cos = cos_ref[...]
    sg = sg_ref[...]

    k = _rope(_dot(xn, w_in_ref[:, C_K:C_V]), cos, sg, lo8)
    v = _dot(xn, w_in_ref[:, C_V:C_GP])
    q = (_rope(_dot(xn, w_in_ref[:, C_Q:C_K]), cos, sg, lo8) * (HEAD_DIM ** -0.5)).astype(BF16)
    q_scr[...] = q
    u = _dot(xn, w_in_ref[:, C_U:C_Q])

    zero = jnp.zeros((tm, LANES), F32)
    for src, lo_scr, hi_scr in ((k, kl_scr, kh_scr), (v, vl_scr, vh_scr)):
        for p in range(KV_W // LANES):
            xp = src[:, p * LANES:(p + 1) * LANES]
            xs = pltpu.roll(xp, HEAD_DIM, 1)
            lo_scr[2 * p, BLOCK:, :] = jnp.where(lo64, xp, zero).astype(BF16)
            hi_scr[2 * p, BLOCK:, :] = jnp.where(lo64, zero, xs).astype(BF16)
            lo_scr[2 * p + 1, BLOCK:, :] = jnp.where(lo64, xs, zero).astype(BF16)
            hi_scr[2 * p + 1, BLOCK:, :] = jnp.where(lo64, zero, xp).astype(BF16)

    qi = lax.broadcasted_iota(jnp.int32, (BLOCK, BLOCK), 0)
    ci = lax.broadcasted_iota(jnp.int32, (BLOCK, BLOCK), 1)
    from_prev = ci > qi
    bias0 = jnp.where(jnp.logical_and(t == 0, from_prev), NEG_INF, 0.0).astype(F32)

    ones_lo = jnp.broadcast_to(jnp.where(lo64, 1.0, 0.0).astype(BF16), (2 * BLOCK, LANES))
    ones_hi = jnp.broadcast_to(jnp.where(lo64, 0.0, 1.0).astype(BF16), (2 * BLOCK, LANES))

    units = [(j, kh) for j in range(tm // BLOCK) for kh in range(N_KV_HEADS)]

    def scores(i):
        j, kh = units[i]
        rows = slice(j * BLOCK, (j + 1) * BLOCK)
        win = slice(j * BLOCK, (j + 2) * BLOCK)
        qq = jnp.concatenate([q_scr[rows, (2 * kh + a) * LANES:(2 * kh + a + 1) * LANES] for a in range(2)], axis=0)
        kcat = jnp.concatenate([kl_scr[kh, win, :], kh_scr[kh, win, :]], axis=0)
        s = _dot_nt(qq, kcat)
        for a in range(2):
            for half in range(2):
                sa = s[a * BLOCK:(a + 1) * BLOCK, half * 2 * BLOCK:(half + 1) * 2 * BLOCK]
                folded = jnp.where(from_prev, sa[:, :BLOCK], sa[:, BLOCK:])
                if j == 0:
                    folded = folded + bias0
                s_scr[i % S_SLOTS, a * BLOCK:(a + 1) * BLOCK, half * BLOCK:(half + 1) * BLOCK] = folded

    units_per_gate = len(units) * GATE_COLS // (2 * D_MODEL)
    for i in range(S_AHEAD):
        scores(i)

    def softmax(i):
        kh = units[i][1]
        ps, sink_terms = [], []
        for a in range(2):
            es, st = [], []
            for half in range(2):
                sh = s_scr[i % S_SLOTS, a * BLOCK:(a + 1) * BLOCK, half * BLOCK:(half + 1) * BLOCK]
                sink = sinks_ref[4 * kh + 2 * a + half]
                m = jnp.maximum(jnp.max(sh, axis=1, keepdims=True), sink)
                e = jnp.exp(sh - m)
                es.append(jnp.where(from_prev, e, 0.0).astype(BF16))
                es.append(jnp.where(from_prev, 0.0, e).astype(BF16))
                st.append(jnp.exp(sink - m))
            ps.append(jnp.concatenate(es, axis=1))
            sink_terms.append(jnp.where(lo64, st[0], st[1]))
        return jnp.concatenate(ps, axis=0), sink_terms

    def weighted_values(i, p, sink_terms):
        j, kh = units[i]
        rows = slice(j * BLOCK, (j + 1) * BLOCK)
        win = slice(j * BLOCK, (j + 2) * BLOCK)
        vcat = jnp.concatenate([
            jnp.concatenate([vl_scr[kh, win, :], ones_lo], axis=1),
            jnp.concatenate([vh_scr[kh, win, :], ones_hi], axis=1)], axis=0)
        o = _dot(p, vcat)
        for a in range(2):
            oa = o[a * BLOCK:(a + 1) * BLOCK]
            attn_scr[rows, (2 * kh + a) * LANES:(2 * kh + a + 1) * LANES] = (
                oa[:, :LANES] / (oa[:, LANES:] + sink_terms[a])).astype(BF16)

    pending = None
    for i in range(len(units)):
        if i + S_AHEAD < len(units):
            scores(i + S_AHEAD)
        current = (i,) + softmax(i)
        if pending is not None:
            weighted_values(*pending)
        pending = current
        if i % units_per_gate == units_per_gate - 1:
            gcols = slice((i // units_per_gate) * GATE_COLS, (i // units_per_gate + 1) * GATE_COLS)
            g_scr[:, gcols] = _sigmoid(_dot(xn, w_gate_ref[:, gcols]))
    weighted_values(*pending)

    u_scr[U_HALO:, :] = u
    po_ref[0] = u_scr[U_HALO + tm - POOL_STATE:U_HALO + tm, :]
    pos = t * tm + lax.broadcasted_iota(jnp.int32, (tm, 1), 0)

    def win_sum(g, w, ug):
        cols = slice(g * POOL_GC, (g + 1) * POOL_GC)
        n = U_HALO - SUBLANES + tm
        src, span = u_scr, 1
        while span < w:
            lvl = src[SUBLANES:SUBLANES + n, cols] + src[SUBLANES - span:SUBLANES - span + n, cols]
            span *= 2
            if span < w:
                lvl_scr[g, SUBLANES:SUBLANES + n, :] = lvl
                src, cols = lvl_scr.at[g], slice(None)
        return lvl[U_HALO - SUBLANES:, :]

    pooled = _pool_mix(u, win_sum, lambda w: jnp.minimum(w, pos + 1).astype(F32), gw_ref, pscale_ref)

    for scr in (kl_scr, kh_scr, vl_scr, vh_scr):
        scr[:, 0:BLOCK, :] = scr[:, tm:tm + BLOCK, :]
    u_scr[0:U_HALO, :] = u_scr[tm:tm + U_HALO, :]

    _shift_caches(step, ckt_ref.shape[0], knt_ref, vnt_ref, ckt_ref, cvt_ref, nkt_ref, nvt_ref)
    for src, dst in zip(cast_src, cast_dst):
        dst[...] = src[...].astype(BF16)
    ko_ref[0] = k[tm - WINDOW:, :].T
    vo_ref[0] = v[tm - WINDOW:, :].T

    merged = g_scr[:, 0:D_MODEL] * _dot(pooled.astype(BF16), w_pb_ref[...])
    merged = merged + g_scr[:, D_MODEL:2 * D_MODEL] * _dot(attn_scr[...], w_ab_ref[...])
    h_ref[0] = x + _dot(merged.astype(BF16), w_out_ref[...])


def _const_spec(shape):
    nd = len(shape)
    return pl.BlockSpec(shape, lambda *_: (0,) * nd, pipeline_mode=pl.Buffered(1))


def _cast_block_rows(rows, steps):
    br = 2 * SUBLANES
    while rows % br or rows // br > steps:
        br *= 2
    return br


def _cast_specs(to_cast, steps, step_of):
    in_specs, out_specs, out_shapes = [], [], []
    for entry in to_cast:
        w, ncols, j = entry if isinstance(entry, tuple) else (entry, entry.shape[1], 0)
        br = _cast_block_rows(w.shape[0], steps)
        last = w.shape[0] // br - 1
        in_specs.append(pl.BlockSpec((br, ncols), lambda *idx, last=last, j=j: (jnp.minimum(step_of(*idx), last), j)))
        out_specs.append(pl.BlockSpec((br, ncols), lambda *idx, last=last: (jnp.minimum(step_of(*idx), last), 0)))
        out_shapes.append(jax.ShapeDtypeStruct((w.shape[0], ncols), BF16))
    return in_specs, out_specs, out_shapes


def _cast_arrays(to_cast):
    return [entry[0] if isinstance(entry, tuple) else entry for entry in to_cast]


def _prompt_mixer(x, cos, sg, sinks, ln1, w_lo, w_hi, gw, pscale, w_pb, w_ab, w_out, knt, vnt, ckt, cvt, to_cast, tm):
    b, s, d = x.shape
    nt = s // tm
    cast_in, cast_out, cast_shapes = _cast_specs(to_cast, b * nt, lambda bi, ti, *_: bi * nt + ti)
    nb = ckt.shape[0] // (b * nt)
    assert nb * b * nt == ckt.shape[0]
    cache_spec = pl.BlockSpec((nb,) + ckt.shape[1:], lambda bi, ti, *_: (bi * nt + ti, 0, 0))
    grid_spec = pltpu.PrefetchScalarGridSpec(
        num_scalar_prefetch=1,
        grid=(b, nt),
        in_specs=[
            pl.BlockSpec((1, tm, d), lambda bi, ti, *_: (bi, ti, 0)),
            pl.BlockSpec((tm, LANES), lambda bi, ti, *_: (ti, 0)),
            pl.BlockSpec((tm, LANES), lambda bi, ti, *_: (ti, 0)),
            _const_spec((1, d)),
            _const_spec(w_lo.shape),
            _const_spec(w_hi.shape),
            _const_spec(gw.shape),
            _const_spec((1, POOL_WIDTH)),
            _const_spec(w_pb.shape),
            _const_spec(w_ab.shape),
            _const_spec(w_out.shape),
            _const_spec(knt.shape),
            _const_spec(vnt.shape),
            cache_spec,
            cache_spec,
        ] + cast_in,
        out_specs=[
            pl.BlockSpec((1, tm, d), lambda bi, ti, *_: (bi, ti, 0)),
            pl.BlockSpec((1, KV_W, WINDOW), lambda bi, ti, *_: (bi, 0, 0)),
            pl.BlockSpec((1, KV_W, WINDOW), lambda bi, ti, *_: (bi, 0, 0)),
            pl.BlockSpec((1, POOL_STATE, POOL_WIDTH), lambda bi, ti, *_: (bi, 0, 0)),
            cache_spec,
            cache_spec,
        ] + cast_out,
        scratch_shapes=[
            pltpu.VMEM((N_KV_HEADS, BLOCK + tm, LANES), BF16),
            pltpu.VMEM((N_KV_HEADS, BLOCK + tm, LANES), BF16),
            pltpu.VMEM((N_KV_HEADS, BLOCK + tm, LANES), BF16),
            pltpu.VMEM((N_KV_HEADS, BLOCK + tm, LANES), BF16),
            pltpu.VMEM((U_HALO + tm, POOL_WIDTH), F32),
            pltpu.VMEM((len(POOL_WINDOWS), U_HALO + tm, POOL_GC), F32),
            pltpu.VMEM((tm, Q_W), BF16),
            pltpu.VMEM((tm, Q_W), BF16),
            pltpu.VMEM((S_SLOTS, 2 * BLOCK, 2 * BLOCK), F32),
            pltpu.VMEM((tm, 2 * D_MODEL), F32),
        ],
    )
    return pl.pallas_call(
        functools.partial(_mixer_kernel, tm=tm, n_cast=len(to_cast)),
        grid_spec=grid_spec,
        out_shape=[
            jax.ShapeDtypeStruct((b, s, d), F32),
            jax.ShapeDtypeStruct((b, KV_W, WINDOW), F32),
            jax.ShapeDtypeStruct((b, KV_W, WINDOW), F32),
            jax.ShapeDtypeStruct((b, POOL_STATE, POOL_WIDTH), F32),
            jax.ShapeDtypeStruct(ckt.shape, F32),
            jax.ShapeDtypeStruct(cvt.shape, F32),
        ] + cast_shapes,
        compiler_params=pltpu.CompilerParams(
            dimension_semantics=("arbitrary", "arbitrary"), vmem_limit_bytes=VMEM_LIMIT),
        name="prompt_mixer",
    )(sinks, x, cos, sg, ln1, w_lo, w_hi, gw, pscale, w_pb, w_ab, w_out, knt, vnt, ckt, cvt, *_cast_arrays(to_cast))


FFN_CHUNKS = ((0, 1024), (1024, 2048), (2048, FFN_HIDDEN))


def _ffn_kernel(h_ref, p_ref, ln2_ref, w1_ref, w2_ref, w_pp_ref, pn_ref, w_pg_ref, fn_ref, y_ref, *, tm):
    def tile_stages(r0):
        rows = slice(r0, r0 + tm)
        st = {}

        def up_proj(c):
            lo, hi = FFN_CHUNKS[c]
            if c == 0:
                st['h'] = h_ref[rows, :]
                st['hn'] = _rms(st['h'], ln2_ref[...]).astype(BF16)
                st['acc'] = st['h']
            st['gate', c] = _dot(st['hn'], w1_ref[:, lo:hi])
            st['up', c] = _dot(st['hn'], w1_ref[:, FFN_HIDDEN + lo:FFN_HIDDEN + hi])
            if c == len(FFN_CHUNKS) - 1:
                st['e'] = _rms(_dot(p_ref[rows, :].astype(BF16), w_pp_ref[...]), pn_ref[...])

        def down_proj(c):
            lo, hi = FFN_CHUNKS[c]
            gate = st.pop(('gate', c))
            act = (gate * _sigmoid(gate) * st.pop(('up', c))).astype(BF16)
            st['acc'] = st['acc'] + _dot(act, w2_ref[lo:hi, :])

        def ple_gate():
            st['g'] = _dot(st['acc'].astype(BF16), w_pg_ref[...])

        def finish():
            h3 = st['acc'] + _sigmoid(st['g']) * st['e']
            y_ref[rows, :] = _rms(h3, fn_ref[...])

        n = len(FFN_CHUNKS)
        steps = [functools.partial(up_proj, 0)]
        for c in range(1, n):
            steps += [functools.partial(up_proj, c), functools.partial(down_proj, c - 1)]
        return steps + [functools.partial(down_proj, n - 1), ple_gate, finish]

    tail = 3
    order = []
    for r0 in range(0, h_ref.shape[0], tm):
        steps = tile_stages(r0)
        held, order = order[len(order) - tail:] if order else [], order[:len(order) - tail] if order else []
        for k in range(max(len(held), tail)):
            order += steps[k:k + 1] + held[k:k + 1]
        order += steps[tail:]
    for step in order:
        step()


def _ffn(h, p, ln2, w1, w2, w_pp, pn, w_pg, fn, tm, nsub):
    n, d = h.shape
    blk = tm * nsub
    return pl.pallas_call(
        functools.partial(_ffn_kernel, tm=tm),
        grid=(n // blk,),
        in_specs=[
            pl.BlockSpec((blk, d), lambda i: (i, 0)),
            pl.BlockSpec((blk, PLE_DIM), lambda i: (i, 0)),
            _const_spec((1, d)),
            _const_spec(w1.shape),
            _const_spec(w2.shape),
            _const_spec(w_pp.shape),
            _const_spec((1, d)),
            _const_spec(w_pg.shape),
            _const_spec((1, d)),
        ],
        out_specs=pl.BlockSpec((blk, d), lambda i: (i, 0)),
        out_shape=jax.ShapeDtypeStruct((n, d), F32),
        compiler_params=pltpu.CompilerParams(dimension_semantics=("arbitrary",), vmem_limit_bytes=VMEM_LIMIT),
        name="ffn_ple_norm",
    )(h, p, ln2, w1, w2, w_pp, pn, w_pg, fn)


def _sample_pre_kernel(x_ref, cos_ref, sg_ref, ln1_ref, w_in_ref, qe_ref, knt_ref, vnt_ref, kn_ref, vn_ref, u_ref,
                       w_lo_ref):
    n = x_ref.shape[0]
    lo8, _ = _lane_masks()
    xn = _rms(x_ref[...], ln1_ref[...]).astype(BF16)
    cos = cos_ref[...]
    sg = sg_ref[...]
    w_lo_ref[...] = w_in_ref[...].astype(BF16)
    w = lambda lo, hi: w_lo_ref[:, lo:hi]
    u_ref[...] = _dot(xn, w(C_U, C_Q))
    q = (_rope(_dot(xn, w(C_Q, C_K)), cos, sg, lo8) * (HEAD_DIM ** -0.5)).astype(BF16)
    kn = _rope(_dot(xn, w(C_K, C_V)), cos, sg, lo8)
    vn = _dot(xn, w(C_V, C_GP))
    kn_ref[...] = kn
    vn_ref[...] = vn
    knt_ref[...] = kn.T
    vnt_ref[...] = vn.T
    ii = lax.broadcasted_iota(jnp.int32, (Q_W, KV_W), 0)
    jj = lax.broadcasted_iota(jnp.int32, (Q_W, KV_W), 1)
    for r in range(N_HEADS):
        kh = r // GROUP
        sel = ((ii - r * HEAD_DIM) == (jj - kh * HEAD_DIM)) & (jj >= kh * HEAD_DIM) & (jj < (kh + 1) * HEAD_DIM)
        qr = _dot(q, jnp.where(sel, 1.0, 0.0).astype(BF16))
        for c in range(KV_W // LANES):
            qe_ref[c, pl.ds(r, n, stride=N_HEADS), :] = qr[:, c * LANES:(c + 1) * LANES]


def _sample_pre(x, cos, sg, ln1, w_in):
    n, d = x.shape
    return pl.pallas_call(
        _sample_pre_kernel,
        grid=(1,),
        in_specs=[
            _const_spec((n, d)),
            _const_spec((1, LANES)),
            _const_spec((1, LANES)),
            _const_spec((1, d)),
            pl.BlockSpec((d, C_GP), lambda i: (0, 0), pipeline_mode=pl.Buffered(1)),
        ],
        out_specs=[
            pl.BlockSpec((KV_W // LANES, n * N_HEADS, LANES), lambda i: (0, 0, 0)),
            pl.BlockSpec((KV_W, n), lambda i: (0, 0)),
            pl.BlockSpec((KV_W, n), lambda i: (0, 0)),
            pl.BlockSpec((n, KV_W), lambda i: (0, 0)),
            pl.BlockSpec((n, KV_W), lambda i: (0, 0)),
            pl.BlockSpec((n, POOL_WIDTH), lambda i: (0, 0)),
            pl.BlockSpec((d, C_GP), lambda i: (0, 0)),
        ],
        out_shape=[
            jax.ShapeDtypeStruct((KV_W // LANES, n * N_HEADS, LANES), F32),
            jax.ShapeDtypeStruct((KV_W, n), F32),
            jax.ShapeDtypeStruct((KV_W, n), F32),
            jax.ShapeDtypeStruct((n, KV_W), F32),
            jax.ShapeDtypeStruct((n, KV_W), F32),
            jax.ShapeDtypeStruct((n, POOL_WIDTH), F32),
            jax.ShapeDtypeStruct((d, C_GP), BF16),
        ],
        compiler_params=pltpu.CompilerParams(dimension_semantics=("arbitrary",), vmem_limit_bytes=VMEM_LIMIT),
        name="sample_pre",
    )(x, cos, sg, ln1, w_in)


def _shift_caches(step, nb, knt_ref, vnt_ref, ckt_ref, cvt_ref, nkt_ref, nvt_ref):
    n = knt_ref.shape[1]
    w_cache = ckt_ref.shape[2]
    newest = lax.broadcasted_iota(jnp.int32, (1, w_cache), 1) == w_cache - 1
    shift = lax.rem(n - lax.rem(step * nb, n), n)
    kcols = pltpu.roll(knt_ref[...], shift, 1)
    vcols = pltpu.roll(vnt_ref[...], shift, 1)
    for bl in range(nb):
        nkt_ref[bl] = jnp.where(newest, kcols[:, bl:bl + 1], pltpu.roll(ckt_ref[bl], w_cache - 1, 1))
        nvt_ref[bl] = jnp.where(newest, vcols[:, bl:bl + 1], pltpu.roll(cvt_ref[bl], w_cache - 1, 1))


def _sample_attn_kernel(qe_ref, kn_ref, vn_ref, ckt_ref, cvt_ref, sink_ref, *rest, bb, n_cast):
    cast_src = rest[:n_cast]
    o_ref = rest[n_cast]
    cast_dst = rest[n_cast + 1:]
    for src, dst in zip(cast_src, cast_dst):
        dst[...] = src[...].astype(BF16)
    w_cache = ckt_ref.shape[2]
    oldest = lax.broadcasted_iota(jnp.int32, (1, w_cache), 1) == 0
    sink = sink_ref[...]
    rounded = lambda a: a.astype(BF16).astype(F32)
    scores = []
    for bl in range(bb):
        rows = slice(bl * N_HEADS, (bl + 1) * N_HEADS)
        qb = jnp.concatenate([qe_ref[c, rows, :] for c in range(KV_W // LANES)], axis=1).astype(BF16)
        s_old = jnp.where(oldest, NEG_INF, _dot(qb, ckt_ref[bl].astype(BF16)))
        s_new = jnp.sum(qb.astype(F32) * rounded(kn_ref[bl:bl + 1, :]), axis=1, keepdims=True)
        scores.append((s_old, s_new))
    probs = []
    for s_old, s_new in scores:
        m = jnp.maximum(jnp.maximum(jnp.max(s_old, axis=-1, keepdims=True), s_new), sink)
        e_old = jnp.exp(s_old - m)
        e_new = jnp.exp(s_new - m)
        denom = jnp.sum(e_old, axis=-1, keepdims=True) + e_new + jnp.exp(sink - m)
        probs.append((e_old.astype(BF16), rounded(e_new), denom))
    for bl, (e_old, e_new, denom) in enumerate(probs):
        rows = slice(bl * N_HEADS, (bl + 1) * N_HEADS)
        o = (_dot_nt(e_old, cvt_ref[bl].astype(BF16)) + e_new * rounded(vn_ref[bl:bl + 1, :])) / denom
        for c in range(KV_W // LANES):
            o_ref[c, rows, :] = o[:, c * LANES:(c + 1) * LANES]


def _sample_attn(qe, kn, vn, ckt, cvt, sink, to_cast, bb):
    n, _, w_cache = ckt.shape
    cast_in, cast_out, cast_shapes = _cast_specs(to_cast, n // bb, lambda i: i)
    cache_spec = pl.BlockSpec((bb, KV_W, w_cache), lambda i: (i, 0, 0))
    new_spec = pl.BlockSpec((bb, KV_W), lambda i: (i, 0))
    head_spec = pl.BlockSpec((KV_W // LANES, bb * N_HEADS, LANES), lambda i: (0, i, 0))
    return pl.pallas_call(
        functools.partial(_sample_attn_kernel, bb=bb, n_cast=len(to_cast)),
        grid=(n // bb,),
        in_specs=[head_spec, new_spec, new_spec, cache_spec, cache_spec, _const_spec((N_HEADS, 1))] + cast_in,
        out_specs=[head_spec] + cast_out,
        out_shape=[jax.ShapeDtypeStruct((KV_W // LANES, n * N_HEADS, LANES), F32)] + cast_shapes,
        compiler_params=pltpu.CompilerParams(dimension_semantics=("arbitrary",), vmem_limit_bytes=VMEM_LIMIT),
        name="sample_attn",
    )(qe, kn, vn, ckt, cvt, sink, *_cast_arrays(to_cast))


def _sample_post_kernel(x_ref, u_ref, st_ref, o_ref, ln1_ref, w_g_ref, gw_ref, pscale_ref, w_pb_ref, w_ab_ref,
                        w_out_ref, h_ref, nst_ref):
    x = x_ref[...]
    n = x.shape[0]
    xn = _rms(x, ln1_ref[...]).astype(BF16)
    u = u_ref[...]
    nst_ref[0:POOL_STATE - 1] = st_ref[1:POOL_STATE]
    nst_ref[POOL_STATE - 1] = u

    def win_sum(g, w, ug):
        acc = ug
        for i in range(1, w):
            acc = acc + st_ref[POOL_STATE - i, :, g * POOL_GC:(g + 1) * POOL_GC]
        return acc

    pooled = _pool_mix(u, win_sum, lambda w: jnp.float32(min(w, PAST_LEN + 1)), gw_ref, pscale_ref)
    merged = _sigmoid(_dot(xn, w_g_ref[:, 0:D_MODEL])) * _dot(pooled.astype(BF16), w_pb_ref[...])
    kv_of_lane = lax.broadcasted_iota(jnp.int32, (1, KV_W), 1) // HEAD_DIM
    ab = jnp.zeros((n, D_MODEL), F32)
    for g in range(GROUP):
        row_g = jnp.zeros((n, KV_W), F32)
        for kh in range(N_KV_HEADS):
            r = kh * GROUP + g
            o_r = jnp.concatenate([o_ref[c, pl.ds(r, n, stride=N_HEADS), :] for c in range(KV_W // LANES)], axis=1)
            row_g = jnp.where(kv_of_lane == kh, o_r, row_g)
        w_g = jnp.concatenate([w_ab_ref[(kh * GROUP + g) * HEAD_DIM:(kh * GROUP + g + 1) * HEAD_DIM, :]
                               for kh in range(N_KV_HEADS)], axis=0)
        ab = ab + _dot(row_g.astype(BF16), w_g)
    merged = merged + _sigmoid(_dot(xn, w_g_ref[:, D_MODEL:2 * D_MODEL])) * ab
    h_ref[...] = x + _dot(merged.astype(BF16), w_out_ref[...])


def _sample_post(x, u, st, o, ln1, w_gate, gw, pscale, w_pb, w_ab, w_out):
    n, d = x.shape
    return pl.pallas_call(
        _sample_post_kernel,
        grid=(1,),
        in_specs=[
            _const_spec((n, d)),
            _const_spec((n, POOL_WIDTH)),
            _const_spec(st.shape),
            _const_spec((KV_W // LANES, n * N_HEADS, LANES)),
            _const_spec((1, d)),
            _const_spec(w_gate.shape),
            _const_spec(gw.shape),
            _const_spec((1, POOL_WIDTH)),
            _const_spec(w_pb.shape),
            _const_spec(w_ab.shape),
            _const_spec(w_out.shape),
        ],
        out_specs=[pl.BlockSpec((n, d), lambda i: (0, 0)), pl.BlockSpec(st.shape, lambda i: (0, 0, 0))],
        out_shape=[jax.ShapeDtypeStruct((n, d), F32), jax.ShapeDtypeStruct(st.shape, F32)],
        compiler_params=pltpu.CompilerParams(dimension_semantics=("arbitrary",), vmem_limit_bytes=VMEM_LIMIT),
        name="sample_post",
    )(x, u, st, o, ln1, w_gate, gw, pscale, w_pb, w_ab, w_out)


def _rope_tables(first_pos, n):
    half = ROT_DIMS // 2
    inv = ROPE_THETA ** (-(np.arange(0, ROT_DIMS, 2, dtype=np.float64) / ROT_DIMS))
    ang = np.arange(first_pos, first_pos + n, dtype=np.float64)[:, None] * inv[None, :]
    cos, sin = np.cos(ang), np.sin(ang)
    rest = HEAD_DIM - 2 * half
    c64 = np.concatenate([cos, cos, np.ones((n, rest))], axis=1)
    s64 = np.concatenate([-sin, sin, np.zeros((n, rest))], axis=1)
    reps = LANES // HEAD_DIM
    return jnp.asarray(np.tile(c64, (1, reps)), F32), jnp.asarray(np.tile(s64, (1, reps)), F32)


def kernel(x_prompt, x_sample, p_prompt, p_sample, cache_k, cache_v, state_pool, ln1, w_in, pool_group_w, pool_scale,
           attn_sinks, w_pool_branch, w_attn_branch, w_out, ln2, w_ffn_in, w_ffn_out, w_ple_proj, ple_norm,
           w_ple_gate, final_norm):
    depth = ln1.shape[0]
    b, s, d = x_prompt.shape
    bd, t_dec, _ = x_sample.shape
    w_cache = cache_k.shape[2]
    assert depth == 1 and t_dec == 1 and w_cache == WINDOW and s % BLOCK == 0 and d == D_MODEL
    tm = 512
    assert s % tm == 0

    cos_p, sg_p = _rope_tables(0, s)
    cos_s, sg_s = _rope_tables(PAST_LEN, t_dec)

    hp = x_prompt
    hs = x_sample.reshape(bd, d)
    row = lambda a: a.reshape(1, -1)
    nkp, nvp, npp, nks, nvs, nps = [], [], [], [], [], []
    for i in range(depth):
        to_fm = lambda c: jnp.transpose(c, (0, 2, 3, 1)).reshape(bd, KV_W, w_cache)
        from_fm = lambda c: jnp.transpose(c.reshape(bd, N_KV_HEADS, HEAD_DIM, w_cache), (0, 3, 1, 2))
        ckt, cvt = to_fm(cache_k[i]), to_fm(cache_v[i])
        qe, knt, vnt, kn, vn, un, wlo = _sample_pre(hs, cos_s, sg_s, row(ln1[i]), w_in[i])
        o, whi, gw, wpb, wab, wo = _sample_attn(
            qe, kn, vn, ckt, cvt, attn_sinks[i].reshape(N_HEADS, 1),
            ((w_in[i], 2 * D_MODEL, 1), pool_group_w[i].reshape(POOL_WIDTH, POOL_GC), w_pool_branch[i],
             w_attn_branch[i], w_out[i]), 16)
        gw = gw.reshape(len(POOL_WINDOWS), POOL_GC, POOL_GC)
        h1, kp, vp, pp, nkt, nvt, w1, w2, wpp, wpg = _prompt_mixer(
            hp, cos_p, sg_p, attn_sinks[i], row(ln1[i]), wlo, whi, gw, row(pool_scale[i]), wpb, wab, wo,
            knt, vnt, ckt, cvt, (w_ffn_in[i], w_ffn_out[i], w_ple_proj[i], w_ple_gate[i]), tm)
        ffn_args = (row(ln2[i]), w1, w2, wpp, row(ple_norm[i]), wpg)
        hp = _ffn(h1.reshape(b * s, d), p_prompt[i].reshape(b * s, PLE_DIM), *ffn_args, row(final_norm),
                  tm, 2).reshape(b, s, d)
        from_fm_p = lambda c: jnp.transpose(c.reshape(b, N_KV_HEADS, HEAD_DIM, w_cache), (0, 3, 1, 2))
        nkp.append(from_fm_p(kp))
        nvp.append(from_fm_p(vp))
        npp.append(pp)

        h1s, nst = _sample_post(hs, un, jnp.transpose(state_pool[i], (1, 0, 2)), o, row(ln1[i]), whi, gw,
                                row(pool_scale[i]), wpb, wab, wo)
        hs = _ffn(h1s, p_sample[i].reshape(bd * t_dec, PLE_DIM), *ffn_args, row(final_norm), bd, 1)
        nks.append(from_fm(nkt))
        nvs.append(from_fm(nvt))
        nps.append(jnp.transpose(nst, (1, 0, 2)))

    return (hp, hs.reshape(bd, t_dec, d), jnp.stack(nkp), jnp.stack(nvp), jnp.stack(npp),
            jnp.stack(nks), jnp.stack(nvs), jnp.stack(nps))
```

```python
import functools

import jax
import jax.numpy as jnp
import numpy as np
from jax import lax
from jax.experimental import pallas as pl
from jax.experimental.pallas import tpu as pltpu

D_MODEL = 1024
HEAD_DIM = 64
N_HEADS = D_MODEL // HEAD_DIM
N_KV_HEADS = N_HEADS // 4
GROUP = N_HEADS // N_KV_HEADS
ROT_DIMS = HEAD_DIM // 4
ROPE_THETA = 500000.0
WINDOW = 128
BLOCK = 128
POOL_WIDTH = D_MODEL // 2
POOL_WINDOWS = (2, 4, 8, 16)
POOL_GC = POOL_WIDTH // len(POOL_WINDOWS)
POOL_STATE = max(POOL_WINDOWS) - 1
FFN_HIDDEN = -(-8 * D_MODEL // (3 * 256)) * 256
PLE_DIM = 256
EPS = 1e-6
NEG_INF = -1e30
PAST_LEN = 16384

Q_W = N_HEADS * HEAD_DIM
KV_W = N_KV_HEADS * HEAD_DIM
C_U, C_Q, C_K, C_V, C_GP, C_GA, C_END = 0, POOL_WIDTH, POOL_WIDTH + Q_W, POOL_WIDTH + Q_W + KV_W, \
    POOL_WIDTH + Q_W + 2 * KV_W, POOL_WIDTH + Q_W + 2 * KV_W + D_MODEL, POOL_WIDTH + Q_W + 2 * KV_W + 2 * D_MODEL

LANES = 128
S_AHEAD = 16
S_SLOTS = 16
GATE_COLS = 256
SUBLANES = 8
U_HALO = 24
VMEM_LIMIT = 56 * 1024 * 1024

BF16 = jnp.bfloat16
F32 = jnp.float32


def _dot(a, b):
    return jnp.dot(a, b, preferred_element_type=F32)


def _dot_nt(a, b):
    return lax.dot_general(a, b, (((1,), (1,)), ((), ())), preferred_element_type=F32)


def _sigmoid(x):
    return 0.5 * jnp.tanh(0.5 * x) + 0.5


def _rms(x, g):
    y = x * lax.rsqrt(jnp.mean(x * x, axis=-1, keepdims=True) + EPS)
    return y * g


def _rope(x, cos, sg, lo8):
    outs = []
    for c in range(x.shape[1] // LANES):
        xc = x[:, c * LANES:(c + 1) * LANES]
        partner = jnp.where(lo8, pltpu.roll(xc, LANES - ROT_DIMS // 2, 1), pltpu.roll(xc, ROT_DIMS // 2, 1))
        outs.append(xc * cos + partner * sg)
    return jnp.concatenate(outs, axis=1)


def _lane_masks():
    lane = lax.broadcasted_iota(jnp.int32, (1, LANES), 1)
    lo8 = (lane % HEAD_DIM) < (ROT_DIMS // 2)
    lo64 = lane < HEAD_DIM
    return lo8, lo64


def _pool_mix(u, win_sum_fn, cnt_fn, gw_ref, pscale_ref):
    mixed = []
    for g, w in enumerate(POOL_WINDOWS):
        cols = slice(g * POOL_GC, (g + 1) * POOL_GC)
        ug = u[:, cols]
        m = win_sum_fn(g, w, ug) / cnt_fn(w) - ug
        mixed.append(_dot(m.astype(BF16), gw_ref[g]) * pscale_ref[:, cols])
    return jnp.concatenate(mixed, axis=1)


def _mixer_kernel(sinks_ref, x_ref, cos_ref, sg_ref, ln1_ref, w_in_ref, gw_ref, pscale_ref, w_pb_ref, w_ab_ref,
                  w_out_ref, knt_ref, vnt_ref, ckt_ref, cvt_ref, *rest, tm, n_cast):
    cast_src = rest[:n_cast]
    h_ref, ko_ref, vo_ref, po_ref, nkt_ref, nvt_ref = rest[n_cast:n_cast + 6]
    cast_dst = rest[n_cast + 6:2 * n_cast + 6]
    kl_scr, kh_scr, vl_scr, vh_scr, u_scr, lvl_scr, attn_scr, q_scr, s_scr, g_scr = rest[2 * n_cast + 6:]
    t = pl.program_id(1)
    step = pl.program_id(0) * pl.num_programs(1) + t
    lo8, lo64 = _lane_masks()

    @pl.when(t == 0)
    def _():
        for scr in (kl_scr, kh_scr, vl_scr, vh_scr):
            scr[:, 0:BLOCK, :] = jnp.zeros((N_KV_HEADS, BLOCK, LANES), BF16)
        u_scr[0:U_HALO, :] = jnp.zeros((U_HALO, POOL_WIDTH), F32)
        lvl_scr[:, 0:SUBLANES, :] = jnp.zeros((len(POOL_WINDOWS), SUBLANES, POOL_GC), F32)

    x = x_ref[0]
    xn = _rms(x, ln1_ref[...]).astype(BF16)
    cos = cos_ref[...]
    sg = sg_ref[...]

    k = _rope(_dot(xn, w_in_ref[:, C_K:C_V]), cos, sg, lo8)
    v = _dot(xn, w_in_ref[:, C_V:C_GP])
    q = (_rope(_dot(xn, w_in_ref[:, C_Q:C_K]), cos, sg, lo8) * (HEAD_DIM ** -0.5)).astype(BF16)
    q_scr[...] = q
    u = _dot(xn, w_in_ref[:, C_U:C_Q])

    zero = jnp.zeros((tm, LANES), F32)
    for src, lo_scr, hi_scr in ((k, kl_scr, kh_scr), (v, vl_scr, vh_scr)):
        for p in range(KV_W // LANES):
            xp = src[:, p * LANES:(p + 1) * LANES]
            xs = pltpu.roll(xp, HEAD_DIM, 1)
            lo_scr[2 * p, BLOCK:, :] = jnp.where(lo64, xp, zero).astype(BF16)
            hi_scr[2 * p, BLOCK:, :] = jnp.where(lo64, zero, xs).astype(BF16)
            lo_scr[2 * p + 1, BLOCK:, :] = jnp.where(lo64, xs, zero).astype(BF16)
            hi_scr[2 * p + 1, BLOCK:, :] = jnp.where(lo64, zero, xp).astype(BF16)

    qi = lax.broadcasted_iota(jnp.int32, (BLOCK, BLOCK), 0)
    ci = lax.broadcasted_iota(jnp.int32, (BLOCK, BLOCK), 1)
    from_prev = ci > qi
    bias0 = jnp.where(jnp.logical_and(t == 0, from_prev), NEG_INF, 0.0).astype(F32)

    ones_lo = jnp.broadcast_to(jnp.where(lo64, 1.0, 0.0).astype(BF16), (2 * BLOCK, LANES))
    ones_hi = jnp.broadcast_to(jnp.where(lo64, 0.0, 1.0).astype(BF16), (2 * BLOCK, LANES))

    units = [(j, kh) for j in range(tm // BLOCK) for kh in range(N_KV_HEADS)]

    def scores(i):
        j, kh = units[i]
        rows = slice(j * BLOCK, (j + 1) * BLOCK)
        win = slice(j * BLOCK, (j + 2) * BLOCK)
        qq = jnp.concatenate([q_scr[rows, (2 * kh + a) * LANES:(2 * kh + a + 1) * LANES] for a in range(2)], axis=0)
        kcat = jnp.concatenate([kl_scr[kh, win, :], kh_scr[kh, win, :]], axis=0)
        s = _dot_nt(qq, kcat)
        for a in range(2):
            for half in range(2):
                sa = s[a * BLOCK:(a + 1) * BLOCK, half * 2 * BLOCK:(half + 1) * 2 * BLOCK]
                folded = jnp.where(from_prev, sa[:, :BLOCK], sa[:, BLOCK:])
                if j == 0:
                    folded = folded + bias0
                s_scr[i % S_SLOTS, a * BLOCK:(a + 1) * BLOCK, half * BLOCK:(half + 1) * BLOCK] = folded

    units_per_gate = len(units) * GATE_COLS // (2 * D_MODEL)
    for i in range(S_AHEAD):
        scores(i)

    def softmax(i):
        kh = units[i][1]
        ps, sink_terms = [], []
        for a in range(2):
            es, st = [], []
            for half in range(2):
                sh = s_scr[i % S_SLOTS, a * BLOCK:(a + 1) * BLOCK, half * BLOCK:(half + 1) * BLOCK]
                sink = sinks_ref[4 * kh + 2 * a + half]
                m = jnp.maximum(jnp.max(sh, axis=1, keepdims=True), sink)
                e = jnp.exp(sh - m)
                es.append(jnp.where(from_prev, e, 0.0).astype(BF16))
                es.append(jnp.where(from_prev, 0.0, e).astype(BF16))
                st.append(jnp.exp(sink - m))
            ps.append(jnp.concatenate(es, axis=1))
            sink_terms.append(jnp.where(lo64, st[0], st[1]))
        return jnp.concatenate(ps, axis=0), sink_terms

    def weighted_values(i, p, sink_terms):
        j, kh = units[i]
        rows = slice(j * BLOCK, (j + 1) * BLOCK)
        win = slice(j * BLOCK, (j + 2) * BLOCK)
        vcat = jnp.concatenate([
            jnp.concatenate([vl_scr[kh, win, :], ones_lo], axis=1),
            jnp.concatenate([vh_scr[kh, win, :], ones_hi], axis=1)], axis=0)
        o = _dot(p, vcat)
        for a in range(2):
            oa = o[a * BLOCK:(a + 1) * BLOCK]
            attn_scr[rows, (2 * kh + a) * LANES:(2 * kh + a + 1) * LANES] = (
                oa[:, :LANES] / (oa[:, LANES:] + sink_terms[a])).astype(BF16)

    pending = None
    for i in range(len(units)):
        if i + S_AHEAD < len(units):
            scores(i + S_AHEAD)
        current = (i,) + softmax(i)
        if pending is not None:
            weighted_values(*pending)
        pending = current
        if i % units_per_gate == units_per_gate - 1:
            gcols = slice((i // units_per_gate) * GATE_COLS, (i // units_per_gate + 1) * GATE_COLS)
            g_scr[:, gcols] = _sigmoid(_dot(xn, w_in_ref[:, C_GP + gcols.start:C_GP + gcols.stop]))
    weighted_values(*pending)

    u_scr[U_HALO:, :] = u
    po_ref[0] = u_scr[U_HALO + tm - POOL_STATE:U_HALO + tm, :]
    pos = t * tm + lax.broadcasted_iota(jnp.int32, (tm, 1), 0)

    def win_sum(g, w, ug):
        cols = slice(g * POOL_GC, (g + 1) * POOL_GC)
        n = U_HALO - SUBLANES + tm
        src, span = u_scr, 1
        while span < w:
            lvl = src[SUBLANES:SUBLANES + n, cols] + src[SUBLANES - span:SUBLANES - span + n, cols]
            span *= 2
            if span < w:
                lvl_scr[g, SUBLANES:SUBLANES + n, :] = lvl
                src, cols = lvl_scr.at[g], slice(None)
        return lvl[U_HALO - SUBLANES:, :]

    pooled = _pool_mix(u, win_sum, lambda w: jnp.minimum(w, pos + 1).astype(F32), gw_ref, pscale_ref)

    for scr in (kl_scr, kh_scr, vl_scr, vh_scr):
        scr[:, 0:BLOCK, :] = scr[:, tm:tm + BLOCK, :]
    u_scr[0:U_HALO, :] = u_scr[tm:tm + U_HALO, :]

    _shift_caches(step, ckt_ref.shape[0], knt_ref, vnt_ref, ckt_ref, cvt_ref, nkt_ref, nvt_ref)
    for src, dst in zip(cast_src, cast_dst):
        dst[...] = src[...].astype(BF16)
    ko_ref[0] = k[tm - WINDOW:, :].T
    vo_ref[0] = v[tm - WINDOW:, :].T

    merged = g_scr[:, 0:D_MODEL] * _dot(pooled.astype(BF16), w_pb_ref[...])
    merged = merged + g_scr[:, D_MODEL:2 * D_MODEL] * _dot(attn_scr[...], w_ab_ref[...])
    h_ref[0] = x + _dot(merged.astype(BF16), w_out_ref[...])


def _const_spec(shape):
    nd = len(shape)
    return pl.BlockSpec(shape, lambda *_: (0,) * nd, pipeline_mode=pl.Buffered(1))


def _cast_block_rows(rows, steps):
    br = 2 * SUBLANES
    while rows % br or rows // br > steps:
        br *= 2
    return br


def _cast_specs(to_cast, steps, step_of):
    specs = []
    for w in to_cast:
        br = _cast_block_rows(w.shape[0], steps)
        last = w.shape[0] // br - 1
        specs.append(pl.BlockSpec(
            (br, w.shape[1]), lambda *idx, last=last: (jnp.minimum(step_of(*idx), last), 0)))
    return specs


def _prompt_mixer(x, cos, sg, sinks, ln1, w_in, gw, pscale, w_pb, w_ab, w_out, knt, vnt, ckt, cvt, to_cast, tm):
    b, s, d = x.shape
    nt = s // tm
    cast_specs = _cast_specs(to_cast, b * nt, lambda bi, ti, *_: bi * nt + ti)
    nb = ckt.shape[0] // (b * nt)
    assert nb * b * nt == ckt.shape[0]
    cache_spec = pl.BlockSpec((nb,) + ckt.shape[1:], lambda bi, ti, *_: (bi * nt + ti, 0, 0))
    grid_spec = pltpu.PrefetchScalarGridSpec(
        num_scalar_prefetch=1,
        grid=(b, nt),
        in_specs=[
            pl.BlockSpec((1, tm, d), lambda bi, ti, *_: (bi, ti, 0)),
            pl.BlockSpec((tm, LANES), lambda bi, ti, *_: (ti, 0)),
            pl.BlockSpec((tm, LANES), lambda bi, ti, *_: (ti, 0)),
            _const_spec((1, d)),
            _const_spec(w_in.shape),
            _const_spec(gw.shape),
            _const_spec((1, POOL_WIDTH)),
            _const_spec(w_pb.shape),
            _const_spec(w_ab.shape),
            _const_spec(w_out.shape),
            _const_spec(knt.shape),
            _const_spec(vnt.shape),
            cache_spec,
            cache_spec,
        ] + cast_specs,
        out_specs=[
            pl.BlockSpec((1, tm, d), lambda bi, ti, *_: (bi, ti, 0)),
            pl.BlockSpec((1, KV_W, WINDOW), lambda bi, ti, *_: (bi, 0, 0)),
            pl.BlockSpec((1, KV_W, WINDOW), lambda bi, ti, *_: (bi, 0, 0)),
            pl.BlockSpec((1, POOL_STATE, POOL_WIDTH), lambda bi, ti, *_: (bi, 0, 0)),
            cache_spec,
            cache_spec,
        ] + cast_specs,
        scratch_shapes=[
            pltpu.VMEM((N_KV_HEADS, BLOCK + tm, LANES), BF16),
            pltpu.VMEM((N_KV_HEADS, BLOCK + tm, LANES), BF16),
            pltpu.VMEM((N_KV_HEADS, BLOCK + tm, LANES), BF16),
            pltpu.VMEM((N_KV_HEADS, BLOCK + tm, LANES), BF16),
            pltpu.VMEM((U_HALO + tm, POOL_WIDTH), F32),
            pltpu.VMEM((len(POOL_WINDOWS), U_HALO + tm, POOL_GC), F32),
            pltpu.VMEM((tm, Q_W), BF16),
            pltpu.VMEM((tm, Q_W), BF16),
            pltpu.VMEM((S_SLOTS, 2 * BLOCK, 2 * BLOCK), F32),
            pltpu.VMEM((tm, 2 * D_MODEL), F32),
        ],
    )
    return pl.pallas_call(
        functools.partial(_mixer_kernel, tm=tm, n_cast=len(to_cast)),
        grid_spec=grid_spec,
        out_shape=[
            jax.ShapeDtypeStruct((b, s, d), F32),
            jax.ShapeDtypeStruct((b, KV_W, WINDOW), F32),
            jax.ShapeDtypeStruct((b, KV_W, WINDOW), F32),
            jax.ShapeDtypeStruct((b, POOL_STATE, POOL_WIDTH), F32),
            jax.ShapeDtypeStruct(ckt.shape, F32),
            jax.ShapeDtypeStruct(cvt.shape, F32),
        ] + [jax.ShapeDtypeStruct(w.shape, BF16) for w in to_cast],
        compiler_params=pltpu.CompilerParams(
            dimension_semantics=("arbitrary", "arbitrary"), vmem_limit_bytes=VMEM_LIMIT),
        name="prompt_mixer",
    )(sinks, x, cos, sg, ln1, w_in, gw, pscale, w_pb, w_ab, w_out, knt, vnt, ckt, cvt, *to_cast)


FFN_CHUNKS = ((0, 1024), (1024, 2048), (2048, FFN_HIDDEN))


def _ffn_kernel(h_ref, p_ref, hs_ref, ps_ref, ln2_ref, w1_ref, w2_ref, w_pp_ref, pn_ref, w_pg_ref, fn_ref, y_ref,
                ys_ref, *, tm):
    ns = hs_ref.shape[0]
    pad = (-ns) % (2 * SUBLANES)

    def tile_stages(r0, with_sample):
        rows = slice(r0, r0 + tm)
        st = {}

        def gather(ref, ref_s):
            if not with_sample:
                return ref[rows, :]
            parts = [ref[rows, :], ref_s[...]] + ([jnp.zeros((pad, ref.shape[1]), F32)] if pad else [])
            return jnp.concatenate(parts, axis=0)

        def up_proj(c):
            lo, hi = FFN_CHUNKS[c]
            if c == 0:
                st['h'] = gather(h_ref, hs_ref)
                st['hn'] = _rms(st['h'], ln2_ref[...]).astype(BF16)
                st['acc'] = st['h']
            st['gate', c] = _dot(st['hn'], w1_ref[:, lo:hi])
            st['up', c] = _dot(st['hn'], w1_ref[:, FFN_HIDDEN + lo:FFN_HIDDEN + hi])
            if c == len(FFN_CHUNKS) - 1:
                st['e'] = _rms(_dot(gather(p_ref, ps_ref).astype(BF16), w_pp_ref[...]), pn_ref[...])

        def down_proj(c):
            lo, hi = FFN_CHUNKS[c]
            gate = st.pop(('gate', c))
            act = (gate * _sigmoid(gate) * st.pop(('up', c))).astype(BF16)
            st['acc'] = st['acc'] + _dot(act, w2_ref[lo:hi, :])

        def ple_gate():
            st['g'] = _dot(st['acc'].astype(BF16), w_pg_ref[...])

        def finish():
            h3 = st['acc'] + _sigmoid(st['g']) * st['e']
            y = _rms(h3, fn_ref[...])
            y_ref[rows, :] = y[:tm]
            if with_sample:
                ys_ref[...] = y[tm:tm + ns]

        n = len(FFN_CHUNKS)
        steps = [functools.partial(up_proj, 0)]
        for c in range(1, n):
            steps += [functools.partial(up_proj, c), functools.partial(down_proj, c - 1)]
        return steps + [functools.partial(down_proj, n - 1), ple_gate, finish]

    tail = 3
    order = []
    starts = list(range(0, h_ref.shape[0], tm))
    for r0 in starts:
        steps = tile_stages(r0, r0 == starts[-1])
        held, order = order[len(order) - tail:] if order else [], order[:len(order) - tail] if order else []
        for k in range(max(len(held), tail)):
            order += steps[k:k + 1] + held[k:k + 1]
        order += steps[tail:]
    for step in order:
        step()


def _ffn(h, p, hs, ps, ln2, w1, w2, w_pp, pn, w_pg, fn, tm, nsub):
    n, d = h.shape
    blk = tm * nsub
    ns = hs.shape[0] // (n // blk)
    assert ns * (n // blk) == hs.shape[0] and ns % SUBLANES == 0
    return pl.pallas_call(
        functools.partial(_ffn_kernel, tm=tm),
        grid=(n // blk,),
        in_specs=[
            pl.BlockSpec((blk, d), lambda i: (i, 0)),
            pl.BlockSpec((blk, PLE_DIM), lambda i: (i, 0)),
            pl.BlockSpec((ns, d), lambda i: (i, 0)),
            pl.BlockSpec((ns, PLE_DIM), lambda i: (i, 0)),
            _const_spec((1, d)),
            _const_spec(w1.shape),
            _const_spec(w2.shape),
            _const_spec(w_pp.shape),
            _const_spec((1, d)),
            _const_spec(w_pg.shape),
            _const_spec((1, d)),
        ],
        out_specs=[pl.BlockSpec((blk, d), lambda i: (i, 0)), pl.BlockSpec((ns, d), lambda i: (i, 0))],
        out_shape=[jax.ShapeDtypeStruct((n, d), F32), jax.ShapeDtypeStruct(hs.shape, F32)],
        compiler_params=pltpu.CompilerParams(dimension_semantics=("arbitrary",), vmem_limit_bytes=VMEM_LIMIT),
        name="ffn_ple_norm",
    )(h, p, hs, ps, ln2, w1, w2, w_pp, pn, w_pg, fn)


def _sample_pre_kernel(x_ref, cos_ref, sg_ref, ln1_ref, w_in_ref, qe_ref, knt_ref, vnt_ref, kn_ref, vn_ref, u_ref):
    n = x_ref.shape[0]
    lo8, _ = _lane_masks()
    xn = _rms(x_ref[...], ln1_ref[...]).astype(BF16)
    cos = cos_ref[...]
    sg = sg_ref[...]
    w = lambda lo, hi: w_in_ref[:, lo:hi].astype(BF16)
    u_ref[...] = _dot(xn, w(C_U, C_Q))
    q = (_rope(_dot(xn, w(C_Q, C_K)), cos, sg, lo8) * (HEAD_DIM ** -0.5)).astype(BF16)
    kn = _rope(_dot(xn, w(C_K, C_V)), cos, sg, lo8)
    vn = _dot(xn, w(C_V, C_GP))
    kn_ref[...] = kn
    vn_ref[...] = vn
    knt_ref[...] = kn.T
    vnt_ref[...] = vn.T
    ii = lax.broadcasted_iota(jnp.int32, (Q_W, KV_W), 0)
    jj = lax.broadcasted_iota(jnp.int32, (Q_W, KV_W), 1)
    for r in range(N_HEADS):
        kh = r // GROUP
        sel = ((ii - r * HEAD_DIM) == (jj - kh * HEAD_DIM)) & (jj >= kh * HEAD_DIM) & (jj < (kh + 1) * HEAD_DIM)
        qr = _dot(q, jnp.where(sel, 1.0, 0.0).astype(BF16))
        for c in range(KV_W // LANES):
            qe_ref[c, pl.ds(r, n, stride=N_HEADS), :] = qr[:, c * LANES:(c + 1) * LANES]


def _sample_pre(x, cos, sg, ln1, w_in):
    n, d = x.shape
    return pl.pallas_call(
        _sample_pre_kernel,
        grid=(1,),
        in_specs=[
            _const_spec((n, d)),
            _const_spec((1, LANES)),
            _const_spec((1, LANES)),
            _const_spec((1, d)),
            pl.BlockSpec((d, C_GP), lambda i: (0, 0), pipeline_mode=pl.Buffered(1)),
        ],
        out_specs=[
            pl.BlockSpec((KV_W // LANES, n * N_HEADS, LANES), lambda i: (0, 0, 0)),
            pl.BlockSpec((KV_W, n), lambda i: (0, 0)),
            pl.BlockSpec((KV_W, n), lambda i: (0, 0)),
            pl.BlockSpec((n, KV_W), lambda i: (0, 0)),
            pl.BlockSpec((n, KV_W), lambda i: (0, 0)),
            pl.BlockSpec((n, POOL_WIDTH), lambda i: (0, 0)),
        ],
        out_shape=[
            jax.ShapeDtypeStruct((KV_W // LANES, n * N_HEADS, LANES), F32),
            jax.ShapeDtypeStruct((KV_W, n), F32),
            jax.ShapeDtypeStruct((KV_W, n), F32),
            jax.ShapeDtypeStruct((n, KV_W), F32),
            jax.ShapeDtypeStruct((n, KV_W), F32),
            jax.ShapeDtypeStruct((n, POOL_WIDTH), F32),
        ],
        compiler_params=pltpu.CompilerParams(dimension_semantics=("arbitrary",), vmem_limit_bytes=VMEM_LIMIT),
        name="sample_pre",
    )(x, cos, sg, ln1, w_in)


def _shift_caches(step, nb, knt_ref, vnt_ref, ckt_ref, cvt_ref, nkt_ref, nvt_ref):
    n = knt_ref.shape[1]
    w_cache = ckt_ref.shape[2]
    newest = lax.broadcasted_iota(jnp.int32, (1, w_cache), 1) == w_cache - 1
    shift = lax.rem(n - lax.rem(step * nb, n), n)
    kcols = pltpu.roll(knt_ref[...], shift, 1)
    vcols = pltpu.roll(vnt_ref[...], shift, 1)
    for bl in range(nb):
        nkt_ref[bl] = jnp.where(newest, kcols[:, bl:bl + 1], pltpu.roll(ckt_ref[bl], w_cache - 1, 1))
        nvt_ref[bl] = jnp.where(newest, vcols[:, bl:bl + 1], pltpu.roll(cvt_ref[bl], w_cache - 1, 1))


def _sample_attn_kernel(qe_ref, kn_ref, vn_ref, ckt_ref, cvt_ref, sink_ref, *rest, bb, n_cast):
    cast_src = rest[:n_cast]
    o_ref = rest[n_cast]
    cast_dst = rest[n_cast + 1:]
    for src, dst in zip(cast_src, cast_dst):
        dst[...] = src[...].astype(BF16)
    w_cache = ckt_ref.shape[2]
    oldest = lax.broadcasted_iota(jnp.int32, (1, w_cache), 1) == 0
    sink = sink_ref[...]
    rounded = lambda a: a.astype(BF16).astype(F32)
    scores = []
    for bl in range(bb):
        rows = slice(bl * N_HEADS, (bl + 1) * N_HEADS)
        qb = jnp.concatenate([qe_ref[c, rows, :] for c in range(KV_W // LANES)], axis=1).astype(BF16)
        s_old = jnp.where(oldest, NEG_INF, _dot(qb, ckt_ref[bl].astype(BF16)))
        s_new = jnp.sum(qb.astype(F32) * rounded(kn_ref[bl:bl + 1, :]), axis=1, keepdims=True)
        scores.append((s_old, s_new))
    probs = []
    for s_old, s_new in scores:
        m = jnp.maximum(jnp.maximum(jnp.max(s_old, axis=-1, keepdims=True), s_new), sink)
        e_old = jnp.exp(s_old - m)
        e_new = jnp.exp(s_new - m)
        denom = jnp.sum(e_old, axis=-1, keepdims=True) + e_new + jnp.exp(sink - m)
        probs.append((e_old.astype(BF16), rounded(e_new), denom))
    for bl, (e_old, e_new, denom) in enumerate(probs):
        rows = slice(bl * N_HEADS, (bl + 1) * N_HEADS)
        o = (_dot_nt(e_old, cvt_ref[bl].astype(BF16)) + e_new * rounded(vn_ref[bl:bl + 1, :])) / denom
        for c in range(KV_W // LANES):
            o_ref[c, rows, :] = o[:, c * LANES:(c + 1) * LANES]


def _sample_attn(qe, kn, vn, ckt, cvt, sink, to_cast, bb):
    n, _, w_cache = ckt.shape
    cast_specs = _cast_specs(to_cast, n // bb, lambda i: i)
    cache_spec = pl.BlockSpec((bb, KV_W, w_cache), lambda i: (i, 0, 0))
    new_spec = pl.BlockSpec((bb, KV_W), lambda i: (i, 0))
    head_spec = pl.BlockSpec((KV_W // LANES, bb * N_HEADS, LANES), lambda i: (0, i, 0))
    return pl.pallas_call(
        functools.partial(_sample_attn_kernel, bb=bb, n_cast=len(to_cast)),
        grid=(n // bb,),
        in_specs=[head_spec, new_spec, new_spec, cache_spec, cache_spec, _const_spec((N_HEADS, 1))] + cast_specs,
        out_specs=[head_spec] + cast_specs,
        out_shape=[jax.ShapeDtypeStruct((KV_W // LANES, n * N_HEADS, LANES), F32)]
        + [jax.ShapeDtypeStruct(w.shape, BF16) for w in to_cast],
        compiler_params=pltpu.CompilerParams(dimension_semantics=("arbitrary",), vmem_limit_bytes=VMEM_LIMIT),
        name="sample_attn",
    )(qe, kn, vn, ckt, cvt, sink, *to_cast)


def _sample_post_kernel(x_ref, u_ref, st_ref, o_ref, ln1_ref, w_g_ref, gw_ref, pscale_ref, w_pb_ref, w_ab_ref,
                        w_out_ref, h_ref, nst_ref):
    x = x_ref[...]
    n = x.shape[0]
    xn = _rms(x, ln1_ref[...]).astype(BF16)
    u = u_ref[...]
    nst_ref[0:POOL_STATE - 1] = st_ref[1:POOL_STATE]
    nst_ref[POOL_STATE - 1] = u

    def win_sum(g, w, ug):
        acc = ug
        for i in range(1, w):
            acc = acc + st_ref[POOL_STATE - i, :, g * POOL_GC:(g + 1) * POOL_GC]
        return acc

    pooled = _pool_mix(u, win_sum, lambda w: jnp.float32(min(w, PAST_LEN + 1)), gw_ref, pscale_ref)
    merged = _sigmoid(_dot(xn, w_g_ref[:, 0:D_MODEL])) * _dot(pooled.astype(BF16), w_pb_ref[...])
    kv_of_lane = lax.broadcasted_iota(jnp.int32, (1, KV_W), 1) // HEAD_DIM
    ab = jnp.zeros((n, D_MODEL), F32)
    for g in range(GROUP):
        row_g = jnp.zeros((n, KV_W), F32)
        for kh in range(N_KV_HEADS):
            r = kh * GROUP + g
            o_r = jnp.concatenate([o_ref[c, pl.ds(r, n, stride=N_HEADS), :] for c in range(KV_W // LANES)], axis=1)
            row_g = jnp.where(kv_of_lane == kh, o_r, row_g)
        w_g = jnp.concatenate([w_ab_ref[(kh * GROUP + g) * HEAD_DIM:(kh * GROUP + g + 1) * HEAD_DIM, :]
                               for kh in range(N_KV_HEADS)], axis=0)
        ab = ab + _dot(row_g.astype(BF16), w_g)
    merged = merged + _sigmoid(_dot(xn, w_g_ref[:, D_MODEL:2 * D_MODEL])) * ab
    h_ref[...] = x + _dot(merged.astype(BF16), w_out_ref[...])


def _sample_post(x, u, st, o, ln1, w_in, gw, pscale, w_pb, w_ab, w_out):
    n, d = x.shape
    return pl.pallas_call(
        _sample_post_kernel,
        grid=(1,),
        in_specs=[
            _const_spec((n, d)),
            _const_spec((n, POOL_WIDTH)),
            _const_spec(st.shape),
            _const_spec((KV_W // LANES, n * N_HEADS, LANES)),
            _const_spec((1, d)),
            pl.BlockSpec((d, 2 * D_MODEL), lambda i: (0, 1), pipeline_mode=pl.Buffered(1)),
            _const_spec(gw.shape),
            _const_spec((1, POOL_WIDTH)),
            _const_spec(w_pb.shape),
            _const_spec(w_ab.shape),
            _const_spec(w_out.shape),
        ],
        out_specs=[pl.BlockSpec((n, d), lambda i: (0, 0)), pl.BlockSpec(st.shape, lambda i: (0, 0, 0))],
        out_shape=[jax.ShapeDtypeStruct((n, d), F32), jax.ShapeDtypeStruct(st.shape, F32)],
        compiler_params=pltpu.CompilerParams(dimension_semantics=("arbitrary",), vmem_limit_bytes=VMEM_LIMIT),
        name="sample_post",
    )(x, u, st, o, ln1, w_in, gw, pscale, w_pb, w_ab, w_out)


def _rope_tables(first_pos, n):
    half = ROT_DIMS // 2
    inv = ROPE_THETA ** (-(np.arange(0, ROT_DIMS, 2, dtype=np.float64) / ROT_DIMS))
    ang = np.arange(first_pos, first_pos + n, dtype=np.float64)[:, None] * inv[None, :]
    cos, sin = np.cos(ang), np.sin(ang)
    rest = HEAD_DIM - 2 * half
    c64 = np.concatenate([cos, cos, np.ones((n, rest))], axis=1)
    s64 = np.concatenate([-sin, sin, np.zeros((n, rest))], axis=1)
    reps = LANES // HEAD_DIM
    return jnp.asarray(np.tile(c64, (1, reps)), F32), jnp.asarray(np.tile(s64, (1, reps)), F32)


def kernel(x_prompt, x_sample, p_prompt, p_sample, cache_k, cache_v, state_pool, ln1, w_in, pool_group_w, pool_scale,
           attn_sinks, w_pool_branch, w_attn_branch, w_out, ln2, w_ffn_in, w_ffn_out, w_ple_proj, ple_norm,
           w_ple_gate, final_norm):
    depth = ln1.shape[0]
    b, s, d = x_prompt.shape
    bd, t_dec, _ = x_sample.shape
    w_cache = cache_k.shape[2]
    assert depth == 1 and t_dec == 1 and w_cache == WINDOW and s % BLOCK == 0 and d == D_MODEL
    tm = 512
    assert s % tm == 0

    cos_p, sg_p = _rope_tables(0, s)
    cos_s, sg_s = _rope_tables(PAST_LEN, t_dec)

    hp = x_prompt
    hs = x_sample.reshape(bd, d)
    row = lambda a: a.reshape(1, -1)
    nkp, nvp, npp, nks, nvs, nps = [], [], [], [], [], []
    for i in range(depth):
        to_fm = lambda c: jnp.transpose(c, (0, 2, 3, 1)).reshape(bd, KV_W, w_cache)
        from_fm = lambda c: jnp.transpose(c.reshape(bd, N_KV_HEADS, HEAD_DIM, w_cache), (0, 3, 1, 2))
        ckt, cvt = to_fm(cache_k[i]), to_fm(cache_v[i])
        qe, knt, vnt, kn, vn, un = _sample_pre(hs, cos_s, sg_s, row(ln1[i]), w_in[i])
        o, wi, gw, wpb, wab, wo = _sample_attn(
            qe, kn, vn, ckt, cvt, attn_sinks[i].reshape(N_HEADS, 1),
            (w_in[i], pool_group_w[i].reshape(POOL_WIDTH, POOL_GC), w_pool_branch[i], w_attn_branch[i], w_out[i]), 16)
        gw = gw.reshape(len(POOL_WINDOWS), POOL_GC, POOL_GC)
        h1, kp, vp, pp, nkt, nvt, w1, w2, wpp, wpg = _prompt_mixer(
            hp, cos_p, sg_p, attn_sinks[i], row(ln1[i]), wi, gw, row(pool_scale[i]), wpb, wab, wo,
            knt, vnt, ckt, cvt, (w_ffn_in[i], w_ffn_out[i], w_ple_proj[i], w_ple_gate[i]), tm)
        h1s, nst = _sample_post(hs, un, jnp.transpose(state_pool[i], (1, 0, 2)), o, row(ln1[i]), wi, gw,
                                row(pool_scale[i]), wpb, wab, wo)
        hp, hs = _ffn(h1.reshape(b * s, d), p_prompt[i].reshape(b * s, PLE_DIM), h1s,
                      p_sample[i].reshape(bd * t_dec, PLE_DIM), row(ln2[i]), w1, w2, wpp, row(ple_norm[i]), wpg,
                      row(final_norm), tm, 2)
        hp = hp.reshape(b, s, d)
        from_fm_p = lambda c: jnp.transpose(c.reshape(b, N_KV_HEADS, HEAD_DIM, w_cache), (0, 3, 1, 2))
        nkp.append(from_fm_p(kp))
        nvp.append(from_fm_p(vp))
        npp.append(pp)
        nks.append(from_fm(nkt))
        nvs.append(from_fm(nvt))
        nps.append(jnp.transpose(nst, (1, 0, 2)))

    return (hp, hs.reshape(bd, t_dec, d), jnp.stack(nkp), jnp.stack(nvp), jnp.stack(npp),
            jnp.stack(nks), jnp.stack(nvs), jnp.stack(nps))
```

```python
import functools

import jax
import jax.numpy as jnp
import numpy as np
from jax import lax
from jax.experimental import pallas as pl
from jax.experimental.pallas import tpu as pltpu

D_MODEL = 1024
HEAD_DIM = 64
N_HEADS = D_MODEL // HEAD_DIM
N_KV_HEADS = N_HEADS // 4
GROUP = N_HEADS // N_KV_HEADS
ROT_DIMS = HEAD_DIM // 4
ROPE_THETA = 500000.0
WINDOW = 128
BLOCK = 128
POOL_WIDTH = D_MODEL // 2
POOL_WINDOWS = (2, 4, 8, 16)
POOL_GC = POOL_WIDTH // len(POOL_WINDOWS)
POOL_STATE = max(POOL_WINDOWS) - 1
FFN_HIDDEN = -(-8 * D_MODEL // (3 * 256)) * 256
PLE_DIM = 256
EPS = 1e-6
NEG_INF = -1e30
PAST_LEN = 16384

Q_W = N_HEADS * HEAD_DIM
KV_W = N_KV_HEADS * HEAD_DIM
C_U, C_Q, C_K, C_V, C_GP, C_GA, C_END = 0, POOL_WIDTH, POOL_WIDTH + Q_W, POOL_WIDTH + Q_W + KV_W, \
    POOL_WIDTH + Q_W + 2 * KV_W, POOL_WIDTH + Q_W + 2 * KV_W + D_MODEL, POOL_WIDTH + Q_W + 2 * KV_W + 2 * D_MODEL

LANES = 128
S_AHEAD = 16
S_SLOTS = 16
GATE_COLS = 256
SUBLANES = 8
U_HALO = 24
VMEM_LIMIT = 56 * 1024 * 1024

BF16 = jnp.bfloat16
F32 = jnp.float32


def _dot(a, b):
    return jnp.dot(a, b, preferred_element_type=F32)


def _dot_nt(a, b):
    return lax.dot_general(a, b, (((1,), (1,)), ((), ())), preferred_element_type=F32)


def _sigmoid(x):
    return 0.5 * jnp.tanh(0.5 * x) + 0.5


def _rms(x, g):
    y = x * lax.rsqrt(jnp.mean(x * x, axis=-1, keepdims=True) + EPS)
    return y * g


def _rope(x, cos, sg, lo8):
    outs = []
    for c in range(x.shape[1] // LANES):
        xc = x[:, c * LANES:(c + 1) * LANES]
        partner = jnp.where(lo8, pltpu.roll(xc, LANES - ROT_DIMS // 2, 1), pltpu.roll(xc, ROT_DIMS // 2, 1))
        outs.append(xc * cos + partner * sg)
    return jnp.concatenate(outs, axis=1)


def _lane_masks():
    lane = lax.broadcasted_iota(jnp.int32, (1, LANES), 1)
    lo8 = (lane % HEAD_DIM) < (ROT_DIMS // 2)
    lo64 = lane < HEAD_DIM
    return lo8, lo64


def _pool_mix(u, win_sum_fn, cnt_fn, gw_ref, pscale_ref):
    mixed = []
    for g, w in enumerate(POOL_WINDOWS):
        cols = slice(g * POOL_GC, (g + 1) * POOL_GC)
        ug = u[:, cols]
        m = win_sum_fn(g, w, ug) / cnt_fn(w) - ug
        mixed.append(_dot(m.astype(BF16), gw_ref[g]) * pscale_ref[:, cols])
    return jnp.concatenate(mixed, axis=1)


def _mixer_kernel(sinks_ref, x_ref, cos_ref, sg_ref, ln1_ref, w_in_ref, gw_ref, pscale_ref, w_pb_ref, w_ab_ref,
                  w_out_ref, knt_ref, vnt_ref, ckt_ref, cvt_ref, *rest, tm, n_cast):
    cast_src = rest[:n_cast]
    h_ref, ko_ref, vo_ref, po_ref, nkt_ref, nvt_ref = rest[n_cast:n_cast + 6]
    cast_dst = rest[n_cast + 6:2 * n_cast + 6]
    kl_scr, kh_scr, vl_scr, vh_scr, u_scr, lvl_scr, attn_scr, q_scr, s_scr, g_scr = rest[2 * n_cast + 6:]
    t = pl.program_id(1)
    step = pl.program_id(0) * pl.num_programs(1) + t
    lo8, lo64 = _lane_masks()

    @pl.when(t == 0)
    def _():
        for scr in (kl_scr, kh_scr, vl_scr, vh_scr):
            scr[:, 0:BLOCK, :] = jnp.zeros((N_KV_HEADS, BLOCK, LANES), BF16)
        u_scr[0:U_HALO, :] = jnp.zeros((U_HALO, POOL_WIDTH), F32)
        lvl_scr[:, 0:SUBLANES, :] = jnp.zeros((len(POOL_WINDOWS), SUBLANES, POOL_GC), F32)

    x = x_ref[0]
    xn = _rms(x, ln1_ref[...]).astype(BF16)
    cos = cos_ref[...]
    sg = sg_ref[...]

    k = _rope(_dot(xn, w_in_ref[:, C_K:C_V]), cos, sg, lo8)
    v = _dot(xn, w_in_ref[:, C_V:C_GP])
    q = (_rope(_dot(xn, w_in_ref[:, C_Q:C_K]), cos, sg, lo8) * (HEAD_DIM ** -0.5)).astype(BF16)
    q_scr[...] = q
    u = _dot(xn, w_in_ref[:, C_U:C_Q])

    zero = jnp.zeros((tm, LANES), F32)
    for src, lo_scr, hi_scr in ((k, kl_scr, kh_scr), (v, vl_scr, vh_scr)):
        for p in range(KV_W // LANES):
            xp = src[:, p * LANES:(p + 1) * LANES]
            xs = pltpu.roll(xp, HEAD_DIM, 1)
            lo_scr[2 * p, BLOCK:, :] = jnp.where(lo64, xp, zero).astype(BF16)
            hi_scr[2 * p, BLOCK:, :] = jnp.where(lo64, zero, xs).astype(BF16)
            lo_scr[2 * p + 1, BLOCK:, :] = jnp.where(lo64, xs, zero).astype(BF16)
            hi_scr[2 * p + 1, BLOCK:, :] = jnp.where(lo64, zero, xp).astype(BF16)

    qi = lax.broadcasted_iota(jnp.int32, (BLOCK, BLOCK), 0)
    ci = lax.broadcasted_iota(jnp.int32, (BLOCK, BLOCK), 1)
    from_prev = ci > qi
    bias0 = jnp.where(jnp.logical_and(t == 0, from_prev), NEG_INF, 0.0).astype(F32)

    ones_lo = jnp.broadcast_to(jnp.where(lo64, 1.0, 0.0).astype(BF16), (2 * BLOCK, LANES))
    ones_hi = jnp.broadcast_to(jnp.where(lo64, 0.0, 1.0).astype(BF16), (2 * BLOCK, LANES))

    units = [(j, kh) for j in range(tm // BLOCK) for kh in range(N_KV_HEADS)]

    def scores(i):
        j, kh = units[i]
        rows = slice(j * BLOCK, (j + 1) * BLOCK)
        win = slice(j * BLOCK, (j + 2) * BLOCK)
        qq = jnp.concatenate([q_scr[rows, (2 * kh + a) * LANES:(2 * kh + a + 1) * LANES] for a in range(2)], axis=0)
        kcat = jnp.concatenate([kl_scr[kh, win, :], kh_scr[kh, win, :]], axis=0)
        s = _dot_nt(qq, kcat)
        for a in range(2):
            for half in range(2):
                sa = s[a * BLOCK:(a + 1) * BLOCK, half * 2 * BLOCK:(half + 1) * 2 * BLOCK]
                folded = jnp.where(from_prev, sa[:, :BLOCK], sa[:, BLOCK:])
                if j == 0:
                    folded = folded + bias0
                s_scr[i % S_SLOTS, a * BLOCK:(a + 1) * BLOCK, half * BLOCK:(half + 1) * BLOCK] = folded

    units_per_gate = len(units) * GATE_COLS // (2 * D_MODEL)
    for i in range(S_AHEAD):
        scores(i)

    def softmax(i):
        kh = units[i][1]
        ps, sink_terms = [], []
        for a in range(2):
            es, st = [], []
            for half in range(2):
                sh = s_scr[i % S_SLOTS, a * BLOCK:(a + 1) * BLOCK, half * BLOCK:(half + 1) * BLOCK]
                sink = sinks_ref[4 * kh + 2 * a + half]
                m = jnp.maximum(jnp.max(sh, axis=1, keepdims=True), sink)
                e = jnp.exp(sh - m)
                es.append(jnp.where(from_prev, e, 0.0).astype(BF16))
                es.append(jnp.where(from_prev, 0.0, e).astype(BF16))
                st.append(jnp.exp(sink - m))
            ps.append(jnp.concatenate(es, axis=1))
            sink_terms.append(jnp.where(lo64, st[0], st[1]))
        return jnp.concatenate(ps, axis=0), sink_terms

    def weighted_values(i, p, sink_terms):
        j, kh = units[i]
        rows = slice(j * BLOCK, (j + 1) * BLOCK)
        win = slice(j * BLOCK, (j + 2) * BLOCK)
        vcat = jnp.concatenate([
            jnp.concatenate([vl_scr[kh, win, :], ones_lo], axis=1),
            jnp.concatenate([vh_scr[kh, win, :], ones_hi], axis=1)], axis=0)
        o = _dot(p, vcat)
        for a in range(2):
            oa = o[a * BLOCK:(a + 1) * BLOCK]
            attn_scr[rows, (2 * kh + a) * LANES:(2 * kh + a + 1) * LANES] = (
                oa[:, :LANES] / (oa[:, LANES:] + sink_terms[a])).astype(BF16)

    pending = None
    for i in range(len(units)):
        if i + S_AHEAD < len(units):
            scores(i + S_AHEAD)
        current = (i,) + softmax(i)
        if pending is not None:
            weighted_values(*pending)
        pending = current
        if i % units_per_gate == units_per_gate - 1:
            gcols = slice((i // units_per_gate) * GATE_COLS, (i // units_per_gate + 1) * GATE_COLS)
            g_scr[:, gcols] = _sigmoid(_dot(xn, w_in_ref[:, C_GP + gcols.start:C_GP + gcols.stop]))
    weighted_values(*pending)

    u_scr[U_HALO:, :] = u
    po_ref[0] = u_scr[U_HALO + tm - POOL_STATE:U_HALO + tm, :]
    pos = t * tm + lax.broadcasted_iota(jnp.int32, (tm, 1), 0)

    def win_sum(g, w, ug):
        cols = slice(g * POOL_GC, (g + 1) * POOL_GC)
        n = U_HALO - SUBLANES + tm
        src, span = u_scr, 1
        while span < w:
            lvl = src[SUBLANES:SUBLANES + n, cols] + src[SUBLANES - span:SUBLANES - span + n, cols]
            span *= 2
            if span < w:
                lvl_scr[g, SUBLANES:SUBLANES + n, :] = lvl
                src, cols = lvl_scr.at[g], slice(None)
        return lvl[U_HALO - SUBLANES:, :]

    pooled = _pool_mix(u, win_sum, lambda w: jnp.minimum(w, pos + 1).astype(F32), gw_ref, pscale_ref)

    for scr in (kl_scr, kh_scr, vl_scr, vh_scr):
        scr[:, 0:BLOCK, :] = scr[:, tm:tm + BLOCK, :]
    u_scr[0:U_HALO, :] = u_scr[tm:tm + U_HALO, :]

    _shift_caches(step, ckt_ref.shape[0], knt_ref, vnt_ref, ckt_ref, cvt_ref, nkt_ref, nvt_ref)
    for src, dst in zip(cast_src, cast_dst):
        dst[...] = src[...].astype(BF16)
    ko_ref[0] = k[tm - WINDOW:, :].T
    vo_ref[0] = v[tm - WINDOW:, :].T

    merged = g_scr[:, 0:D_MODEL] * _dot(pooled.astype(BF16), w_pb_ref[...])
    merged = merged + g_scr[:, D_MODEL:2 * D_MODEL] * _dot(attn_scr[...], w_ab_ref[...])
    h_ref[0] = x + _dot(merged.astype(BF16), w_out_ref[...])


def _const_spec(shape):
    nd = len(shape)
    return pl.BlockSpec(shape, lambda *_: (0,) * nd, pipeline_mode=pl.Buffered(1))


def _cast_block_rows(rows, steps):
    br = 2 * SUBLANES
    while rows % br or rows // br > steps:
        br *= 2
    return br


def _cast_specs(to_cast, steps, step_of):
    specs = []
    for w in to_cast:
        br = _cast_block_rows(w.shape[0], steps)
        last = w.shape[0] // br - 1
        specs.append(pl.BlockSpec(
            (br, w.shape[1]), lambda *idx, last=last: (jnp.minimum(step_of(*idx), last), 0)))
    return specs


def _prompt_mixer(x, cos, sg, sinks, ln1, w_in, gw, pscale, w_pb, w_ab, w_out, knt, vnt, ckt, cvt, to_cast, tm):
    b, s, d = x.shape
    nt = s // tm
    cast_specs = _cast_specs(to_cast, b * nt, lambda bi, ti, *_: bi * nt + ti)
    nb = ckt.shape[0] // (b * nt)
    assert nb * b * nt == ckt.shape[0]
    cache_spec = pl.BlockSpec((nb,) + ckt.shape[1:], lambda bi, ti, *_: (bi * nt + ti, 0, 0))
    grid_spec = pltpu.PrefetchScalarGridSpec(
        num_scalar_prefetch=1,
        grid=(b, nt),
        in_specs=[
            pl.BlockSpec((1, tm, d), lambda bi, ti, *_: (bi, ti, 0)),
            pl.BlockSpec((tm, LANES), lambda bi, ti, *_: (ti, 0)),
            pl.BlockSpec((tm, LANES), lambda bi, ti, *_: (ti, 0)),
            _const_spec((1, d)),
            _const_spec(w_in.shape),
            _const_spec(gw.shape),
            _const_spec((1, POOL_WIDTH)),
            _const_spec(w_pb.shape),
            _const_spec(w_ab.shape),
            _const_spec(w_out.shape),
            _const_spec(knt.shape),
            _const_spec(vnt.shape),
            cache_spec,
            cache_spec,
        ] + cast_specs,
        out_specs=[
            pl.BlockSpec((1, tm, d), lambda bi, ti, *_: (bi, ti, 0)),
            pl.BlockSpec((1, KV_W, WINDOW), lambda bi, ti, *_: (bi, 0, 0)),
            pl.BlockSpec((1, KV_W, WINDOW), lambda bi, ti, *_: (bi, 0, 0)),
            pl.BlockSpec((1, POOL_STATE, POOL_WIDTH), lambda bi, ti, *_: (bi, 0, 0)),
            cache_spec,
            cache_spec,
        ] + cast_specs,
        scratch_shapes=[
            pltpu.VMEM((N_KV_HEADS, BLOCK + tm, LANES), BF16),
            pltpu.VMEM((N_KV_HEADS, BLOCK + tm, LANES), BF16),
            pltpu.VMEM((N_KV_HEADS, BLOCK + tm, LANES), BF16),
            pltpu.VMEM((N_KV_HEADS, BLOCK + tm, LANES), BF16),
            pltpu.VMEM((U_HALO + tm, POOL_WIDTH), F32),
            pltpu.VMEM((len(POOL_WINDOWS), U_HALO + tm, POOL_GC), F32),
            pltpu.VMEM((tm, Q_W), BF16),
            pltpu.VMEM((tm, Q_W), BF16),
            pltpu.VMEM((S_SLOTS, 2 * BLOCK, 2 * BLOCK), F32),
            pltpu.VMEM((tm, 2 * D_MODEL), F32),
        ],
    )
    return pl.pallas_call(
        functools.partial(_mixer_kernel, tm=tm, n_cast=len(to_cast)),
        grid_spec=grid_spec,
        out_shape=[
            jax.ShapeDtypeStruct((b, s, d), F32),
            jax.ShapeDtypeStruct((b, KV_W, WINDOW), F32),
            jax.ShapeDtypeStruct((b, KV_W, WINDOW), F32),
            jax.ShapeDtypeStruct((b, POOL_STATE, POOL_WIDTH), F32),
            jax.ShapeDtypeStruct(ckt.shape, F32),
            jax.ShapeDtypeStruct(cvt.shape, F32),
        ] + [jax.ShapeDtypeStruct(w.shape, BF16) for w in to_cast],
        compiler_params=pltpu.CompilerParams(
            dimension_semantics=("arbitrary", "arbitrary"), vmem_limit_bytes=VMEM_LIMIT),
        name="prompt_mixer",
    )(sinks, x, cos, sg, ln1, w_in, gw, pscale, w_pb, w_ab, w_out, knt, vnt, ckt, cvt, *to_cast)


FFN_CHUNKS = ((0, 1024), (1024, 2048), (2048, FFN_HIDDEN))


def _ffn_kernel(h_ref, p_ref, hs_ref, ps_ref, ln2_ref, w1_ref, w2_ref, w_pp_ref, pn_ref, w_pg_ref, fn_ref, y_ref,
                ys_ref, *, tm):
    ns = hs_ref.shape[0]
    pad = (-ns) % (2 * SUBLANES)

    def tile_stages(r0, with_sample):
        rows = slice(r0, r0 + tm)
        st = {}

        def gather(ref, ref_s):
            if not with_sample:
                return ref[rows, :]
            parts = [ref[rows, :], ref_s[...]] + ([jnp.zeros((pad, ref.shape[1]), F32)] if pad else [])
            return jnp.concatenate(parts, axis=0)

        def up_proj(c):
            lo, hi = FFN_CHUNKS[c]
            if c == 0:
                st['h'] = gather(h_ref, hs_ref)
                st['hn'] = _rms(st['h'], ln2_ref[...]).astype(BF16)
                st['acc'] = st['h']
            st['gate', c] = _dot(st['hn'], w1_ref[:, lo:hi])
            st['up', c] = _dot(st['hn'], w1_ref[:, FFN_HIDDEN + lo:FFN_HIDDEN + hi])
            if c == len(FFN_CHUNKS) - 1:
                st['e'] = _rms(_dot(gather(p_ref, ps_ref).astype(BF16), w_pp_ref[...]), pn_ref[...])

        def down_proj(c):
            lo, hi = FFN_CHUNKS[c]
            gate = st.pop(('gate', c))
            act = (gate * _sigmoid(gate) * st.pop(('up', c))).astype(BF16)
            st['acc'] = st['acc'] + _dot(act, w2_ref[lo:hi, :])

        def ple_gate():
            st['g'] = _dot(st['acc'].astype(BF16), w_pg_ref[...])

        def finish():
            h3 = st['acc'] + _sigmoid(st['g']) * st['e']
            y = _rms(h3, fn_ref[...])
            y_ref[rows, :] = y[:tm]
            if with_sample:
                ys_ref[...] = y[tm:tm + ns]

        n = len(FFN_CHUNKS)
        steps = [functools.partial(up_proj, 0)]
        for c in range(1, n):
            steps += [functools.partial(up_proj, c), functools.partial(down_proj, c - 1)]
        return steps + [functools.partial(down_proj, n - 1), ple_gate, finish]

    tail = 3
    order = []
    starts = list(range(0, h_ref.shape[0], tm))
    for r0 in starts:
        steps = tile_stages(r0, r0 == starts[-1])
        held, order = order[len(order) - tail:] if order else [], order[:len(order) - tail] if order else []
        for k in range(max(len(held), tail)):
            order += steps[k:k + 1] + held[k:k + 1]
        order += steps[tail:]
    for step in order:
        step()


def _ffn(h, p, hs, ps, ln2, w1, w2, w_pp, pn, w_pg, fn, tm, nsub):
    n, d = h.shape
    blk = tm * nsub
    ns = hs.shape[0] // (n // blk)
    assert ns * (n // blk) == hs.shape[0] and ns % SUBLANES == 0
    return pl.pallas_call(
        functools.partial(_ffn_kernel, tm=tm),
        grid=(n // blk,),
        in_specs=[
            pl.BlockSpec((blk, d), lambda i: (i, 0)),
            pl.BlockSpec((blk, PLE_DIM), lambda i: (i, 0)),
            pl.BlockSpec((ns, d), lambda i: (i, 0)),
            pl.BlockSpec((ns, PLE_DIM), lambda i: (i, 0)),
            _const_spec((1, d)),
            _const_spec(w1.shape),
            _const_spec(w2.shape),
            _const_spec(w_pp.shape),
            _const_spec((1, d)),
            _const_spec(w_pg.shape),
            _const_spec((1, d)),
        ],
        out_specs=[pl.BlockSpec((blk, d), lambda i: (i, 0)), pl.BlockSpec((ns, d), lambda i: (i, 0))],
        out_shape=[jax.ShapeDtypeStruct((n, d), F32), jax.ShapeDtypeStruct(hs.shape, F32)],
        compiler_params=pltpu.CompilerParams(dimension_semantics=("arbitrary",), vmem_limit_bytes=VMEM_LIMIT),
        name="ffn_ple_norm",
    )(h, p, hs, ps, ln2, w1, w2, w_pp, pn, w_pg, fn)


def _sample_pre_kernel(x_ref, cos_ref, sg_ref, ln1_ref, w_in_ref, qe_ref, knt_ref, vnt_ref, kn_ref, vn_ref, u_ref):
    n = x_ref.shape[0]
    lo8, _ = _lane_masks()
    xn = _rms(x_ref[...], ln1_ref[...]).astype(BF16)
    cos = cos_ref[...]
    sg = sg_ref[...]
    w = lambda lo, hi: w_in_ref[:, lo:hi].astype(BF16)
    u_ref[...] = _dot(xn, w(C_U, C_Q))
    q = (_rope(_dot(xn, w(C_Q, C_K)), cos, sg, lo8) * (HEAD_DIM ** -0.5)).astype(BF16)
    kn = _rope(_dot(xn, w(C_K, C_V)), cos, sg, lo8)
    vn = _dot(xn, w(C_V, C_GP))
    kn_ref[...] = kn
    vn_ref[...] = vn
    knt_ref[...] = kn.T
    vnt_ref[...] = vn.T
    _, lo64 = _lane_masks()
    qf = q.astype(F32)
    zero = jnp.zeros((n, LANES), F32)
    for r in range(N_HEADS):
        kh = r // GROUP
        chunk = qf[:, (r // 2) * LANES:(r // 2 + 1) * LANES]
        if r % 2 != kh % 2:
            chunk = pltpu.roll(chunk, HEAD_DIM, 1)
        placed = jnp.where(lo64, chunk, zero) if kh % 2 == 0 else jnp.where(lo64, zero, chunk)
        for c in range(KV_W // LANES):
            qe_ref[c, pl.ds(r, n, stride=N_HEADS), :] = placed if c == kh // 2 else zero


def _sample_pre(x, cos, sg, ln1, w_in):
    n, d = x.shape
    return pl.pallas_call(
        _sample_pre_kernel,
        grid=(1,),
        in_specs=[
            _const_spec((n, d)),
            _const_spec((1, LANES)),
            _const_spec((1, LANES)),
            _const_spec((1, d)),
            pl.BlockSpec((d, C_GP), lambda i: (0, 0), pipeline_mode=pl.Buffered(1)),
        ],
        out_specs=[
            pl.BlockSpec((KV_W // LANES, n * N_HEADS, LANES), lambda i: (0, 0, 0)),
            pl.BlockSpec((KV_W, n), lambda i: (0, 0)),
            pl.BlockSpec((KV_W, n), lambda i: (0, 0)),
            pl.BlockSpec((n, KV_W), lambda i: (0, 0)),
            pl.BlockSpec((n, KV_W), lambda i: (0, 0)),
            pl.BlockSpec((n, POOL_WIDTH), lambda i: (0, 0)),
        ],
        out_shape=[
            jax.ShapeDtypeStruct((KV_W // LANES, n * N_HEADS, LANES), F32),
            jax.ShapeDtypeStruct((KV_W, n), F32),
            jax.ShapeDtypeStruct((KV_W, n), F32),
            jax.ShapeDtypeStruct((n, KV_W), F32),
            jax.ShapeDtypeStruct((n, KV_W), F32),
            jax.ShapeDtypeStruct((n, POOL_WIDTH), F32),
        ],
        compiler_params=pltpu.CompilerParams(dimension_semantics=("arbitrary",), vmem_limit_bytes=VMEM_LIMIT),
        name="sample_pre",
    )(x, cos, sg, ln1, w_in)


def _shift_caches(step, nb, knt_ref, vnt_ref, ckt_ref, cvt_ref, nkt_ref, nvt_ref):
    n = knt_ref.shape[1]
    w_cache = ckt_ref.shape[2]
    newest = lax.broadcasted_iota(jnp.int32, (1, w_cache), 1) == w_cache - 1
    shift = lax.rem(n - lax.rem(step * nb, n), n)
    kcols = pltpu.roll(knt_ref[...], shift, 1)
    vcols = pltpu.roll(vnt_ref[...], shift, 1)
    for bl in range(nb):
        nkt_ref[bl] = jnp.where(newest, kcols[:, bl:bl + 1], pltpu.roll(ckt_ref[bl], w_cache - 1, 1))
        nvt_ref[bl] = jnp.where(newest, vcols[:, bl:bl + 1], pltpu.roll(cvt_ref[bl], w_cache - 1, 1))


def _sample_attn_kernel(qe_ref, kn_ref, vn_ref, ckt_ref, cvt_ref, sink_ref, *rest, bb, n_cast):
    cast_src = rest[:n_cast]
    o_ref = rest[n_cast]
    cast_dst = rest[n_cast + 1:]
    for src, dst in zip(cast_src, cast_dst):
        dst[...] = src[...].astype(BF16)
    w_cache = ckt_ref.shape[2]
    oldest = lax.broadcasted_iota(jnp.int32, (1, w_cache), 1) == 0
    sink = sink_ref[...]
    rounded = lambda a: a.astype(BF16).astype(F32)
    scores = []
    for bl in range(bb):
        rows = slice(bl * N_HEADS, (bl + 1) * N_HEADS)
        qb = jnp.concatenate([qe_ref[c, rows, :] for c in range(KV_W // LANES)], axis=1).astype(BF16)
        s_old = jnp.where(oldest, NEG_INF, _dot(qb, ckt_ref[bl].astype(BF16)))
        s_new = jnp.sum(qb.astype(F32) * rounded(kn_ref[bl:bl + 1, :]), axis=1, keepdims=True)
        scores.append((s_old, s_new))
    probs = []
    for s_old, s_new in scores:
        m = jnp.maximum(jnp.maximum(jnp.max(s_old, axis=-1, keepdims=True), s_new), sink)
        e_old = jnp.exp(s_old - m)
        e_new = jnp.exp(s_new - m)
        denom = jnp.sum(e_old, axis=-1, keepdims=True) + e_new + jnp.exp(sink - m)
        probs.append((e_old.astype(BF16), rounded(e_new), denom))
    for bl, (e_old, e_new, denom) in enumerate(probs):
        rows = slice(bl * N_HEADS, (bl + 1) * N_HEADS)
        o = (_dot_nt(e_old, cvt_ref[bl].astype(BF16)) + e_new * rounded(vn_ref[bl:bl + 1, :])) / denom
        for c in range(KV_W // LANES):
            o_ref[c, rows, :] = o[:, c * LANES:(c + 1) * LANES]


def _sample_attn(qe, kn, vn, ckt, cvt, sink, to_cast, bb):
    n, _, w_cache = ckt.shape
    cast_specs = _cast_specs(to_cast, n // bb, lambda i: i)
    cache_spec = pl.BlockSpec((bb, KV_W, w_cache), lambda i: (i, 0, 0))
    new_spec = pl.BlockSpec((bb, KV_W), lambda i: (i, 0))
    head_spec = pl.BlockSpec((KV_W // LANES, bb * N_HEADS, LANES), lambda i: (0, i, 0))
    return pl.pallas_call(
        functools.partial(_sample_attn_kernel, bb=bb, n_cast=len(to_cast)),
        grid=(n // bb,),
        in_specs=[head_spec, new_spec, new_spec, cache_spec, cache_spec, _const_spec((N_HEADS, 1))] + cast_specs,
        out_specs=[head_spec] + cast_specs,
        out_shape=[jax.ShapeDtypeStruct((KV_W // LANES, n * N_HEADS, LANES), F32)]
        + [jax.ShapeDtypeStruct(w.shape, BF16) for w in to_cast],
        compiler_params=pltpu.CompilerParams(dimension_semantics=("arbitrary",), vmem_limit_bytes=VMEM_LIMIT),
        name="sample_attn",
    )(qe, kn, vn, ckt, cvt, sink, *to_cast)


def _sample_post_kernel(x_ref, u_ref, st_ref, o_ref, ln1_ref, w_g_ref, gw_ref, pscale_ref, w_pb_ref, w_ab_ref,
                        w_out_ref, h_ref, nst_ref):
    x = x_ref[...]
    n = x.shape[0]
    xn = _rms(x, ln1_ref[...]).astype(BF16)
    u = u_ref[...]
    nst_ref[0:POOL_STATE - 1] = st_ref[1:POOL_STATE]
    nst_ref[POOL_STATE - 1] = u

    def win_sum(g, w, ug):
        acc = ug
        for i in range(1, w):
            acc = acc + st_ref[POOL_STATE - i, :, g * POOL_GC:(g + 1) * POOL_GC]
        return acc

    pooled = _pool_mix(u, win_sum, lambda w: jnp.float32(min(w, PAST_LEN + 1)), gw_ref, pscale_ref)
    merged = _sigmoid(_dot(xn, w_g_ref[:, 0:D_MODEL])) * _dot(pooled.astype(BF16), w_pb_ref[...])
    kv_of_lane = lax.broadcasted_iota(jnp.int32, (1, KV_W), 1) // HEAD_DIM
    ab = jnp.zeros((n, D_MODEL), F32)
    for g in range(GROUP):
        row_g = jnp.zeros((n, KV_W), F32)
        for kh in range(N_KV_HEADS):
            r = kh * GROUP + g
            o_r = jnp.concatenate([o_ref[c, pl.ds(r, n, stride=N_HEADS), :] for c in range(KV_W // LANES)], axis=1)
            row_g = jnp.where(kv_of_lane == kh, o_r, row_g)
        w_g = jnp.concatenate([w_ab_ref[(kh * GROUP + g) * HEAD_DIM:(kh * GROUP + g + 1) * HEAD_DIM, :]
                               for kh in range(N_KV_HEADS)], axis=0)
        ab = ab + _dot(row_g.astype(BF16), w_g)
    merged = merged + _sigmoid(_dot(xn, w_g_ref[:, D_MODEL:2 * D_MODEL])) * ab
    h_ref[...] = x + _dot(merged.astype(BF16), w_out_ref[...])


def _sample_post(x, u, st, o, ln1, w_in, gw, pscale, w_pb, w_ab, w_out):
    n, d = x.shape
    return pl.pallas_call(
        _sample_post_kernel,
        grid=(1,),
        in_specs=[
            _const_spec((n, d)),
            _const_spec((n, POOL_WIDTH)),
            _const_spec(st.shape),
            _const_spec((KV_W // LANES, n * N_HEADS, LANES)),
            _const_spec((1, d)),
            pl.BlockSpec((d, 2 * D_MODEL), lambda i: (0, 1), pipeline_mode=pl.Buffered(1)),
            _const_spec(gw.shape),
            _const_spec((1, POOL_WIDTH)),
            _const_spec(w_pb.shape),
            _const_spec(w_ab.shape),
            _const_spec(w_out.shape),
        ],
        out_specs=[pl.BlockSpec((n, d), lambda i: (0, 0)), pl.BlockSpec(st.shape, lambda i: (0, 0, 0))],
        out_shape=[jax.ShapeDtypeStruct((n, d), F32), jax.ShapeDtypeStruct(st.shape, F32)],
        compiler_params=pltpu.CompilerParams(dimension_semantics=("arbitrary",), vmem_limit_bytes=VMEM_LIMIT),
        name="sample_post",
    )(x, u, st, o, ln1, w_in, gw, pscale, w_pb, w_ab, w_out)


def _rope_tables(first_pos, n):
    half = ROT_DIMS // 2
    inv = ROPE_THETA ** (-(np.arange(0, ROT_DIMS, 2, dtype=np.float64) / ROT_DIMS))
    ang = np.arange(first_pos, first_pos + n, dtype=np.float64)[:, None] * inv[None, :]
    cos, sin = np.cos(ang), np.sin(ang)
    rest = HEAD_DIM - 2 * half
    c64 = np.concatenate([cos, cos, np.ones((n, rest))], axis=1)
    s64 = np.concatenate([-sin, sin, np.zeros((n, rest))], axis=1)
    reps = LANES // HEAD_DIM
    return jnp.asarray(np.tile(c64, (1, reps)), F32), jnp.asarray(np.tile(s64, (1, reps)), F32)


def kernel(x_prompt, x_sample, p_prompt, p_sample, cache_k, cache_v, state_pool, ln1, w_in, pool_group_w, pool_scale,
           attn_sinks, w_pool_branch, w_attn_branch, w_out, ln2, w_ffn_in, w_ffn_out, w_ple_proj, ple_norm,
           w_ple_gate, final_norm):
    depth = ln1.shape[0]
    b, s, d = x_prompt.shape
    bd, t_dec, _ = x_sample.shape
    w_cache = cache_k.shape[2]
    assert depth == 1 and t_dec == 1 and w_cache == WINDOW and s % BLOCK == 0 and d == D_MODEL
    tm = 512
    assert s % tm == 0

    cos_p, sg_p = _rope_tables(0, s)
    cos_s, sg_s = _rope_tables(PAST_LEN, t_dec)

    hp = x_prompt
    hs = x_sample.reshape(bd, d)
    row = lambda a: a.reshape(1, -1)
    nkp, nvp, npp, nks, nvs, nps = [], [], [], [], [], []
    for i in range(depth):
        to_fm = lambda c: jnp.transpose(c, (0, 2, 3, 1)).reshape(bd, KV_W, w_cache)
        from_fm = lambda c: jnp.transpose(c.reshape(bd, N_KV_HEADS, HEAD_DIM, w_cache), (0, 3, 1, 2))
        ckt, cvt = to_fm(cache_k[i]), to_fm(cache_v[i])
        qe, knt, vnt, kn, vn, un = _sample_pre(hs, cos_s, sg_s, row(ln1[i]), w_in[i])
        o, wi, gw, wpb, wab, wo = _sample_attn(
            qe, kn, vn, ckt, cvt, attn_sinks[i].reshape(N_HEADS, 1),
            (w_in[i], pool_group_w[i].reshape(POOL_WIDTH, POOL_GC), w_pool_branch[i], w_attn_branch[i], w_out[i]), 16)
        gw = gw.reshape(len(POOL_WINDOWS), POOL_GC, POOL_GC)
        h1, kp, vp, pp, nkt, nvt, w1, w2, wpp, wpg = _prompt_mixer(
            hp, cos_p, sg_p, attn_sinks[i], row(ln1[i]), wi, gw, row(pool_scale[i]), wpb, wab, wo,
            knt, vnt, ckt, cvt, (w_ffn_in[i], w_ffn_out[i], w_ple_proj[i], w_ple_gate[i]), tm)
        h1s, nst = _sample_post(hs, un, jnp.transpose(state_pool[i], (1, 0, 2)), o, row(ln1[i]), wi, gw,
                                row(pool_scale[i]), wpb, wab, wo)
        hp, hs = _ffn(h1.reshape(b * s, d), p_prompt[i].reshape(b * s, PLE_DIM), h1s,
                      p_sample[i].reshape(bd * t_dec, PLE_DIM), row(ln2[i]), w1, w2, wpp, row(ple_norm[i]), wpg,
                      row(final_norm), tm, 2)
        hp = hp.reshape(b, s, d)
        from_fm_p = lambda c: jnp.transpose(c.reshape(b, N_KV_HEADS, HEAD_DIM, w_cache), (0, 3, 1, 2))
        nkp.append(from_fm_p(kp))
        nvp.append(from_fm_p(vp))
        npp.append(pp)
        nks.append(from_fm(nkt))
        nvs.append(from_fm(nvt))
        nps.append(jnp.transpose(nst, (1, 0, 2)))

    return (hp, hs.reshape(bd, t_dec, d), jnp.stack(nkp), jnp.stack(nvp), jnp.stack(npp),
            jnp.stack(nks), jnp.stack(nvs), jnp.stack(nps))
```

```python
import functools

import jax
import jax.numpy as jnp
import numpy as np
from jax import lax
from jax.experimental import pallas as pl
from jax.experimental.pallas import tpu as pltpu

D_MODEL = 1024
HEAD_DIM = 64
N_HEADS = D_MODEL // HEAD_DIM
N_KV_HEADS = N_HEADS // 4
GROUP = N_HEADS // N_KV_HEADS
ROT_DIMS = HEAD_DIM // 4
ROPE_THETA = 500000.0
WINDOW = 128
BLOCK = 128
POOL_WIDTH = D_MODEL // 2
POOL_WINDOWS = (2, 4, 8, 16)
POOL_GC = POOL_WIDTH // len(POOL_WINDOWS)
POOL_STATE = max(POOL_WINDOWS) - 1
FFN_HIDDEN = -(-8 * D_MODEL // (3 * 256)) * 256
PLE_DIM = 256
EPS = 1e-6
NEG_INF = -1e30
PAST_LEN = 16384

Q_W = N_HEADS * HEAD_DIM
KV_W = N_KV_HEADS * HEAD_DIM
C_U, C_Q, C_K, C_V, C_GP, C_GA, C_END = 0, POOL_WIDTH, POOL_WIDTH + Q_W, POOL_WIDTH + Q_W + KV_W, \
    POOL_WIDTH + Q_W + 2 * KV_W, POOL_WIDTH + Q_W + 2 * KV_W + D_MODEL, POOL_WIDTH + Q_W + 2 * KV_W + 2 * D_MODEL

LANES = 128
S_AHEAD = 16
S_SLOTS = 16
GATE_COLS = 256
SUBLANES = 8
U_HALO = 24
VMEM_LIMIT = 56 * 1024 * 1024

BF16 = jnp.bfloat16
F32 = jnp.float32


def _dot(a, b):
    return jnp.dot(a, b, preferred_element_type=F32)


def _dot_nt(a, b):
    return lax.dot_general(a, b, (((1,), (1,)), ((), ())), preferred_element_type=F32)


def _sigmoid(x):
    return 0.5 * jnp.tanh(0.5 * x) + 0.5


def _rms(x, g):
    y = x * lax.rsqrt(jnp.mean(x * x, axis=-1, keepdims=True) + EPS)
    return y * g


def _rope(x, cos, sg, lo8):
    outs = []
    for c in range(x.shape[1] // LANES):
        xc = x[:, c * LANES:(c + 1) * LANES]
        partner = jnp.where(lo8, pltpu.roll(xc, LANES - ROT_DIMS // 2, 1), pltpu.roll(xc, ROT_DIMS // 2, 1))
        outs.append(xc * cos + partner * sg)
    return jnp.concatenate(outs, axis=1)


def _lane_masks():
    lane = lax.broadcasted_iota(jnp.int32, (1, LANES), 1)
    lo8 = (lane % HEAD_DIM) < (ROT_DIMS // 2)
    lo64 = lane < HEAD_DIM
    return lo8, lo64


def _pool_mix(u, win_sum_fn, cnt_fn, gw_ref, pscale_ref):
    mixed = []
    for g, w in enumerate(POOL_WINDOWS):
        cols = slice(g * POOL_GC, (g + 1) * POOL_GC)
        ug = u[:, cols]
        m = win_sum_fn(g, w, ug) / cnt_fn(w) - ug
        mixed.append(_dot(m.astype(BF16), gw_ref[g]) * pscale_ref[:, cols])
    return jnp.concatenate(mixed, axis=1)


def _mixer_kernel(sinks_ref, x_ref, cos_ref, sg_ref, ln1_ref, w_in_ref, gw_ref, pscale_ref, w_pb_ref, w_ab_ref,
                  w_out_ref, knt_ref, vnt_ref, ckt_ref, cvt_ref, *rest, tm, n_cast):
    cast_src = rest[:n_cast]
    h_ref, ko_ref, vo_ref, po_ref, nkt_ref, nvt_ref = rest[n_cast:n_cast + 6]
    cast_dst = rest[n_cast + 6:2 * n_cast + 6]
    kl_scr, kh_scr, vl_scr, vh_scr, u_scr, lvl_scr, attn_scr, q_scr, s_scr, g_scr = rest[2 * n_cast + 6:]
    t = pl.program_id(1)
    step = pl.program_id(0) * pl.num_programs(1) + t
    lo8, lo64 = _lane_masks()

    @pl.when(t == 0)
    def _():
        for scr in (kl_scr, kh_scr, vl_scr, vh_scr):
            scr[:, 0:BLOCK, :] = jnp.zeros((N_KV_HEADS, BLOCK, LANES), BF16)
        u_scr[0:U_HALO, :] = jnp.zeros((U_HALO, POOL_WIDTH), F32)
        lvl_scr[:, 0:SUBLANES, :] = jnp.zeros((len(POOL_WINDOWS), SUBLANES, POOL_GC), F32)

    x = x_ref[0]
    xn = _rms(x, ln1_ref[...]).astype(BF16)
    cos = cos_ref[...]
    sg = sg_ref[...]

    k = _rope(_dot(xn, w_in_ref[:, C_K:C_V]), cos, sg, lo8)
    v = _dot(xn, w_in_ref[:, C_V:C_GP])
    q = (_rope(_dot(xn, w_in_ref[:, C_Q:C_K]), cos, sg, lo8) * (HEAD_DIM ** -0.5)).astype(BF16)
    q_scr[...] = q
    u = _dot(xn, w_in_ref[:, C_U:C_Q])

    zero = jnp.zeros((tm, LANES), F32)
    for src, lo_scr, hi_scr in ((k, kl_scr, kh_scr), (v, vl_scr, vh_scr)):
        for p in range(KV_W // LANES):
            xp = src[:, p * LANES:(p + 1) * LANES]
            xs = pltpu.roll(xp, HEAD_DIM, 1)
            lo_scr[2 * p, BLOCK:, :] = jnp.where(lo64, xp, zero).astype(BF16)
            hi_scr[2 * p, BLOCK:, :] = jnp.where(lo64, zero, xs).astype(BF16)
            lo_scr[2 * p + 1, BLOCK:, :] = jnp.where(lo64, xs, zero).astype(BF16)
            hi_scr[2 * p + 1, BLOCK:, :] = jnp.where(lo64, zero, xp).astype(BF16)

    qi = lax.broadcasted_iota(jnp.int32, (BLOCK, BLOCK), 0)
    ci = lax.broadcasted_iota(jnp.int32, (BLOCK, BLOCK), 1)
    from_prev = ci > qi
    bias0 = jnp.where(jnp.logical_and(t == 0, from_prev), NEG_INF, 0.0).astype(F32)

    ones_lo = jnp.broadcast_to(jnp.where(lo64, 1.0, 0.0).astype(BF16), (2 * BLOCK, LANES))
    ones_hi = jnp.broadcast_to(jnp.where(lo64, 0.0, 1.0).astype(BF16), (2 * BLOCK, LANES))

    units = [(j, kh) for j in range(tm // BLOCK) for kh in range(N_KV_HEADS)]

    def scores(i):
        j, kh = units[i]
        rows = slice(j * BLOCK, (j + 1) * BLOCK)
        win = slice(j * BLOCK, (j + 2) * BLOCK)
        qq = jnp.concatenate([q_scr[rows, (2 * kh + a) * LANES:(2 * kh + a + 1) * LANES] for a in range(2)], axis=0)
        kcat = jnp.concatenate([kl_scr[kh, win, :], kh_scr[kh, win, :]], axis=0)
        s = _dot_nt(qq, kcat)
        for a in range(2):
            for half in range(2):
                sa = s[a * BLOCK:(a + 1) * BLOCK, half * 2 * BLOCK:(half + 1) * 2 * BLOCK]
                folded = jnp.where(from_prev, sa[:, :BLOCK], sa[:, BLOCK:])
                if j == 0:
                    folded = folded + bias0
                s_scr[i % S_SLOTS, a * BLOCK:(a + 1) * BLOCK, half * BLOCK:(half + 1) * BLOCK] = folded

    units_per_gate = len(units) * GATE_COLS // (2 * D_MODEL)
    for i in range(S_AHEAD):
        scores(i)

    def softmax(i):
        kh = units[i][1]
        ps, sink_terms = [], []
        for a in range(2):
            es, st = [], []
            for half in range(2):
                sh = s_scr[i % S_SLOTS, a * BLOCK:(a + 1) * BLOCK, half * BLOCK:(half + 1) * BLOCK]
                sink = sinks_ref[4 * kh + 2 * a + half]
                m = jnp.maximum(jnp.max(sh, axis=1, keepdims=True), sink)
                e = jnp.exp(sh - m)
                es.append(jnp.where(from_prev, e, 0.0).astype(BF16))
                es.append(jnp.where(from_prev, 0.0, e).astype(BF16))
                st.append(jnp.exp(sink - m))
            ps.append(jnp.concatenate(es, axis=1))
            sink_terms.append(jnp.where(lo64, st[0], st[1]))
        return jnp.concatenate(ps, axis=0), sink_terms

    def weighted_values(i, p, sink_terms):
        j, kh = units[i]
        rows = slice(j * BLOCK, (j + 1) * BLOCK)
        win = slice(j * BLOCK, (j + 2) * BLOCK)
        vcat = jnp.concatenate([
            jnp.concatenate([vl_scr[kh, win, :], ones_lo], axis=1),
            jnp.concatenate([vh_scr[kh, win, :], ones_hi], axis=1)], axis=0)
        o = _dot(p, vcat)
        for a in range(2):
            oa = o[a * BLOCK:(a + 1) * BLOCK]
            attn_scr[rows, (2 * kh + a) * LANES:(2 * kh + a + 1) * LANES] = (
                oa[:, :LANES] / (oa[:, LANES:] + sink_terms[a])).astype(BF16)

    pending = None
    for i in range(len(units)):
        if i + S_AHEAD < len(units):
            scores(i + S_AHEAD)
        current = (i,) + softmax(i)
        if pending is not None:
            weighted_values(*pending)
        pending = current
        if i % units_per_gate == units_per_gate - 1:
            gcols = slice((i // units_per_gate) * GATE_COLS, (i // units_per_gate + 1) * GATE_COLS)
            g_scr[:, gcols] = _sigmoid(_dot(xn, w_in_ref[:, C_GP + gcols.start:C_GP + gcols.stop]))
    weighted_values(*pending)

    u_scr[U_HALO:, :] = u
    po_ref[0] = u_scr[U_HALO + tm - POOL_STATE:U_HALO + tm, :]
    pos = t * tm + lax.broadcasted_iota(jnp.int32, (tm, 1), 0)

    def win_sum(g, w, ug):
        cols = slice(g * POOL_GC, (g + 1) * POOL_GC)
        n = U_HALO - SUBLANES + tm
        src, span = u_scr, 1
        while span < w:
            lvl = src[SUBLANES:SUBLANES + n, cols] + src[SUBLANES - span:SUBLANES - span + n, cols]
            span *= 2
            if span < w:
                lvl_scr[g, SUBLANES:SUBLANES + n, :] = lvl
                src, cols = lvl_scr.at[g], slice(None)
        return lvl[U_HALO - SUBLANES:, :]

    pooled = _pool_mix(u, win_sum, lambda w: jnp.minimum(w, pos + 1).astype(F32), gw_ref, pscale_ref)

    for scr in (kl_scr, kh_scr, vl_scr, vh_scr):
        scr[:, 0:BLOCK, :] = scr[:, tm:tm + BLOCK, :]
    u_scr[0:U_HALO, :] = u_scr[tm:tm + U_HALO, :]

    _shift_caches(step, ckt_ref.shape[0], knt_ref, vnt_ref, ckt_ref, cvt_ref, nkt_ref, nvt_ref)
    for src, dst in zip(cast_src, cast_dst):
        dst[...] = src[...].astype(BF16)
    ko_ref[0] = k[tm - WINDOW:, :].T
    vo_ref[0] = v[tm - WINDOW:, :].T

    merged = g_scr[:, 0:D_MODEL] * _dot(pooled.astype(BF16), w_pb_ref[...])
    merged = merged + g_scr[:, D_MODEL:2 * D_MODEL] * _dot(attn_scr[...], w_ab_ref[...])
    h_ref[0] = x + _dot(merged.astype(BF16), w_out_ref[...])


def _const_spec(shape):
    nd = len(shape)
    return pl.BlockSpec(shape, lambda *_: (0,) * nd, pipeline_mode=pl.Buffered(1))


def _cast_block_rows(rows, steps):
    br = 2 * SUBLANES
    while rows % br or rows // br > steps:
        br *= 2
    return br


def _cast_specs(to_cast, steps, step_of):
    specs = []
    for w in to_cast:
        br = _cast_block_rows(w.shape[0], steps)
        last = w.shape[0] // br - 1
        specs.append(pl.BlockSpec(
            (br, w.shape[1]), lambda *idx, last=last: (jnp.minimum(step_of(*idx), last), 0)))
    return specs


def _prompt_mixer(x, cos, sg, sinks, ln1, w_in, gw, pscale, w_pb, w_ab, w_out, knt, vnt, ckt, cvt, to_cast, tm):
    b, s, d = x.shape
    nt = s // tm
    cast_specs = _cast_specs(to_cast, b * nt, lambda bi, ti, *_: bi * nt + ti)
    nb = ckt.shape[0] // (b * nt)
    assert nb * b * nt == ckt.shape[0]
    cache_spec = pl.BlockSpec((nb,) + ckt.shape[1:], lambda bi, ti, *_: (bi * nt + ti, 0, 0))
    grid_spec = pltpu.PrefetchScalarGridSpec(
        num_scalar_prefetch=1,
        grid=(b, nt),
        in_specs=[
            pl.BlockSpec((1, tm, d), lambda bi, ti, *_: (bi, ti, 0)),
            pl.BlockSpec((tm, LANES), lambda bi, ti, *_: (ti, 0)),
            pl.BlockSpec((tm, LANES), lambda bi, ti, *_: (ti, 0)),
            _const_spec((1, d)),
            _const_spec(w_in.shape),
            _const_spec(gw.shape),
            _const_spec((1, POOL_WIDTH)),
            _const_spec(w_pb.shape),
            _const_spec(w_ab.shape),
            _const_spec(w_out.shape),
            _const_spec(knt.shape),
            _const_spec(vnt.shape),
            cache_spec,
            cache_spec,
        ] + cast_specs,
        out_specs=[
            pl.BlockSpec((1, tm, d), lambda bi, ti, *_: (bi, ti, 0)),
            pl.BlockSpec((1, KV_W, WINDOW), lambda bi, ti, *_: (bi, 0, 0)),
            pl.BlockSpec((1, KV_W, WINDOW), lambda bi, ti, *_: (bi, 0, 0)),
            pl.BlockSpec((1, POOL_STATE, POOL_WIDTH), lambda bi, ti, *_: (bi, 0, 0)),
            cache_spec,
            cache_spec,
        ] + cast_specs,
        scratch_shapes=[
            pltpu.VMEM((N_KV_HEADS, BLOCK + tm, LANES), BF16),
            pltpu.VMEM((N_KV_HEADS, BLOCK + tm, LANES), BF16),
            pltpu.VMEM((N_KV_HEADS, BLOCK + tm, LANES), BF16),
            pltpu.VMEM((N_KV_HEADS, BLOCK + tm, LANES), BF16),
            pltpu.VMEM((U_HALO + tm, POOL_WIDTH), F32),
            pltpu.VMEM((len(POOL_WINDOWS), U_HALO + tm, POOL_GC), F32),
            pltpu.VMEM((tm, Q_W), BF16),
            pltpu.VMEM((tm, Q_W), BF16),
            pltpu.VMEM((S_SLOTS, 2 * BLOCK, 2 * BLOCK), F32),
            pltpu.VMEM((tm, 2 * D_MODEL), F32),
        ],
    )
    return pl.pallas_call(
        functools.partial(_mixer_kernel, tm=tm, n_cast=len(to_cast)),
        grid_spec=grid_spec,
        out_shape=[
            jax.ShapeDtypeStruct((b, s, d), F32),
            jax.ShapeDtypeStruct((b, KV_W, WINDOW), F32),
            jax.ShapeDtypeStruct((b, KV_W, WINDOW), F32),
            jax.ShapeDtypeStruct((b, POOL_STATE, POOL_WIDTH), F32),
            jax.ShapeDtypeStruct(ckt.shape, F32),
            jax.ShapeDtypeStruct(cvt.shape, F32),
        ] + [jax.ShapeDtypeStruct(w.shape, BF16) for w in to_cast],
        compiler_params=pltpu.CompilerParams(
            dimension_semantics=("arbitrary", "arbitrary"), vmem_limit_bytes=VMEM_LIMIT),
        name="prompt_mixer",
    )(sinks, x, cos, sg, ln1, w_in, gw, pscale, w_pb, w_ab, w_out, knt, vnt, ckt, cvt, *to_cast)


FFN_CHUNKS = ((0, 1024), (1024, 2048), (2048, FFN_HIDDEN))


def _ffn_kernel(h_ref, p_ref, hs_ref, ps_ref, ln2_ref, w1_ref, w2_ref, w_pp_ref, pn_ref, w_pg_ref, fn_ref, y_ref,
                ys_ref, *, tm):
    ns = hs_ref.shape[0]
    pad = (-ns) % (2 * SUBLANES)

    def tile_stages(r0, with_sample):
        rows = slice(r0, r0 + tm)
        st = {}

        def gather(ref, ref_s):
            if not with_sample:
                return ref[rows, :]
            parts = [ref[rows, :], ref_s[...]] + ([jnp.zeros((pad, ref.shape[1]), F32)] if pad else [])
            return jnp.concatenate(parts, axis=0)

        def up_proj(c):
            lo, hi = FFN_CHUNKS[c]
            if c == 0:
                st['h'] = gather(h_ref, hs_ref)
                st['hn'] = _rms(st['h'], ln2_ref[...]).astype(BF16)
                st['acc'] = st['h']
            st['gate', c] = _dot(st['hn'], w1_ref[:, lo:hi])
            st['up', c] = _dot(st['hn'], w1_ref[:, FFN_HIDDEN + lo:FFN_HIDDEN + hi])
            if c == len(FFN_CHUNKS) - 1:
                st['e'] = _rms(_dot(gather(p_ref, ps_ref).astype(BF16), w_pp_ref[...]), pn_ref[...])

        def down_proj(c):
            lo, hi = FFN_CHUNKS[c]
            gate = st.pop(('gate', c))
            act = (gate * _sigmoid(gate) * st.pop(('up', c))).astype(BF16)
            st['acc'] = st['acc'] + _dot(act, w2_ref[lo:hi, :])

        def ple_gate():
            st['g'] = _dot(st['acc'].astype(BF16), w_pg_ref[...])

        def finish():
            h3 = st['acc'] + _sigmoid(st['g']) * st['e']
            y = _rms(h3, fn_ref[...])
            y_ref[rows, :] = y[:tm]
            if with_sample:
                ys_ref[...] = y[tm:tm + ns]

        n = len(FFN_CHUNKS)
        steps = [functools.partial(up_proj, 0)]
        for c in range(1, n):
            steps += [functools.partial(up_proj, c), functools.partial(down_proj, c - 1)]
        return steps + [functools.partial(down_proj, n - 1), ple_gate, finish]

    tail = 3
    order = []
    starts = list(range(0, h_ref.shape[0], tm))
    for r0 in starts:
        steps = tile_stages(r0, r0 == starts[-1])
        held, order = order[len(order) - tail:] if order else [], order[:len(order) - tail] if order else []
        for k in range(max(len(held), tail)):
            order += steps[k:k + 1] + held[k:k + 1]
        order += steps[tail:]
    for step in order:
        step()


def _ffn(h, p, hs, ps, ln2, w1, w2, w_pp, pn, w_pg, fn, tm, nsub):
    n, d = h.shape
    blk = tm * nsub
    ns = hs.shape[0] // (n // blk)
    assert ns * (n // blk) == hs.shape[0] and ns % SUBLANES == 0
    return pl.pallas_call(
        functools.partial(_ffn_kernel, tm=tm),
        grid=(n // blk,),
        in_specs=[
            pl.BlockSpec((blk, d), lambda i: (i, 0)),
            pl.BlockSpec((blk, PLE_DIM), lambda i: (i, 0)),
            pl.BlockSpec((ns, d), lambda i: (i, 0)),
            pl.BlockSpec((ns, PLE_DIM), lambda i: (i, 0)),
            _const_spec((1, d)),
            _const_spec(w1.shape),
            _const_spec(w2.shape),
            _const_spec(w_pp.shape),
            _const_spec((1, d)),
            _const_spec(w_pg.shape),
            _const_spec((1, d)),
        ],
        out_specs=[pl.BlockSpec((blk, d), lambda i: (i, 0)), pl.BlockSpec((ns, d), lambda i: (i, 0))],
        out_shape=[jax.ShapeDtypeStruct((n, d), F32), jax.ShapeDtypeStruct(hs.shape, F32)],
        compiler_params=pltpu.CompilerParams(dimension_semantics=("arbitrary",), vmem_limit_bytes=VMEM_LIMIT),
        name="ffn_ple_norm",
    )(h, p, hs, ps, ln2, w1, w2, w_pp, pn, w_pg, fn)


def _sample_pre_kernel(x_ref, cos_ref, sg_ref, ln1_ref, w_in_ref, qe_ref, knt_ref, vnt_ref, kn_ref, vn_ref, u_ref):
    n = x_ref.shape[0]
    lo8, _ = _lane_masks()
    xn = _rms(x_ref[...], ln1_ref[...]).astype(BF16)
    cos = cos_ref[...]
    sg = sg_ref[...]
    w = lambda lo, hi: w_in_ref[:, lo:hi].astype(BF16)
    u_ref[...] = _dot(xn, w(C_U, C_Q))
    q = (_rope(_dot(xn, w(C_Q, C_K)), cos, sg, lo8) * (HEAD_DIM ** -0.5)).astype(BF16)
    kn = _rope(_dot(xn, w(C_K, C_V)), cos, sg, lo8)
    vn = _dot(xn, w(C_V, C_GP))
    kn_ref[...] = kn
    vn_ref[...] = vn
    knt_ref[...] = kn.T
    vnt_ref[...] = vn.T
    _, lo64 = _lane_masks()
    qf = q.astype(F32)
    zero = jnp.zeros((n, LANES), F32)
    for r in range(N_HEADS):
        kh = r // GROUP
        chunk = qf[:, (r // 2) * LANES:(r // 2 + 1) * LANES]
        if r % 2 != kh % 2:
            chunk = pltpu.roll(chunk, HEAD_DIM, 1)
        placed = jnp.where(lo64, chunk, zero) if kh % 2 == 0 else jnp.where(lo64, zero, chunk)
        for c in range(KV_W // LANES):
            qe_ref[c, pl.ds(r, n, stride=N_HEADS), :] = placed if c == kh // 2 else zero


def _sample_pre(x, cos, sg, ln1, w_in):
    n, d = x.shape
    return pl.pallas_call(
        _sample_pre_kernel,
        grid=(1,),
        in_specs=[
            _const_spec((n, d)),
            _const_spec((1, LANES)),
            _const_spec((1, LANES)),
            _const_spec((1, d)),
            pl.BlockSpec((d, C_GP), lambda i: (0, 0), pipeline_mode=pl.Buffered(1)),
        ],
        out_specs=[
            pl.BlockSpec((KV_W // LANES, n * N_HEADS, LANES), lambda i: (0, 0, 0)),
            pl.BlockSpec((KV_W, n), lambda i: (0, 0)),
            pl.BlockSpec((KV_W, n), lambda i: (0, 0)),
            pl.BlockSpec((n, KV_W), lambda i: (0, 0)),
            pl.BlockSpec((n, KV_W), lambda i: (0, 0)),
            pl.BlockSpec((n, POOL_WIDTH), lambda i: (0, 0)),
        ],
        out_shape=[
            jax.ShapeDtypeStruct((KV_W // LANES, n * N_HEADS, LANES), F32),
            jax.ShapeDtypeStruct((KV_W, n), F32),
            jax.ShapeDtypeStruct((KV_W, n), F32),
            jax.ShapeDtypeStruct((n, KV_W), F32),
            jax.ShapeDtypeStruct((n, KV_W), F32),
            jax.ShapeDtypeStruct((n, POOL_WIDTH), F32),
        ],
        compiler_params=pltpu.CompilerParams(dimension_semantics=("arbitrary",), vmem_limit_bytes=VMEM_LIMIT),
        name="sample_pre",
    )(x, cos, sg, ln1, w_in)


def _shift_caches(step, nb, knt_ref, vnt_ref, ckt_ref, cvt_ref, nkt_ref, nvt_ref):
    n = knt_ref.shape[1]
    w_cache = ckt_ref.shape[2]
    newest = lax.broadcasted_iota(jnp.int32, (1, w_cache), 1) == w_cache - 1
    shift = lax.rem(n - lax.rem(step * nb, n), n)
    kcols = pltpu.roll(knt_ref[...], shift, 1)
    vcols = pltpu.roll(vnt_ref[...], shift, 1)
    for bl in range(nb):
        nkt_ref[bl] = jnp.where(newest, kcols[:, bl:bl + 1], pltpu.roll(ckt_ref[bl], w_cache - 1, 1))
        nvt_ref[bl] = jnp.where(newest, vcols[:, bl:bl + 1], pltpu.roll(cvt_ref[bl], w_cache - 1, 1))


def _sample_attn_kernel(qe_ref, kn_ref, vn_ref, ckt_ref, cvt_ref, sink_ref, *rest, bb, n_cast):
    cast_src = rest[:n_cast]
    o_ref = rest[n_cast]
    cast_dst = rest[n_cast + 1:]
    for src, dst in zip(cast_src, cast_dst):
        dst[...] = src[...].astype(BF16)
    w_cache = ckt_ref.shape[2]
    oldest = lax.broadcasted_iota(jnp.int32, (1, w_cache), 1) == 0
    sink = sink_ref[...]
    rounded = lambda a: a.astype(BF16).astype(F32)
    scores = []
    for bl in range(bb):
        rows = slice(bl * N_HEADS, (bl + 1) * N_HEADS)
        qb = jnp.concatenate([qe_ref[c, rows, :] for c in range(KV_W // LANES)], axis=1).astype(BF16)
        s_old = jnp.where(oldest, NEG_INF, _dot(qb, ckt_ref[bl].astype(BF16)))
        s_new = jnp.sum(qb.astype(F32) * rounded(kn_ref[bl:bl + 1, :]), axis=1, keepdims=True)
        scores.append((s_old, s_new))
    probs = []
    for s_old, s_new in scores:
        m = jnp.maximum(jnp.maximum(jnp.max(s_old, axis=-1, keepdims=True), s_new), sink)
        e_old = jnp.exp(s_old - m)
        e_new = jnp.exp(s_new - m)
        denom = jnp.sum(e_old, axis=-1, keepdims=True) + e_new + jnp.exp(sink - m)
        probs.append((e_old.astype(BF16), rounded(e_new), denom))
    for bl, (e_old, e_new, denom) in enumerate(probs):
        rows = slice(bl * N_HEADS, (bl + 1) * N_HEADS)
        o = (_dot_nt(e_old, cvt_ref[bl].astype(BF16)) + e_new * rounded(vn_ref[bl:bl + 1, :])) / denom
        for c in range(KV_W // LANES):
            o_ref[c, rows, :] = o[:, c * LANES:(c + 1) * LANES]


def _sample_attn(qe, kn, vn, ckt, cvt, sink, to_cast, bb):
    n, _, w_cache = ckt.shape
    cast_specs = _cast_specs(to_cast, n // bb, lambda i: i)
    cache_spec = pl.BlockSpec((bb, KV_W, w_cache), lambda i: (i, 0, 0))
    new_spec = pl.BlockSpec((bb, KV_W), lambda i: (i, 0))
    head_spec = pl.BlockSpec((KV_W // LANES, bb * N_HEADS, LANES), lambda i: (0, i, 0))
    return pl.pallas_call(
        functools.partial(_sample_attn_kernel, bb=bb, n_cast=len(to_cast)),
        grid=(n // bb,),
        in_specs=[head_spec, new_spec, new_spec, cache_spec, cache_spec, _const_spec((N_HEADS, 1))] + cast_specs,
        out_specs=[head_spec] + cast_specs,
        out_shape=[jax.ShapeDtypeStruct((KV_W // LANES, n * N_HEADS, LANES), F32)]
        + [jax.ShapeDtypeStruct(w.shape, BF16) for w in to_cast],
        compiler_params=pltpu.CompilerParams(dimension_semantics=("arbitrary",), vmem_limit_bytes=VMEM_LIMIT),
        name="sample_attn",
    )(qe, kn, vn, ckt, cvt, sink, *to_cast)


def _sample_post_kernel(x_ref, u_ref, st_ref, o_ref, ln1_ref, w_g_ref, gw_ref, pscale_ref, w_pb_ref, w_ab_ref,
                        w_out_ref, h_ref, nst_ref):
    x = x_ref[...]
    n = x.shape[0]
    xn = _rms(x, ln1_ref[...]).astype(BF16)
    u = u_ref[...]
    nst_ref[0:POOL_STATE - 1] = st_ref[1:POOL_STATE]
    nst_ref[POOL_STATE - 1] = u

    def win_sum(g, w, ug):
        acc = ug
        for i in range(1, w):
            acc = acc + st_ref[POOL_STATE - i, :, g * POOL_GC:(g + 1) * POOL_GC]
        return acc

    pooled = _pool_mix(u, win_sum, lambda w: jnp.float32(min(w, PAST_LEN + 1)), gw_ref, pscale_ref)
    merged = _sigmoid(_dot(xn, w_g_ref[:, 0:D_MODEL])) * _dot(pooled.astype(BF16), w_pb_ref[...])
    kv_of_lane = lax.broadcasted_iota(jnp.int32, (1, KV_W), 1) // HEAD_DIM
    ab = jnp.zeros((n, D_MODEL), F32)
    for g in range(GROUP):
        row_g = jnp.zeros((n, KV_W), F32)
        for kh in range(N_KV_HEADS):
            r = kh * GROUP + g
            o_r = jnp.concatenate([o_ref[c, pl.ds(r, n, stride=N_HEADS), :] for c in range(KV_W // LANES)], axis=1)
            row_g = jnp.where(kv_of_lane == kh, o_r, row_g)
        w_g = jnp.concatenate([w_ab_ref[(kh * GROUP + g) * HEAD_DIM:(kh * GROUP + g + 1) * HEAD_DIM, :]
                               for kh in range(N_KV_HEADS)], axis=0)
        ab = ab + _dot(row_g.astype(BF16), w_g)
    merged = merged + _sigmoid(_dot(xn, w_g_ref[:, D_MODEL:2 * D_MODEL])) * ab
    h_ref[...] = x + _dot(merged.astype(BF16), w_out_ref[...])


def _sample_post(x, u, st, o, ln1, w_in, gw, pscale, w_pb, w_ab, w_out):
    n, d = x.shape
    return pl.pallas_call(
        _sample_post_kernel,
        grid=(1,),
        in_specs=[
            _const_spec((n, d)),
            _const_spec((n, POOL_WIDTH)),
            _const_spec(st.shape),
            _const_spec((KV_W // LANES, n * N_HEADS, LANES)),
            _const_spec((1, d)),
            pl.BlockSpec((d, 2 * D_MODEL), lambda i: (0, 1), pipeline_mode=pl.Buffered(1)),
            _const_spec(gw.shape),
            _const_spec((1, POOL_WIDTH)),
            _const_spec(w_pb.shape),
            _const_spec(w_ab.shape),
            _const_spec(w_out.shape),
        ],
        out_specs=[pl.BlockSpec((n, d), lambda i: (0, 0)), pl.BlockSpec(st.shape, lambda i: (0, 0, 0))],
        out_shape=[jax.ShapeDtypeStruct((n, d), F32), jax.ShapeDtypeStruct(st.shape, F32)],
        compiler_params=pltpu.CompilerParams(dimension_semantics=("arbitrary",), vmem_limit_bytes=VMEM_LIMIT),
        name="sample_post",
    )(x, u, st, o, ln1, w_in, gw, pscale, w_pb, w_ab, w_out)


def _rope_tables(first_pos, n):
    half = ROT_DIMS // 2
    inv = ROPE_THETA ** (-(np.arange(0, ROT_DIMS, 2, dtype=np.float64) / ROT_DIMS))
    ang = np.arange(first_pos, first_pos + n, dtype=np.float64)[:, None] * inv[None, :]
    cos, sin = np.cos(ang), np.sin(ang)
    rest = HEAD_DIM - 2 * half
    c64 = np.concatenate([cos, cos, np.ones((n, rest))], axis=1)
    s64 = np.concatenate([-sin, sin, np.zeros((n, rest))], axis=1)
    reps = LANES // HEAD_DIM
    return jnp.asarray(np.tile(c64, (1, reps)), F32), jnp.asarray(np.tile(s64, (1, reps)), F32)


def kernel(x_prompt, x_sample, p_prompt, p_sample, cache_k, cache_v, state_pool, ln1, w_in, pool_group_w, pool_scale,
           attn_sinks, w_pool_branch, w_attn_branch, w_out, ln2, w_ffn_in, w_ffn_out, w_ple_proj, ple_norm,
           w_ple_gate, final_norm):
    depth = ln1.shape[0]
    b, s, d = x_prompt.shape
    bd, t_dec, _ = x_sample.shape
    w_cache = cache_k.shape[2]
    assert depth == 1 and t_dec == 1 and w_cache == WINDOW and s % BLOCK == 0 and d == D_MODEL
    tm = 512
    assert s % tm == 0

    cos_p, sg_p = _rope_tables(0, s)
    cos_s, sg_s = _rope_tables(PAST_LEN, t_dec)

    hp = x_prompt
    hs = x_sample.reshape(bd, d)
    row = lambda a: a.reshape(1, -1)
    nkp, nvp, npp, nks, nvs, nps = [], [], [], [], [], []
    for i in range(depth):
        to_fm = lambda c: jnp.transpose(c, (0, 2, 3, 1)).reshape(bd, KV_W, w_cache)
        from_fm = lambda c: jnp.transpose(c.reshape(bd, N_KV_HEADS, HEAD_DIM, w_cache), (0, 3, 1, 2))
        ckt, cvt = to_fm(cache_k[i]), to_fm(cache_v[i])
        qe, knt, vnt, kn, vn, un = _sample_pre(hs, cos_s, sg_s, row(ln1[i]), w_in[i])
        o, wi, gw, wpb, wab, wo = _sample_attn(
            qe, kn, vn, ckt, cvt, attn_sinks[i].reshape(N_HEADS, 1),
            (w_in[i], pool_group_w[i].reshape(POOL_WIDTH, POOL_GC), w_pool_branch[i], w_attn_branch[i], w_out[i]), 32)
        gw = gw.reshape(len(POOL_WINDOWS), POOL_GC, POOL_GC)
        h1, kp, vp, pp, nkt, nvt, w1, w2, wpp, wpg = _prompt_mixer(
            hp, cos_p, sg_p, attn_sinks[i], row(ln1[i]), wi, gw, row(pool_scale[i]), wpb, wab, wo,
            knt, vnt, ckt, cvt, (w_ffn_in[i], w_ffn_out[i], w_ple_proj[i], w_ple_gate[i]), tm)
        h1s, nst = _sample_post(hs, un, jnp.transpose(state_pool[i], (1, 0, 2)), o, row(ln1[i]), wi, gw,
                                row(pool_scale[i]), wpb, wab, wo)
        hp, hs = _ffn(h1.reshape(b * s, d), p_prompt[i].reshape(b * s, PLE_DIM), h1s,
                      p_sample[i].reshape(bd * t_dec, PLE_DIM), row(ln2[i]), w1, w2, wpp, row(ple_norm[i]), wpg,
                      row(final_norm), tm, 2)
        hp = hp.reshape(b, s, d)
        from_fm_p = lambda c: jnp.transpose(c.reshape(b, N_KV_HEADS, HEAD_DIM, w_cache), (0, 3, 1, 2))
        nkp.append(from_fm_p(kp))
        nvp.append(from_fm_p(vp))
        npp.append(pp)
        nks.append(from_fm(nkt))
        nvs.append(from_fm(nvt))
        nps.append(jnp.transpose(nst, (1, 0, 2)))

    return (hp, hs.reshape(bd, t_dec, d), jnp.stack(nkp), jnp.stack(nvp), jnp.stack(npp),
            jnp.stack(nks), jnp.stack(nvs), jnp.stack(nps))
```

```python
import functools

import jax
import jax.numpy as jnp
import numpy as np
from jax import lax
from jax.experimental import pallas as pl
from jax.experimental.pallas import tpu as pltpu

D_MODEL = 1024
HEAD_DIM = 64
N_HEADS = D_MODEL // HEAD_DIM
N_KV_HEADS = N_HEADS // 4
GROUP = N_HEADS // N_KV_HEADS
ROT_DIMS = HEAD_DIM // 4
ROPE_THETA = 500000.0
WINDOW = 128
BLOCK = 128
POOL_WIDTH = D_MODEL // 2
POOL_WINDOWS = (2, 4, 8, 16)
POOL_GC = POOL_WIDTH // len(POOL_WINDOWS)
POOL_STATE = max(POOL_WINDOWS) - 1
FFN_HIDDEN = -(-8 * D_MODEL // (3 * 256)) * 256
PLE_DIM = 256
EPS = 1e-6
NEG_INF = -1e30
PAST_LEN = 16384

Q_W = N_HEADS * HEAD_DIM
KV_W = N_KV_HEADS * HEAD_DIM
C_U, C_Q, C_K, C_V, C_GP, C_GA, C_END = 0, POOL_WIDTH, POOL_WIDTH + Q_W, POOL_WIDTH + Q_W + KV_W, \
    POOL_WIDTH + Q_W + 2 * KV_W, POOL_WIDTH + Q_W + 2 * KV_W + D_MODEL, POOL_WIDTH + Q_W + 2 * KV_W + 2 * D_MODEL

LANES = 128
S_AHEAD = 16
S_SLOTS = 16
GATE_COLS = 256
SUBLANES = 8
U_HALO = 24
VMEM_LIMIT = 56 * 1024 * 1024

BF16 = jnp.bfloat16
F32 = jnp.float32


def _dot(a, b):
    return jnp.dot(a, b, preferred_element_type=F32)


def _dot_nt(a, b):
    return lax.dot_general(a, b, (((1,), (1,)), ((), ())), preferred_element_type=F32)


def _sigmoid(x):
    return 0.5 * jnp.tanh(0.5 * x) + 0.5


def _rms(x, g):
    y = x * lax.rsqrt(jnp.mean(x * x, axis=-1, keepdims=True) + EPS)
    return y * g


def _rope(x, cos, sg, lo8):
    outs = []
    for c in range(x.shape[1] // LANES):
        xc = x[:, c * LANES:(c + 1) * LANES]
        partner = jnp.where(lo8, pltpu.roll(xc, LANES - ROT_DIMS // 2, 1), pltpu.roll(xc, ROT_DIMS // 2, 1))
        outs.append(xc * cos + partner * sg)
    return jnp.concatenate(outs, axis=1)


def _lane_masks():
    lane = lax.broadcasted_iota(jnp.int32, (1, LANES), 1)
    lo8 = (lane % HEAD_DIM) < (ROT_DIMS // 2)
    lo64 = lane < HEAD_DIM
    return lo8, lo64


def _pool_mix(u, win_sum_fn, cnt_fn, gw_ref, pscale_ref):
    mixed = []
    for g, w in enumerate(POOL_WINDOWS):
        cols = slice(g * POOL_GC, (g + 1) * POOL_GC)
        ug = u[:, cols]
        m = win_sum_fn(g, w, ug) / cnt_fn(w) - ug
        mixed.append(_dot(m.astype(BF16), gw_ref[g]) * pscale_ref[:, cols])
    return jnp.concatenate(mixed, axis=1)


def _mixer_kernel(sinks_ref, x_ref, cos_ref, sg_ref, ln1_ref, w_in_ref, gw_ref, pscale_ref, w_pb_ref, w_ab_ref,
                  w_out_ref, knt_ref, vnt_ref, ckt_ref, cvt_ref, *rest, tm, n_cast):
    cast_src = rest[:n_cast]
    h_ref, ko_ref, vo_ref, po_ref, nkt_ref, nvt_ref = rest[n_cast:n_cast + 6]
    cast_dst = rest[n_cast + 6:2 * n_cast + 6]
    kl_scr, kh_scr, vl_scr, vh_scr, u_scr, lvl_scr, attn_scr, q_scr, s_scr, g_scr = rest[2 * n_cast + 6:]
    t = pl.program_id(1)
    step = pl.program_id(0) * pl.num_programs(1) + t
    lo8, lo64 = _lane_masks()

    @pl.when(t == 0)
    def _():
        for scr in (kl_scr, kh_scr, vl_scr, vh_scr):
            scr[:, 0:BLOCK, :] = jnp.zeros((N_KV_HEADS, BLOCK, LANES), BF16)
        u_scr[0:U_HALO, :] = jnp.zeros((U_HALO, POOL_WIDTH), F32)
        lvl_scr[:, 0:SUBLANES, :] = jnp.zeros((len(POOL_WINDOWS), SUBLANES, POOL_GC), F32)

    x = x_ref[0]
    xn = _rms(x, ln1_ref[...]).astype(BF16)
    cos = cos_ref[...]
    sg = sg_ref[...]

    k = _rope(_dot(xn, w_in_ref[:, C_K:C_V]), cos, sg, lo8)
    v = _dot(xn, w_in_ref[:, C_V:C_GP])
    q = (_rope(_dot(xn, w_in_ref[:, C_Q:C_K]), cos, sg, lo8) * (HEAD_DIM ** -0.5)).astype(BF16)
    q_scr[...] = q
    u = _dot(xn, w_in_ref[:, C_U:C_Q])

    zero = jnp.zeros((tm, LANES), F32)
    for src, lo_scr, hi_scr in ((k, kl_scr, kh_scr), (v, vl_scr, vh_scr)):
        for p in range(KV_W // LANES):
            xp = src[:, p * LANES:(p + 1) * LANES]
            xs = pltpu.roll(xp, HEAD_DIM, 1)
            lo_scr[2 * p, BLOCK:, :] = jnp.where(lo64, xp, zero).astype(BF16)
            hi_scr[2 * p, BLOCK:, :] = jnp.where(lo64, zero, xs).astype(BF16)
            lo_scr[2 * p + 1, BLOCK:, :] = jnp.where(lo64, xs, zero).astype(BF16)
            hi_scr[2 * p + 1, BLOCK:, :] = jnp.where(lo64, zero, xp).astype(BF16)

    qi = lax.broadcasted_iota(jnp.int32, (BLOCK, BLOCK), 0)
    ci = lax.broadcasted_iota(jnp.int32, (BLOCK, BLOCK), 1)
    from_prev = ci > qi
    bias0 = jnp.where(jnp.logical_and(t == 0, from_prev), NEG_INF, 0.0).astype(F32)

    ones_lo = jnp.broadcast_to(jnp.where(lo64, 1.0, 0.0).astype(BF16), (2 * BLOCK, LANES))
    ones_hi = jnp.broadcast_to(jnp.where(lo64, 0.0, 1.0).astype(BF16), (2 * BLOCK, LANES))

    units = [(j, kh) for j in range(tm // BLOCK) for kh in range(N_KV_HEADS)]

    def scores(i):
        j, kh = units[i]
        rows = slice(j * BLOCK, (j + 1) * BLOCK)
        win = slice(j * BLOCK, (j + 2) * BLOCK)
        qq = jnp.concatenate([q_scr[rows, (2 * kh + a) * LANES:(2 * kh + a + 1) * LANES] for a in range(2)], axis=0)
        kcat = jnp.concatenate([kl_scr[kh, win, :], kh_scr[kh, win, :]], axis=0)
        s = _dot_nt(qq, kcat)
        for a in range(2):
            for half in range(2):
                sa = s[a * BLOCK:(a + 1) * BLOCK, half * 2 * BLOCK:(half + 1) * 2 * BLOCK]
                folded = jnp.where(from_prev, sa[:, :BLOCK], sa[:, BLOCK:])
                if j == 0:
                    folded = folded + bias0
                s_scr[i % S_SLOTS, a * BLOCK:(a + 1) * BLOCK, half * BLOCK:(half + 1) * BLOCK] = folded

    units_per_gate = len(units) * GATE_COLS // (2 * D_MODEL)
    for i in range(S_AHEAD):
        scores(i)

    def softmax(i):
        kh = units[i][1]
        ps, sink_terms = [], []
        for a in range(2):
            es, st = [], []
            for half in range(2):
                sh = s_scr[i % S_SLOTS, a * BLOCK:(a + 1) * BLOCK, half * BLOCK:(half + 1) * BLOCK]
                sink = sinks_ref[4 * kh + 2 * a + half]
                m = jnp.maximum(jnp.max(sh, axis=1, keepdims=True), sink)
                e = jnp.exp(sh - m)
                es.append(jnp.where(from_prev, e, 0.0).astype(BF16))
                es.append(jnp.where(from_prev, 0.0, e).astype(BF16))
                st.append(jnp.exp(sink - m))
            ps.append(jnp.concatenate(es, axis=1))
            sink_terms.append(jnp.where(lo64, st[0], st[1]))
        return jnp.concatenate(ps, axis=0), sink_terms

    def weighted_values(i, p, sink_terms):
        j, kh = units[i]
        rows = slice(j * BLOCK, (j + 1) * BLOCK)
        win = slice(j * BLOCK, (j + 2) * BLOCK)
        vcat = jnp.concatenate([
            jnp.concatenate([vl_scr[kh, win, :], ones_lo], axis=1),
            jnp.concatenate([vh_scr[kh, win, :], ones_hi], axis=1)], axis=0)
        o = _dot(p, vcat)
        for a in range(2):
            oa = o[a * BLOCK:(a + 1) * BLOCK]
            attn_scr[rows, (2 * kh + a) * LANES:(2 * kh + a + 1) * LANES] = (
                oa[:, :LANES] / (oa[:, LANES:] + sink_terms[a])).astype(BF16)

    pending = None
    for i in range(len(units)):
        if i + S_AHEAD < len(units):
            scores(i + S_AHEAD)
        current = (i,) + softmax(i)
        if pending is not None:
            weighted_values(*pending)
        pending = current
        if i % units_per_gate == units_per_gate - 1:
            gcols = slice((i // units_per_gate) * GATE_COLS, (i // units_per_gate + 1) * GATE_COLS)
            g_scr[:, gcols] = _sigmoid(_dot(xn, w_in_ref[:, C_GP + gcols.start:C_GP + gcols.stop]))
    weighted_values(*pending)

    u_scr[U_HALO:, :] = u
    po_ref[0] = u_scr[U_HALO + tm - POOL_STATE:U_HALO + tm, :]
    pos = t * tm + lax.broadcasted_iota(jnp.int32, (tm, 1), 0)

    def win_sum(g, w, ug):
        cols = slice(g * POOL_GC, (g + 1) * POOL_GC)
        n = U_HALO - SUBLANES + tm
        src, span = u_scr, 1
        while span < w:
            lvl = src[SUBLANES:SUBLANES + n, cols] + src[SUBLANES - span:SUBLANES - span + n, cols]
            span *= 2
            if span < w:
                lvl_scr[g, SUBLANES:SUBLANES + n, :] = lvl
                src, cols = lvl_scr.at[g], slice(None)
        return lvl[U_HALO - SUBLANES:, :]

    pooled = _pool_mix(u, win_sum, lambda w: jnp.minimum(w, pos + 1).astype(F32), gw_ref, pscale_ref)

    for scr in (kl_scr, kh_scr, vl_scr, vh_scr):
        scr[:, 0:BLOCK, :] = scr[:, tm:tm + BLOCK, :]
    u_scr[0:U_HALO, :] = u_scr[tm:tm + U_HALO, :]

    _shift_caches(step, ckt_ref.shape[0], knt_ref, vnt_ref, ckt_ref, cvt_ref, nkt_ref, nvt_ref)
    for src, dst in zip(cast_src, cast_dst):
        dst[...] = src[...].astype(BF16)
    ko_ref[0] = k[tm - WINDOW:, :].T
    vo_ref[0] = v[tm - WINDOW:, :].T

    merged = g_scr[:, 0:D_MODEL] * _dot(pooled.astype(BF16), w_pb_ref[...])
    merged = merged + g_scr[:, D_MODEL:2 * D_MODEL] * _dot(attn_scr[...], w_ab_ref[...])
    h_ref[0] = x + _dot(merged.astype(BF16), w_out_ref[...])


def _const_spec(shape):
    nd = len(shape)
    return pl.BlockSpec(shape, lambda *_: (0,) * nd, pipeline_mode=pl.Buffered(1))


def _cast_block_rows(rows, steps):
    br = 2 * SUBLANES
    while rows % br or rows // br > steps:
        br *= 2
    return br


def _cast_specs(to_cast, steps, step_of):
    specs = []
    for w in to_cast:
        br = _cast_block_rows(w.shape[0], steps)
        last = w.shape[0] // br - 1
        specs.append(pl.BlockSpec(
            (br, w.shape[1]), lambda *idx, last=last: (jnp.minimum(step_of(*idx), last), 0)))
    return specs


def _prompt_mixer(x, cos, sg, sinks, ln1, w_in, gw, pscale, w_pb, w_ab, w_out, knt, vnt, ckt, cvt, to_cast, tm):
    b, s, d = x.shape
    nt = s // tm
    cast_specs = _cast_specs(to_cast, b * nt, lambda bi, ti, *_: bi * nt + ti)
    nb = ckt.shape[0] // (b * nt)
    assert nb * b * nt == ckt.shape[0]
    cache_spec = pl.BlockSpec((nb,) + ckt.shape[1:], lambda bi, ti, *_: (bi * nt + ti, 0, 0))
    grid_spec = pltpu.PrefetchScalarGridSpec(
        num_scalar_prefetch=1,
        grid=(b, nt),
        in_specs=[
            pl.BlockSpec((1, tm, d), lambda bi, ti, *_: (bi, ti, 0)),
            pl.BlockSpec((tm, LANES), lambda bi, ti, *_: (ti, 0)),
            pl.BlockSpec((tm, LANES), lambda bi, ti, *_: (ti, 0)),
            _const_spec((1, d)),
            _const_spec(w_in.shape),
            _const_spec(gw.shape),
            _const_spec((1, POOL_WIDTH)),
            _const_spec(w_pb.shape),
            _const_spec(w_ab.shape),
            _const_spec(w_out.shape),
            _const_spec(knt.shape),
            _const_spec(vnt.shape),
            cache_spec,
            cache_spec,
        ] + cast_specs,
        out_specs=[
            pl.BlockSpec((1, tm, d), lambda bi, ti, *_: (bi, ti, 0)),
            pl.BlockSpec((1, KV_W, WINDOW), lambda bi, ti, *_: (bi, 0, 0)),
            pl.BlockSpec((1, KV_W, WINDOW), lambda bi, ti, *_: (bi, 0, 0)),
            pl.BlockSpec((1, POOL_STATE, POOL_WIDTH), lambda bi, ti, *_: (bi, 0, 0)),
            cache_spec,
            cache_spec,
        ] + cast_specs,
        scratch_shapes=[
            pltpu.VMEM((N_KV_HEADS, BLOCK + tm, LANES), BF16),
            pltpu.VMEM((N_KV_HEADS, BLOCK + tm, LANES), BF16),
            pltpu.VMEM((N_KV_HEADS, BLOCK + tm, LANES), BF16),
            pltpu.VMEM((N_KV_HEADS, BLOCK + tm, LANES), BF16),
            pltpu.VMEM((U_HALO + tm, POOL_WIDTH), F32),
            pltpu.VMEM((len(POOL_WINDOWS), U_HALO + tm, POOL_GC), F32),
            pltpu.VMEM((tm, Q_W), BF16),
            pltpu.VMEM((tm, Q_W), BF16),
            pltpu.VMEM((S_SLOTS, 2 * BLOCK, 2 * BLOCK), F32),
            pltpu.VMEM((tm, 2 * D_MODEL), F32),
        ],
    )
    return pl.pallas_call(
        functools.partial(_mixer_kernel, tm=tm, n_cast=len(to_cast)),
        grid_spec=grid_spec,
        out_shape=[
            jax.ShapeDtypeStruct((b, s, d), F32),
            jax.ShapeDtypeStruct((b, KV_W, WINDOW), F32),
            jax.ShapeDtypeStruct((b, KV_W, WINDOW), F32),
            jax.ShapeDtypeStruct((b, POOL_STATE, POOL_WIDTH), F32),
            jax.ShapeDtypeStruct(ckt.shape, F32),
            jax.ShapeDtypeStruct(cvt.shape, F32),
        ] + [jax.ShapeDtypeStruct(w.shape, BF16) for w in to_cast],
        compiler_params=pltpu.CompilerParams(
            dimension_semantics=("arbitrary", "arbitrary"), vmem_limit_bytes=VMEM_LIMIT),
        name="prompt_mixer",
    )(sinks, x, cos, sg, ln1, w_in, gw, pscale, w_pb, w_ab, w_out, knt, vnt, ckt, cvt, *to_cast)


FFN_CHUNKS = ((0, 1024), (1024, 2048), (2048, FFN_HIDDEN))


def _ffn_kernel(h_ref, p_ref, hs_ref, ps_ref, ln2_ref, w1_ref, w2_ref, w_pp_ref, pn_ref, w_pg_ref, fn_ref, y_ref,
                ys_ref, *, tm):
    ns = hs_ref.shape[0]
    pad = (-ns) % (2 * SUBLANES)

    def tile_stages(r0, with_sample):
        rows = slice(r0, r0 + tm)
        st = {}

        def gather(ref, ref_s):
            if not with_sample:
                return ref[rows, :]
            extra = ref_s[:, 0, :] if len(ref_s.shape) == 3 else ref_s[...]
            parts = [ref[rows, :], extra] + ([jnp.zeros((pad, ref.shape[1]), F32)] if pad else [])
            return jnp.concatenate(parts, axis=0)

        def up_proj(c):
            lo, hi = FFN_CHUNKS[c]
            if c == 0:
                st['h'] = gather(h_ref, hs_ref)
                st['hn'] = _rms(st['h'], ln2_ref[...]).astype(BF16)
                st['acc'] = st['h']
            st['gate', c] = _dot(st['hn'], w1_ref[:, lo:hi])
            st['up', c] = _dot(st['hn'], w1_ref[:, FFN_HIDDEN + lo:FFN_HIDDEN + hi])
            if c == len(FFN_CHUNKS) - 1:
                st['e'] = _rms(_dot(gather(p_ref, ps_ref).astype(BF16), w_pp_ref[...]), pn_ref[...])

        def down_proj(c):
            lo, hi = FFN_CHUNKS[c]
            gate = st.pop(('gate', c))
            act = (gate * _sigmoid(gate) * st.pop(('up', c))).astype(BF16)
            st['acc'] = st['acc'] + _dot(act, w2_ref[lo:hi, :])

        def ple_gate():
            st['g'] = _dot(st['acc'].astype(BF16), w_pg_ref[...])

        def finish():
            h3 = st['acc'] + _sigmoid(st['g']) * st['e']
            y = _rms(h3, fn_ref[...])
            y_ref[rows, :] = y[:tm]
            if with_sample:
                ys_ref[:, 0, :] = y[tm:tm + ns]

        n = len(FFN_CHUNKS)
        steps = [functools.partial(up_proj, 0)]
        for c in range(1, n):
            steps += [functools.partial(up_proj, c), functools.partial(down_proj, c - 1)]
        return steps + [functools.partial(down_proj, n - 1), ple_gate, finish]

    tail = 3
    order = []
    starts = list(range(0, h_ref.shape[0], tm))
    for r0 in starts:
        steps = tile_stages(r0, r0 == starts[-1])
        held, order = order[len(order) - tail:] if order else [], order[:len(order) - tail] if order else []
        for k in range(max(len(held), tail)):
            order += steps[k:k + 1] + held[k:k + 1]
        order += steps[tail:]
    for step in order:
        step()


def _ffn(h, p, hs, ps, ln2, w1, w2, w_pp, pn, w_pg, fn, tm, nsub):
    n, d = h.shape
    blk = tm * nsub
    ns = hs.shape[0] // (n // blk)
    assert ns * (n // blk) == hs.shape[0] and ns % SUBLANES == 0
    return pl.pallas_call(
        functools.partial(_ffn_kernel, tm=tm),
        grid=(n // blk,),
        in_specs=[
            pl.BlockSpec((blk, d), lambda i: (i, 0)),
            pl.BlockSpec((blk, PLE_DIM), lambda i: (i, 0)),
            pl.BlockSpec((ns, d), lambda i: (i, 0)),
            pl.BlockSpec((ns, 1, PLE_DIM), lambda i: (i, 0, 0)),
            _const_spec((1, d)),
            _const_spec(w1.shape),
            _const_spec(w2.shape),
            _const_spec(w_pp.shape),
            _const_spec((1, d)),
            _const_spec(w_pg.shape),
            _const_spec((1, d)),
        ],
        out_specs=[pl.BlockSpec((blk, d), lambda i: (i, 0)), pl.BlockSpec((ns, 1, d), lambda i: (i, 0, 0))],
        out_shape=[jax.ShapeDtypeStruct((n, d), F32), jax.ShapeDtypeStruct((hs.shape[0], 1, d), F32)],
        compiler_params=pltpu.CompilerParams(dimension_semantics=("arbitrary",), vmem_limit_bytes=VMEM_LIMIT),
        name="ffn_ple_norm",
    )(h, p, hs, ps, ln2, w1, w2, w_pp, pn, w_pg, fn)


def _sample_pre_kernel(x_ref, cos_ref, sg_ref, ln1_ref, w_in_ref, qe_ref, knt_ref, vnt_ref, kn_ref, vn_ref, u_ref):
    n = x_ref.shape[0]
    lo8, _ = _lane_masks()
    xn = _rms(x_ref[:, 0, :], ln1_ref[...]).astype(BF16)
    cos = cos_ref[...]
    sg = sg_ref[...]
    w = lambda lo, hi: w_in_ref[:, lo:hi].astype(BF16)
    u_ref[...] = _dot(xn, w(C_U, C_Q))
    q = (_rope(_dot(xn, w(C_Q, C_K)), cos, sg, lo8) * (HEAD_DIM ** -0.5)).astype(BF16)
    kn = _rope(_dot(xn, w(C_K, C_V)), cos, sg, lo8)
    vn = _dot(xn, w(C_V, C_GP))
    kn_ref[...] = kn
    vn_ref[...] = vn
    knt_ref[...] = kn.T
    vnt_ref[...] = vn.T
    _, lo64 = _lane_masks()
    qf = q.astype(F32)
    zero = jnp.zeros((n, LANES), F32)
    for r in range(N_HEADS):
        kh = r // GROUP
        chunk = qf[:, (r // 2) * LANES:(r // 2 + 1) * LANES]
        if r % 2 != kh % 2:
            chunk = pltpu.roll(chunk, HEAD_DIM, 1)
        placed = jnp.where(lo64, chunk, zero) if kh % 2 == 0 else jnp.where(lo64, zero, chunk)
        for c in range(KV_W // LANES):
            qe_ref[c, pl.ds(r, n, stride=N_HEADS), :] = placed if c == kh // 2 else zero


def _sample_pre(x, cos, sg, ln1, w_in):
    n, _, d = x.shape
    return pl.pallas_call(
        _sample_pre_kernel,
        grid=(1,),
        in_specs=[
            _const_spec((n, 1, d)),
            _const_spec((1, LANES)),
            _const_spec((1, LANES)),
            _const_spec((1, d)),
            pl.BlockSpec((d, C_GP), lambda i: (0, 0), pipeline_mode=pl.Buffered(1)),
        ],
        out_specs=[
            pl.BlockSpec((KV_W // LANES, n * N_HEADS, LANES), lambda i: (0, 0, 0)),
            pl.BlockSpec((KV_W, n), lambda i: (0, 0)),
            pl.BlockSpec((KV_W, n), lambda i: (0, 0)),
            pl.BlockSpec((n, KV_W), lambda i: (0, 0)),
            pl.BlockSpec((n, KV_W), lambda i: (0, 0)),
            pl.BlockSpec((n, POOL_WIDTH), lambda i: (0, 0)),
        ],
        out_shape=[
            jax.ShapeDtypeStruct((KV_W // LANES, n * N_HEADS, LANES), F32),
            jax.ShapeDtypeStruct((KV_W, n), F32),
            jax.ShapeDtypeStruct((KV_W, n), F32),
            jax.ShapeDtypeStruct((n, KV_W), F32),
            jax.ShapeDtypeStruct((n, KV_W), F32),
            jax.ShapeDtypeStruct((n, POOL_WIDTH), F32),
        ],
        compiler_params=pltpu.CompilerParams(dimension_semantics=("arbitrary",), vmem_limit_bytes=VMEM_LIMIT),
        name="sample_pre",
    )(x, cos, sg, ln1, w_in)


def _shift_caches(step, nb, knt_ref, vnt_ref, ckt_ref, cvt_ref, nkt_ref, nvt_ref):
    n = knt_ref.shape[1]
    w_cache = ckt_ref.shape[2]
    newest = lax.broadcasted_iota(jnp.int32, (1, w_cache), 1) == w_cache - 1
    shift = lax.rem(n - lax.rem(step * nb, n), n)
    kcols = pltpu.roll(knt_ref[...], shift, 1)
    vcols = pltpu.roll(vnt_ref[...], shift, 1)
    for bl in range(nb):
        nkt_ref[bl] = jnp.where(newest, kcols[:, bl:bl + 1], pltpu.roll(ckt_ref[bl], w_cache - 1, 1))
        nvt_ref[bl] = jnp.where(newest, vcols[:, bl:bl + 1], pltpu.roll(cvt_ref[bl], w_cache - 1, 1))


def _sample_attn_kernel(qe_ref, kn_ref, vn_ref, ckt_ref, cvt_ref, sink_ref, *rest, bb, n_cast):
    cast_src = rest[:n_cast]
    o_ref = rest[n_cast]
    cast_dst = rest[n_cast + 1:]
    for src, dst in zip(cast_src, cast_dst):
        dst[...] = src[...].astype(BF16)
    w_cache = ckt_ref.shape[2]
    oldest = lax.broadcasted_iota(jnp.int32, (1, w_cache), 1) == 0
    sink = sink_ref[...]
    rounded = lambda a: a.astype(BF16).astype(F32)
    scores = []
    for bl in range(bb):
        rows = slice(bl * N_HEADS, (bl + 1) * N_HEADS)
        qb = jnp.concatenate([qe_ref[c, rows, :] for c in range(KV_W // LANES)], axis=1).astype(BF16)
        s_old = jnp.where(oldest, NEG_INF, _dot(qb, ckt_ref[bl].astype(BF16)))
        s_new = jnp.sum(qb.astype(F32) * rounded(kn_ref[bl:bl + 1, :]), axis=1, keepdims=True)
        scores.append((s_old, s_new))
    probs = []
    for s_old, s_new in scores:
        m = jnp.maximum(jnp.maximum(jnp.max(s_old, axis=-1, keepdims=True), s_new), sink)
        e_old = jnp.exp(s_old - m)
        e_new = jnp.exp(s_new - m)
        denom = jnp.sum(e_old, axis=-1, keepdims=True) + e_new + jnp.exp(sink - m)
        probs.append((e_old.astype(BF16), rounded(e_new), denom))
    for bl, (e_old, e_new, denom) in enumerate(probs):
        rows = slice(bl * N_HEADS, (bl + 1) * N_HEADS)
        o = (_dot_nt(e_old, cvt_ref[bl].astype(BF16)) + e_new * rounded(vn_ref[bl:bl + 1, :])) / denom
        for c in range(KV_W // LANES):
            o_ref[c, rows, :] = o[:, c * LANES:(c + 1) * LANES]


def _sample_attn(qe, kn, vn, ckt, cvt, sink, to_cast, bb):
    n, _, w_cache = ckt.shape
    cast_specs = _cast_specs(to_cast, n // bb, lambda i: i)
    cache_spec = pl.BlockSpec((bb, KV_W, w_cache), lambda i: (i, 0, 0))
    new_spec = pl.BlockSpec((bb, KV_W), lambda i: (i, 0))
    head_spec = pl.BlockSpec((KV_W // LANES, bb * N_HEADS, LANES), lambda i: (0, i, 0))
    return pl.pallas_call(
        functools.partial(_sample_attn_kernel, bb=bb, n_cast=len(to_cast)),
        grid=(n // bb,),
        in_specs=[head_spec, new_spec, new_spec, cache_spec, cache_spec, _const_spec((N_HEADS, 1))] + cast_specs,
        out_specs=[head_spec] + cast_specs,
        out_shape=[jax.ShapeDtypeStruct((KV_W // LANES, n * N_HEADS, LANES), F32)]
        + [jax.ShapeDtypeStruct(w.shape, BF16) for w in to_cast],
        compiler_params=pltpu.CompilerParams(dimension_semantics=("arbitrary",), vmem_limit_bytes=VMEM_LIMIT),
        name="sample_attn",
    )(qe, kn, vn, ckt, cvt, sink, *to_cast)


def _sample_post_kernel(x_ref, u_ref, st_ref, o_ref, ln1_ref, w_g_ref, gw_ref, pscale_ref, w_pb_ref, w_ab_ref,
                        w_out_ref, h_ref, nst_ref):
    x = x_ref[:, 0, :]
    n = x.shape[0]
    xn = _rms(x, ln1_ref[...]).astype(BF16)
    u = u_ref[...]
    nst_ref[0:POOL_STATE - 1] = st_ref[1:POOL_STATE]
    nst_ref[POOL_STATE - 1] = u

    def win_sum(g, w, ug):
        acc = ug
        for i in range(1, w):
            acc = acc + st_ref[POOL_STATE - i, :, g * POOL_GC:(g + 1) * POOL_GC]
        return acc

    pooled = _pool_mix(u, win_sum, lambda w: jnp.float32(min(w, PAST_LEN + 1)), gw_ref, pscale_ref)
    merged = _sigmoid(_dot(xn, w_g_ref[:, 0:D_MODEL])) * _dot(pooled.astype(BF16), w_pb_ref[...])
    kv_of_lane = lax.broadcasted_iota(jnp.int32, (1, KV_W), 1) // HEAD_DIM
    ab = jnp.zeros((n, D_MODEL), F32)
    for g in range(GROUP):
        row_g = jnp.zeros((n, KV_W), F32)
        for kh in range(N_KV_HEADS):
            r = kh * GROUP + g
            o_r = jnp.concatenate([o_ref[c, pl.ds(r, n, stride=N_HEADS), :] for c in range(KV_W // LANES)], axis=1)
            row_g = jnp.where(kv_of_lane == kh, o_r, row_g)
        w_g = jnp.concatenate([w_ab_ref[(kh * GROUP + g) * HEAD_DIM:(kh * GROUP + g + 1) * HEAD_DIM, :]
                               for kh in range(N_KV_HEADS)], axis=0)
        ab = ab + _dot(row_g.astype(BF16), w_g)
    merged = merged + _sigmoid(_dot(xn, w_g_ref[:, D_MODEL:2 * D_MODEL])) * ab
    h_ref[...] = x + _dot(merged.astype(BF16), w_out_ref[...])


def _sample_post(x, u, st, o, ln1, w_in, gw, pscale, w_pb, w_ab, w_out):
    n, _, d = x.shape
    return pl.pallas_call(
        _sample_post_kernel,
        grid=(1,),
        in_specs=[
            _const_spec((n, 1, d)),
            _const_spec((n, POOL_WIDTH)),
            _const_spec(st.shape),
            _const_spec((KV_W // LANES, n * N_HEADS, LANES)),
            _const_spec((1, d)),
            pl.BlockSpec((d, 2 * D_MODEL), lambda i: (0, 1), pipeline_mode=pl.Buffered(1)),
            _const_spec(gw.shape),
            _const_spec((1, POOL_WIDTH)),
            _const_spec(w_pb.shape),
            _const_spec(w_ab.shape),
            _const_spec(w_out.shape),
        ],
        out_specs=[pl.BlockSpec((n, d), lambda i: (0, 0)), pl.BlockSpec(st.shape, lambda i: (0, 0, 0))],
        out_shape=[jax.ShapeDtypeStruct((n, d), F32), jax.ShapeDtypeStruct(st.shape, F32)],
        compiler_params=pltpu.CompilerParams(dimension_semantics=("arbitrary",), vmem_limit_bytes=VMEM_LIMIT),
        name="sample_post",
    )(x, u, st, o, ln1, w_in, gw, pscale, w_pb, w_ab, w_out)


def _rope_tables(first_pos, n):
    half = ROT_DIMS // 2
    inv = ROPE_THETA ** (-(np.arange(0, ROT_DIMS, 2, dtype=np.float64) / ROT_DIMS))
    ang = np.arange(first_pos, first_pos + n, dtype=np.float64)[:, None] * inv[None, :]
    cos, sin = np.cos(ang), np.sin(ang)
    rest = HEAD_DIM - 2 * half
    c64 = np.concatenate([cos, cos, np.ones((n, rest))], axis=1)
    s64 = np.concatenate([-sin, sin, np.zeros((n, rest))], axis=1)
    reps = LANES // HEAD_DIM
    return jnp.asarray(np.tile(c64, (1, reps)), F32), jnp.asarray(np.tile(s64, (1, reps)), F32)


def kernel(x_prompt, x_sample, p_prompt, p_sample, cache_k, cache_v, state_pool, ln1, w_in, pool_group_w, pool_scale,
           attn_sinks, w_pool_branch, w_attn_branch, w_out, ln2, w_ffn_in, w_ffn_out, w_ple_proj, ple_norm,
           w_ple_gate, final_norm):
    depth = ln1.shape[0]
    b, s, d = x_prompt.shape
    bd, t_dec, _ = x_sample.shape
    w_cache = cache_k.shape[2]
    assert depth == 1 and t_dec == 1 and w_cache == WINDOW and s % BLOCK == 0 and d == D_MODEL
    tm = 512
    assert s % tm == 0

    cos_p, sg_p = _rope_tables(0, s)
    cos_s, sg_s = _rope_tables(PAST_LEN, t_dec)

    hp = x_prompt
    hs = x_sample
    row = lambda a: a.reshape(1, -1)
    nkp, nvp, npp, nks, nvs, nps = [], [], [], [], [], []
    for i in range(depth):
        to_fm = lambda c: jnp.transpose(c, (0, 2, 3, 1)).reshape(bd, KV_W, w_cache)
        from_fm = lambda c: jnp.transpose(c.reshape(bd, N_KV_HEADS, HEAD_DIM, w_cache), (0, 3, 1, 2))
        ckt, cvt = to_fm(cache_k[i]), to_fm(cache_v[i])
        qe, knt, vnt, kn, vn, un = _sample_pre(hs, cos_s, sg_s, row(ln1[i]), w_in[i])
        o, wi, gw, wpb, wab, wo = _sample_attn(
            qe, kn, vn, ckt, cvt, attn_sinks[i].reshape(N_HEADS, 1),
            (w_in[i], pool_group_w[i].reshape(POOL_WIDTH, POOL_GC), w_pool_branch[i], w_attn_branch[i], w_out[i]), 16)
        gw = gw.reshape(len(POOL_WINDOWS), POOL_GC, POOL_GC)
        h1, kp, vp, pp, nkt, nvt, w1, w2, wpp, wpg = _prompt_mixer(
            hp, cos_p, sg_p, attn_sinks[i], row(ln1[i]), wi, gw, row(pool_scale[i]), wpb, wab, wo,
            knt, vnt, ckt, cvt, (w_ffn_in[i], w_ffn_out[i], w_ple_proj[i], w_ple_gate[i]), tm)
        h1s, nst = _sample_post(hs, un, jnp.transpose(state_pool[i], (1, 0, 2)), o, row(ln1[i]), wi, gw,
                                row(pool_scale[i]), wpb, wab, wo)
        hp, hs = _ffn(h1.reshape(b * s, d), p_prompt[i].reshape(b * s, PLE_DIM), h1s,
                      p_sample[i], row(ln2[i]), w1, w2, wpp, row(ple_norm[i]), wpg,
                      row(final_norm), tm, 2)
        hp = hp.reshape(b, s, d)
        from_fm_p = lambda c: jnp.transpose(c.reshape(b, N_KV_HEADS, HEAD_DIM, w_cache), (0, 3, 1, 2))
        nkp.append(from_fm_p(kp))
        nvp.append(from_fm_p(vp))
        npp.append(pp)
        nks.append(from_fm(nkt))
        nvs.append(from_fm(nvt))
        nps.append(jnp.transpose(nst, (1, 0, 2)))

    return (hp, hs, jnp.stack(nkp), jnp.stack(nvp), jnp.stack(npp),
            jnp.stack(nks), jnp.stack(nvs), jnp.stack(nps))
```

```python
import functools

import jax
import jax.numpy as jnp
import numpy as np
from jax import lax
from jax.experimental import pallas as pl
from jax.experimental.pallas import tpu as pltpu

D_MODEL = 1024
HEAD_DIM = 64
N_HEADS = D_MODEL // HEAD_DIM
N_KV_HEADS = N_HEADS // 4
GROUP = N_HEADS // N_KV_HEADS
ROT_DIMS = HEAD_DIM // 4
ROPE_THETA = 500000.0
WINDOW = 128
BLOCK = 128
POOL_WIDTH = D_MODEL // 2
POOL_WINDOWS = (2, 4, 8, 16)
POOL_GC = POOL_WIDTH // len(POOL_WINDOWS)
POOL_STATE = max(POOL_WINDOWS) - 1
FFN_HIDDEN = -(-8 * D_MODEL // (3 * 256)) * 256
PLE_DIM = 256
EPS = 1e-6
NEG_INF = -1e30
LOG2E = 1.4426950408889634
PAST_LEN = 16384

Q_W = N_HEADS * HEAD_DIM
KV_W = N_KV_HEADS * HEAD_DIM
C_U, C_Q, C_K, C_V, C_GP, C_GA, C_END = 0, POOL_WIDTH, POOL_WIDTH + Q_W, POOL_WIDTH + Q_W + KV_W, \
    POOL_WIDTH + Q_W + 2 * KV_W, POOL_WIDTH + Q_W + 2 * KV_W + D_MODEL, POOL_WIDTH + Q_W + 2 * KV_W + 2 * D_MODEL

LANES = 128
S_AHEAD = 16
S_SLOTS = 16
GATE_COLS = 256
SUBLANES = 8
U_HALO = 24
VMEM_LIMIT = 56 * 1024 * 1024

BF16 = jnp.bfloat16
F32 = jnp.float32


def _dot(a, b):
    return jnp.dot(a, b, preferred_element_type=F32)


def _dot_nt(a, b):
    return lax.dot_general(a, b, (((1,), (1,)), ((), ())), preferred_element_type=F32)


def _sigmoid(x):
    return 0.5 * jnp.tanh(0.5 * x) + 0.5


def _rms(x, g):
    y = x * lax.rsqrt(jnp.mean(x * x, axis=-1, keepdims=True) + EPS)
    return y * g


def _rope(x, cos, sg, lo8):
    outs = []
    for c in range(x.shape[1] // LANES):
        xc = x[:, c * LANES:(c + 1) * LANES]
        partner = jnp.where(lo8, pltpu.roll(xc, LANES - ROT_DIMS // 2, 1), pltpu.roll(xc, ROT_DIMS // 2, 1))
        outs.append(xc * cos + partner * sg)
    return jnp.concatenate(outs, axis=1)


def _lane_masks():
    lane = lax.broadcasted_iota(jnp.int32, (1, LANES), 1)
    lo8 = (lane % HEAD_DIM) < (ROT_DIMS // 2)
    lo64 = lane < HEAD_DIM
    return lo8, lo64


def _pool_mix(u, win_sum_fn, cnt_fn, gw_ref, pscale_ref):
    mixed = []
    for g, w in enumerate(POOL_WINDOWS):
        cols = slice(g * POOL_GC, (g + 1) * POOL_GC)
        ug = u[:, cols]
        m = win_sum_fn(g, w, ug) / cnt_fn(w) - ug
        mixed.append(_dot(m.astype(BF16), gw_ref[g]) * pscale_ref[:, cols])
    return jnp.concatenate(mixed, axis=1)


def _mixer_kernel(sinks_ref, x_ref, cos_ref, sg_ref, ln1_ref, w_in_ref, gw_ref, pscale_ref, w_pb_ref, w_ab_ref,
                  w_out_ref, knt_ref, vnt_ref, ckt_ref, cvt_ref, *rest, tm, n_cast):
    cast_src = rest[:n_cast]
    h_ref, ko_ref, vo_ref, po_ref, nkt_ref, nvt_ref = rest[n_cast:n_cast + 6]
    cast_dst = rest[n_cast + 6:2 * n_cast + 6]
    kl_scr, kh_scr, vl_scr, vh_scr, u_scr, lvl_scr, attn_scr, q_scr, s_scr, g_scr = rest[2 * n_cast + 6:]
    t = pl.program_id(1)
    step = pl.program_id(0) * pl.num_programs(1) + t
    lo8, lo64 = _lane_masks()

    @pl.when(t == 0)
    def _():
        for scr in (kl_scr, kh_scr, vl_scr, vh_scr):
            scr[:, 0:BLOCK, :] = jnp.zeros((N_KV_HEADS, BLOCK, LANES), BF16)
        u_scr[0:U_HALO, :] = jnp.zeros((U_HALO, POOL_WIDTH), F32)
        lvl_scr[:, 0:SUBLANES, :] = jnp.zeros((len(POOL_WINDOWS), SUBLANES, POOL_GC), F32)

    x = x_ref[0]
    xn = _rms(x, ln1_ref[...]).astype(BF16)
    cos = cos_ref[...]
    sg = sg_ref[...]

    k = _rope(_dot(xn, w_in_ref[:, C_K:C_V]), cos, sg, lo8)
    v = _dot(xn, w_in_ref[:, C_V:C_GP])
    q = (_rope(_dot(xn, w_in_ref[:, C_Q:C_K]), cos, sg, lo8) * (HEAD_DIM ** -0.5 * LOG2E)).astype(BF16)
    q_scr[...] = q
    u = _dot(xn, w_in_ref[:, C_U:C_Q])

    zero = jnp.zeros((tm, LANES), F32)
    for src, lo_scr, hi_scr in ((k, kl_scr, kh_scr), (v, vl_scr, vh_scr)):
        for p in range(KV_W // LANES):
            xp = src[:, p * LANES:(p + 1) * LANES]
            xs = pltpu.roll(xp, HEAD_DIM, 1)
            lo_scr[2 * p, BLOCK:, :] = jnp.where(lo64, xp, zero).astype(BF16)
            hi_scr[2 * p, BLOCK:, :] = jnp.where(lo64, zero, xs).astype(BF16)
            lo_scr[2 * p + 1, BLOCK:, :] = jnp.where(lo64, xs, zero).astype(BF16)
            hi_scr[2 * p + 1, BLOCK:, :] = jnp.where(lo64, zero, xp).astype(BF16)

    qi = lax.broadcasted_iota(jnp.int32, (BLOCK, BLOCK), 0)
    ci = lax.broadcasted_iota(jnp.int32, (BLOCK, BLOCK), 1)
    from_prev = ci > qi
    bias0 = jnp.where(jnp.logical_and(t == 0, from_prev), NEG_INF, 0.0).astype(F32)

    ones_lo = jnp.broadcast_to(jnp.where(lo64, 1.0, 0.0).astype(BF16), (2 * BLOCK, LANES))
    ones_hi = jnp.broadcast_to(jnp.where(lo64, 0.0, 1.0).astype(BF16), (2 * BLOCK, LANES))

    units = [(j, kh) for j in range(tm // BLOCK) for kh in range(N_KV_HEADS)]

    def scores(i):
        j, kh = units[i]
        rows = slice(j * BLOCK, (j + 1) * BLOCK)
        win = slice(j * BLOCK, (j + 2) * BLOCK)
        qq = jnp.concatenate([q_scr[rows, (2 * kh + a) * LANES:(2 * kh + a + 1) * LANES] for a in range(2)], axis=0)
        kcat = jnp.concatenate([kl_scr[kh, win, :], kh_scr[kh, win, :]], axis=0)
        s = _dot_nt(qq, kcat)
        for a in range(2):
            for half in range(2):
                sa = s[a * BLOCK:(a + 1) * BLOCK, half * 2 * BLOCK:(half + 1) * 2 * BLOCK]
                folded = jnp.where(from_prev, sa[:, :BLOCK], sa[:, BLOCK:])
                if j == 0:
                    folded = folded + bias0
                s_scr[i % S_SLOTS, a * BLOCK:(a + 1) * BLOCK, half * BLOCK:(half + 1) * BLOCK] = folded

    units_per_gate = len(units) * GATE_COLS // (2 * D_MODEL)
    for i in range(S_AHEAD):
        scores(i)

    def softmax(i):
        kh = units[i][1]
        ps, sink_terms = [], []
        for a in range(2):
            es, st = [], []
            for half in range(2):
                sh = s_scr[i % S_SLOTS, a * BLOCK:(a + 1) * BLOCK, half * BLOCK:(half + 1) * BLOCK]
                sink = sinks_ref[4 * kh + 2 * a + half] * LOG2E
                m = jnp.maximum(jnp.max(sh, axis=1, keepdims=True), sink)
                e = jnp.exp2(sh - m)
                es.append(jnp.where(from_prev, e, 0.0).astype(BF16))
                es.append(jnp.where(from_prev, 0.0, e).astype(BF16))
                st.append(jnp.exp2(sink - m))
            ps.append(jnp.concatenate(es, axis=1))
            sink_terms.append(jnp.where(lo64, st[0], st[1]))
        return jnp.concatenate(ps, axis=0), sink_terms

    def weighted_values(i, p, sink_terms):
        j, kh = units[i]
        rows = slice(j * BLOCK, (j + 1) * BLOCK)
        win = slice(j * BLOCK, (j + 2) * BLOCK)
        vcat = jnp.concatenate([
            jnp.concatenate([vl_scr[kh, win, :], ones_lo], axis=1),
            jnp.concatenate([vh_scr[kh, win, :], ones_hi], axis=1)], axis=0)
        o = _dot(p, vcat)
        for a in range(2):
            oa = o[a * BLOCK:(a + 1) * BLOCK]
            attn_scr[rows, (2 * kh + a) * LANES:(2 * kh + a + 1) * LANES] = (
                oa[:, :LANES] / (oa[:, LANES:] + sink_terms[a])).astype(BF16)

    pending = None
    for i in range(len(units)):
        if i + S_AHEAD < len(units):
            scores(i + S_AHEAD)
        current = (i,) + softmax(i)
        if pending is not None:
            weighted_values(*pending)
        pending = current
        if i % units_per_gate == units_per_gate - 1:
            gcols = slice((i // units_per_gate) * GATE_COLS, (i // units_per_gate + 1) * GATE_COLS)
            g_scr[:, gcols] = _sigmoid(_dot(xn, w_in_ref[:, C_GP + gcols.start:C_GP + gcols.stop]))
    weighted_values(*pending)

    u_scr[U_HALO:, :] = u
    po_ref[0] = u_scr[U_HALO + tm - POOL_STATE:U_HALO + tm, :]
    pos = t * tm + lax.broadcasted_iota(jnp.int32, (tm, 1), 0)

    def win_sum(g, w, ug):
        cols = slice(g * POOL_GC, (g + 1) * POOL_GC)
        n = U_HALO - SUBLANES + tm
        src, span = u_scr, 1
        while span < w:
            lvl = src[SUBLANES:SUBLANES + n, cols] + src[SUBLANES - span:SUBLANES - span + n, cols]
            span *= 2
            if span < w:
                lvl_scr[g, SUBLANES:SUBLANES + n, :] = lvl
                src, cols = lvl_scr.at[g], slice(None)
        return lvl[U_HALO - SUBLANES:, :]

    pooled = _pool_mix(u, win_sum, lambda w: jnp.minimum(w, pos + 1).astype(F32), gw_ref, pscale_ref)

    for scr in (kl_scr, kh_scr, vl_scr, vh_scr):
        scr[:, 0:BLOCK, :] = scr[:, tm:tm + BLOCK, :]
    u_scr[0:U_HALO, :] = u_scr[tm:tm + U_HALO, :]

    _shift_caches(step, ckt_ref.shape[0], knt_ref, vnt_ref, ckt_ref, cvt_ref, nkt_ref, nvt_ref)
    for src, dst in zip(cast_src, cast_dst):
        dst[...] = src[...].astype(BF16)
    ko_ref[0] = k[tm - WINDOW:, :].T
    vo_ref[0] = v[tm - WINDOW:, :].T

    merged = g_scr[:, 0:D_MODEL] * _dot(pooled.astype(BF16), w_pb_ref[...])
    merged = merged + g_scr[:, D_MODEL:2 * D_MODEL] * _dot(attn_scr[...], w_ab_ref[...])
    h_ref[0] = x + _dot(merged.astype(BF16), w_out_ref[...])


def _const_spec(shape):
    nd = len(shape)
    return pl.BlockSpec(shape, lambda *_: (0,) * nd, pipeline_mode=pl.Buffered(1))


def _cast_block_rows(rows, steps):
    br = 2 * SUBLANES
    while rows % br or rows // br > steps:
        br *= 2
    return br


def _cast_specs(to_cast, steps, step_of):
    specs = []
    for w in to_cast:
        br = _cast_block_rows(w.shape[0], steps)
        last = w.shape[0] // br - 1
        specs.append(pl.BlockSpec(
            (br, w.shape[1]), lambda *idx, last=last: (jnp.minimum(step_of(*idx), last), 0)))
    return specs


def _prompt_mixer(x, cos, sg, sinks, ln1, w_in, gw, pscale, w_pb, w_ab, w_out, knt, vnt, ckt, cvt, to_cast, tm):
    b, s, d = x.shape
    nt = s // tm
    cast_specs = _cast_specs(to_cast, b * nt, lambda bi, ti, *_: bi * nt + ti)
    nb = ckt.shape[0] // (b * nt)
    assert nb * b * nt == ckt.shape[0]
    cache_spec = pl.BlockSpec((nb,) + ckt.shape[1:], lambda bi, ti, *_: (bi * nt + ti, 0, 0))
    grid_spec = pltpu.PrefetchScalarGridSpec(
        num_scalar_prefetch=1,
        grid=(b, nt),
        in_specs=[
            pl.BlockSpec((1, tm, d), lambda bi, ti, *_: (bi, ti, 0)),
            pl.BlockSpec((tm, LANES), lambda bi, ti, *_: (ti, 0)),
            pl.BlockSpec((tm, LANES), lambda bi, ti, *_: (ti, 0)),
            _const_spec((1, d)),
            _const_spec(w_in.shape),
            _const_spec(gw.shape),
            _const_spec((1, POOL_WIDTH)),
            _const_spec(w_pb.shape),
            _const_spec(w_ab.shape),
            _const_spec(w_out.shape),
            _const_spec(knt.shape),
            _const_spec(vnt.shape),
            cache_spec,
            cache_spec,
        ] + cast_specs,
        out_specs=[
            pl.BlockSpec((1, tm, d), lambda bi, ti, *_: (bi, ti, 0)),
            pl.BlockSpec((1, KV_W, WINDOW), lambda bi, ti, *_: (bi, 0, 0)),
            pl.BlockSpec((1, KV_W, WINDOW), lambda bi, ti, *_: (bi, 0, 0)),
            pl.BlockSpec((1, POOL_STATE, POOL_WIDTH), lambda bi, ti, *_: (bi, 0, 0)),
            cache_spec,
            cache_spec,
        ] + cast_specs,
        scratch_shapes=[
            pltpu.VMEM((N_KV_HEADS, BLOCK + tm, LANES), BF16),
            pltpu.VMEM((N_KV_HEADS, BLOCK + tm, LANES), BF16),
            pltpu.VMEM((N_KV_HEADS, BLOCK + tm, LANES), BF16),
            pltpu.VMEM((N_KV_HEADS, BLOCK + tm, LANES), BF16),
            pltpu.VMEM((U_HALO + tm, POOL_WIDTH), F32),
            pltpu.VMEM((len(POOL_WINDOWS), U_HALO + tm, POOL_GC), F32),
            pltpu.VMEM((tm, Q_W), BF16),
            pltpu.VMEM((tm, Q_W), BF16),
            pltpu.VMEM((S_SLOTS, 2 * BLOCK, 2 * BLOCK), F32),
            pltpu.VMEM((tm, 2 * D_MODEL), F32),
        ],
    )
    return pl.pallas_call(
        functools.partial(_mixer_kernel, tm=tm, n_cast=len(to_cast)),
        grid_spec=grid_spec,
        out_shape=[
            jax.ShapeDtypeStruct((b, s, d), F32),
            jax.ShapeDtypeStruct((b, KV_W, WINDOW), F32),
            jax.ShapeDtypeStruct((b, KV_W, WINDOW), F32),
            jax.ShapeDtypeStruct((b, POOL_STATE, POOL_WIDTH), F32),
            jax.ShapeDtypeStruct(ckt.shape, F32),
            jax.ShapeDtypeStruct(cvt.shape, F32),
        ] + [jax.ShapeDtypeStruct(w.shape, BF16) for w in to_cast],
        compiler_params=pltpu.CompilerParams(
            dimension_semantics=("arbitrary", "arbitrary"), vmem_limit_bytes=VMEM_LIMIT),
        name="prompt_mixer",
    )(sinks, x, cos, sg, ln1, w_in, gw, pscale, w_pb, w_ab, w_out, knt, vnt, ckt, cvt, *to_cast)


FFN_CHUNKS = ((0, 1024), (1024, 2048), (2048, FFN_HIDDEN))


def _ffn_kernel(h_ref, p_ref, hs_ref, ps_ref, ln2_ref, w1_ref, w2_ref, w_pp_ref, pn_ref, w_pg_ref, fn_ref, y_ref,
                ys_ref, *, tm):
    ns = hs_ref.shape[0]
    pad = (-ns) % (2 * SUBLANES)

    def tile_stages(r0, with_sample):
        rows = slice(r0, r0 + tm)
        st = {}

        def gather(ref, ref_s):
            if not with_sample:
                return ref[rows, :]
            extra = ref_s[:, 0, :] if len(ref_s.shape) == 3 else ref_s[...]
            parts = [ref[rows, :], extra] + ([jnp.zeros((pad, ref.shape[1]), F32)] if pad else [])
            return jnp.concatenate(parts, axis=0)

        def up_proj(c):
            lo, hi = FFN_CHUNKS[c]
            if c == 0:
                st['h'] = gather(h_ref, hs_ref)
                st['hn'] = _rms(st['h'], ln2_ref[...]).astype(BF16)
                st['acc'] = st['h']
            st['gate', c] = _dot(st['hn'], w1_ref[:, lo:hi])
            st['up', c] = _dot(st['hn'], w1_ref[:, FFN_HIDDEN + lo:FFN_HIDDEN + hi])
            if c == len(FFN_CHUNKS) - 1:
                st['e'] = _rms(_dot(gather(p_ref, ps_ref).astype(BF16), w_pp_ref[...]), pn_ref[...])

        def down_proj(c):
            lo, hi = FFN_CHUNKS[c]
            gate = st.pop(('gate', c))
            act = (gate * _sigmoid(gate) * st.pop(('up', c))).astype(BF16)
            st['acc'] = st['acc'] + _dot(act, w2_ref[lo:hi, :])

        def ple_gate():
            st['g'] = _dot(st['acc'].astype(BF16), w_pg_ref[...])

        def finish():
            h3 = st['acc'] + _sigmoid(st['g']) * st['e']
            y = _rms(h3, fn_ref[...])
            y_ref[rows, :] = y[:tm]
            if with_sample:
                ys_ref[:, 0, :] = y[tm:tm + ns]

        n = len(FFN_CHUNKS)
        steps = [functools.partial(up_proj, 0)]
        for c in range(1, n):
            steps += [functools.partial(up_proj, c), functools.partial(down_proj, c - 1)]
        return steps + [functools.partial(down_proj, n - 1), ple_gate, finish]

    tail = 3
    order = []
    starts = list(range(0, h_ref.shape[0], tm))
    for r0 in starts:
        steps = tile_stages(r0, r0 == starts[-1])
        held, order = order[len(order) - tail:] if order else [], order[:len(order) - tail] if order else []
        for k in range(max(len(held), tail)):
            order += steps[k:k + 1] + held[k:k + 1]
        order += steps[tail:]
    for step in order:
        step()


def _ffn(h, p, hs, ps, ln2, w1, w2, w_pp, pn, w_pg, fn, tm, nsub):
    n, d = h.shape
    blk = tm * nsub
    ns = hs.shape[0] // (n // blk)
    assert ns * (n // blk) == hs.shape[0] and ns % SUBLANES == 0
    return pl.pallas_call(
        functools.partial(_ffn_kernel, tm=tm),
        grid=(n // blk,),
        in_specs=[
            pl.BlockSpec((blk, d), lambda i: (i, 0)),
            pl.BlockSpec((blk, PLE_DIM), lambda i: (i, 0)),
            pl.BlockSpec((ns, d), lambda i: (i, 0)),
            pl.BlockSpec((ns, 1, PLE_DIM), lambda i: (i, 0, 0)),
            _const_spec((1, d)),
            _const_spec(w1.shape),
            _const_spec(w2.shape),
            _const_spec(w_pp.shape),
            _const_spec((1, d)),
            _const_spec(w_pg.shape),
            _const_spec((1, d)),
        ],
        out_specs=[pl.BlockSpec((blk, d), lambda i: (i, 0)), pl.BlockSpec((ns, 1, d), lambda i: (i, 0, 0))],
        out_shape=[jax.ShapeDtypeStruct((n, d), F32), jax.ShapeDtypeStruct((hs.shape[0], 1, d), F32)],
        compiler_params=pltpu.CompilerParams(dimension_semantics=("arbitrary",), vmem_limit_bytes=VMEM_LIMIT),
        name="ffn_ple_norm",
    )(h, p, hs, ps, ln2, w1, w2, w_pp, pn, w_pg, fn)


def _sample_pre_kernel(x_ref, cos_ref, sg_ref, ln1_ref, w_in_ref, qe_ref, knt_ref, vnt_ref, kn_ref, vn_ref, u_ref):
    n = x_ref.shape[0]
    lo8, _ = _lane_masks()
    xn = _rms(x_ref[:, 0, :], ln1_ref[...]).astype(BF16)
    cos = cos_ref[...]
    sg = sg_ref[...]
    w = lambda lo, hi: w_in_ref[:, lo:hi].astype(BF16)
    u_ref[...] = _dot(xn, w(C_U, C_Q))
    q = (_rope(_dot(xn, w(C_Q, C_K)), cos, sg, lo8) * (HEAD_DIM ** -0.5)).astype(BF16)
    kn = _rope(_dot(xn, w(C_K, C_V)), cos, sg, lo8)
    vn = _dot(xn, w(C_V, C_GP))
    kn_ref[...] = kn
    vn_ref[...] = vn
    knt_ref[...] = kn.T
    vnt_ref[...] = vn.T
    _, lo64 = _lane_masks()
    qf = q.astype(F32)
    zero = jnp.zeros((n, LANES), F32)
    for r in range(N_HEADS):
        kh = r // GROUP
        chunk = qf[:, (r // 2) * LANES:(r // 2 + 1) * LANES]
        if r % 2 != kh % 2:
            chunk = pltpu.roll(chunk, HEAD_DIM, 1)
        placed = jnp.where(lo64, chunk, zero) if kh % 2 == 0 else jnp.where(lo64, zero, chunk)
        for c in range(KV_W // LANES):
            qe_ref[c, pl.ds(r, n, stride=N_HEADS), :] = placed if c == kh // 2 else zero


def _sample_pre(x, cos, sg, ln1, w_in):
    n, _, d = x.shape
    return pl.pallas_call(
        _sample_pre_kernel,
        grid=(1,),
        in_specs=[
            _const_spec((n, 1, d)),
            _const_spec((1, LANES)),
            _const_spec((1, LANES)),
            _const_spec((1, d)),
            pl.BlockSpec((d, C_GP), lambda i: (0, 0), pipeline_mode=pl.Buffered(1)),
        ],
        out_specs=[
            pl.BlockSpec((KV_W // LANES, n * N_HEADS, LANES), lambda i: (0, 0, 0)),
            pl.BlockSpec((KV_W, n), lambda i: (0, 0)),
            pl.BlockSpec((KV_W, n), lambda i: (0, 0)),
            pl.BlockSpec((n, KV_W), lambda i: (0, 0)),
            pl.BlockSpec((n, KV_W), lambda i: (0, 0)),
            pl.BlockSpec((n, POOL_WIDTH), lambda i: (0, 0)),
        ],
        out_shape=[
            jax.ShapeDtypeStruct((KV_W // LANES, n * N_HEADS, LANES), F32),
            jax.ShapeDtypeStruct((KV_W, n), F32),
            jax.ShapeDtypeStruct((KV_W, n), F32),
            jax.ShapeDtypeStruct((n, KV_W), F32),
            jax.ShapeDtypeStruct((n, KV_W), F32),
            jax.ShapeDtypeStruct((n, POOL_WIDTH), F32),
        ],
        compiler_params=pltpu.CompilerParams(dimension_semantics=("arbitrary",), vmem_limit_bytes=VMEM_LIMIT),
        name="sample_pre",
    )(x, cos, sg, ln1, w_in)


def _shift_caches(step, nb, knt_ref, vnt_ref, ckt_ref, cvt_ref, nkt_ref, nvt_ref):
    n = knt_ref.shape[1]
    w_cache = ckt_ref.shape[2]
    newest = lax.broadcasted_iota(jnp.int32, (1, w_cache), 1) == w_cache - 1
    shift = lax.rem(n - lax.rem(step * nb, n), n)
    kcols = pltpu.roll(knt_ref[...], shift, 1)
    vcols = pltpu.roll(vnt_ref[...], shift, 1)
    for bl in range(nb):
        nkt_ref[bl] = jnp.where(newest, kcols[:, bl:bl + 1], pltpu.roll(ckt_ref[bl], w_cache - 1, 1))
        nvt_ref[bl] = jnp.where(newest, vcols[:, bl:bl + 1], pltpu.roll(cvt_ref[bl], w_cache - 1, 1))


def _sample_attn_kernel(qe_ref, kn_ref, vn_ref, ckt_ref, cvt_ref, sink_ref, *rest, bb, n_cast):
    cast_src = rest[:n_cast]
    o_ref = rest[n_cast]
    cast_dst = rest[n_cast + 1:]
    for src, dst in zip(cast_src, cast_dst):
        dst[...] = src[...].astype(BF16)
    w_cache = ckt_ref.shape[2]
    oldest = lax.broadcasted_iota(jnp.int32, (1, w_cache), 1) == 0
    sink = sink_ref[...]
    rounded = lambda a: a.astype(BF16).astype(F32)
    scores = []
    for bl in range(bb):
        rows = slice(bl * N_HEADS, (bl + 1) * N_HEADS)
        qb = jnp.concatenate([qe_ref[c, rows, :] for c in range(KV_W // LANES)], axis=1).astype(BF16)
        s_old = jnp.where(oldest, NEG_INF, _dot(qb, ckt_ref[bl].astype(BF16)))
        s_new = jnp.sum(qb.astype(F32) * rounded(kn_ref[bl:bl + 1, :]), axis=1, keepdims=True)
        scores.append((s_old, s_new))
    probs = []
    for s_old, s_new in scores:
        m = jnp.maximum(jnp.maximum(jnp.max(s_old, axis=-1, keepdims=True), s_new), sink)
        e_old = jnp.exp(s_old - m)
        e_new = jnp.exp(s_new - m)
        denom = jnp.sum(e_old, axis=-1, keepdims=True) + e_new + jnp.exp(sink - m)
        probs.append((e_old.astype(BF16), rounded(e_new), denom))
    for bl, (e_old, e_new, denom) in enumerate(probs):
        rows = slice(bl * N_HEADS, (bl + 1) * N_HEADS)
        o = (_dot_nt(e_old, cvt_ref[bl].astype(BF16)) + e_new * rounded(vn_ref[bl:bl + 1, :])) / denom
        for c in range(KV_W // LANES):
            o_ref[c, rows, :] = o[:, c * LANES:(c + 1) * LANES]


def _sample_attn(qe, kn, vn, ckt, cvt, sink, to_cast, bb):
    n, _, w_cache = ckt.shape
    cast_specs = _cast_specs(to_cast, n // bb, lambda i: i)
    cache_spec = pl.BlockSpec((bb, KV_W, w_cache), lambda i: (i, 0, 0))
    new_spec = pl.BlockSpec((bb, KV_W), lambda i: (i, 0))
    head_spec = pl.BlockSpec((KV_W // LANES, bb * N_HEADS, LANES), lambda i: (0, i, 0))
    return pl.pallas_call(
        functools.partial(_sample_attn_kernel, bb=bb, n_cast=len(to_cast)),
        grid=(n // bb,),
        in_specs=[head_spec, new_spec, new_spec, cache_spec, cache_spec, _const_spec((N_HEADS, 1))] + cast_specs,
        out_specs=[head_spec] + cast_specs,
        out_shape=[jax.ShapeDtypeStruct((KV_W // LANES, n * N_HEADS, LANES), F32)]
        + [jax.ShapeDtypeStruct(w.shape, BF16) for w in to_cast],
        compiler_params=pltpu.CompilerParams(dimension_semantics=("arbitrary",), vmem_limit_bytes=VMEM_LIMIT),
        name="sample_attn",
    )(qe, kn, vn, ckt, cvt, sink, *to_cast)


def _sample_post_kernel(x_ref, u_ref, st_ref, o_ref, ln1_ref, w_g_ref, gw_ref, pscale_ref, w_pb_ref, w_ab_ref,
                        w_out_ref, h_ref, nst_ref):
    x = x_ref[:, 0, :]
    n = x.shape[0]
    xn = _rms(x, ln1_ref[...]).astype(BF16)
    u = u_ref[...]
    nst_ref[0:POOL_STATE - 1] = st_ref[1:POOL_STATE]
    nst_ref[POOL_STATE - 1] = u

    def win_sum(g, w, ug):
        acc = ug
        for i in range(1, w):
            acc = acc + st_ref[POOL_STATE - i, :, g * POOL_GC:(g + 1) * POOL_GC]
        return acc

    pooled = _pool_mix(u, win_sum, lambda w: jnp.float32(min(w, PAST_LEN + 1)), gw_ref, pscale_ref)
    merged = _sigmoid(_dot(xn, w_g_ref[:, 0:D_MODEL])) * _dot(pooled.astype(BF16), w_pb_ref[...])
    kv_of_lane = lax.broadcasted_iota(jnp.int32, (1, KV_W), 1) // HEAD_DIM
    ab = jnp.zeros((n, D_MODEL), F32)
    for g in range(GROUP):
        row_g = jnp.zeros((n, KV_W), F32)
        for kh in range(N_KV_HEADS):
            r = kh * GROUP + g
            o_r = jnp.concatenate([o_ref[c, pl.ds(r, n, stride=N_HEADS), :] for c in range(KV_W // LANES)], axis=1)
            row_g = jnp.where(kv_of_lane == kh, o_r, row_g)
        w_g = jnp.concatenate([w_ab_ref[(kh * GROUP + g) * HEAD_DIM:(kh * GROUP + g + 1) * HEAD_DIM, :]
                               for kh in range(N_KV_HEADS)], axis=0)
        ab = ab + _dot(row_g.astype(BF16), w_g)
    merged = merged + _sigmoid(_dot(xn, w_g_ref[:, D_MODEL:2 * D_MODEL])) * ab
    h_ref[...] = x + _dot(merged.astype(BF16), w_out_ref[...])


def _sample_post(x, u, st, o, ln1, w_in, gw, pscale, w_pb, w_ab, w_out):
    n, _, d = x.shape
    return pl.pallas_call(
        _sample_post_kernel,
        grid=(1,),
        in_specs=[
            _const_spec((n, 1, d)),
            _const_spec((n, POOL_WIDTH)),
            _const_spec(st.shape),
            _const_spec((KV_W // LANES, n * N_HEADS, LANES)),
            _const_spec((1, d)),
            pl.BlockSpec((d, 2 * D_MODEL), lambda i: (0, 1), pipeline_mode=pl.Buffered(1)),
            _const_spec(gw.shape),
            _const_spec((1, POOL_WIDTH)),
            _const_spec(w_pb.shape),
            _const_spec(w_ab.shape),
            _const_spec(w_out.shape),
        ],
        out_specs=[pl.BlockSpec((n, d), lambda i: (0, 0)), pl.BlockSpec(st.shape, lambda i: (0, 0, 0))],
        out_shape=[jax.ShapeDtypeStruct((n, d), F32), jax.ShapeDtypeStruct(st.shape, F32)],
        compiler_params=pltpu.CompilerParams(dimension_semantics=("arbitrary",), vmem_limit_bytes=VMEM_LIMIT),
        name="sample_post",
    )(x, u, st, o, ln1, w_in, gw, pscale, w_pb, w_ab, w_out)


def _rope_tables(first_pos, n):
    half = ROT_DIMS // 2
    inv = ROPE_THETA ** (-(np.arange(0, ROT_DIMS, 2, dtype=np.float64) / ROT_DIMS))
    ang = np.arange(first_pos, first_pos + n, dtype=np.float64)[:, None] * inv[None, :]
    cos, sin = np.cos(ang), np.sin(ang)
    rest = HEAD_DIM - 2 * half
    c64 = np.concatenate([cos, cos, np.ones((n, rest))], axis=1)
    s64 = np.concatenate([-sin, sin, np.zeros((n, rest))], axis=1)
    reps = LANES // HEAD_DIM
    return jnp.asarray(np.tile(c64, (1, reps)), F32), jnp.asarray(np.tile(s64, (1, reps)), F32)


def kernel(x_prompt, x_sample, p_prompt, p_sample, cache_k, cache_v, state_pool, ln1, w_in, pool_group_w, pool_scale,
           attn_sinks, w_pool_branch, w_attn_branch, w_out, ln2, w_ffn_in, w_ffn_out, w_ple_proj, ple_norm,
           w_ple_gate, final_norm):
    depth = ln1.shape[0]
    b, s, d = x_prompt.shape
    bd, t_dec, _ = x_sample.shape
    w_cache = cache_k.shape[2]
    assert depth == 1 and t_dec == 1 and w_cache == WINDOW and s % BLOCK == 0 and d == D_MODEL
    tm = 512
    assert s % tm == 0

    cos_p, sg_p = _rope_tables(0, s)
    cos_s, sg_s = _rope_tables(PAST_LEN, t_dec)

    hp = x_prompt
    hs = x_sample
    row = lambda a: a.reshape(1, -1)
    nkp, nvp, npp, nks, nvs, nps = [], [], [], [], [], []
    for i in range(depth):
        to_fm = lambda c: jnp.transpose(c, (0, 2, 3, 1)).reshape(bd, KV_W, w_cache)
        from_fm = lambda c: jnp.transpose(c.reshape(bd, N_KV_HEADS, HEAD_DIM, w_cache), (0, 3, 1, 2))
        ckt, cvt = to_fm(cache_k[i]), to_fm(cache_v[i])
        qe, knt, vnt, kn, vn, un = _sample_pre(hs, cos_s, sg_s, row(ln1[i]), w_in[i])
        o, wi, gw, wpb, wab, wo = _sample_attn(
            qe, kn, vn, ckt, cvt, attn_sinks[i].reshape(N_HEADS, 1),
            (w_in[i], pool_group_w[i].reshape(POOL_WIDTH, POOL_GC), w_pool_branch[i], w_attn_branch[i], w_out[i]), 16)
        gw = gw.reshape(len(POOL_WINDOWS), POOL_GC, POOL_GC)
        h1, kp, vp, pp, nkt, nvt, w1, w2, wpp, wpg = _prompt_mixer(
            hp, cos_p, sg_p, attn_sinks[i], row(ln1[i]), wi, gw, row(pool_scale[i]), wpb, wab, wo,
            knt, vnt, ckt, cvt, (w_ffn_in[i], w_ffn_out[i], w_ple_proj[i], w_ple_gate[i]), tm)
        h1s, nst = _sample_post(hs, un, jnp.transpose(state_pool[i], (1, 0, 2)), o, row(ln1[i]), wi, gw,
                                row(pool_scale[i]), wpb, wab, wo)
        hp, hs = _ffn(h1.reshape(b * s, d), p_prompt[i].reshape(b * s, PLE_DIM), h1s,
                      p_sample[i], row(ln2[i]), w1, w2, wpp, row(ple_norm[i]), wpg,
                      row(final_norm), tm, 2)
        hp = hp.reshape(b, s, d)
        from_fm_p = lambda c: jnp.transpose(c.reshape(b, N_KV_HEADS, HEAD_DIM, w_cache), (0, 3, 1, 2))
        nkp.append(from_fm_p(kp))
        nvp.append(from_fm_p(vp))
        npp.append(pp)
        nks.append(from_fm(nkt))
        nvs.append(from_fm(nvt))
        nps.append(jnp.transpose(nst, (1, 0, 2)))

    return (hp, hs, jnp.stack(nkp), jnp.stack(nvp), jnp.stack(npp),
            jnp.stack(nks), jnp.stack(nvs), jnp.stack(nps))
```

```python
import functools

import jax
import jax.numpy as jnp
import numpy as np
from jax import lax
from jax.experimental import pallas as pl
from jax.experimental.pallas import tpu as pltpu

D_MODEL = 1024
HEAD_DIM = 64
N_HEADS = D_MODEL // HEAD_DIM
N_KV_HEADS = N_HEADS // 4
GROUP = N_HEADS // N_KV_HEADS
ROT_DIMS = HEAD_DIM // 4
ROPE_THETA = 500000.0
WINDOW = 128
BLOCK = 128
POOL_WIDTH = D_MODEL // 2
POOL_WINDOWS = (2, 4, 8, 16)
POOL_GC = POOL_WIDTH // len(POOL_WINDOWS)
POOL_STATE = max(POOL_WINDOWS) - 1
FFN_HIDDEN = -(-8 * D_MODEL // (3 * 256)) * 256
PLE_DIM = 256
EPS = 1e-6
NEG_INF = -1e30
LOG2E = 1.4426950408889634
PAST_LEN = 16384

Q_W = N_HEADS * HEAD_DIM
KV_W = N_KV_HEADS * HEAD_DIM
C_U, C_Q, C_K, C_V, C_GP, C_GA, C_END = 0, POOL_WIDTH, POOL_WIDTH + Q_W, POOL_WIDTH + Q_W + KV_W, \
    POOL_WIDTH + Q_W + 2 * KV_W, POOL_WIDTH + Q_W + 2 * KV_W + D_MODEL, POOL_WIDTH + Q_W + 2 * KV_W + 2 * D_MODEL

LANES = 128
S_AHEAD = 16
S_SLOTS = 16
GATE_COLS = 256
SUBLANES = 8
U_HALO = 24
VMEM_LIMIT = 56 * 1024 * 1024

BF16 = jnp.bfloat16
F32 = jnp.float32


def _dot(a, b):
    return jnp.dot(a, b, preferred_element_type=F32)


def _dot_nt(a, b):
    return lax.dot_general(a, b, (((1,), (1,)), ((), ())), preferred_element_type=F32)


def _sigmoid(x):
    return 0.5 * jnp.tanh(0.5 * x) + 0.5


def _rms(x, g):
    y = x * lax.rsqrt(jnp.mean(x * x, axis=-1, keepdims=True) + EPS)
    return y * g


def _rope(x, cos, sg, lo8):
    outs = []
    for c in range(x.shape[1] // LANES):
        xc = x[:, c * LANES:(c + 1) * LANES]
        partner = jnp.where(lo8, pltpu.roll(xc, LANES - ROT_DIMS // 2, 1), pltpu.roll(xc, ROT_DIMS // 2, 1))
        outs.append(xc * cos + partner * sg)
    return jnp.concatenate(outs, axis=1)


def _lane_masks():
    lane = lax.broadcasted_iota(jnp.int32, (1, LANES), 1)
    lo8 = (lane % HEAD_DIM) < (ROT_DIMS // 2)
    lo64 = lane < HEAD_DIM
    return lo8, lo64


def _pool_mix(u, win_sum_fn, cnt_fn, gw_ref, pscale_ref):
    mixed = []
    for g, w in enumerate(POOL_WINDOWS):
        cols = slice(g * POOL_GC, (g + 1) * POOL_GC)
        ug = u[:, cols]
        m = win_sum_fn(g, w, ug) / cnt_fn(w) - ug
        mixed.append(_dot(m.astype(BF16), gw_ref[g]) * pscale_ref[:, cols])
    return jnp.concatenate(mixed, axis=1)


def _mixer_kernel(sinks_ref, x_ref, cos_ref, sg_ref, ln1_ref, w_in_ref, gw_ref, pscale_ref, w_pb_ref, w_ab_ref,
                  w_out_ref, knt_ref, vnt_ref, ckt_ref, cvt_ref, *rest, tm, n_cast):
    cast_src = rest[:n_cast]
    h_ref, ko_ref, vo_ref, po_ref, nkt_ref, nvt_ref = rest[n_cast:n_cast + 6]
    cast_dst = rest[n_cast + 6:2 * n_cast + 6]
    kl_scr, kh_scr, vl_scr, vh_scr, u_scr, lvl_scr, attn_scr, q_scr, s_scr, g_scr = rest[2 * n_cast + 6:]
    t = pl.program_id(1)
    step = pl.program_id(0) * pl.num_programs(1) + t
    lo8, lo64 = _lane_masks()

    @pl.when(t == 0)
    def _():
        for scr in (kl_scr, kh_scr, vl_scr, vh_scr):
            scr[:, 0:BLOCK, :] = jnp.zeros((N_KV_HEADS, BLOCK, LANES), BF16)
        u_scr[0:U_HALO, :] = jnp.zeros((U_HALO, POOL_WIDTH), F32)
        lvl_scr[:, 0:SUBLANES, :] = jnp.zeros((len(POOL_WINDOWS), SUBLANES, POOL_GC), F32)

    x = x_ref[0]
    xn = _rms(x, ln1_ref[...]).astype(BF16)
    cos = cos_ref[...]
    sg = sg_ref[...]

    k = _rope(_dot(xn, w_in_ref[:, C_K:C_V]), cos, sg, lo8)
    v = _dot(xn, w_in_ref[:, C_V:C_GP])
    q = (_rope(_dot(xn, w_in_ref[:, C_Q:C_K]), cos, sg, lo8) * (HEAD_DIM ** -0.5 * LOG2E)).astype(BF16)
    q_scr[...] = q
    u = _dot(xn, w_in_ref[:, C_U:C_Q])

    zero = jnp.zeros((tm, LANES), F32)
    for src, lo_scr, hi_scr in ((k, kl_scr, kh_scr), (v, vl_scr, vh_scr)):
        for p in range(KV_W // LANES):
            xp = src[:, p * LANES:(p + 1) * LANES]
            xs = pltpu.roll(xp, HEAD_DIM, 1)
            lo_scr[2 * p, BLOCK:, :] = jnp.where(lo64, xp, zero).astype(BF16)
            hi_scr[2 * p, BLOCK:, :] = jnp.where(lo64, zero, xs).astype(BF16)
            lo_scr[2 * p + 1, BLOCK:, :] = jnp.where(lo64, xs, zero).astype(BF16)
            hi_scr[2 * p + 1, BLOCK:, :] = jnp.where(lo64, zero, xp).astype(BF16)

    qi = lax.broadcasted_iota(jnp.int32, (BLOCK, BLOCK), 0)
    ci = lax.broadcasted_iota(jnp.int32, (BLOCK, BLOCK), 1)
    from_prev = ci > qi
    bias0 = jnp.where(jnp.logical_and(t == 0, from_prev), NEG_INF, 0.0).astype(F32)

    ones_lo = jnp.broadcast_to(jnp.where(lo64, 1.0, 0.0).astype(BF16), (2 * BLOCK, LANES))
    ones_hi = jnp.broadcast_to(jnp.where(lo64, 0.0, 1.0).astype(BF16), (2 * BLOCK, LANES))

    units = [(j, kh) for j in range(tm // BLOCK) for kh in range(N_KV_HEADS)]

    def scores(i):
        j, kh = units[i]
        rows = slice(j * BLOCK, (j + 1) * BLOCK)
        win = slice(j * BLOCK, (j + 2) * BLOCK)
        qq = jnp.concatenate([q_scr[rows, (2 * kh + a) * LANES:(2 * kh + a + 1) * LANES] for a in range(2)], axis=0)
        kcat = jnp.concatenate([kl_scr[kh, win, :], kh_scr[kh, win, :]], axis=0)
        s = _dot_nt(qq, kcat)
        for a in range(2):
            for half in range(2):
                sa = s[a * BLOCK:(a + 1) * BLOCK, half * 2 * BLOCK:(half + 1) * 2 * BLOCK]
                folded = jnp.where(from_prev, sa[:, :BLOCK], sa[:, BLOCK:])
                if j == 0:
                    folded = folded + bias0
                s_scr[i % S_SLOTS, a * BLOCK:(a + 1) * BLOCK, half * BLOCK:(half + 1) * BLOCK] = folded

    units_per_gate = len(units) * GATE_COLS // (2 * D_MODEL)
    for i in range(S_AHEAD):
        scores(i)

    def softmax(i):
        kh = units[i][1]
        ps, sink_terms = [], []
        for a in range(2):
            es, st = [], []
            for half in range(2):
                sh = s_scr[i % S_SLOTS, a * BLOCK:(a + 1) * BLOCK, half * BLOCK:(half + 1) * BLOCK]
                sink = sinks_ref[4 * kh + 2 * a + half] * LOG2E
                m = jnp.maximum(jnp.max(sh, axis=1, keepdims=True), sink)
                e = jnp.exp2(sh - m)
                eb = e.astype(BF16)
                es.append(jnp.where(from_prev, eb, jnp.zeros_like(eb)))
                es.append(jnp.where(from_prev, jnp.zeros_like(eb), eb))
                st.append(jnp.exp2(sink - m))
            ps.append(jnp.concatenate(es, axis=1))
            sink_terms.append(jnp.where(lo64, st[0], st[1]))
        return jnp.concatenate(ps, axis=0), sink_terms

    def weighted_values(i, p, sink_terms):
        j, kh = units[i]
        rows = slice(j * BLOCK, (j + 1) * BLOCK)
        win = slice(j * BLOCK, (j + 2) * BLOCK)
        vcat = jnp.concatenate([
            jnp.concatenate([vl_scr[kh, win, :], ones_lo], axis=1),
            jnp.concatenate([vh_scr[kh, win, :], ones_hi], axis=1)], axis=0)
        o = _dot(p, vcat)
        for a in range(2):
            oa = o[a * BLOCK:(a + 1) * BLOCK]
            attn_scr[rows, (2 * kh + a) * LANES:(2 * kh + a + 1) * LANES] = (
                oa[:, :LANES] / (oa[:, LANES:] + sink_terms[a])).astype(BF16)

    pending = None
    for i in range(len(units)):
        if i + S_AHEAD < len(units):
            scores(i + S_AHEAD)
        current = (i,) + softmax(i)
        if pending is not None:
            weighted_values(*pending)
        pending = current
        if i % units_per_gate == units_per_gate - 1:
            gcols = slice((i // units_per_gate) * GATE_COLS, (i // units_per_gate + 1) * GATE_COLS)
            g_scr[:, gcols] = _sigmoid(_dot(xn, w_in_ref[:, C_GP + gcols.start:C_GP + gcols.stop]))
    weighted_values(*pending)

    u_scr[U_HALO:, :] = u
    po_ref[0] = u_scr[U_HALO + tm - POOL_STATE:U_HALO + tm, :]
    pos = t * tm + lax.broadcasted_iota(jnp.int32, (tm, 1), 0)

    def win_sum(g, w, ug):
        cols = slice(g * POOL_GC, (g + 1) * POOL_GC)
        n = U_HALO - SUBLANES + tm
        src, span = u_scr, 1
        while span < w:
            lvl = src[SUBLANES:SUBLANES + n, cols] + src[SUBLANES - span:SUBLANES - span + n, cols]
            span *= 2
            if span < w:
                lvl_scr[g, SUBLANES:SUBLANES + n, :] = lvl
                src, cols = lvl_scr.at[g], slice(None)
        return lvl[U_HALO - SUBLANES:, :]

    pooled = _pool_mix(u, win_sum, lambda w: jnp.minimum(w, pos + 1).astype(F32), gw_ref, pscale_ref)

    for scr in (kl_scr, kh_scr, vl_scr, vh_scr):
        scr[:, 0:BLOCK, :] = scr[:, tm:tm + BLOCK, :]
    u_scr[0:U_HALO, :] = u_scr[tm:tm + U_HALO, :]

    _shift_caches(step, ckt_ref.shape[0], knt_ref, vnt_ref, ckt_ref, cvt_ref, nkt_ref, nvt_ref)
    for src, dst in zip(cast_src, cast_dst):
        dst[...] = src[...].astype(BF16)
    ko_ref[0] = k[tm - WINDOW:, :].T
    vo_ref[0] = v[tm - WINDOW:, :].T

    merged = g_scr[:, 0:D_MODEL] * _dot(pooled.astype(BF16), w_pb_ref[...])
    merged = merged + g_scr[:, D_MODEL:2 * D_MODEL] * _dot(attn_scr[...], w_ab_ref[...])
    h_ref[0] = x + _dot(merged.astype(BF16), w_out_ref[...])


def _const_spec(shape):
    nd = len(shape)
    return pl.BlockSpec(shape, lambda *_: (0,) * nd, pipeline_mode=pl.Buffered(1))


def _cast_block_rows(rows, steps):
    br = 2 * SUBLANES
    while rows % br or rows // br > steps:
        br *= 2
    return br


def _cast_specs(to_cast, steps, step_of):
    specs = []
    for w in to_cast:
        br = _cast_block_rows(w.shape[0], steps)
        last = w.shape[0] // br - 1
        specs.append(pl.BlockSpec(
            (br, w.shape[1]), lambda *idx, last=last: (jnp.minimum(step_of(*idx), last), 0)))
    return specs


def _prompt_mixer(x, cos, sg, sinks, ln1, w_in, gw, pscale, w_pb, w_ab, w_out, knt, vnt, ckt, cvt, to_cast, tm):
    b, s, d = x.shape
    nt = s // tm
    cast_specs = _cast_specs(to_cast, b * nt, lambda bi, ti, *_: bi * nt + ti)
    nb = ckt.shape[0] // (b * nt)
    assert nb * b * nt == ckt.shape[0]
    cache_spec = pl.BlockSpec((nb,) + ckt.shape[1:], lambda bi, ti, *_: (bi * nt + ti, 0, 0))
    grid_spec = pltpu.PrefetchScalarGridSpec(
        num_scalar_prefetch=1,
        grid=(b, nt),
        in_specs=[
            pl.BlockSpec((1, tm, d), lambda bi, ti, *_: (bi, ti, 0)),
            pl.BlockSpec((tm, LANES), lambda bi, ti, *_: (ti, 0)),
            pl.BlockSpec((tm, LANES), lambda bi, ti, *_: (ti, 0)),
            _const_spec((1, d)),
            _const_spec(w_in.shape),
            _const_spec(gw.shape),
            _const_spec((1, POOL_WIDTH)),
            _const_spec(w_pb.shape),
            _const_spec(w_ab.shape),
            _const_spec(w_out.shape),
            _const_spec(knt.shape),
            _const_spec(vnt.shape),
            cache_spec,
            cache_spec,
        ] + cast_specs,
        out_specs=[
            pl.BlockSpec((1, tm, d), lambda bi, ti, *_: (bi, ti, 0)),
            pl.BlockSpec((1, KV_W, WINDOW), lambda bi, ti, *_: (bi, 0, 0)),
            pl.BlockSpec((1, KV_W, WINDOW), lambda bi, ti, *_: (bi, 0, 0)),
            pl.BlockSpec((1, POOL_STATE, POOL_WIDTH), lambda bi, ti, *_: (bi, 0, 0)),
            cache_spec,
            cache_spec,
        ] + cast_specs,
        scratch_shapes=[
            pltpu.VMEM((N_KV_HEADS, BLOCK + tm, LANES), BF16),
            pltpu.VMEM((N_KV_HEADS, BLOCK + tm, LANES), BF16),
            pltpu.VMEM((N_KV_HEADS, BLOCK + tm, LANES), BF16),
            pltpu.VMEM((N_KV_HEADS, BLOCK + tm, LANES), BF16),
            pltpu.VMEM((U_HALO + tm, POOL_WIDTH), F32),
            pltpu.VMEM((len(POOL_WINDOWS), U_HALO + tm, POOL_GC), F32),
            pltpu.VMEM((tm, Q_W), BF16),
            pltpu.VMEM((tm, Q_W), BF16),
            pltpu.VMEM((S_SLOTS, 2 * BLOCK, 2 * BLOCK), F32),
            pltpu.VMEM((tm, 2 * D_MODEL), F32),
        ],
    )
    return pl.pallas_call(
        functools.partial(_mixer_kernel, tm=tm, n_cast=len(to_cast)),
        grid_spec=grid_spec,
        out_shape=[
            jax.ShapeDtypeStruct((b, s, d), F32),
            jax.ShapeDtypeStruct((b, KV_W, WINDOW), F32),
            jax.ShapeDtypeStruct((b, KV_W, WINDOW), F32),
            jax.ShapeDtypeStruct((b, POOL_STATE, POOL_WIDTH), F32),
            jax.ShapeDtypeStruct(ckt.shape, F32),
            jax.ShapeDtypeStruct(cvt.shape, F32),
        ] + [jax.ShapeDtypeStruct(w.shape, BF16) for w in to_cast],
        compiler_params=pltpu.CompilerParams(
            dimension_semantics=("arbitrary", "arbitrary"), vmem_limit_bytes=VMEM_LIMIT),
        name="prompt_mixer",
    )(sinks, x, cos, sg, ln1, w_in, gw, pscale, w_pb, w_ab, w_out, knt, vnt, ckt, cvt, *to_cast)


FFN_CHUNKS = ((0, 1024), (1024, 2048), (2048, FFN_HIDDEN))


def _ffn_kernel(h_ref, p_ref, hs_ref, ps_ref, ln2_ref, w1_ref, w2_ref, w_pp_ref, pn_ref, w_pg_ref, fn_ref, y_ref,
                ys_ref, *, tm):
    ns = hs_ref.shape[0]
    pad = (-ns) % (2 * SUBLANES)

    def tile_stages(r0, with_sample):
        rows = slice(r0, r0 + tm)
        st = {}

        def gather(ref, ref_s):
            if not with_sample:
                return ref[rows, :]
            extra = ref_s[:, 0, :] if len(ref_s.shape) == 3 else ref_s[...]
            parts = [ref[rows, :], extra] + ([jnp.zeros((pad, ref.shape[1]), F32)] if pad else [])
            return jnp.concatenate(parts, axis=0)

        def up_proj(c):
            lo, hi = FFN_CHUNKS[c]
            if c == 0:
                st['h'] = gather(h_ref, hs_ref)
                st['hn'] = _rms(st['h'], ln2_ref[...]).astype(BF16)
                st['acc'] = st['h']
            st['gate', c] = _dot(st['hn'], w1_ref[:, lo:hi])
            st['up', c] = _dot(st['hn'], w1_ref[:, FFN_HIDDEN + lo:FFN_HIDDEN + hi])
            if c == len(FFN_CHUNKS) - 1:
                st['e'] = _rms(_dot(gather(p_ref, ps_ref).astype(BF16), w_pp_ref[...]), pn_ref[...])

        def down_proj(c):
            lo, hi = FFN_CHUNKS[c]
            gate = st.pop(('gate', c))
            act = (gate * _sigmoid(gate) * st.pop(('up', c))).astype(BF16)
            st['acc'] = st['acc'] + _dot(act, w2_ref[lo:hi, :])

        def ple_gate():
            st['g'] = _dot(st['acc'].astype(BF16), w_pg_ref[...])

        def finish():
            h3 = st['acc'] + _sigmoid(st['g']) * st['e']
            y = _rms(h3, fn_ref[...])
            y_ref[rows, :] = y[:tm]
            if with_sample:
                ys_ref[:, 0, :] = y[tm:tm + ns]

        n = len(FFN_CHUNKS)
        steps = [functools.partial(up_proj, 0)]
        for c in range(1, n):
            steps += [functools.partial(up_proj, c), functools.partial(down_proj, c - 1)]
        return steps + [functools.partial(down_proj, n - 1), ple_gate, finish]

    tail = 3
    order = []
    starts = list(range(0, h_ref.shape[0], tm))
    for r0 in starts:
        steps = tile_stages(r0, r0 == starts[-1])
        held, order = order[len(order) - tail:] if order else [], order[:len(order) - tail] if order else []
        for k in range(max(len(held), tail)):
            order += steps[k:k + 1] + held[k:k + 1]
        order += steps[tail:]
    for step in order:
        step()


def _ffn(h, p, hs, ps, ln2, w1, w2, w_pp, pn, w_pg, fn, tm, nsub):
    n, d = h.shape
    blk = tm * nsub
    ns = hs.shape[0] // (n // blk)
    assert ns * (n // blk) == hs.shape[0] and ns % SUBLANES == 0
    return pl.pallas_call(
        functools.partial(_ffn_kernel, tm=tm),
        grid=(n // blk,),
        in_specs=[
            pl.BlockSpec((blk, d), lambda i: (i, 0)),
            pl.BlockSpec((blk, PLE_DIM), lambda i: (i, 0)),
            pl.BlockSpec((ns, d), lambda i: (i, 0)),
            pl.BlockSpec((ns, 1, PLE_DIM), lambda i: (i, 0, 0)),
            _const_spec((1, d)),
            _const_spec(w1.shape),
            _const_spec(w2.shape),
            _const_spec(w_pp.shape),
            _const_spec((1, d)),
            _const_spec(w_pg.shape),
            _const_spec((1, d)),
        ],
        out_specs=[pl.BlockSpec((blk, d), lambda i: (i, 0)), pl.BlockSpec((ns, 1, d), lambda i: (i, 0, 0))],
        out_shape=[jax.ShapeDtypeStruct((n, d), F32), jax.ShapeDtypeStruct((hs.shape[0], 1, d), F32)],
        compiler_params=pltpu.CompilerParams(dimension_semantics=("arbitrary",), vmem_limit_bytes=VMEM_LIMIT),
        name="ffn_ple_norm",
    )(h, p, hs, ps, ln2, w1, w2, w_pp, pn, w_pg, fn)


def _sample_pre_kernel(x_ref, cos_ref, sg_ref, ln1_ref, w_in_ref, qe_ref, knt_ref, vnt_ref, kn_ref, vn_ref, u_ref):
    n = x_ref.shape[0]
    lo8, _ = _lane_masks()
    xn = _rms(x_ref[:, 0, :], ln1_ref[...]).astype(BF16)
    cos = cos_ref[...]
    sg = sg_ref[...]
    w = lambda lo, hi: w_in_ref[:, lo:hi].astype(BF16)
    u_ref[...] = _dot(xn, w(C_U, C_Q))
    q = (_rope(_dot(xn, w(C_Q, C_K)), cos, sg, lo8) * (HEAD_DIM ** -0.5)).astype(BF16)
    kn = _rope(_dot(xn, w(C_K, C_V)), cos, sg, lo8)
    vn = _dot(xn, w(C_V, C_GP))
    kn_ref[...] = kn
    vn_ref[...] = vn
    knt_ref[...] = kn.T
    vnt_ref[...] = vn.T
    _, lo64 = _lane_masks()
    qf = q.astype(F32)
    zero = jnp.zeros((n, LANES), F32)
    for r in range(N_HEADS):
        kh = r // GROUP
        chunk = qf[:, (r // 2) * LANES:(r // 2 + 1) * LANES]
        if r % 2 != kh % 2:
            chunk = pltpu.roll(chunk, HEAD_DIM, 1)
        placed = jnp.where(lo64, chunk, zero) if kh % 2 == 0 else jnp.where(lo64, zero, chunk)
        for c in range(KV_W // LANES):
            qe_ref[c, pl.ds(r, n, stride=N_HEADS), :] = placed if c == kh // 2 else zero


def _sample_pre(x, cos, sg, ln1, w_in):
    n, _, d = x.shape
    return pl.pallas_call(
        _sample_pre_kernel,
        grid=(1,),
        in_specs=[
            _const_spec((n, 1, d)),
            _const_spec((1, LANES)),
            _const_spec((1, LANES)),
            _const_spec((1, d)),
            pl.BlockSpec((d, C_GP), lambda i: (0, 0), pipeline_mode=pl.Buffered(1)),
        ],
        out_specs=[
            pl.BlockSpec((KV_W // LANES, n * N_HEADS, LANES), lambda i: (0, 0, 0)),
            pl.BlockSpec((KV_W, n), lambda i: (0, 0)),
            pl.BlockSpec((KV_W, n), lambda i: (0, 0)),
            pl.BlockSpec((n, KV_W), lambda i: (0, 0)),
            pl.BlockSpec((n, KV_W), lambda i: (0, 0)),
            pl.BlockSpec((n, POOL_WIDTH), lambda i: (0, 0)),
        ],
        out_shape=[
            jax.ShapeDtypeStruct((KV_W // LANES, n * N_HEADS, LANES), F32),
            jax.ShapeDtypeStruct((KV_W, n), F32),
            jax.ShapeDtypeStruct((KV_W, n), F32),
            jax.ShapeDtypeStruct((n, KV_W), F32),
            jax.ShapeDtypeStruct((n, KV_W), F32),
            jax.ShapeDtypeStruct((n, POOL_WIDTH), F32),
        ],
        compiler_params=pltpu.CompilerParams(dimension_semantics=("arbitrary",), vmem_limit_bytes=VMEM_LIMIT),
        name="sample_pre",
    )(x, cos, sg, ln1, w_in)


def _shift_caches(step, nb, knt_ref, vnt_ref, ckt_ref, cvt_ref, nkt_ref, nvt_ref):
    n = knt_ref.shape[1]
    w_cache = ckt_ref.shape[2]
    newest = lax.broadcasted_iota(jnp.int32, (1, w_cache), 1) == w_cache - 1
    shift = lax.rem(n - lax.rem(step * nb, n), n)
    kcols = pltpu.roll(knt_ref[...], shift, 1)
    vcols = pltpu.roll(vnt_ref[...], shift, 1)
    for bl in range(nb):
        nkt_ref[bl] = jnp.where(newest, kcols[:, bl:bl + 1], pltpu.roll(ckt_ref[bl], w_cache - 1, 1))
        nvt_ref[bl] = jnp.where(newest, vcols[:, bl:bl + 1], pltpu.roll(cvt_ref[bl], w_cache - 1, 1))


def _sample_attn_kernel(qe_ref, kn_ref, vn_ref, ckt_ref, cvt_ref, sink_ref, *rest, bb, n_cast):
    cast_src = rest[:n_cast]
    o_ref = rest[n_cast]
    cast_dst = rest[n_cast + 1:]
    for src, dst in zip(cast_src, cast_dst):
        dst[...] = src[...].astype(BF16)
    w_cache = ckt_ref.shape[2]
    oldest = lax.broadcasted_iota(jnp.int32, (1, w_cache), 1) == 0
    sink = sink_ref[...]
    rounded = lambda a: a.astype(BF16).astype(F32)
    scores = []
    for bl in range(bb):
        rows = slice(bl * N_HEADS, (bl + 1) * N_HEADS)
        qb = jnp.concatenate([qe_ref[c, rows, :] for c in range(KV_W // LANES)], axis=1).astype(BF16)
        s_old = jnp.where(oldest, NEG_INF, _dot(qb, ckt_ref[bl].astype(BF16)))
        s_new = jnp.sum(qb.astype(F32) * rounded(kn_ref[bl:bl + 1, :]), axis=1, keepdims=True)
        scores.append((s_old, s_new))
    probs = []
    for s_old, s_new in scores:
        m = jnp.maximum(jnp.maximum(jnp.max(s_old, axis=-1, keepdims=True), s_new), sink)
        e_old = jnp.exp(s_old - m)
        e_new = jnp.exp(s_new - m)
        denom = jnp.sum(e_old, axis=-1, keepdims=True) + e_new + jnp.exp(sink - m)
        probs.append((e_old.astype(BF16), rounded(e_new), denom))
    for bl, (e_old, e_new, denom) in enumerate(probs):
        rows = slice(bl * N_HEADS, (bl + 1) * N_HEADS)
        o = (_dot_nt(e_old, cvt_ref[bl].astype(BF16)) + e_new * rounded(vn_ref[bl:bl + 1, :])) / denom
        for c in range(KV_W // LANES):
            o_ref[c, rows, :] = o[:, c * LANES:(c + 1) * LANES]


def _sample_attn(qe, kn, vn, ckt, cvt, sink, to_cast, bb):
    n, _, w_cache = ckt.shape
    cast_specs = _cast_specs(to_cast, n // bb, lambda i: i)
    cache_spec = pl.BlockSpec((bb, KV_W, w_cache), lambda i: (i, 0, 0))
    new_spec = pl.BlockSpec((bb, KV_W), lambda i: (i, 0))
    head_spec = pl.BlockSpec((KV_W // LANES, bb * N_HEADS, LANES), lambda i: (0, i, 0))
    return pl.pallas_call(
        functools.partial(_sample_attn_kernel, bb=bb, n_cast=len(to_cast)),
        grid=(n // bb,),
        in_specs=[head_spec, new_spec, new_spec, cache_spec, cache_spec, _const_spec((N_HEADS, 1))] + cast_specs,
        out_specs=[head_spec] + cast_specs,
        out_shape=[jax.ShapeDtypeStruct((KV_W // LANES, n * N_HEADS, LANES), F32)]
        + [jax.ShapeDtypeStruct(w.shape, BF16) for w in to_cast],
        compiler_params=pltpu.CompilerParams(dimension_semantics=("arbitrary",), vmem_limit_bytes=VMEM_LIMIT),
        name="sample_attn",
    )(qe, kn, vn, ckt, cvt, sink, *to_cast)


def _sample_post_kernel(x_ref, u_ref, st_ref, o_ref, ln1_ref, w_g_ref, gw_ref, pscale_ref, w_pb_ref, w_ab_ref,
                        w_out_ref, h_ref, nst_ref):
    x = x_ref[:, 0, :]
    n = x.shape[0]
    xn = _rms(x, ln1_ref[...]).astype(BF16)
    u = u_ref[...]
    nst_ref[0:POOL_STATE - 1] = st_ref[1:POOL_STATE]
    nst_ref[POOL_STATE - 1] = u

    def win_sum(g, w, ug):
        acc = ug
        for i in range(1, w):
            acc = acc + st_ref[POOL_STATE - i, :, g * POOL_GC:(g + 1) * POOL_GC]
        return acc

    pooled = _pool_mix(u, win_sum, lambda w: jnp.float32(min(w, PAST_LEN + 1)), gw_ref, pscale_ref)
    merged = _sigmoid(_dot(xn, w_g_ref[:, 0:D_MODEL])) * _dot(pooled.astype(BF16), w_pb_ref[...])
    kv_of_lane = lax.broadcasted_iota(jnp.int32, (1, KV_W), 1) // HEAD_DIM
    ab = jnp.zeros((n, D_MODEL), F32)
    for g in range(GROUP):
        row_g = jnp.zeros((n, KV_W), F32)
        for kh in range(N_KV_HEADS):
            r = kh * GROUP + g
            o_r = jnp.concatenate([o_ref[c, pl.ds(r, n, stride=N_HEADS), :] for c in range(KV_W // LANES)], axis=1)
            row_g = jnp.where(kv_of_lane == kh, o_r, row_g)
        w_g = jnp.concatenate([w_ab_ref[(kh * GROUP + g) * HEAD_DIM:(kh * GROUP + g + 1) * HEAD_DIM, :]
                               for kh in range(N_KV_HEADS)], axis=0)
        ab = ab + _dot(row_g.astype(BF16), w_g)
    merged = merged + _sigmoid(_dot(xn, w_g_ref[:, D_MODEL:2 * D_MODEL])) * ab
    h_ref[...] = x + _dot(merged.astype(BF16), w_out_ref[...])


def _sample_post(x, u, st, o, ln1, w_in, gw, pscale, w_pb, w_ab, w_out):
    n, _, d = x.shape
    return pl.pallas_call(
        _sample_post_kernel,
        grid=(1,),
        in_specs=[
            _const_spec((n, 1, d)),
            _const_spec((n, POOL_WIDTH)),
            _const_spec(st.shape),
            _const_spec((KV_W // LANES, n * N_HEADS, LANES)),
            _const_spec((1, d)),
            pl.BlockSpec((d, 2 * D_MODEL), lambda i: (0, 1), pipeline_mode=pl.Buffered(1)),
            _const_spec(gw.shape),
            _const_spec((1, POOL_WIDTH)),
            _const_spec(w_pb.shape),
            _const_spec(w_ab.shape),
            _const_spec(w_out.shape),
        ],
        out_specs=[pl.BlockSpec((n, d), lambda i: (0, 0)), pl.BlockSpec(st.shape, lambda i: (0, 0, 0))],
        out_shape=[jax.ShapeDtypeStruct((n, d), F32), jax.ShapeDtypeStruct(st.shape, F32)],
        compiler_params=pltpu.CompilerParams(dimension_semantics=("arbitrary",), vmem_limit_bytes=VMEM_LIMIT),
        name="sample_post",
    )(x, u, st, o, ln1, w_in, gw, pscale, w_pb, w_ab, w_out)


def _rope_tables(first_pos, n):
    half = ROT_DIMS // 2
    inv = ROPE_THETA ** (-(np.arange(0, ROT_DIMS, 2, dtype=np.float64) / ROT_DIMS))
    ang = np.arange(first_pos, first_pos + n, dtype=np.float64)[:, None] * inv[None, :]
    cos, sin = np.cos(ang), np.sin(ang)
    rest = HEAD_DIM - 2 * half
    c64 = np.concatenate([cos, cos, np.ones((n, rest))], axis=1)
    s64 = np.concatenate([-sin, sin, np.zeros((n, rest))], axis=1)
    reps = LANES // HEAD_DIM
    return jnp.asarray(np.tile(c64, (1, reps)), F32), jnp.asarray(np.tile(s64, (1, reps)), F32)


def kernel(x_prompt, x_sample, p_prompt, p_sample, cache_k, cache_v, state_pool, ln1, w_in, pool_group_w, pool_scale,
           attn_sinks, w_pool_branch, w_attn_branch, w_out, ln2, w_ffn_in, w_ffn_out, w_ple_proj, ple_norm,
           w_ple_gate, final_norm):
    depth = ln1.shape[0]
    b, s, d = x_prompt.shape
    bd, t_dec, _ = x_sample.shape
    w_cache = cache_k.shape[2]
    assert depth == 1 and t_dec == 1 and w_cache == WINDOW and s % BLOCK == 0 and d == D_MODEL
    tm = 512
    assert s % tm == 0

    cos_p, sg_p = _rope_tables(0, s)
    cos_s, sg_s = _rope_tables(PAST_LEN, t_dec)

    hp = x_prompt
    hs = x_sample
    row = lambda a: a.reshape(1, -1)
    nkp, nvp, npp, nks, nvs, nps = [], [], [], [], [], []
    for i in range(depth):
        to_fm = lambda c: jnp.transpose(c, (0, 2, 3, 1)).reshape(bd, KV_W, w_cache)
        from_fm = lambda c: jnp.transpose(c.reshape(bd, N_KV_HEADS, HEAD_DIM, w_cache), (0, 3, 1, 2))
        ckt, cvt = to_fm(cache_k[i]), to_fm(cache_v[i])
        qe, knt, vnt, kn, vn, un = _sample_pre(hs, cos_s, sg_s, row(ln1[i]), w_in[i])
        o, wi, gw, wpb, wab, wo = _sample_attn(
            qe, kn, vn, ckt, cvt, attn_sinks[i].reshape(N_HEADS, 1),
            (w_in[i], pool_group_w[i].reshape(POOL_WIDTH, POOL_GC), w_pool_branch[i], w_attn_branch[i], w_out[i]), 16)
        gw = gw.reshape(len(POOL_WINDOWS), POOL_GC, POOL_GC)
        h1, kp, vp, pp, nkt, nvt, w1, w2, wpp, wpg = _prompt_mixer(
            hp, cos_p, sg_p, attn_sinks[i], row(ln1[i]), wi, gw, row(pool_scale[i]), wpb, wab, wo,
            knt, vnt, ckt, cvt, (w_ffn_in[i], w_ffn_out[i], w_ple_proj[i], w_ple_gate[i]), tm)
        h1s, nst = _sample_post(hs, un, jnp.transpose(state_pool[i], (1, 0, 2)), o, row(ln1[i]), wi, gw,
                                row(pool_scale[i]), wpb, wab, wo)
        hp, hs = _ffn(h1.reshape(b * s, d), p_prompt[i].reshape(b * s, PLE_DIM), h1s,
                      p_sample[i], row(ln2[i]), w1, w2, wpp, row(ple_norm[i]), wpg,
                      row(final_norm), tm, 2)
        hp = hp.reshape(b, s, d)
        from_fm_p = lambda c: jnp.transpose(c.reshape(b, N_KV_HEADS, HEAD_DIM, w_cache), (0, 3, 1, 2))
        nkp.append(from_fm_p(kp))
        nvp.append(from_fm_p(vp))
        npp.append(pp)
        nks.append(from_fm(nkt))
        nvs.append(from_fm(nvt))
        nps.append(jnp.transpose(nst, (1, 0, 2)))

    return (hp, hs, jnp.stack(nkp), jnp.stack(nvp), jnp.stack(npp),
            jnp.stack(nks), jnp.stack(nvs), jnp.stack(nps))
```
